```python
import math
import jax, jax.numpy as jnp
from jax import lax
import numpy as np

D_MODEL = 2048
BATCH = 4
SEQ = 2048
DEPTH = 2

N_MIXERS = 2
EPS = 1e-6

DIL_PAIRS = ((128, 1), (512, 4), (2048, 16))
N_DIL_GROUPS = 3
DSA_HEADS_PER_GROUP = 8
DSA_HEAD_DIM = 128
DSA_WIDTH = DSA_HEADS_PER_GROUP * DSA_HEAD_DIM
DSA_PROJ = N_DIL_GROUPS * 3 * DSA_WIDTH
BAND = 128

MLA_HEADS = 16
MLA_Q_LORA = 512
MLA_KV_LORA = 512
MLA_NOPE = 128
MLA_ROPE = 64
MLA_V = 128
MLA_QK = MLA_NOPE + MLA_ROPE
MLA_PROJ = MLA_Q_LORA + MLA_KV_LORA + MLA_ROPE
ROPE_THETA = 10000.0
ATTN_BLOCK = 128

N_GROUPS = 4
EXPERTS_PER_GROUP = 16
N_EXPERTS = N_GROUPS * EXPERTS_PER_GROUP
TOP_K = 2
D_EXPERT = 768
EXPERT_BLOCK = 128

kernel_name = "hybrid_dilated_mla_hmoe_adaln"


def rmsnorm(x, g):
    xf = x.astype(jnp.float32)
    y = xf * lax.rsqrt(jnp.mean(xf * xf, axis=-1, keepdims=True) + EPS)
    return (y * g.astype(jnp.float32)).astype(x.dtype)


def modulate(xn, shift, scale):
    return xn * (1 + scale[:, None, :]) + shift[:, None, :]


def alibi_slopes(n):
    return 2.0 ** (-8.0 * (jnp.arange(n, dtype=jnp.float32) + 1.0) / n)


def rope(x):
    S, R = x.shape[1], x.shape[-1]
    half = R // 2
    inv = ROPE_THETA ** (-jnp.arange(half, dtype=jnp.float32) / half)
    ang = jnp.arange(S, dtype=jnp.float32)[:, None] * inv[None, :]
    cos = jnp.cos(ang)[None, :, None, :]
    sin = jnp.sin(ang)[None, :, None, :]
    x1 = x[..., :half].astype(jnp.float32)
    x2 = x[..., half:].astype(jnp.float32)
    return jnp.concatenate([x1 * cos - x2 * sin, x1 * sin + x2 * cos], axis=-1).astype(x.dtype)


def dilated_group(q, k, v, window, dilation, slopes):
    B, S, H, E = q.shape
    steps = window // dilation
    L = S // dilation
    nb = -(-L // BAND)
    Lp = nb * BAND

    def to_sub(t):
        t = t.reshape(B, L, dilation, H, E).transpose(0, 2, 1, 3, 4)
        return jnp.pad(t, ((0, 0), (0, 0), (0, Lp - L), (0, 0), (0, 0)))

    def band(t):
        prev = jnp.pad(t, ((0, 0), (0, 0), (BAND, 0), (0, 0), (0, 0)))[:, :, :Lp]
        prev = prev.reshape(B, dilation, nb, BAND, H, E)
        cur = t.reshape(B, dilation, nb, BAND, H, E)
        return jnp.concatenate([prev, cur], axis=3)

    qb = to_sub(q).reshape(B, dilation, nb, BAND, H, E)
    kb = band(to_sub(k))
    vb = band(to_sub(v))
    scale = 1.0 / math.sqrt(E)
    s = jnp.einsum('brnqhe,brnkhe->brnhqk', qb, kb).astype(jnp.float32) * scale
    qi = jnp.arange(BAND)[:, None]
    kj = jnp.arange(2 * BAND)[None, :]
    delta = qi + BAND - kj
    key_m = jnp.arange(nb)[:, None, None] * BAND - BAND + kj[None]
    valid = (delta >= 0)[None] & (delta <= steps)[None] & (key_m >= 0)
    dist = (delta * dilation).astype(jnp.float32)
    bias = -slopes.astype(jnp.float32)[:, None, None] * dist[None]
    s = jnp.where(valid[None, None, :, None], s + bias[None, None, None], -jnp.inf)
    lse = jax.nn.logsumexp(s, axis=-1)
    p = jnp.exp(s - lse[..., None])
    o = jnp.einsum('brnhqk,brnkhe->brnqhe', p.astype(v.dtype), vb)
    o = o.reshape(B, dilation, Lp, H, E)[:, :, :L].transpose(0, 2, 1, 3, 4).reshape(B, S, H, E)
    lse = lse.transpose(0, 1, 2, 4, 3).reshape(B, dilation, Lp, H)[:, :, :L]
    lse = lse.transpose(0, 2, 1, 3).reshape(B, S, H)
    return o, lse


def dilated_attention(h, w_in, q_gain, k_gain, w_out):
    B, S, _ = h.shape
    qkv = (h @ w_in).reshape(B, S, N_DIL_GROUPS, 3, DSA_HEADS_PER_GROUP, DSA_HEAD_DIM)
    slopes = alibi_slopes(N_DIL_GROUPS * DSA_HEADS_PER_GROUP).reshape(
        DSA_HEADS_PER_GROUP, N_DIL_GROUPS).T
    outs, lses = [], []
    for g, (window, dilation) in enumerate(DIL_PAIRS):
        q = rmsnorm(qkv[:, :, g, 0], q_gain)
        k = rmsnorm(qkv[:, :, g, 1], k_gain)
        v = qkv[:, :, g, 2]
        o, lse = dilated_group(q, k, v, window, dilation, slopes[g])
        outs.append(o)
        lses.append(lse)
    alpha = jax.nn.softmax(jnp.stack(lses), axis=0)
    o = jnp.sum(alpha[..., None].astype(h.dtype) * jnp.stack(outs), axis=0)
    return o.reshape(B, S, DSA_WIDTH) @ w_out


def causal_block_attention(q, k, v):
    B, S, H, E = q.shape
    nq = S // ATTN_BLOCK
    scale = 1.0 / math.sqrt(E)
    qb = q.reshape(B, nq, ATTN_BLOCK, H, E).transpose(1, 0, 2, 3, 4)
    kpos = jnp.arange(S)

    def step(args):
        qblk, start = args
        s = jnp.einsum('bqhe,bkhe->bhqk', qblk, k).astype(jnp.float32) * scale
        qpos = start + jnp.arange(ATTN_BLOCK)
        s = jnp.where(kpos[None, :] <= qpos[:, None], s, -jnp.inf)
        p = jax.nn.softmax(s, axis=-1)
        return jnp.einsum('bhqk,bkhv->bqhv', p.astype(v.dtype), v)

    o = lax.map(step, (qb, jnp.arange(nq) * ATTN_BLOCK))
    return o.transpose(1, 0, 2, 3, 4).reshape(B, S, H, v.shape[-1])


def mla_attention(h, w_in, cq_gain, ckv_gain, w_q_up, w_kv_up, q_gain, k_gain, w_out):
    B, S, _ = h.shape
    proj = h @ w_in
    c_q = rmsnorm(proj[..., :MLA_Q_LORA], cq_gain)
    c_kv = rmsnorm(proj[..., MLA_Q_LORA:MLA_Q_LORA + MLA_KV_LORA], ckv_gain)
    k_pe = proj[..., MLA_Q_LORA + MLA_KV_LORA:]
    q = (c_q @ w_q_up).reshape(B, S, MLA_HEADS, MLA_QK)
    kv = (c_kv @ w_kv_up).reshape(B, S, MLA_HEADS, MLA_NOPE + MLA_V)
    v = kv[..., MLA_NOPE:]
    k = jnp.concatenate(
        [kv[..., :MLA_NOPE], jnp.broadcast_to(k_pe[:, :, None, :], (B, S, MLA_HEADS, MLA_ROPE))],
        axis=-1)
    q = rmsnorm(q, q_gain)
    k = rmsnorm(k, k_gain)
    q = jnp.concatenate([q[..., :MLA_NOPE], rope(q[..., MLA_NOPE:])], axis=-1)
    k = jnp.concatenate([k[..., :MLA_NOPE], rope(k[..., MLA_NOPE:])], axis=-1)
    o = causal_block_attention(q, k, v)
    return o.reshape(B, S, MLA_HEADS * MLA_V) @ w_out


def grouped_expert_ffn(xf, expert_id, weight, w_gate, w_up, w_down):
    T, D = xf.shape
    A = T * TOP_K
    e_flat = expert_id.reshape(A)
    w_flat = weight.reshape(A)
    tok_flat = jnp.arange(A) // TOP_K
    order = jnp.argsort(e_flat)
    e_sorted = e_flat[order]
    counts = jnp.bincount(e_flat, length=N_EXPERTS)
    padded = (counts + EXPERT_BLOCK - 1) // EXPERT_BLOCK * EXPERT_BLOCK
    starts = jnp.cumsum(counts) - counts
    pends = jnp.cumsum(padded)
    pstarts = pends - padded
    dest = pstarts[e_sorted] + jnp.arange(A) - starts[e_sorted]
    n_blocks = (A + N_EXPERTS * (EXPERT_BLOCK - 1) + EXPERT_BLOCK - 1) // EXPERT_BLOCK
    P = n_blocks * EXPERT_BLOCK
    row_tok = jnp.zeros((P,), jnp.int32).at[dest].set(tok_flat[order].astype(jnp.int32))
    row_w = jnp.zeros((P,), jnp.float32).at[dest].set(w_flat[order])
    blk_exp = jnp.minimum(
        jnp.searchsorted(pends, jnp.arange(n_blocks) * EXPERT_BLOCK, side='right'),
        N_EXPERTS - 1)
    xs = xf[row_tok].reshape(n_blocks, EXPERT_BLOCK, D)

    def expert_block(args):
        xb, e = args
        hid = jax.nn.silu(xb @ w_gate[e]) * (xb @ w_up[e])
        return hid @ w_down[e]

    ys = lax.map(expert_block, (xs, blk_exp)).reshape(P, D)
    ys = ys * row_w[:, None].astype(ys.dtype)
    return jnp.zeros((T, D), ys.dtype).at[row_tok].add(ys)


def hier_moe(h, w_rg, b_rg, w_re, b_re, w_gate, w_up, w_down):
    B, S, D = h.shape
    T = B * S
    xf = h.reshape(T, D)
    g_logits = (xf @ w_rg).astype(jnp.float32) + b_rg.astype(jnp.float32)
    g_probs = jax.nn.softmax(g_logits, axis=-1)
    _, g_idx = lax.top_k(g_logits, 1)
    g_p = jnp.take_along_axis(g_probs, g_idx, axis=1)[:, 0]
    e_logits = ((xf @ w_re).astype(jnp.float32) + b_re.astype(jnp.float32)).reshape(
        T, N_GROUPS, EXPERTS_PER_GROUP)
    e_in_group = jnp.take_along_axis(e_logits, g_idx[:, :, None], axis=1)[:, 0]
    top_v, top_i = lax.top_k(e_in_group, TOP_K)
    w = jax.nn.softmax(top_v, axis=-1) * g_p[:, None]
    expert_id = g_idx * EXPERTS_PER_GROUP + top_i
    return grouped_expert_ffn(xf, expert_id, w, w_gate, w_up, w_down).reshape(B, S, D)


def setup_inputs(seed: int = 0) -> dict:
    key = jax.random.key(seed)
    ks = iter(jax.random.split(key, 40))
    n_a = len(range(0, DEPTH, N_MIXERS))
    n_b = len(range(1, DEPTH, N_MIXERS))
    D = D_MODEL

    def nrm(shape, scale):
        return jax.random.normal(next(ks), shape, jnp.float32) * scale

    def gain(shape):
        return 1.0 + nrm(shape, 0.02)

    return {
        "x": nrm((BATCH, SEQ, D), 1.0),
        "c": nrm((BATCH, D), 1.0),
        "ada_w": nrm((DEPTH, D, 6 * D), 0.5 * D ** -0.5),
        "ada_b": nrm((DEPTH, 6 * D), 0.02),
        "norm1_g": gain((DEPTH, D)),
        "norm2_g": gain((DEPTH, D)),
        "dsa_w_in": nrm((n_a, D, DSA_PROJ), D ** -0.5),
        "dsa_q_gain": gain((n_a, DSA_HEAD_DIM)),
        "dsa_k_gain": gain((n_a, DSA_HEAD_DIM)),
        "dsa_w_out": nrm((n_a, DSA_WIDTH, D), DSA_WIDTH ** -0.5),
        "mla_w_in": nrm((n_b, D, MLA_PROJ), D ** -0.5),
        "mla_cq_gain": gain((n_b, MLA_Q_LORA)),
        "mla_ckv_gain": gain((n_b, MLA_KV_LORA)),
        "mla_w_q_up": nrm((n_b, MLA_Q_LORA, MLA_HEADS * MLA_QK), MLA_Q_LORA ** -0.5),
        "mla_w_kv_up": nrm((n_b, MLA_KV_LORA, MLA_HEADS * (MLA_NOPE + MLA_V)), MLA_KV_LORA ** -0.5),
        "mla_q_gain": gain((n_b, MLA_QK)),
        "mla_k_gain": gain((n_b, MLA_QK)),
        "mla_w_out": nrm((n_b, MLA_HEADS * MLA_V, D), (MLA_HEADS * MLA_V) ** -0.5),
        "router_group_w": nrm((DEPTH, D, N_GROUPS), D ** -0.5),
        "router_group_b": nrm((DEPTH, N_GROUPS), 0.01),
        "router_expert_w": nrm((DEPTH, D, N_EXPERTS), D ** -0.5),
        "router_expert_b": nrm((DEPTH, N_EXPERTS), 0.01),
        "expert_w_gate": nrm((DEPTH, N_EXPERTS, D, D_EXPERT), D ** -0.5),
        "expert_w_up": nrm((DEPTH, N_EXPERTS, D, D_EXPERT), D ** -0.5),
        "expert_w_down": nrm((DEPTH, N_EXPERTS, D_EXPERT, D), D_EXPERT ** -0.5),
    }


def reference(x, c, ada_w, ada_b, norm1_g, norm2_g, dsa_w_in, dsa_q_gain, dsa_k_gain,
              dsa_w_out, mla_w_in, mla_cq_gain, mla_ckv_gain, mla_w_q_up, mla_w_kv_up,
              mla_q_gain, mla_k_gain, mla_w_out, router_group_w, router_group_b,
              router_expert_w, router_expert_b, expert_w_gate, expert_w_up, expert_w_down):
    c_act = jax.nn.silu(c)
    for i in range(DEPTH):
        mod = c_act @ ada_w[i] + ada_b[i]
        shift1, scale1, gate1, shift2, scale2, gate2 = jnp.split(mod, 6, axis=-1)
        h = modulate(rmsnorm(x, norm1_g[i]), shift1, scale1)
        j = i // N_MIXERS
        if i % N_MIXERS == 0:
            y = dilated_attention(h, dsa_w_in[j], dsa_q_gain[j], dsa_k_gain[j], dsa_w_out[j])
        else:
            y = mla_attention(h, mla_w_in[j], mla_cq_gain[j], mla_ckv_gain[j], mla_w_q_up[j],
                              mla_w_kv_up[j], mla_q_gain[j], mla_k_gain[j], mla_w_out[j])
        x = x + gate1[:, None, :] * y
        h = modulate(rmsnorm(x, norm2_g[i]), shift2, scale2)
        y = hier_moe(h, router_group_w[i], router_group_b[i], router_expert_w[i],
                     router_expert_b[i], expert_w_gate[i], expert_w_up[i], expert_w_down[i])
        x = x + gate2[:, None, :] * y
    return x
```

```python
import functools
import math

import jax
import jax.numpy as jnp
from jax import lax
from jax.experimental import pallas as pl
from jax.experimental.pallas import tpu as pltpu

F32 = jnp.float32
BF16 = jnp.bfloat16

D_MODEL = 2048
EPS = 1e-6
LANES = 128
NEG_BIG = -1e30

DIL_PAIRS = ((128, 1), (512, 4), (2048, 16))
DSA_HEADS = 8
DSA_HEAD_DIM = 128
DSA_WIDTH = DSA_HEADS * DSA_HEAD_DIM
BAND = 128

MLA_HEADS = 16
MLA_Q_LORA = 512
MLA_KV_LORA = 512
MLA_NOPE = 128
MLA_ROPE = 64
MLA_V = 128
MLA_QK = MLA_NOPE + MLA_ROPE
MLA_QK_PAD = 256
ROPE_THETA = 10000.0

N_GROUPS = 4
EXPERTS_PER_GROUP = 16
N_EXPERTS = N_GROUPS * EXPERTS_PER_GROUP
D_EXPERT = 768
EXPERT_BLOCK = 128

VMEM_LIMIT = 48 * 1024 * 1024


def _cparams(sem, vmem=VMEM_LIMIT):
    return pltpu.CompilerParams(dimension_semantics=sem, vmem_limit_bytes=vmem)


def _silu(x):
    return x * (1.0 / (1.0 + jnp.exp(-x)))


def _norm_mod(x, g, shift, scale):
    ms = jnp.mean(x * x, axis=-1, keepdims=True)
    y = x * lax.rsqrt(ms + EPS) * g
    return y * (1.0 + scale) + shift


def _ada_kernel(c_ref, w_ref, b_ref, o_ref):
    ca = _silu(c_ref[...])
    hi = ca.astype(BF16)
    lo = (ca - hi.astype(F32)).astype(BF16)
    lhs = jnp.concatenate([hi, lo], axis=0)
    res = jnp.dot(lhs, w_ref[...].astype(BF16), preferred_element_type=F32)
    o_ref[...] = res[:8] + res[8:] + b_ref[...]


def _ada_mod(c, ada_w, ada_b):
    depth, d, n = ada_w.shape
    b = c.shape[0]
    c8 = jnp.pad(c, ((0, 8 - b), (0, 0)))
    tn = 1024
    out = pl.pallas_call(
        _ada_kernel,
        grid=(depth, n // tn),
        in_specs=[
            pl.BlockSpec((8, d), lambda i, j: (0, 0)),
            pl.BlockSpec((None, d, tn), lambda i, j: (i, 0, j)),
            pl.BlockSpec((None, 1, tn), lambda i, j: (i, 0, j)),
        ],
        out_specs=pl.BlockSpec((None, 8, tn), lambda i, j: (i, 0, j)),
        out_shape=jax.ShapeDtypeStruct((depth, 8, n), F32),
        compiler_params=_cparams(("arbitrary", "arbitrary")),
        name="ada_mod",
    )(c8, ada_w, ada_b.reshape(depth, 1, n))
    return out[:, :b].reshape(depth, b * 6, 1, d)


def _normmod_kernel(x_ref, g_ref, sh_ref, sc_ref, o_ref):
    o_ref[...] = _norm_mod(x_ref[...], g_ref[...], sh_ref[...], sc_ref[...]).astype(o_ref.dtype)


def _normmod(x2, g, mod, seq, k_shift, k_scale, tm=512):
    t, d = x2.shape
    per = seq // tm
    return pl.pallas_call(
        _normmod_kernel,
        grid=(t // tm,),
        in_specs=[
            pl.BlockSpec((tm, d), lambda i: (i, 0)),
            pl.BlockSpec((1, d), lambda i: (0, 0)),
            pl.BlockSpec((None, 1, d), lambda i: ((i // per) * 6 + k_shift, 0, 0)),
            pl.BlockSpec((None, 1, d), lambda i: ((i // per) * 6 + k_scale, 0, 0)),
        ],
        out_specs=pl.BlockSpec((tm, d), lambda i: (i, 0)),
        out_shape=jax.ShapeDtypeStruct((t, d), BF16),
        compiler_params=_cparams(("arbitrary",)),
        name="normmod",
    )(x2, g.reshape(1, d), mod, mod)


def _cast_weight_once(w_ref, wb_ref):
    @pl.when(pl.program_id(1) == 0)
    def _():
        wb_ref[...] = w_ref[...].astype(BF16)


def _dsa_qkv_kernel(a_ref, w_ref, qg_ref, kg_ref, o_ref, wb_ref):
    _cast_weight_once(w_ref, wb_ref)
    which = pl.program_id(0) % 3
    acc = jnp.dot(a_ref[...], wb_ref[...], preferred_element_type=F32)

    @pl.when(which == 2)
    def _():
        o_ref[...] = acc.astype(o_ref.dtype)

    @pl.when(which != 2)
    def _():
        q_scale = 1.0 / math.sqrt(DSA_HEAD_DIM)
        gain = jnp.where(which == 0, qg_ref[...] * q_scale, kg_ref[...])
        for h in range(DSA_HEADS):
            blk = acc[:, h * LANES:(h + 1) * LANES]
            ms = jnp.mean(blk * blk, axis=-1, keepdims=True)
            o_ref[:, h * LANES:(h + 1) * LANES] = (blk * lax.rsqrt(ms + EPS) * gain).astype(o_ref.dtype)


def _dsa_qkv(h, w_in, q_gain, k_gain, tm=512, tn=DSA_WIDTH):
    t, d = h.shape
    n = w_in.shape[1]
    return pl.pallas_call(
        _dsa_qkv_kernel,
        grid=(n // tn, t // tm),
        in_specs=[
            pl.BlockSpec((tm, d), lambda j, i: (i, 0)),
            pl.BlockSpec((d, tn), lambda j, i: (0, j)),
            pl.BlockSpec((1, LANES), lambda j, i: (0, 0)),
            pl.BlockSpec((1, LANES), lambda j, i: (0, 0)),
        ],
        out_specs=pl.BlockSpec((tm, tn), lambda j, i: (i, j)),
        out_shape=jax.ShapeDtypeStruct((t, n), BF16),
        scratch_shapes=[pltpu.VMEM((d, tn), BF16)],
        compiler_params=_cparams(("arbitrary", "arbitrary")),
        name="dsa_qkv",
    )(h, w_in, q_gain.reshape(1, LANES), k_gain.reshape(1, LANES))


def _resid_mm_kernel(a_ref, w_ref, x_ref, gate_ref, o_ref, wb_ref):
    _cast_weight_once(w_ref, wb_ref)
    y = jnp.dot(a_ref[...], wb_ref[...], preferred_element_type=F32)
    o_ref[...] = x_ref[...] + gate_ref[...] * y


def _resid_mm(a, w, x2, mod, seq, k_gate, tm=512, tn=1024):
    t, k = a.shape
    n = w.shape[1]
    per = seq // tm
    return pl.pallas_call(
        _resid_mm_kernel,
        grid=(n // tn, t // tm),
        in_specs=[
            pl.BlockSpec((tm, k), lambda j, i: (i, 0)),
            pl.BlockSpec((k, tn), lambda j, i: (0, j)),
            pl.BlockSpec((tm, tn), lambda j, i: (i, j)),
            pl.BlockSpec((None, 1, tn), lambda j, i: ((i // per) * 6 + k_gate, 0, j)),
        ],
        out_specs=pl.BlockSpec((tm, tn), lambda j, i: (i, j)),
        out_shape=jax.ShapeDtypeStruct((t, n), F32),
        scratch_shapes=[pltpu.VMEM((k, tn), BF16)],
        compiler_params=_cparams(("arbitrary", "arbitrary")),
        name="resid_mm",
    )(a, w, x2, mod)


def _alibi_slope(head_slot, group):
    n = len(DIL_PAIRS) * DSA_HEADS
    return 2.0 ** (-8.0 * (head_slot * len(DIL_PAIRS) + group + 1.0) / n)


def _dsa_attn_kernel(*refs, group, dilation, has_prev, merge):
    refs = list(refs)
    q_ref, kc_ref, vc_ref = refs[:3]
    pos = 3
    if has_prev:
        kp_ref, vp_ref = refs[pos:pos + 2]
        pos += 2
    if merge:
        other = refs[pos:pos + 4]
        pos += 4
        o_ref = refs[pos]
    else:
        o_ref, lse_ref = refs[pos:pos + 2]

    n = pl.program_id(2)
    qi = lax.broadcasted_iota(jnp.int32, (BAND, BAND), 0)
    kj = lax.broadcasted_iota(jnp.int32, (BAND, BAND), 1)
    dist_c = ((qi - kj) * dilation).astype(F32)
    mask_c = kj <= qi
    if has_prev:
        dist_p = dist_c + float(BAND * dilation)
        mask_p = jnp.logical_and(kj >= qi, n > 0)
    lane = lax.broadcasted_iota(jnp.int32, (BAND, LANES), 1)
    nt = (((1,), (1,)), ((), ()))
    lse_tile = jnp.zeros((BAND, LANES), F32)

    for h in range(DSA_HEADS):
        hs = slice(h * LANES, (h + 1) * LANES)
        slope = _alibi_slope(h, group)
        q = q_ref[:, hs]
        s_c = lax.dot_general(q, kc_ref[:, hs], nt, preferred_element_type=F32)
        s_c = jnp.where(mask_c, s_c - slope * dist_c, NEG_BIG)
        m = jnp.max(s_c, axis=-1, keepdims=True)
        if has_prev:
            s_p = lax.dot_general(q, kp_ref[:, hs], nt, preferred_element_type=F32)
            s_p = jnp.where(mask_p, s_p - slope * dist_p, NEG_BIG)
            m = jnp.maximum(m, jnp.max(s_p, axis=-1, keepdims=True))
        p_c = jnp.exp(s_c - m)
        l = jnp.sum(p_c, axis=-1, keepdims=True)
        acc = jnp.dot(p_c.astype(BF16), vc_ref[:, hs], preferred_element_type=F32)
        if has_prev:
            p_p = jnp.exp(s_p - m)
            l = l + jnp.sum(p_p, axis=-1, keepdims=True)
            acc = acc + jnp.dot(p_p.astype(BF16), vp_ref[:, hs], preferred_element_type=F32)
        o = acc / l
        lse = m + jnp.log(l)
        if merge:
            lses = [lse] + [other[2 * g + 1][:, h * 16:h * 16 + 1] for g in range(2)]
            outs = [o] + [other[2 * g][:, hs].astype(F32) for g in range(2)]
            top = jnp.maximum(jnp.maximum(lses[0], lses[1]), lses[2])
            es = [jnp.exp(x - top) for x in lses]
            den = es[0] + es[1] + es[2]
            o = (es[0] * outs[0] + es[1] * outs[1] + es[2] * outs[2]) / den
        else:
            in_head = jnp.logical_and(lane >= h * 16, lane < (h + 1) * 16)
            lse_tile = jnp.where(in_head, lse, lse_tile)
        o_ref[:, hs] = o.astype(o_ref.dtype)
    if not merge:
        lse_ref[...] = lse_tile


def _dsa_attn(qkv3, group, others=None):
    b, s, ncol = qkv3.shape
    window, d = DIL_PAIRS[group]
    sub_len = s // d
    nb = sub_len // BAND
    has_prev = nb > 1
    merge = others is not None
    wcols = ncol // DSA_WIDTH
    qkv_v = qkv3.reshape(b, sub_len, d * ncol)

    def col(which):
        return lambda bi, r, n: (bi, n, r * wcols + group * 3 + which)

    def col_prev(which):
        return lambda bi, r, n: (bi, jnp.maximum(n - 1, 0), r * wcols + group * 3 + which)

    blk = (None, BAND, DSA_WIDTH)
    in_specs = [pl.BlockSpec(blk, col(0)), pl.BlockSpec(blk, col(1)), pl.BlockSpec(blk, col(2))]
    args = [qkv_v, qkv_v, qkv_v]
    if has_prev:
        in_specs += [pl.BlockSpec(blk, col_prev(1)), pl.BlockSpec(blk, col_prev(2))]
        args += [qkv_v, qkv_v]
    nat = lambda bi, r, n: (bi, n, r)
    if merge:
        for o_g, lse_g in others:
            in_specs += [pl.BlockSpec(blk, nat), pl.BlockSpec((None, BAND, LANES), nat)]
            args += [o_g.reshape(b, sub_len, d * DSA_WIDTH), lse_g.reshape(b, sub_len, d * LANES)]
    o_shape = jax.ShapeDtypeStruct((b, sub_len, d * DSA_WIDTH), BF16)
    o_spec = pl.BlockSpec(blk, nat)
    if merge:
        out_shape, out_specs = o_shape, o_spec
    else:
        out_shape = (o_shape, jax.ShapeDtypeStruct((b, sub_len, d * LANES), F32))
        out_specs = (o_spec, pl.BlockSpec((None, BAND, LANES), nat))
    res = pl.pallas_call(
        functools.partial(_dsa_attn_kernel, group=group, dilation=d, has_prev=has_prev, merge=merge),
        grid=(b, d, nb),
        in_specs=in_specs,
        out_specs=out_specs,
        out_shape=out_shape,
        compiler_params=_cparams(("arbitrary", "arbitrary", "arbitrary")),
        name=f"dsa_attn_g{group}",
    )(*args)
    if merge:
        return res.reshape(b, s, DSA_WIDTH)
    return res[0].reshape(b, s, DSA_WIDTH), res[1].reshape(b, s, LANES)


def _rope_tables(seq):
    half = MLA_ROPE // 2
    inv = ROPE_THETA ** (-jnp.arange(half, dtype=F32) / half)
    ang = jnp.arange(seq, dtype=F32)[:, None] * inv[None, :]
    cos, sin = jnp.cos(ang), jnp.sin(ang)
    z = jnp.zeros((seq, LANES - MLA_ROPE), F32)
    zh = jnp.zeros((seq, half), F32)
    cos_t = jnp.concatenate([cos, cos, z], axis=1)
    sin_a = jnp.concatenate([zh, sin, z], axis=1)
    sin_b = jnp.concatenate([-sin, zh, z], axis=1)
    return jnp.concatenate([cos_t, sin_a, sin_b], axis=1)


def _rope_lanes(x, tab):
    half = MLA_ROPE // 2
    cos_t = tab[:, 0:LANES]
    sin_a = tab[:, LANES:2 * LANES]
    sin_b = tab[:, 2 * LANES:3 * LANES]
    return (x * cos_t + pltpu.roll(x, half, 1) * sin_a
            + pltpu.roll(x, LANES - half, 1) * sin_b)


def _mla_in_kernel(a_ref, w_ref, cqg_ref, ckvg_ref, cq_ref, ckv_ref, kpe_ref, wb_ref):
    n = w_ref.shape[1]

    @pl.when(pl.program_id(0) == 0)
    def _():
        wb_ref[...] = jnp.zeros(wb_ref.shape, BF16)
        wb_ref[:, :n] = w_ref[...].astype(BF16)

    acc = jnp.dot(a_ref[...], wb_ref[...], preferred_element_type=F32)
    cq = acc[:, :MLA_Q_LORA]
    cq_ref[...] = (cq * lax.rsqrt(jnp.mean(cq * cq, axis=-1, keepdims=True) + EPS)
                   * cqg_ref[...]).astype(BF16)
    ckv = acc[:, MLA_Q_LORA:MLA_Q_LORA + MLA_KV_LORA]
    ckv_ref[...] = (ckv * lax.rsqrt(jnp.mean(ckv * ckv, axis=-1, keepdims=True) + EPS)
                    * ckvg_ref[...]).astype(BF16)
    kpe_ref[...] = acc[:, MLA_Q_LORA + MLA_KV_LORA:]


def _mla_in(h, w_in, cq_gain, ckv_gain, tm=512):
    t, d = h.shape
    n = w_in.shape[1]
    n_pad = MLA_Q_LORA + MLA_KV_LORA + LANES
    return pl.pallas_call(
        _mla_in_kernel,
        grid=(t // tm,),
        in_specs=[
            pl.BlockSpec((tm, d), lambda i: (i, 0)),
            pl.BlockSpec((d, n), lambda i: (0, 0)),
            pl.BlockSpec((1, MLA_Q_LORA), lambda i: (0, 0)),
            pl.BlockSpec((1, MLA_KV_LORA), lambda i: (0, 0)),
        ],
        out_specs=(
            pl.BlockSpec((tm, MLA_Q_LORA), lambda i: (i, 0)),
            pl.BlockSpec((tm, MLA_KV_LORA), lambda i: (i, 0)),
            pl.BlockSpec((tm, LANES), lambda i: (i, 0)),
        ),
        out_shape=(
            jax.ShapeDtypeStruct((t, MLA_Q_LORA), BF16),
            jax.ShapeDtypeStruct((t, MLA_KV_LORA), BF16),
            jax.ShapeDtypeStruct((t, LANES), F32),
        ),
        scratch_shapes=[pltpu.VMEM((d, n_pad), BF16)],
        compiler_params=_cparams(("arbitrary",)),
        name="mla_in",
    )(h, w_in, cq_gain.reshape(1, -1), ckv_gain.reshape(1, -1))


def _mla_q_kernel(a_ref, w_ref, g0_ref, g1_ref, tab_ref, o_ref, wb_ref):
    @pl.when(pl.program_id(0) == 0)
    def _():
        wb_ref[...] = w_ref[...].astype(BF16)

    acc = jnp.dot(a_ref[...], wb_ref[...], preferred_element_type=F32)
    tab = tab_ref[...]
    scale = 1.0 / math.sqrt(MLA_QK)
    g0 = g0_ref[...] * scale
    g1 = g1_ref[...] * scale
    for h in range(MLA_HEADS):
        base = h * MLA_QK_PAD
        x0 = acc[:, base:base + LANES]
        x1 = acc[:, base + LANES:base + 2 * LANES]
        ss = jnp.sum(x0 * x0, axis=-1, keepdims=True) + jnp.sum(x1 * x1, axis=-1, keepdims=True)
        rs = lax.rsqrt(ss / MLA_QK + EPS)
        o_ref[:, base:base + LANES] = (x0 * rs * g0).astype(BF16)
        o_ref[:, base + LANES:base + 2 * LANES] = _rope_lanes(x1 * rs * g1, tab).astype(BF16)


def _mla_q(cq, w_q_pad, q_gain, tab, seq, tm=512):
    t, k = cq.shape
    n = w_q_pad.shape[1]
    per = seq // tm
    g0 = q_gain[:MLA_NOPE].reshape(1, LANES)
    g1 = jnp.pad(q_gain[MLA_NOPE:], (0, LANES - MLA_ROPE)).reshape(1, LANES)
    return pl.pallas_call(
        _mla_q_kernel,
        grid=(t // tm,),
        in_specs=[
            pl.BlockSpec((tm, k), lambda i: (i, 0)),
            pl.BlockSpec((k, n), lambda i: (0, 0)),
            pl.BlockSpec((1, LANES), lambda i: (0, 0)),
            pl.BlockSpec((1, LANES), lambda i: (0, 0)),
            pl.BlockSpec((tm, 3 * LANES), lambda i: (i % per, 0)),
        ],
        out_specs=pl.BlockSpec((tm, n), lambda i: (i, 0)),
        out_shape=jax.ShapeDtypeStruct((t, n), BF16),
        scratch_shapes=[pltpu.VMEM((k, n), BF16)],
        compiler_params=_cparams(("arbitrary",)),
        name="mla_q_up",
    )(cq, w_q_pad, g0, g1, tab)


def _mla_kv_kernel(a_ref, w_ref, kpe_ref, g0_ref, g1_ref, tab_ref, k_ref, v_ref, wb_ref):
    @pl.when(pl.program_id(0) == 0)
    def _():
        wb_ref[...] = w_ref[...].astype(BF16)

    acc = jnp.dot(a_ref[...], wb_ref[...], preferred_element_type=F32)
    tab = tab_ref[...]
    kpe = kpe_ref[...]
    ss_pe = jnp.sum(kpe * kpe, axis=-1, keepdims=True)
    g0 = g0_ref[...]
    g1 = g1_ref[...]
    for h in range(MLA_HEADS):
        base = h * (MLA_NOPE + MLA_V)
        kn = acc[:, base:base + MLA_NOPE]
        ss = jnp.sum(kn * kn, axis=-1, keepdims=True) + ss_pe
        rs = lax.rsqrt(ss / MLA_QK + EPS)
        kb = h * MLA_QK_PAD
        k_ref[:, kb:kb + LANES] = (kn * rs * g0).astype(BF16)
        k_ref[:, kb + LANES:kb + 2 * LANES] = _rope_lanes(kpe * rs * g1, tab).astype(BF16)
        v_ref[:, h * MLA_V:(h + 1) * MLA_V] = acc[:, base + MLA_NOPE:base + MLA_NOPE + MLA_V].astype(BF16)


def _mla_kv(ckv, w_kv_up, kpe, k_gain, tab, seq, tm=512):
    t, k = ckv.shape
    n = w_kv_up.shape[1]
    per = seq // tm
    g0 = k_gain[:MLA_NOPE].reshape(1, LANES)
    g1 = jnp.pad(k_gain[MLA_NOPE:], (0, LANES - MLA_ROPE)).reshape(1, LANES)
    return pl.pallas_call(
        _mla_kv_kernel,
        grid=(t // tm,),
        in_specs=[
            pl.BlockSpec((tm, k), lambda i: (i, 0)),
            pl.BlockSpec((k, n), lambda i: (0, 0)),
            pl.BlockSpec((tm, LANES), lambda i: (i, 0)),
            pl.BlockSpec((1, LANES), lambda i: (0, 0)),
            pl.BlockSpec((1, LANES), lambda i: (0, 0)),
            pl.BlockSpec((tm, 3 * LANES), lambda i: (i % per, 0)),
        ],
        out_specs=(
            pl.BlockSpec((tm, MLA_HEADS * MLA_QK_PAD), lambda i: (i, 0)),
            pl.BlockSpec((tm, MLA_HEADS * MLA_V), lambda i: (i, 0)),
        ),
        out_shape=(
            jax.ShapeDtypeStruct((t, MLA_HEADS * MLA_QK_PAD), BF16),
            jax.ShapeDtypeStruct((t, MLA_HEADS * MLA_V), BF16),
        ),
        scratch_shapes=[pltpu.VMEM((k, n), BF16)],
        compiler_params=_cparams(("arbitrary",)),
        name="mla_kv_up",
    )(ckv, w_kv_up, kpe, g0, g1, tab)


def _mla_attn_kernel(q_ref, k_ref, v_ref, o_ref, *, tq):
    qi = pl.program_id(2)
    q = q_ref[...]
    nt = (((1,), (1,)), ((), ()))

    def block(j, carry, masked):
        m, l, acc = carry
        start = pl.multiple_of(j * tq, tq)
        k = k_ref[pl.ds(start, tq), :]
        v = v_ref[pl.ds(start, tq), :]
        s = lax.dot_general(q, k, nt, preferred_element_type=F32)
        if masked:
            r = lax.broadcasted_iota(jnp.int32, (tq, tq), 0)
            c = lax.broadcasted_iota(jnp.int32, (tq, tq), 1)
            s = jnp.where(c <= r, s, NEG_BIG)
        m_new = jnp.maximum(m, jnp.max(s, axis=-1, keepdims=True))
        a = jnp.exp(m - m_new)
        p = jnp.exp(s - m_new)
        l = a * l + jnp.sum(p, axis=-1, keepdims=True)
        acc = a * acc + jnp.dot(p.astype(BF16), v, preferred_element_type=F32)
        return m_new, l, acc

    init = (jnp.full((tq, 1), NEG_BIG, F32), jnp.zeros((tq, 1), F32), jnp.zeros((tq, MLA_V), F32))
    carry = lax.fori_loop(0, qi, lambda j, c: block(j, c, False), init)
    m, l, acc = block(qi, carry, True)
    o_ref[...] = (acc / l).astype(o_ref.dtype)


def _mla_attn(q, k, v, tq=256):
    b, s, _ = q.shape
    return pl.pallas_call(
        functools.partial(_mla_attn_kernel, tq=tq),
        grid=(b, MLA_HEADS, s // tq),
        in_specs=[
            pl.BlockSpec((None, tq, MLA_QK_PAD), lambda bi, h, i: (bi, i, h)),
            pl.BlockSpec((None, s, MLA_QK_PAD), lambda bi, h, i: (bi, 0, h)),
            pl.BlockSpec((None, s, MLA_V), lambda bi, h, i: (bi, 0, h)),
        ],
        out_specs=pl.BlockSpec((None, tq, MLA_V), lambda bi, h, i: (bi, i, h)),
        out_shape=jax.ShapeDtypeStruct((b, s, MLA_HEADS * MLA_V), BF16),
        compiler_params=_cparams(("arbitrary", "arbitrary", "arbitrary")),
        name="mla_attn",
    )(q, k, v)


def _router_kernel(x_ref, g_ref, sh_ref, sc_ref, wr_ref, br_ref, h_ref, info_ref, cnt_ref, carry_ref):
    i = pl.program_id(0)
    tm = x_ref.shape[0]

    @pl.when(i == 0)
    def _():
        carry_ref[...] = jnp.zeros(carry_ref.shape, F32)

    h = _norm_mod(x_ref[...], g_ref[...], sh_ref[...], sc_ref[...])
    h_ref[...] = h
    w = wr_ref[...]
    w_hi = w.astype(BF16)
    w_lo = (w - w_hi.astype(F32)).astype(BF16)
    h_hi = h.astype(BF16)
    h_lo = (h - h_hi.astype(F32)).astype(BF16)
    lg = (jnp.dot(h_hi, w_hi, preferred_element_type=F32)
          + jnp.dot(h_lo, w_hi, preferred_element_type=F32)
          + jnp.dot(h_hi, w_lo, preferred_element_type=F32)) + br_ref[...]

    lane = lax.broadcasted_iota(jnp.int32, (tm, LANES), 1).astype(F32)
    no_lane = float(LANES)
    gl = jnp.where(lane < N_GROUPS, lg, NEG_BIG)
    gmax = jnp.max(gl, axis=-1, keepdims=True)
    g_idx = jnp.min(jnp.where(gl == gmax, lane, no_lane), axis=-1, keepdims=True)
    g_p = 1.0 / jnp.sum(jnp.exp(gl - gmax), axis=-1, keepdims=True)
    lo_lane = N_GROUPS + g_idx * EXPERTS_PER_GROUP
    in_grp = jnp.logical_and(lane >= lo_lane, lane < lo_lane + EXPERTS_PER_GROUP)
    ev = jnp.where(in_grp, lg, NEG_BIG)
    v1 = jnp.max(ev, axis=-1, keepdims=True)
    i1 = jnp.min(jnp.where(ev == v1, lane, no_lane), axis=-1, keepdims=True)
    ev2 = jnp.where(lane == i1, NEG_BIG, ev)
    v2 = jnp.max(ev2, axis=-1, keepdims=True)
    i2 = jnp.min(jnp.where(ev2 == v2, lane, no_lane), axis=-1, keepdims=True)
    e2 = jnp.exp(v2 - v1)
    den = 1.0 + e2
    w1 = (1.0 / den) * g_p
    w2 = (e2 / den) * g_p
    id1 = i1 - N_GROUPS
    id2 = i2 - N_GROUPS

    oh1 = lane == id1
    oh2 = lane == id2
    both = jnp.where(jnp.logical_or(oh1, oh2), 1.0, 0.0)
    r = lax.broadcasted_iota(jnp.int32, (tm, tm), 0)
    c = lax.broadcasted_iota(jnp.int32, (tm, tm), 1)
    tril = jnp.where(c < r, 1.0, 0.0).astype(BF16)
    before = jnp.dot(tril, both.astype(BF16), preferred_element_type=F32) + carry_ref[...]
    rank1 = jnp.sum(jnp.where(oh1, before, 0.0), axis=-1, keepdims=True)
    rank2 = jnp.sum(jnp.where(oh2, before, 0.0), axis=-1, keepdims=True)
    carry_ref[...] = carry_ref[...] + jnp.sum(both, axis=0, keepdims=True)

    info = jnp.zeros((tm, LANES), F32)
    for col, val in enumerate((id1, id2, rank1, rank2, w1, w2)):
        info = jnp.where(lane == col, val, info)
    info_ref[...] = info
    cnt_ref[...] = carry_ref[...]


def _router(x2, g, mod, seq, wr, br, tm=256):
    t, d = x2.shape
    per = seq // tm
    return pl.pallas_call(
        _router_kernel,
        grid=(t // tm,),
        in_specs=[
            pl.BlockSpec((tm, d), lambda i: (i, 0)),
            pl.BlockSpec((1, d), lambda i: (0, 0)),
            pl.BlockSpec((None, 1, d), lambda i: ((i // per) * 6 + 3, 0, 0)),
            pl.BlockSpec((None, 1, d), lambda i: ((i // per) * 6 + 4, 0, 0)),
            pl.BlockSpec((d, LANES), lambda i: (0, 0)),
            pl.BlockSpec((1, LANES), lambda i: (0, 0)),
        ],
        out_specs=(
            pl.BlockSpec((tm, d), lambda i: (i, 0)),
            pl.BlockSpec((tm, LANES), lambda i: (i, 0)),
            pl.BlockSpec((1, LANES), lambda i: (0, 0)),
        ),
        out_shape=(
            jax.ShapeDtypeStruct((t, d), F32),
            jax.ShapeDtypeStruct((t, LANES), F32),
            jax.ShapeDtypeStruct((1, LANES), F32),
        ),
        scratch_shapes=[pltpu.VMEM((1, LANES), F32)],
        compiler_params=_cparams(("arbitrary",)),
        name="moe_router",
    )(x2, g.reshape(1, d), mod, mod, wr, br)


def _expert_kernel(blk_exp, blk_first, blk_next, n_used, row_tok,
                   h_hbm, wg_hbm, wu_hbm, wd_hbm, ys_ref,
                   xbuf, xsem, sg, su, sd, wsem, wgb, wub, wdb):
    i = pl.program_id(0)
    nu = n_used[0]

    def weight_copies(e):
        return (pltpu.make_async_copy(wg_hbm.at[e], sg, wsem.at[0]),
                pltpu.make_async_copy(wu_hbm.at[e], su, wsem.at[1]),
                pltpu.make_async_copy(wd_hbm.at[e], sd, wsem.at[2]))

    def start_gather(blk, slot):
        base = blk * EXPERT_BLOCK

        def body(r, carry):
            tok = row_tok[base + r]
            pltpu.make_async_copy(h_hbm.at[pl.ds(tok, 1), :],
                                  xbuf.at[slot, pl.ds(r, 1), :], xsem.at[slot]).start()
            return carry

        lax.fori_loop(0, EXPERT_BLOCK, body, 0, unroll=8)

    def wait_gather(slot):
        pltpu.make_async_copy(h_hbm.at[pl.ds(0, EXPERT_BLOCK), :], xbuf.at[slot], xsem.at[slot]).wait()

    @pl.when(jnp.logical_and(i == 0, nu > 0))
    def _():
        for cp in weight_copies(blk_exp[0]):
            cp.start()
        start_gather(0, 0)

    @pl.when(i < nu)
    def _():
        slot = i % 2

        @pl.when(i + 1 < nu)
        def _():
            start_gather(i + 1, 1 - slot)

        @pl.when(blk_first[i] == 1)
        def _():
            nxt = blk_next[i]
            cps = weight_copies(blk_exp[i])
            nxt_cps = weight_copies(jnp.maximum(nxt, 0))
            for cp, ncp, stage, dst in zip(cps, nxt_cps, (sg, su, sd), (wgb, wub, wdb)):
                cp.wait()
                dst[...] = stage[...].astype(BF16)

                @pl.when(nxt >= 0)
                def _():
                    ncp.start()

        wait_gather(slot)
        x = xbuf[slot].astype(BF16)
        g = jnp.dot(x, wgb[...], preferred_element_type=F32)
        u = jnp.dot(x, wub[...], preferred_element_type=F32)
        hid = (_silu(g) * u).astype(BF16)
        ys_ref[...] = jnp.dot(hid, wdb[...], preferred_element_type=F32)

    @pl.when(i >= nu)
    def _():
        ys_ref[...] = jnp.zeros(ys_ref.shape, ys_ref.dtype)


def _expert_ffn(h2, w_gate, w_up, w_down, blk_exp, blk_first, blk_next, n_used, row_tok, n_blocks):
    t, d = h2.shape
    f = w_gate.shape[2]
    grid_spec = pltpu.PrefetchScalarGridSpec(
        num_scalar_prefetch=5,
        grid=(n_blocks,),
        in_specs=[pl.BlockSpec(memory_space=pl.ANY)] * 4,
        out_specs=pl.BlockSpec((EXPERT_BLOCK, d), lambda i, *_: (i, 0)),
        scratch_shapes=[
            pltpu.VMEM((2, EXPERT_BLOCK, d), F32),
            pltpu.SemaphoreType.DMA((2,)),
            pltpu.VMEM((d, f), F32),
            pltpu.VMEM((d, f), F32),
            pltpu.VMEM((f, d), F32),
            pltpu.SemaphoreType.DMA((3,)),
            pltpu.VMEM((d, f), BF16),
            pltpu.VMEM((d, f), BF16),
            pltpu.VMEM((f, d), BF16),
        ],
    )
    return pl.pallas_call(
        _expert_kernel,
        grid_spec=grid_spec,
        out_shape=jax.ShapeDtypeStruct((n_blocks * EXPERT_BLOCK, d), F32),
        compiler_params=_cparams(("arbitrary",)),
        name="moe_experts",
    )(blk_exp, blk_first, blk_next, n_used, row_tok, h2, w_gate, w_up, w_down)


def _combine_kernel(pos0, pos1, x_ref, info_ref, gate_ref, ys_hbm, o_ref, buf, sem, *, tm):
    i = pl.program_id(0)
    nsteps = pl.num_programs(0)

    def start_gather(step, slot):
        base = step * tm

        def body(r, carry):
            p0 = pos0[base + r]
            p1 = pos1[base + r]
            pltpu.make_async_copy(ys_hbm.at[pl.ds(p0, 1), :],
                                  buf.at[slot, 0, pl.ds(r, 1), :], sem.at[slot]).start()
            pltpu.make_async_copy(ys_hbm.at[pl.ds(p1, 1), :],
                                  buf.at[slot, 1, pl.ds(r, 1), :], sem.at[slot]).start()
            return carry

        lax.fori_loop(0, tm, body, 0, unroll=8)

    @pl.when(i == 0)
    def _():
        start_gather(0, 0)

    slot = i % 2

    @pl.when(i + 1 < nsteps)
    def _():
        start_gather(i + 1, 1 - slot)

    for k in range(2):
        pltpu.make_async_copy(ys_hbm.at[pl.ds(0, tm), :], buf.at[slot, k], sem.at[slot]).wait()
    info = info_ref[...]
    w0 = info[:, 4:5]
    w1 = info[:, 5:6]
    y = w0 * buf[slot, 0] + w1 * buf[slot, 1]
    o_ref[...] = x_ref[...] + gate_ref[...] * y


def _combine(x2, info, mod, seq, ys, pos0, pos1, tm=128):
    t, d = x2.shape
    per = seq // tm
    grid_spec = pltpu.PrefetchScalarGridSpec(
        num_scalar_prefetch=2,
        grid=(t // tm,),
        in_specs=[
            pl.BlockSpec((tm, d), lambda i, *_: (i, 0)),
            pl.BlockSpec((tm, LANES), lambda i, *_: (i, 0)),
            pl.BlockSpec((None, 1, d), lambda i, *_: ((i // per) * 6 + 5, 0, 0)),
            pl.BlockSpec(memory_space=pl.ANY),
        ],
        out_specs=pl.BlockSpec((tm, d), lambda i, *_: (i, 0)),
        scratch_shapes=[
            pltpu.VMEM((2, 2, tm, d), F32),
            pltpu.SemaphoreType.DMA((2,)),
        ],
    )
    return pl.pallas_call(
        functools.partial(_combine_kernel, tm=tm),
        grid_spec=grid_spec,
        out_shape=jax.ShapeDtypeStruct((t, d), F32),
        compiler_params=_cparams(("arbitrary",)),
        name="moe_combine",
    )(pos0, pos1, x2, info, mod, ys)


def _dispatch_plan(info, counts, n_blocks):
    t = info.shape[0]
    e0 = info[:, 0].astype(jnp.int32)
    e1 = info[:, 1].astype(jnp.int32)
    r0 = info[:, 2].astype(jnp.int32)
    r1 = info[:, 3].astype(jnp.int32)
    cnt = counts[0, :N_EXPERTS].astype(jnp.int32)
    nblk = (cnt + EXPERT_BLOCK - 1) // EXPERT_BLOCK
    bend = jnp.cumsum(nblk)
    bstart = bend - nblk
    n_used = bend[-1:]
    pos0 = bstart[e0] * EXPERT_BLOCK + r0
    pos1 = bstart[e1] * EXPERT_BLOCK + r1
    tok = jnp.arange(t, dtype=jnp.int32)
    row_tok = jnp.zeros((n_blocks * EXPERT_BLOCK,), jnp.int32)
    row_tok = row_tok.at[jnp.concatenate([pos0, pos1])].set(jnp.concatenate([tok, tok]))
    blk = jnp.arange(n_blocks, dtype=jnp.int32)
    blk_exp = jnp.minimum(jnp.searchsorted(bend, blk, side="right"), N_EXPERTS - 1).astype(jnp.int32)
    blk_first = jnp.logical_and(blk == bstart[blk_exp], blk < n_used[0]).astype(jnp.int32)
    ids = jnp.arange(N_EXPERTS, dtype=jnp.int32)
    used_id = jnp.where(nblk > 0, ids, N_EXPERTS)
    nxt = lax.cummin(used_id, axis=0, reverse=True)
    nxt_after = jnp.concatenate([nxt[1:], jnp.full((1,), N_EXPERTS, jnp.int32)])
    nxt_after = jnp.where(nxt_after >= N_EXPERTS, -1, nxt_after)
    blk_next = nxt_after[blk_exp]
    return pos0, pos1, row_tok, blk_exp, blk_first, blk_next, n_used.astype(jnp.int32)


def _hier_moe(x2, norm_g, mod, seq, w_rg, b_rg, w_re, b_re, w_gate, w_up, w_down):
    t, d = x2.shape
    pad = LANES - N_GROUPS - N_EXPERTS
    wr = jnp.concatenate([w_rg, w_re, jnp.zeros((d, pad), F32)], axis=1)
    br = jnp.concatenate([b_rg, b_re, jnp.zeros((pad,), F32)]).reshape(1, LANES)
    h2, info, counts = _router(x2, norm_g, mod, seq, wr, br)
    n_assign = 2 * t
    n_blocks = (n_assign + N_EXPERTS * (EXPERT_BLOCK - 1) + EXPERT_BLOCK - 1) // EXPERT_BLOCK
    pos0, pos1, row_tok, blk_exp, blk_first, blk_next, n_used = _dispatch_plan(info, counts, n_blocks)
    ys = _expert_ffn(h2, w_gate, w_up, w_down, blk_exp, blk_first, blk_next, n_used, row_tok, n_blocks)
    return _combine(x2, info, mod, seq, ys, pos0, pos1)


def kernel(x, c, ada_w, ada_b, norm1_g, norm2_g, dsa_w_in, dsa_q_gain, dsa_k_gain, dsa_w_out, mla_w_in, mla_cq_gain, mla_ckv_gain, mla_w_q_up, mla_w_kv_up, mla_q_gain, mla_k_gain, mla_w_out, router_group_w, router_group_b, router_expert_w, router_expert_b, expert_w_gate, expert_w_up, expert_w_down):
    b, s, d = x.shape
    t = b * s
    mods = _ada_mod(c, ada_w, ada_b)
    x2 = x.reshape(t, d)

    mod = mods[0]
    h = _normmod(x2, norm1_g[0], mod, s, 0, 1)
    qkv = _dsa_qkv(h, dsa_w_in[0], dsa_q_gain[0], dsa_k_gain[0]).reshape(b, s, -1)
    og2 = _dsa_attn(qkv, 2)
    og1 = _dsa_attn(qkv, 1)
    o = _dsa_attn(qkv, 0, others=(og1, og2))
    x2 = _resid_mm(o.reshape(t, DSA_WIDTH), dsa_w_out[0], x2, mod, s, 2)
    x2 = _hier_moe(x2, norm2_g[0], mod, s, router_group_w[0], router_group_b[0],
                   router_expert_w[0], router_expert_b[0],
                   expert_w_gate[0], expert_w_up[0], expert_w_down[0])

    mod = mods[1]
    h = _normmod(x2, norm1_g[1], mod, s, 0, 1)
    cq, ckv, kpe = _mla_in(h, mla_w_in[0], mla_cq_gain[0], mla_ckv_gain[0])
    tab = _rope_tables(s)
    w_q_pad = jnp.pad(mla_w_q_up[0].reshape(MLA_Q_LORA, MLA_HEADS, MLA_QK),
                      ((0, 0), (0, 0), (0, MLA_QK_PAD - MLA_QK))).reshape(MLA_Q_LORA, MLA_HEADS * MLA_QK_PAD)
    q = _mla_q(cq, w_q_pad, mla_q_gain[0], tab, s)
    k, v = _mla_kv(ckv, mla_w_kv_up[0], kpe, mla_k_gain[0], tab, s)
    o = _mla_attn(q.reshape(b, s, -1), k.reshape(b, s, -1), v.reshape(b, s, -1))
    x2 = _resid_mm(o.reshape(t, MLA_HEADS * MLA_V), mla_w_out[0], x2, mod, s, 2)
    x2 = _hier_moe(x2, norm2_g[1], mod, s, router_group_w[1], router_group_b[1],
                   router_expert_w[1], router_expert_b[1],
                   expert_w_gate[1], expert_w_up[1], expert_w_down[1])
    return x2.reshape(b, s, d)
```

```python
import functools
import math

import jax
import jax.numpy as jnp
from jax import lax
from jax.experimental import pallas as pl
from jax.experimental.pallas import tpu as pltpu

F32 = jnp.float32
BF16 = jnp.bfloat16

D_MODEL = 2048
EPS = 1e-6
LANES = 128
NEG_BIG = -1e30

DIL_PAIRS = ((128, 1), (512, 4), (2048, 16))
DSA_HEADS = 8
DSA_HEAD_DIM = 128
DSA_WIDTH = DSA_HEADS * DSA_HEAD_DIM
BAND = 128

MLA_HEADS = 16
MLA_Q_LORA = 512
MLA_KV_LORA = 512
MLA_NOPE = 128
MLA_ROPE = 64
MLA_V = 128
MLA_QK = MLA_NOPE + MLA_ROPE
MLA_QK_PAD = 256
ROPE_THETA = 10000.0

N_GROUPS = 4
EXPERTS_PER_GROUP = 16
N_EXPERTS = N_GROUPS * EXPERTS_PER_GROUP
D_EXPERT = 768
EXPERT_BLOCK = 128

VMEM_LIMIT = 48 * 1024 * 1024


def _cparams(sem, vmem=VMEM_LIMIT):
    return pltpu.CompilerParams(dimension_semantics=sem, vmem_limit_bytes=vmem)


def _silu(x):
    return x * (1.0 / (1.0 + jnp.exp(-x)))


def _norm_mod(x, g, shift, scale):
    ms = jnp.mean(x * x, axis=-1, keepdims=True)
    y = x * lax.rsqrt(ms + EPS) * g
    return y * (1.0 + scale) + shift


def _ada_kernel(c_ref, w_ref, b_ref, o_ref):
    ca = _silu(c_ref[...])
    hi = ca.astype(BF16)
    lo = (ca - hi.astype(F32)).astype(BF16)
    lhs = jnp.concatenate([hi, lo], axis=0)
    res = jnp.dot(lhs, w_ref[...].astype(BF16), preferred_element_type=F32)
    o_ref[...] = res[:8] + res[8:] + b_ref[...]


def _ada_mod(c, ada_w, ada_b):
    depth, d, n = ada_w.shape
    b = c.shape[0]
    c8 = jnp.pad(c, ((0, 8 - b), (0, 0)))
    tn = 1024
    out = pl.pallas_call(
        _ada_kernel,
        grid=(depth, n // tn),
        in_specs=[
            pl.BlockSpec((8, d), lambda i, j: (0, 0)),
            pl.BlockSpec((None, d, tn), lambda i, j: (i, 0, j)),
            pl.BlockSpec((None, 1, tn), lambda i, j: (i, 0, j)),
        ],
        out_specs=pl.BlockSpec((None, 8, tn), lambda i, j: (i, 0, j)),
        out_shape=jax.ShapeDtypeStruct((depth, 8, n), F32),
        compiler_params=_cparams(("arbitrary", "arbitrary")),
        name="ada_mod",
    )(c8, ada_w, ada_b.reshape(depth, 1, n))
    return out[:, :b].reshape(depth, b * 6, 1, d)


def _normmod_kernel(x_ref, g_ref, sh_ref, sc_ref, o_ref):
    o_ref[...] = _norm_mod(x_ref[...], g_ref[...], sh_ref[...], sc_ref[...]).astype(o_ref.dtype)


def _normmod(x2, g, mod, seq, k_shift, k_scale, tm=512):
    t, d = x2.shape
    per = seq // tm
    return pl.pallas_call(
        _normmod_kernel,
        grid=(t // tm,),
        in_specs=[
            pl.BlockSpec((tm, d), lambda i: (i, 0)),
            pl.BlockSpec((1, d), lambda i: (0, 0)),
            pl.BlockSpec((None, 1, d), lambda i: ((i // per) * 6 + k_shift, 0, 0)),
            pl.BlockSpec((None, 1, d), lambda i: ((i // per) * 6 + k_scale, 0, 0)),
        ],
        out_specs=pl.BlockSpec((tm, d), lambda i: (i, 0)),
        out_shape=jax.ShapeDtypeStruct((t, d), BF16),
        compiler_params=_cparams(("arbitrary",)),
        name="normmod",
    )(x2, g.reshape(1, d), mod, mod)


def _cast_weight_once(w_ref, wb_ref):
    @pl.when(pl.program_id(1) == 0)
    def _():
        wb_ref[...] = w_ref[...].astype(BF16)


def _dsa_qkv_kernel(a_ref, w_ref, qg_ref, kg_ref, o_ref, wb_ref, res_ref, *, dilation):
    _cast_weight_once(w_ref, wb_ref)
    which = pl.program_id(0)
    acc = jnp.dot(a_ref[...], wb_ref[...], preferred_element_type=F32)

    @pl.when(which == 2)
    def _():
        for h in range(DSA_HEADS):
            res_ref[h] = acc[:, h * LANES:(h + 1) * LANES]

    @pl.when(which != 2)
    def _():
        q_scale = 1.0 / math.sqrt(DSA_HEAD_DIM)
        gain = jnp.where(which == 0, qg_ref[...] * q_scale, kg_ref[...])
        for h in range(DSA_HEADS):
            blk = acc[:, h * LANES:(h + 1) * LANES]
            ms = jnp.mean(blk * blk, axis=-1, keepdims=True)
            res_ref[h] = blk * lax.rsqrt(ms + EPS) * gain

    rows = res_ref.shape[1] // dilation
    for r in range(dilation):
        for h in range(DSA_HEADS):
            if dilation == 1:
                sub = res_ref[h]
            else:
                sub = res_ref[h, pl.ds(r, rows, stride=dilation), :]
            o_ref[r, :, h * LANES:(h + 1) * LANES] = sub.astype(o_ref.dtype)


def _dsa_qkv(h, w_in, q_gain, k_gain, group, batch, tm=512, tn=DSA_WIDTH):
    t, d = h.shape
    dil = DIL_PAIRS[group][1]
    seq = t // batch
    per = seq // tm
    return pl.pallas_call(
        functools.partial(_dsa_qkv_kernel, dilation=dil),
        grid=(3, t // tm),
        in_specs=[
            pl.BlockSpec((tm, d), lambda j, i: (i, 0)),
            pl.BlockSpec((d, tn), lambda j, i: (0, group * 3 + j)),
            pl.BlockSpec((1, LANES), lambda j, i: (0, 0)),
            pl.BlockSpec((1, LANES), lambda j, i: (0, 0)),
        ],
        out_specs=pl.BlockSpec((None, dil, tm // dil, tn), lambda j, i: (i // per, 0, i % per, j)),
        out_shape=jax.ShapeDtypeStruct((batch, dil, seq // dil, 3 * tn), BF16),
        scratch_shapes=[pltpu.VMEM((d, tn), BF16), pltpu.VMEM((DSA_HEADS, tm, LANES), F32)],
        compiler_params=_cparams(("arbitrary", "arbitrary")),
        name=f"dsa_qkv_g{group}",
    )(h, w_in, q_gain.reshape(1, LANES), k_gain.reshape(1, LANES))


def _resid_mm_kernel(a_ref, w_ref, x_ref, gate_ref, o_ref, wb_ref):
    _cast_weight_once(w_ref, wb_ref)
    y = jnp.dot(a_ref[...], wb_ref[...], preferred_element_type=F32)
    o_ref[...] = x_ref[...] + gate_ref[...] * y


def _resid_mm(a, w, x2, mod, seq, k_gate, tm=512, tn=1024):
    t, k = a.shape
    n = w.shape[1]
    per = seq // tm
    return pl.pallas_call(
        _resid_mm_kernel,
        grid=(n // tn, t // tm),
        in_specs=[
            pl.BlockSpec((tm, k), lambda j, i: (i, 0)),
            pl.BlockSpec((k, tn), lambda j, i: (0, j)),
            pl.BlockSpec((tm, tn), lambda j, i: (i, j)),
            pl.BlockSpec((None, 1, tn), lambda j, i: ((i // per) * 6 + k_gate, 0, j)),
        ],
        out_specs=pl.BlockSpec((tm, tn), lambda j, i: (i, j)),
        out_shape=jax.ShapeDtypeStruct((t, n), F32),
        scratch_shapes=[pltpu.VMEM((k, tn), BF16)],
        compiler_params=_cparams(("arbitrary", "arbitrary")),
        name="resid_mm",
    )(a, w, x2, mod)


def _alibi_slope(head_slot, group):
    n = len(DIL_PAIRS) * DSA_HEADS
    return 2.0 ** (-8.0 * (head_slot * len(DIL_PAIRS) + group + 1.0) / n)


def _dsa_attn_kernel(*refs, group, dilation, has_prev, merge):
    refs = list(refs)
    q_ref, kc_ref, vc_ref = refs[:3]
    pos = 3
    if has_prev:
        kp_ref, vp_ref = refs[pos:pos + 2]
        pos += 2
    if merge:
        other = refs[pos:pos + 4]
        pos += 4
        o_ref = refs[pos]
    else:
        o_ref, lse_ref = refs[pos:pos + 2]

    n = pl.program_id(1)
    r = pl.program_id(2)
    qi = lax.broadcasted_iota(jnp.int32, (BAND, BAND), 0)
    kj = lax.broadcasted_iota(jnp.int32, (BAND, BAND), 1)
    dist_c = ((qi - kj) * dilation).astype(F32)
    mask_c = kj <= qi
    if has_prev:
        dist_p = dist_c + float(BAND * dilation)
        mask_p = jnp.logical_and(kj >= qi, n > 0)
    lane = lax.broadcasted_iota(jnp.int32, (BAND, LANES), 1)
    nt = (((1,), (1,)), ((), ()))
    lse_tile = jnp.zeros((BAND, LANES), F32)

    for h in range(DSA_HEADS):
        hs = slice(h * LANES, (h + 1) * LANES)
        slope = _alibi_slope(h, group)
        q = q_ref[:, hs]
        s_c = lax.dot_general(q, kc_ref[:, hs], nt, preferred_element_type=F32)
        s_c = jnp.where(mask_c, s_c - slope * dist_c, NEG_BIG)
        m = jnp.max(s_c, axis=-1, keepdims=True)
        if has_prev:
            s_p = lax.dot_general(q, kp_ref[:, hs], nt, preferred_element_type=F32)
            s_p = jnp.where(mask_p, s_p - slope * dist_p, NEG_BIG)
            m = jnp.maximum(m, jnp.max(s_p, axis=-1, keepdims=True))
        p_c = jnp.exp(s_c - m)
        l = jnp.sum(p_c, axis=-1, keepdims=True)
        acc = jnp.dot(p_c.astype(BF16), vc_ref[:, hs], preferred_element_type=F32)
        if has_prev:
            p_p = jnp.exp(s_p - m)
            l = l + jnp.sum(p_p, axis=-1, keepdims=True)
            acc = acc + jnp.dot(p_p.astype(BF16), vp_ref[:, hs], preferred_element_type=F32)
        o = acc / l
        lse = m + jnp.log(l)
        if merge:
            lses = [lse] + [other[2 * g + 1][:, h * 16:h * 16 + 1] for g in range(2)]
            outs = [o] + [other[2 * g][h] for g in range(2)]
            top = jnp.maximum(jnp.maximum(lses[0], lses[1]), lses[2])
            es = [jnp.exp(x - top) for x in lses]
            den = es[0] + es[1] + es[2]
            o = (es[0] * outs[0] + es[1] * outs[1] + es[2] * outs[2]) / den
            o_ref[:, hs] = o.astype(o_ref.dtype)
        else:
            in_head = jnp.logical_and(lane >= h * 16, lane < (h + 1) * 16)
            lse_tile = jnp.where(in_head, lse, lse_tile)
            o_ref[h, pl.ds(r, BAND, stride=dilation), :] = o
    if not merge:
        lse_ref[pl.ds(r, BAND, stride=dilation), :] = lse_tile


def _dsa_attn(qkv_g, group, others=None):
    b, d, sub_len, _ = qkv_g.shape
    s = d * sub_len
    nb = sub_len // BAND
    has_prev = nb > 1
    merge = others is not None
    assert not merge or d == 1

    def col(which):
        return lambda bi, n, r: (bi, r, n, which)

    def col_prev(which):
        return lambda bi, n, r: (bi, r, jnp.maximum(n - 1, 0), which)

    blk = (None, None, BAND, DSA_WIDTH)
    in_specs = [pl.BlockSpec(blk, col(0)), pl.BlockSpec(blk, col(1)), pl.BlockSpec(blk, col(2))]
    args = [qkv_g, qkv_g, qkv_g]
    if has_prev:
        in_specs += [pl.BlockSpec(blk, col_prev(1)), pl.BlockSpec(blk, col_prev(2))]
        args += [qkv_g, qkv_g]
    span = BAND * d
    nat = lambda bi, n, r: (bi, n, 0)
    nat_heads = lambda bi, n, r: (bi, 0, n, 0)
    if merge:
        for o_g, lse_g in others:
            in_specs += [pl.BlockSpec((None, DSA_HEADS, BAND, LANES), nat_heads),
                         pl.BlockSpec((None, BAND, LANES), nat)]
            args += [o_g, lse_g]
        out_shape = jax.ShapeDtypeStruct((b, s, DSA_WIDTH), BF16)
        out_specs = pl.BlockSpec((None, BAND, DSA_WIDTH), nat)
    else:
        out_shape = (jax.ShapeDtypeStruct((b, DSA_HEADS, s, LANES), F32), jax.ShapeDtypeStruct((b, s, LANES), F32))
        out_specs = (pl.BlockSpec((None, DSA_HEADS, span, LANES), nat_heads),
                     pl.BlockSpec((None, span, LANES), nat))
    return pl.pallas_call(
        functools.partial(_dsa_attn_kernel, group=group, dilation=d, has_prev=has_prev, merge=merge),
        grid=(b, nb, d),
        in_specs=in_specs,
        out_specs=out_specs,
        out_shape=out_shape,
        compiler_params=_cparams(("arbitrary", "arbitrary", "arbitrary")),
        name=f"dsa_attn_g{group}",
    )(*args)


def _rope_tables(seq):
    half = MLA_ROPE // 2
    inv = ROPE_THETA ** (-jnp.arange(half, dtype=F32) / half)
    ang = jnp.arange(seq, dtype=F32)[:, None] * inv[None, :]
    cos, sin = jnp.cos(ang), jnp.sin(ang)
    z = jnp.zeros((seq, LANES - MLA_ROPE), F32)
    zh = jnp.zeros((seq, half), F32)
    cos_t = jnp.concatenate([cos, cos, z], axis=1)
    sin_a = jnp.concatenate([zh, sin, z], axis=1)
    sin_b = jnp.concatenate([-sin, zh, z], axis=1)
    return jnp.concatenate([cos_t, sin_a, sin_b], axis=1)


def _rope_lanes(x, tab):
    half = MLA_ROPE // 2
    cos_t = tab[:, 0:LANES]
    sin_a = tab[:, LANES:2 * LANES]
    sin_b = tab[:, 2 * LANES:3 * LANES]
    return (x * cos_t + pltpu.roll(x, half, 1) * sin_a
            + pltpu.roll(x, LANES - half, 1) * sin_b)


def _mla_in_kernel(a_ref, w_ref, cqg_ref, ckvg_ref, cq_ref, ckv_ref, kpe_ref, wb_ref):
    n = w_ref.shape[1]

    @pl.when(pl.program_id(0) == 0)
    def _():
        wb_ref[...] = jnp.zeros(wb_ref.shape, BF16)
        wb_ref[:, :n] = w_ref[...].astype(BF16)

    acc = jnp.dot(a_ref[...], wb_ref[...], preferred_element_type=F32)
    cq = acc[:, :MLA_Q_LORA]
    cq_ref[...] = (cq * lax.rsqrt(jnp.mean(cq * cq, axis=-1, keepdims=True) + EPS)
                   * cqg_ref[...]).astype(BF16)
    ckv = acc[:, MLA_Q_LORA:MLA_Q_LORA + MLA_KV_LORA]
    ckv_ref[...] = (ckv * lax.rsqrt(jnp.mean(ckv * ckv, axis=-1, keepdims=True) + EPS)
                    * ckvg_ref[...]).astype(BF16)
    kpe_ref[...] = acc[:, MLA_Q_LORA + MLA_KV_LORA:]


def _mla_in(h, w_in, cq_gain, ckv_gain, tm=512):
    t, d = h.shape
    n = w_in.shape[1]
    n_pad = MLA_Q_LORA + MLA_KV_LORA + LANES
    return pl.pallas_call(
        _mla_in_kernel,
        grid=(t // tm,),
        in_specs=[
            pl.BlockSpec((tm, d), lambda i: (i, 0)),
            pl.BlockSpec((d, n), lambda i: (0, 0)),
            pl.BlockSpec((1, MLA_Q_LORA), lambda i: (0, 0)),
            pl.BlockSpec((1, MLA_KV_LORA), lambda i: (0, 0)),
        ],
        out_specs=(
            pl.BlockSpec((tm, MLA_Q_LORA), lambda i: (i, 0)),
            pl.BlockSpec((tm, MLA_KV_LORA), lambda i: (i, 0)),
            pl.BlockSpec((tm, LANES), lambda i: (i, 0)),
        ),
        out_shape=(
            jax.ShapeDtypeStruct((t, MLA_Q_LORA), BF16),
            jax.ShapeDtypeStruct((t, MLA_KV_LORA), BF16),
            jax.ShapeDtypeStruct((t, LANES), F32),
        ),
        scratch_shapes=[pltpu.VMEM((d, n_pad), BF16)],
        compiler_params=_cparams(("arbitrary",)),
        name="mla_in",
    )(h, w_in, cq_gain.reshape(1, -1), ckv_gain.reshape(1, -1))


def _mla_q_kernel(a_ref, w_ref, g0_ref, g1_ref, tab_ref, o_ref, wb_ref):
    @pl.when(pl.program_id(0) == 0)
    def _():
        wb_ref[...] = w_ref[...].astype(BF16)

    acc = jnp.dot(a_ref[...], wb_ref[...], preferred_element_type=F32)
    tab = tab_ref[...]
    scale = 1.0 / math.sqrt(MLA_QK)
    g0 = g0_ref[...] * scale
    g1 = g1_ref[...] * scale
    for h in range(MLA_HEADS):
        base = h * MLA_QK_PAD
        x0 = acc[:, base:base + LANES]
        x1 = acc[:, base + LANES:base + 2 * LANES]
        ss = jnp.sum(x0 * x0, axis=-1, keepdims=True) + jnp.sum(x1 * x1, axis=-1, keepdims=True)
        rs = lax.rsqrt(ss / MLA_QK + EPS)
        o_ref[:, base:base + LANES] = (x0 * rs * g0).astype(BF16)
        o_ref[:, base + LANES:base + 2 * LANES] = _rope_lanes(x1 * rs * g1, tab).astype(BF16)


def _mla_q(cq, w_q_pad, q_gain, tab, seq, tm=512):
    t, k = cq.shape
    n = w_q_pad.shape[1]
    per = seq // tm
    g0 = q_gain[:MLA_NOPE].reshape(1, LANES)
    g1 = jnp.pad(q_gain[MLA_NOPE:], (0, LANES - MLA_ROPE)).reshape(1, LANES)
    return pl.pallas_call(
        _mla_q_kernel,
        grid=(t // tm,),
        in_specs=[
            pl.BlockSpec((tm, k), lambda i: (i, 0)),
            pl.BlockSpec((k, n), lambda i: (0, 0)),
            pl.BlockSpec((1, LANES), lambda i: (0, 0)),
            pl.BlockSpec((1, LANES), lambda i: (0, 0)),
            pl.BlockSpec((tm, 3 * LANES), lambda i: (i % per, 0)),
        ],
        out_specs=pl.BlockSpec((tm, n), lambda i: (i, 0)),
        out_shape=jax.ShapeDtypeStruct((t, n), BF16),
        scratch_shapes=[pltpu.VMEM((k, n), BF16)],
        compiler_params=_cparams(("arbitrary",)),
        name="mla_q_up",
    )(cq, w_q_pad, g0, g1, tab)


def _mla_kv_kernel(a_ref, w_ref, kpe_ref, g0_ref, g1_ref, tab_ref, k_ref, v_ref, wb_ref):
    @pl.when(pl.program_id(0) == 0)
    def _():
        wb_ref[...] = w_ref[...].astype(BF16)

    acc = jnp.dot(a_ref[...], wb_ref[...], preferred_element_type=F32)
    tab = tab_ref[...]
    kpe = kpe_ref[...]
    ss_pe = jnp.sum(kpe * kpe, axis=-1, keepdims=True)
    g0 = g0_ref[...]
    g1 = g1_ref[...]
    for h in range(MLA_HEADS):
        base = h * (MLA_NOPE + MLA_V)
        kn = acc[:, base:base + MLA_NOPE]
        ss = jnp.sum(kn * kn, axis=-1, keepdims=True) + ss_pe
        rs = lax.rsqrt(ss / MLA_QK + EPS)
        kb = h * MLA_QK_PAD
        k_ref[:, kb:kb + LANES] = (kn * rs * g0).astype(BF16)
        k_ref[:, kb + LANES:kb + 2 * LANES] = _rope_lanes(kpe * rs * g1, tab).astype(BF16)
        vb = h * 2 * MLA_V
        v_ref[:, vb:vb + MLA_V] = acc[:, base + MLA_NOPE:base + MLA_NOPE + MLA_V].astype(BF16)
        v_ref[:, vb + MLA_V:vb + 2 * MLA_V] = jnp.ones((acc.shape[0], MLA_V), BF16)


def _mla_kv(ckv, w_kv_up, kpe, k_gain, tab, seq, tm=512):
    t, k = ckv.shape
    n = w_kv_up.shape[1]
    per = seq // tm
    g0 = k_gain[:MLA_NOPE].reshape(1, LANES)
    g1 = jnp.pad(k_gain[MLA_NOPE:], (0, LANES - MLA_ROPE)).reshape(1, LANES)
    return pl.pallas_call(
        _mla_kv_kernel,
        grid=(t // tm,),
        in_specs=[
            pl.BlockSpec((tm, k), lambda i: (i, 0)),
            pl.BlockSpec((k, n), lambda i: (0, 0)),
            pl.BlockSpec((tm, LANES), lambda i: (i, 0)),
            pl.BlockSpec((1, LANES), lambda i: (0, 0)),
            pl.BlockSpec((1, LANES), lambda i: (0, 0)),
            pl.BlockSpec((tm, 3 * LANES), lambda i: (i % per, 0)),
        ],
        out_specs=(
            pl.BlockSpec((tm, MLA_HEADS * MLA_QK_PAD), lambda i: (i, 0)),
            pl.BlockSpec((tm, MLA_HEADS * 2 * MLA_V), lambda i: (i, 0)),
        ),
        out_shape=(
            jax.ShapeDtypeStruct((t, MLA_HEADS * MLA_QK_PAD), BF16),
            jax.ShapeDtypeStruct((t, MLA_HEADS * 2 * MLA_V), BF16),
        ),
        scratch_shapes=[pltpu.VMEM((k, n), BF16)],
        compiler_params=_cparams(("arbitrary",)),
        name="mla_kv_up",
    )(ckv, w_kv_up, kpe, g0, g1, tab)


def _mla_attn_kernel(q_ref, k_ref, v_ref, o_ref, *, tq):
    seq = q_ref.shape[0]
    nt = (((1,), (1,)), ((), ()))
    r = lax.broadcasted_iota(jnp.int32, (tq, tq), 0)
    c = lax.broadcasted_iota(jnp.int32, (tq, tq), 1)
    causal = c <= r
    for qi in range(seq // tq):
        q = q_ref[qi * tq:(qi + 1) * tq, :]
        scores = []
        for j in range(qi + 1):
            s = lax.dot_general(q, k_ref[j * tq:(j + 1) * tq, :], nt, preferred_element_type=F32)
            if j == qi:
                s = jnp.where(causal, s, NEG_BIG)
            scores.append(s)
        top = scores[0]
        for s in scores[1:]:
            top = jnp.maximum(top, s)
        m = jnp.max(top, axis=-1, keepdims=True)
        acc = None
        for j, s in enumerate(scores):
            p = jnp.exp(s - m).astype(BF16)
            pv = jnp.dot(p, v_ref[j * tq:(j + 1) * tq, :], preferred_element_type=F32)
            acc = pv if acc is None else acc + pv
        o_ref[qi * tq:(qi + 1) * tq, :] = (acc[:, :MLA_V] / acc[:, MLA_V:]).astype(o_ref.dtype)


def _mla_attn(q, k, v, tq=256):
    b, s, _ = q.shape
    return pl.pallas_call(
        functools.partial(_mla_attn_kernel, tq=tq),
        grid=(b, MLA_HEADS),
        in_specs=[
            pl.BlockSpec((None, s, MLA_QK_PAD), lambda bi, h: (bi, 0, h)),
            pl.BlockSpec((None, s, MLA_QK_PAD), lambda bi, h: (bi, 0, h)),
            pl.BlockSpec((None, s, 2 * MLA_V), lambda bi, h: (bi, 0, h)),
        ],
        out_specs=pl.BlockSpec((None, s, MLA_V), lambda bi, h: (bi, 0, h)),
        out_shape=jax.ShapeDtypeStruct((b, s, MLA_HEADS * MLA_V), BF16),
        compiler_params=_cparams(("arbitrary", "arbitrary")),
        name="mla_attn",
    )(q, k, v)


PLAN_COLS = 256
PLAN_EXP, PLAN_FIRST, PLAN_NEXT, PLAN_NUSED, PLAN_NVALID, PLAN_BSTART, PLAN_COUNT = range(7)


def _dispatch_plan_tile(cnt):
    nblk = jnp.floor((cnt + (EXPERT_BLOCK - 1.0)) * (1.0 / EXPERT_BLOCK))
    e_r = lax.broadcasted_iota(jnp.int32, (LANES, LANES), 0)
    e_c = lax.broadcasted_iota(jnp.int32, (LANES, LANES), 1)
    before = jnp.where(e_r < e_c, 1.0, 0.0).astype(BF16)
    bstart = jnp.dot(jnp.broadcast_to(nblk, (8, LANES)).astype(BF16), before,
                     preferred_element_type=F32)[0:1]
    bend = bstart + nblk
    n_used = jnp.max(bend, axis=-1, keepdims=True)
    row = lax.broadcasted_iota(jnp.int32, (PLAN_COLS, LANES), 0).astype(F32)
    lane = lax.broadcasted_iota(jnp.int32, (PLAN_COLS, LANES), 1).astype(F32)
    is_exp = lane < N_EXPERTS
    row1 = row[:, 0:1]
    done = jnp.logical_and(bend <= row, is_exp)
    blk_exp = jnp.minimum(jnp.sum(jnp.where(done, 1.0, 0.0), axis=-1, keepdims=True), N_EXPERTS - 1.0)
    mine = lane == blk_exp
    bstart_of = jnp.sum(jnp.where(mine, bstart, 0.0), axis=-1, keepdims=True)
    cnt_of = jnp.sum(jnp.where(mine, cnt, 0.0), axis=-1, keepdims=True)
    valid = row1 < n_used
    first = jnp.where(jnp.logical_and(valid, row1 == bstart_of), 1.0, 0.0)
    nvalid = jnp.clip(cnt_of - EXPERT_BLOCK * (row1 - bstart_of), 0.0, float(EXPERT_BLOCK))
    nvalid = jnp.where(valid, nvalid, 0.0)
    later = jnp.logical_and(jnp.logical_and(lane > blk_exp, nblk > 0.0), is_exp)
    nxt = jnp.min(jnp.where(later, lane, 999.0), axis=-1, keepdims=True)
    nxt = jnp.where(nxt > 998.0, -1.0, nxt)
    bstart_col = jnp.sum(jnp.where(lane < row, nblk, 0.0), axis=-1, keepdims=True)
    cnt_col = jnp.sum(jnp.where(lane == row, cnt, 0.0), axis=-1, keepdims=True)
    tile = jnp.zeros((PLAN_COLS, LANES), F32)
    cols = {PLAN_EXP: blk_exp, PLAN_FIRST: first, PLAN_NEXT: nxt, PLAN_NUSED: n_used,
            PLAN_NVALID: nvalid, PLAN_BSTART: bstart_col, PLAN_COUNT: cnt_col}
    for k, val in cols.items():
        tile = jnp.where(lane == k, val, tile)
    return tile


def _router_kernel(x_ref, g_ref, sh_ref, sc_ref, wr_ref, br_ref, h_ref, info_ref, idx_ref, plan_ref, carry_ref):
    i = pl.program_id(0)
    tm = x_ref.shape[0]

    @pl.when(i == 0)
    def _():
        carry_ref[...] = jnp.zeros(carry_ref.shape, F32)

    h = _norm_mod(x_ref[...], g_ref[...], sh_ref[...], sc_ref[...])
    h_ref[...] = h
    w = wr_ref[...]
    w_hi = w.astype(BF16)
    w_lo = (w - w_hi.astype(F32)).astype(BF16)
    h_hi = h.astype(BF16)
    h_lo = (h - h_hi.astype(F32)).astype(BF16)
    lg = (jnp.dot(h_hi, w_hi, preferred_element_type=F32)
          + jnp.dot(h_lo, w_hi, preferred_element_type=F32)
          + jnp.dot(h_hi, w_lo, preferred_element_type=F32)) + br_ref[...]

    lane = lax.broadcasted_iota(jnp.int32, (tm, LANES), 1).astype(F32)
    no_lane = float(LANES)
    gl = jnp.where(lane < N_GROUPS, lg, NEG_BIG)
    gmax = jnp.max(gl, axis=-1, keepdims=True)
    g_idx = jnp.min(jnp.where(gl == gmax, lane, no_lane), axis=-1, keepdims=True)
    g_p = 1.0 / jnp.sum(jnp.exp(gl - gmax), axis=-1, keepdims=True)
    lo_lane = N_GROUPS + g_idx * EXPERTS_PER_GROUP
    in_grp = jnp.logical_and(lane >= lo_lane, lane < lo_lane + EXPERTS_PER_GROUP)
    ev = jnp.where(in_grp, lg, NEG_BIG)
    v1 = jnp.max(ev, axis=-1, keepdims=True)
    i1 = jnp.min(jnp.where(ev == v1, lane, no_lane), axis=-1, keepdims=True)
    ev2 = jnp.where(lane == i1, NEG_BIG, ev)
    v2 = jnp.max(ev2, axis=-1, keepdims=True)
    i2 = jnp.min(jnp.where(ev2 == v2, lane, no_lane), axis=-1, keepdims=True)
    e2 = jnp.exp(v2 - v1)
    den = 1.0 + e2
    w1 = (1.0 / den) * g_p
    w2 = (e2 / den) * g_p
    id1 = i1 - N_GROUPS
    id2 = i2 - N_GROUPS

    oh1 = lane == id1
    oh2 = lane == id2
    both = jnp.where(jnp.logical_or(oh1, oh2), 1.0, 0.0)
    r = lax.broadcasted_iota(jnp.int32, (tm, tm), 0)
    c = lax.broadcasted_iota(jnp.int32, (tm, tm), 1)
    tril = jnp.where(c < r, 1.0, 0.0).astype(BF16)
    before = jnp.dot(tril, both.astype(BF16), preferred_element_type=F32) + carry_ref[...]
    rank1 = jnp.sum(jnp.where(oh1, before, 0.0), axis=-1, keepdims=True)
    rank2 = jnp.sum(jnp.where(oh2, before, 0.0), axis=-1, keepdims=True)
    carry_ref[...] = carry_ref[...] + jnp.sum(both, axis=0, keepdims=True)

    info = jnp.zeros((tm, LANES), F32)
    for col, val in enumerate((id1, id2, rank1, rank2, w1, w2)):
        info = jnp.where(lane == col, val, info)
    info_ref[...] = info
    idx_ref[...] = jnp.transpose(info)[0:8].astype(jnp.int32)

    @pl.when(i == pl.num_programs(0) - 1)
    def _():
        plan = _dispatch_plan_tile(carry_ref[...])
        plan_ref[...] = jnp.transpose(plan)[0:8].astype(jnp.int32)


def _router(x2, g, mod, seq, wr, br, tm=256):
    t, d = x2.shape
    per = seq // tm
    return pl.pallas_call(
        _router_kernel,
        grid=(t // tm,),
        in_specs=[
            pl.BlockSpec((tm, d), lambda i: (i, 0)),
            pl.BlockSpec((1, d), lambda i: (0, 0)),
            pl.BlockSpec((None, 1, d), lambda i: ((i // per) * 6 + 3, 0, 0)),
            pl.BlockSpec((None, 1, d), lambda i: ((i // per) * 6 + 4, 0, 0)),
            pl.BlockSpec((d, LANES), lambda i: (0, 0)),
            pl.BlockSpec((1, LANES), lambda i: (0, 0)),
        ],
        out_specs=(
            pl.BlockSpec((tm, d), lambda i: (i, 0)),
            pl.BlockSpec((tm, LANES), lambda i: (i, 0)),
            pl.BlockSpec((8, tm), lambda i: (0, i)),
            pl.BlockSpec((8, PLAN_COLS), lambda i: (0, 0)),
        ),
        out_shape=(
            jax.ShapeDtypeStruct((t, d), F32),
            jax.ShapeDtypeStruct((t, LANES), F32),
            jax.ShapeDtypeStruct((8, t), jnp.int32),
            jax.ShapeDtypeStruct((8, PLAN_COLS), jnp.int32),
        ),
        scratch_shapes=[pltpu.VMEM((1, LANES), F32)],
        compiler_params=_cparams(("arbitrary",)),
        name="moe_router",
    )(x2, g.reshape(1, d), mod, mod, wr, br)


GATHER_GROUP = 8


def _slot_of(plan, idx, n_tok, k, t):
    e = idx[k * n_tok + t]
    return plan[PLAN_BSTART * PLAN_COLS + e] * EXPERT_BLOCK + idx[(2 + k) * n_tok + t]


def _expert_kernel(plan, idx, h_hbm, wg_hbm, wu_hbm, wd_hbm, ys_ref,
                   row_tok, xbuf, xsem, sg, su, sd, wsem, wgb, wub, wdb, *, layer, n_tok):
    i = pl.program_id(0)
    nu = plan[PLAN_NUSED * PLAN_COLS]
    n_rows = row_tok.shape[0]

    def weight_copies(e):
        return (pltpu.make_async_copy(wg_hbm.at[layer, e], sg, wsem.at[0]),
                pltpu.make_async_copy(wu_hbm.at[layer, e], su, wsem.at[1]),
                pltpu.make_async_copy(wd_hbm.at[layer, e], sd, wsem.at[2]))

    def n_groups(blk):
        return (plan[PLAN_NVALID * PLAN_COLS + blk] + GATHER_GROUP - 1) // GATHER_GROUP

    def start_gather(blk, slot):
        base = blk * EXPERT_BLOCK

        def body(g, carry):
            for k in range(GATHER_GROUP):
                r = g * GATHER_GROUP + k
                tok = row_tok[base + r]
                pltpu.make_async_copy(h_hbm.at[pl.ds(tok, 1), :],
                                      xbuf.at[slot, pl.ds(r, 1), :], xsem.at[slot]).start()
            return carry

        lax.fori_loop(0, n_groups(blk), body, 0)

    def wait_gather(blk, slot):
        def body(g, carry):
            pltpu.make_async_copy(h_hbm.at[pl.ds(0, GATHER_GROUP), :],
                                  xbuf.at[slot, pl.ds(0, GATHER_GROUP), :], xsem.at[slot]).wait()
            return carry

        lax.fori_loop(0, n_groups(blk), body, 0)

    @pl.when(i == 0)
    def _():
        for cp in weight_copies(plan[PLAN_EXP * PLAN_COLS]):
            cp.start(priority=1)
        xbuf[...] = jnp.zeros(xbuf.shape, xbuf.dtype)

        def pad_body(e, carry):
            end = plan[PLAN_BSTART * PLAN_COLS + e] * EXPERT_BLOCK + plan[PLAN_COUNT * PLAN_COLS + e]
            for k in range(GATHER_GROUP - 1):
                row_tok[jnp.minimum(end + k, n_rows - 1)] = 0
            return carry

        lax.fori_loop(0, N_EXPERTS, pad_body, 0)

        def fill_body(t, carry):
            row_tok[_slot_of(plan, idx, n_tok, 0, t)] = t
            row_tok[_slot_of(plan, idx, n_tok, 1, t)] = t
            return carry

        lax.fori_loop(0, n_tok, fill_body, 0, unroll=8)
        start_gather(0, 0)

    @pl.when(i < nu)
    def _():
        slot = i % 2

        @pl.when(i + 1 < nu)
        def _():
            start_gather(i + 1, 1 - slot)

        @pl.when(plan[PLAN_FIRST * PLAN_COLS + i] == 1)
        def _():
            nxt = plan[PLAN_NEXT * PLAN_COLS + i]
            cps = weight_copies(plan[PLAN_EXP * PLAN_COLS + i])
            nxt_cps = weight_copies(jnp.maximum(nxt, 0))
            for cp, ncp, stage, dst in zip(cps, nxt_cps, (sg, su, sd), (wgb, wub, wdb)):
                cp.wait()
                dst[...] = stage[...].astype(BF16)

                @pl.when(nxt >= 0)
                def _():
                    ncp.start(priority=1)

        wait_gather(i, slot)
        x = xbuf[slot].astype(BF16)
        g = jnp.dot(x, wgb[...], preferred_element_type=F32)
        u = jnp.dot(x, wub[...], preferred_element_type=F32)
        hid = (_silu(g) * u).astype(BF16)
        ys_ref[...] = jnp.dot(hid, wdb[...], preferred_element_type=F32)

    @pl.when(i >= nu)
    def _():
        ys_ref[...] = jnp.zeros(ys_ref.shape, ys_ref.dtype)


def _expert_ffn(h2, w_gate, w_up, w_down, layer, plan, idx, n_blocks):
    t, d = h2.shape
    f = w_gate.shape[3]
    n_rows = n_blocks * EXPERT_BLOCK
    grid_spec = pltpu.PrefetchScalarGridSpec(
        num_scalar_prefetch=2,
        grid=(n_blocks,),
        in_specs=[pl.BlockSpec(memory_space=pl.ANY)] * 4,
        out_specs=pl.BlockSpec((EXPERT_BLOCK, d), lambda i, *_: (i, 0)),
        scratch_shapes=[
            pltpu.SMEM((n_rows,), jnp.int32),
            pltpu.VMEM((2, EXPERT_BLOCK, d), F32),
            pltpu.SemaphoreType.DMA((2,)),
            pltpu.VMEM((d, f), F32),
            pltpu.VMEM((d, f), F32),
            pltpu.VMEM((f, d), F32),
            pltpu.SemaphoreType.DMA((3,)),
            pltpu.VMEM((d, f), BF16),
            pltpu.VMEM((d, f), BF16),
            pltpu.VMEM((f, d), BF16),
        ],
    )
    return pl.pallas_call(
        functools.partial(_expert_kernel, layer=layer, n_tok=t),
        grid_spec=grid_spec,
        out_shape=jax.ShapeDtypeStruct((n_rows, d), F32),
        compiler_params=_cparams(("arbitrary",)),
        name="moe_experts",
    )(plan, idx, h2, w_gate, w_up, w_down)


def _combine_kernel(plan, idx, x_ref, info_ref, gate_ref, ys_hbm, o_ref, buf, sem, *, tm, n_tok):
    i = pl.program_id(0)
    nsteps = pl.num_programs(0)

    def start_gather(step, slot):
        base = step * tm

        def body(r, carry):
            for k in range(2):
                p = _slot_of(plan, idx, n_tok, k, base + r)
                pltpu.make_async_copy(ys_hbm.at[pl.ds(p, 1), :],
                                      buf.at[slot, k, pl.ds(r, 1), :], sem.at[slot]).start()
            return carry

        lax.fori_loop(0, tm, body, 0, unroll=8)

    @pl.when(i == 0)
    def _():
        start_gather(0, 0)

    slot = i % 2

    @pl.when(i + 1 < nsteps)
    def _():
        start_gather(i + 1, 1 - slot)

    for k in range(2):
        pltpu.make_async_copy(ys_hbm.at[pl.ds(0, tm), :], buf.at[slot, k], sem.at[slot]).wait()
    info = info_ref[...]
    w0 = info[:, 4:5]
    w1 = info[:, 5:6]
    y = w0 * buf[slot, 0] + w1 * buf[slot, 1]
    o_ref[...] = x_ref[...] + gate_ref[...] * y


def _combine(x2, info, mod, seq, ys, plan, idx, tm=128):
    t, d = x2.shape
    per = seq // tm
    grid_spec = pltpu.PrefetchScalarGridSpec(
        num_scalar_prefetch=2,
        grid=(t // tm,),
        in_specs=[
            pl.BlockSpec((tm, d), lambda i, *_: (i, 0)),
            pl.BlockSpec((tm, LANES), lambda i, *_: (i, 0)),
            pl.BlockSpec((None, 1, d), lambda i, *_: ((i // per) * 6 + 5, 0, 0)),
            pl.BlockSpec(memory_space=pl.ANY),
        ],
        out_specs=pl.BlockSpec((tm, d), lambda i, *_: (i, 0)),
        scratch_shapes=[
            pltpu.VMEM((2, 2, tm, d), F32),
            pltpu.SemaphoreType.DMA((2,)),
        ],
    )
    return pl.pallas_call(
        functools.partial(_combine_kernel, tm=tm, n_tok=t),
        grid_spec=grid_spec,
        out_shape=jax.ShapeDtypeStruct((t, d), F32),
        compiler_params=_cparams(("arbitrary",)),
        name="moe_combine",
    )(plan, idx, x2, info, mod, ys)


def _hier_moe(x2, norm_g, mod, seq, w_rg, b_rg, w_re, b_re, w_gate, w_up, w_down, layer):
    t, d = x2.shape
    pad = LANES - N_GROUPS - N_EXPERTS
    wr = jnp.concatenate([w_rg, w_re, jnp.zeros((d, pad), F32)], axis=1)
    br = jnp.concatenate([b_rg, b_re, jnp.zeros((pad,), F32)]).reshape(1, LANES)
    h2, info, idx, plan = _router(x2, norm_g, mod, seq, wr, br)
    n_assign = 2 * t
    n_blocks = (n_assign + N_EXPERTS * (EXPERT_BLOCK - 1) + EXPERT_BLOCK - 1) // EXPERT_BLOCK
    assert n_blocks <= PLAN_COLS
    plan = plan.reshape(-1)
    idx = idx.reshape(-1)
    ys = _expert_ffn(h2, w_gate, w_up, w_down, layer, plan, idx, n_blocks)
    return _combine(x2, info, mod, seq, ys, plan, idx)


def kernel(x, c, ada_w, ada_b, norm1_g, norm2_g, dsa_w_in, dsa_q_gain, dsa_k_gain, dsa_w_out, mla_w_in, mla_cq_gain, mla_ckv_gain, mla_w_q_up, mla_w_kv_up, mla_q_gain, mla_k_gain, mla_w_out, router_group_w, router_group_b, router_expert_w, router_expert_b, expert_w_gate, expert_w_up, expert_w_down):
    b, s, d = x.shape
    t = b * s
    mods = _ada_mod(c, ada_w, ada_b)
    x2 = x.reshape(t, d)

    mod = mods[0]
    h = _normmod(x2, norm1_g[0], mod, s, 0, 1)
    qkv = [_dsa_qkv(h, dsa_w_in[0], dsa_q_gain[0], dsa_k_gain[0], g, b) for g in range(len(DIL_PAIRS))]
    og2 = _dsa_attn(qkv[2], 2)
    og1 = _dsa_attn(qkv[1], 1)
    o = _dsa_attn(qkv[0], 0, others=(og1, og2))
    x2 = _resid_mm(o.reshape(t, DSA_WIDTH), dsa_w_out[0], x2, mod, s, 2)
    x2 = _hier_moe(x2, norm2_g[0], mod, s, router_group_w[0], router_group_b[0],
                   router_expert_w[0], router_expert_b[0],
                   expert_w_gate, expert_w_up, expert_w_down, 0)

    mod = mods[1]
    h = _normmod(x2, norm1_g[1], mod, s, 0, 1)
    cq, ckv, kpe = _mla_in(h, mla_w_in[0], mla_cq_gain[0], mla_ckv_gain[0])
    tab = _rope_tables(s)
    w_q_pad = jnp.pad(mla_w_q_up[0].reshape(MLA_Q_LORA, MLA_HEADS, MLA_QK),
                      ((0, 0), (0, 0), (0, MLA_QK_PAD - MLA_QK))).reshape(MLA_Q_LORA, MLA_HEADS * MLA_QK_PAD)
    q = _mla_q(cq, w_q_pad, mla_q_gain[0], tab, s)
    k, v = _mla_kv(ckv, mla_w_kv_up[0], kpe, mla_k_gain[0], tab, s)
    o = _mla_attn(q.reshape(b, s, -1), k.reshape(b, s, -1), v.reshape(b, s, -1))
    x2 = _resid_mm(o.reshape(t, MLA_HEADS * MLA_V), mla_w_out[0], x2, mod, s, 2)
    x2 = _hier_moe(x2, norm2_g[1], mod, s, router_group_w[1], router_group_b[1],
                   router_expert_w[1], router_expert_b[1],
                   expert_w_gate, expert_w_up, expert_w_down, 1)
    return x2.reshape(b, s, d)
```

```python
import functools
import math

import jax
import jax.numpy as jnp
import numpy as np
from jax import lax
from jax.experimental import pallas as pl
from jax.experimental.pallas import tpu as pltpu

F32 = jnp.float32
BF16 = jnp.bfloat16

D_MODEL = 2048
EPS = 1e-6
LANES = 128
NEG_BIG = -1e30

DIL_PAIRS = ((128, 1), (512, 4), (2048, 16))
DSA_HEADS = 8
DSA_HEAD_DIM = 128
DSA_WIDTH = DSA_HEADS * DSA_HEAD_DIM
BAND = 128

MLA_HEADS = 16
MLA_Q_LORA = 512
MLA_KV_LORA = 512
MLA_NOPE = 128
MLA_ROPE = 64
MLA_V = 128
MLA_QK = MLA_NOPE + MLA_ROPE
MLA_QK_PAD = 256
ROPE_THETA = 10000.0

N_GROUPS = 4
EXPERTS_PER_GROUP = 16
N_EXPERTS = N_GROUPS * EXPERTS_PER_GROUP
D_EXPERT = 768
EXPERT_BLOCK = 128

VMEM_LIMIT = 48 * 1024 * 1024


def _cparams(sem, vmem=VMEM_LIMIT):
    return pltpu.CompilerParams(dimension_semantics=sem, vmem_limit_bytes=vmem)


def _silu(x):
    return x * (1.0 / (1.0 + jnp.exp(-x)))


def _pack_halves(x):
    n = x.shape[1] // 2
    xb = x.astype(BF16).astype(F32)
    lo = lax.bitcast_convert_type(xb[:, :n], jnp.uint32) >> 16
    hi = lax.bitcast_convert_type(xb[:, n:], jnp.uint32) & jnp.uint32(0xFFFF0000)
    return hi | lo


def _unpack_halves(w):
    lo = lax.bitcast_convert_type(w << 16, F32)
    hi = lax.bitcast_convert_type(w & jnp.uint32(0xFFFF0000), F32)
    return lo, hi


def _norm_mod(x, g, shift, scale):
    ms = jnp.mean(x * x, axis=-1, keepdims=True)
    y = x * lax.rsqrt(ms + EPS) * g
    return y * (1.0 + scale) + shift


def _ada_kernel(c_ref, w_ref, b_ref, o_ref):
    ca = _silu(c_ref[...])
    hi = ca.astype(BF16)
    lo = (ca - hi.astype(F32)).astype(BF16)
    lhs = jnp.concatenate([hi, lo], axis=0)
    res = jnp.dot(lhs, w_ref[...].astype(BF16), preferred_element_type=F32)
    o_ref[...] = res[:8] + res[8:] + b_ref[...]


def _ada_mod(c, ada_w, ada_b):
    depth, d, n = ada_w.shape
    b = c.shape[0]
    c8 = jnp.pad(c, ((0, 8 - b), (0, 0)))
    tn = 1024
    out = pl.pallas_call(
        _ada_kernel,
        grid=(depth, n // tn),
        in_specs=[
            pl.BlockSpec((8, d), lambda i, j: (0, 0)),
            pl.BlockSpec((None, d, tn), lambda i, j: (i, 0, j)),
            pl.BlockSpec((None, 1, tn), lambda i, j: (i, 0, j)),
        ],
        out_specs=pl.BlockSpec((None, 8, tn), lambda i, j: (i, 0, j)),
        out_shape=jax.ShapeDtypeStruct((depth, 8, n), F32),
        compiler_params=_cparams(("arbitrary", "arbitrary")),
        name="ada_mod",
    )(c8, ada_w, ada_b.reshape(depth, 1, n))
    return out[:, :b].reshape(depth, b * 6, 1, d)


def _normmod_kernel(x_ref, g_ref, sh_ref, sc_ref, o_ref):
    o_ref[...] = _norm_mod(x_ref[...], g_ref[...], sh_ref[...], sc_ref[...]).astype(o_ref.dtype)


def _normmod(x2, g, mod, seq, k_shift, k_scale, tm=512):
    t, d = x2.shape
    per = seq // tm
    return pl.pallas_call(
        _normmod_kernel,
        grid=(t // tm,),
        in_specs=[
            pl.BlockSpec((tm, d), lambda i: (i, 0)),
            pl.BlockSpec((1, d), lambda i: (0, 0)),
            pl.BlockSpec((None, 1, d), lambda i: ((i // per) * 6 + k_shift, 0, 0)),
            pl.BlockSpec((None, 1, d), lambda i: ((i // per) * 6 + k_scale, 0, 0)),
        ],
        out_specs=pl.BlockSpec((tm, d), lambda i: (i, 0)),
        out_shape=jax.ShapeDtypeStruct((t, d), BF16),
        compiler_params=_cparams(("arbitrary",)),
        name="normmod",
    )(x2, g.reshape(1, d), mod, mod)


def _cast_weight_once(w_ref, wb_ref):
    @pl.when(pl.program_id(1) == 0)
    def _():
        wb_ref[...] = w_ref[...].astype(BF16)


def _dsa_proj_kernel(a_ref, w_ref, g_ref, o_ref, wb_ref, res_ref, *, dilation, normed, gain_scale):
    @pl.when(pl.program_id(0) == 0)
    def _():
        wb_ref[...] = w_ref[...].astype(BF16)

    acc = jnp.dot(a_ref[...], wb_ref[...], preferred_element_type=F32)
    gain = g_ref[...] * gain_scale
    rows = res_ref.shape[1] // dilation
    for h in range(DSA_HEADS):
        blk = acc[:, h * LANES:(h + 1) * LANES]
        if normed:
            ms = jnp.mean(blk * blk, axis=-1, keepdims=True)
            blk = blk * lax.rsqrt(ms + EPS) * gain
        if dilation == 1:
            o_ref[0, :, h * LANES:(h + 1) * LANES] = blk.astype(o_ref.dtype)
        else:
            res_ref[h] = blk
            for r in range(dilation):
                sub = res_ref[h, pl.ds(r, rows, stride=dilation), :]
                o_ref[r, :, h * LANES:(h + 1) * LANES] = sub.astype(o_ref.dtype)


def _dsa_proj(h, w_in, gain, group, which, batch, tm=512, tn=DSA_WIDTH):
    t, d = h.shape
    dil = DIL_PAIRS[group][1]
    seq = t // batch
    per = seq // tm
    gain_scale = 1.0 / math.sqrt(DSA_HEAD_DIM) if which == 0 else 1.0
    return pl.pallas_call(
        functools.partial(_dsa_proj_kernel, dilation=dil, normed=which != 2, gain_scale=gain_scale),
        grid=(t // tm,),
        in_specs=[
            pl.BlockSpec((tm, d), lambda i: (i, 0)),
            pl.BlockSpec((d, tn), lambda i: (0, group * 3 + which)),
            pl.BlockSpec((1, LANES), lambda i: (0, 0)),
        ],
        out_specs=pl.BlockSpec((None, dil, tm // dil, tn), lambda i: (i // per, 0, i % per, 0)),
        out_shape=jax.ShapeDtypeStruct((batch, dil, seq // dil, tn), BF16),
        scratch_shapes=[pltpu.VMEM((d, tn), BF16), pltpu.VMEM((DSA_HEADS, tm, LANES), F32)],
        compiler_params=_cparams(("arbitrary",)),
        name=f"dsa_proj_g{group}_{'qkv'[which]}",
    )(h, w_in, gain.reshape(1, LANES))


def _resid_mm_kernel(a_ref, w_ref, x_ref, gate_ref, o_ref, wb_ref):
    _cast_weight_once(w_ref, wb_ref)
    y = jnp.dot(a_ref[...], wb_ref[...], preferred_element_type=F32)
    o_ref[...] = x_ref[...] + gate_ref[...] * y


def _resid_mm(a, w, x2, mod, seq, k_gate, tm=512, tn=1024):
    t, k = a.shape
    n = w.shape[1]
    per = seq // tm
    return pl.pallas_call(
        _resid_mm_kernel,
        grid=(n // tn, t // tm),
        in_specs=[
            pl.BlockSpec((tm, k), lambda j, i: (i, 0)),
            pl.BlockSpec((k, tn), lambda j, i: (0, j)),
            pl.BlockSpec((tm, tn), lambda j, i: (i, j)),
            pl.BlockSpec((None, 1, tn), lambda j, i: ((i // per) * 6 + k_gate, 0, j)),
        ],
        out_specs=pl.BlockSpec((tm, tn), lambda j, i: (i, j)),
        out_shape=jax.ShapeDtypeStruct((t, n), F32),
        scratch_shapes=[pltpu.VMEM((k, tn), BF16)],
        compiler_params=_cparams(("arbitrary", "arbitrary")),
        name="resid_mm",
    )(a, w, x2, mod)


def _alibi_slope(head_slot, group):
    n = len(DIL_PAIRS) * DSA_HEADS
    return 2.0 ** (-8.0 * (head_slot * len(DIL_PAIRS) + group + 1.0) / n)


def _dsa_bias_table(group, has_prev):
    window, d = DIL_PAIRS[group]
    steps = window // d
    qi = np.arange(BAND)[:, None]
    if has_prev:
        kj = np.arange(2 * BAND)[None, :]
        delta = qi + BAND - kj
        prev_key = np.broadcast_to(kj < BAND, delta.shape)
    else:
        kj = np.arange(BAND)[None, :]
        delta = qi - kj
        prev_key = np.zeros(delta.shape, bool)
    inside = (delta >= 0) & (delta <= steps)
    tabs = []
    for first in (True, False):
        valid = inside & ~(prev_key & first)
        per_head = [np.where(valid, -_alibi_slope(h, group) * (delta * d), NEG_BIG) for h in range(DSA_HEADS)]
        tabs.append(np.stack(per_head))
    return jnp.asarray(np.stack(tabs), F32)


def _dsa_attn_kernel(*refs, dilation, has_prev, merge):
    refs = list(refs)
    q_ref, kc_ref, vc_ref, bias_ref = refs[:4]
    pos = 4
    if has_prev:
        kp_ref, vp_ref = refs[pos:pos + 2]
        pos += 2
    if merge:
        other = refs[pos:pos + 4]
        pos += 4
        o_ref = refs[pos]
    else:
        o_ref, lse_ref = refs[pos:pos + 2]

    r = pl.program_id(2)
    lane = lax.broadcasted_iota(jnp.int32, (BAND, LANES), 1)
    nt = (((1,), (1,)), ((), ()))
    lse_tile = jnp.zeros((BAND, LANES), F32)
    nk = 2 * BAND if has_prev else BAND
    ones = jnp.ones((nk, LANES), BF16)

    for h in range(DSA_HEADS):
        hs = slice(h * LANES, (h + 1) * LANES)
        q = q_ref[:, hs]
        if has_prev:
            k = jnp.concatenate([kp_ref[:, hs], kc_ref[:, hs]], axis=0)
            v = jnp.concatenate([vp_ref[:, hs], vc_ref[:, hs]], axis=0)
        else:
            k = kc_ref[:, hs]
            v = vc_ref[:, hs]
        s = lax.dot_general(q, k, nt, preferred_element_type=F32) + bias_ref[h]
        m = jnp.max(s, axis=-1, keepdims=True)
        p = jnp.exp(s - m).astype(BF16)
        acc = jnp.dot(p, jnp.concatenate([v, ones], axis=1), preferred_element_type=F32)
        l = acc[:, LANES:]
        o = acc[:, :LANES] / l
        lse = m + jnp.log(l)
        if merge:
            lses = [lse] + [other[2 * g + 1][:, h * 16:h * 16 + 1] for g in range(2)]
            outs = [o] + [other[2 * g][h] for g in range(2)]
            top = jnp.maximum(jnp.maximum(lses[0], lses[1]), lses[2])
            es = [jnp.exp(x - top) for x in lses]
            den = es[0] + es[1] + es[2]
            o = (es[0] * outs[0] + es[1] * outs[1] + es[2] * outs[2]) / den
            o_ref[:, hs] = o.astype(o_ref.dtype)
        else:
            in_head = jnp.logical_and(lane >= h * 16, lane < (h + 1) * 16)
            lse_tile = jnp.where(in_head, lse, lse_tile)
            o_ref[h, pl.ds(r, BAND, stride=dilation), :] = o
    if not merge:
        lse_ref[pl.ds(r, BAND, stride=dilation), :] = lse_tile


def _dsa_attn(q_g, k_g, v_g, group, others=None):
    b, d, sub_len, _ = q_g.shape
    s = d * sub_len
    nb = sub_len // BAND
    has_prev = nb > 1
    merge = others is not None
    assert not merge or d == 1
    bias = _dsa_bias_table(group, has_prev)

    cur = lambda bi, n, r: (bi, r, n, 0)
    prev = lambda bi, n, r: (bi, r, jnp.maximum(n - 1, 0), 0)
    blk = (None, None, BAND, DSA_WIDTH)
    in_specs = [pl.BlockSpec(blk, cur), pl.BlockSpec(blk, cur), pl.BlockSpec(blk, cur),
                pl.BlockSpec((None,) + bias.shape[1:], lambda bi, n, r: (jnp.minimum(n, 1), 0, 0, 0))]
    args = [q_g, k_g, v_g, bias]
    if has_prev:
        in_specs += [pl.BlockSpec(blk, prev), pl.BlockSpec(blk, prev)]
        args += [k_g, v_g]
    span = BAND * d
    nat = lambda bi, n, r: (bi, n, 0)
    nat_heads = lambda bi, n, r: (bi, 0, n, 0)
    if merge:
        for o_g, lse_g in others:
            in_specs += [pl.BlockSpec((None, DSA_HEADS, BAND, LANES), nat_heads),
                         pl.BlockSpec((None, BAND, LANES), nat)]
            args += [o_g, lse_g]
        out_shape = jax.ShapeDtypeStruct((b, s, DSA_WIDTH), BF16)
        out_specs = pl.BlockSpec((None, BAND, DSA_WIDTH), nat)
    else:
        out_shape = (jax.ShapeDtypeStruct((b, DSA_HEADS, s, LANES), F32), jax.ShapeDtypeStruct((b, s, LANES), F32))
        out_specs = (pl.BlockSpec((None, DSA_HEADS, span, LANES), nat_heads),
                     pl.BlockSpec((None, span, LANES), nat))
    return pl.pallas_call(
        functools.partial(_dsa_attn_kernel, dilation=d, has_prev=has_prev, merge=merge),
        grid=(b, nb, d),
        in_specs=in_specs,
        out_specs=out_specs,
        out_shape=out_shape,
        compiler_params=_cparams(("arbitrary", "arbitrary", "arbitrary")),
        name=f"dsa_attn_g{group}",
    )(*args)


def _rope_tables(seq):
    half = MLA_ROPE // 2
    inv = ROPE_THETA ** (-jnp.arange(half, dtype=F32) / half)
    ang = jnp.arange(seq, dtype=F32)[:, None] * inv[None, :]
    cos, sin = jnp.cos(ang), jnp.sin(ang)
    z = jnp.zeros((seq, LANES - MLA_ROPE), F32)
    zh = jnp.zeros((seq, half), F32)
    cos_t = jnp.concatenate([cos, cos, z], axis=1)
    sin_a = jnp.concatenate([zh, sin, z], axis=1)
    sin_b = jnp.concatenate([-sin, zh, z], axis=1)
    return jnp.concatenate([cos_t, sin_a, sin_b], axis=1)


def _rope_lanes(x, tab):
    half = MLA_ROPE // 2
    cos_t = tab[:, 0:LANES]
    sin_a = tab[:, LANES:2 * LANES]
    sin_b = tab[:, 2 * LANES:3 * LANES]
    return (x * cos_t + pltpu.roll(x, half, 1) * sin_a
            + pltpu.roll(x, LANES - half, 1) * sin_b)


def _mla_in_kernel(a_ref, w_ref, cqg_ref, ckvg_ref, cq_ref, ckv_ref, kpe_ref, wb_ref):
    n = w_ref.shape[1]

    @pl.when(pl.program_id(0) == 0)
    def _():
        wb_ref[...] = jnp.zeros(wb_ref.shape, BF16)
        wb_ref[:, :n] = w_ref[...].astype(BF16)

    acc = jnp.dot(a_ref[...], wb_ref[...], preferred_element_type=F32)
    cq = acc[:, :MLA_Q_LORA]
    cq_ref[...] = (cq * lax.rsqrt(jnp.mean(cq * cq, axis=-1, keepdims=True) + EPS)
                   * cqg_ref[...]).astype(BF16)
    ckv = acc[:, MLA_Q_LORA:MLA_Q_LORA + MLA_KV_LORA]
    ckv_ref[...] = (ckv * lax.rsqrt(jnp.mean(ckv * ckv, axis=-1, keepdims=True) + EPS)
                    * ckvg_ref[...]).astype(BF16)
    kpe_ref[...] = acc[:, MLA_Q_LORA + MLA_KV_LORA:]


def _mla_in(h, w_in, cq_gain, ckv_gain, tm=512):
    t, d = h.shape
    n = w_in.shape[1]
    n_pad = MLA_Q_LORA + MLA_KV_LORA + LANES
    return pl.pallas_call(
        _mla_in_kernel,
        grid=(t // tm,),
        in_specs=[
            pl.BlockSpec((tm, d), lambda i: (i, 0)),
            pl.BlockSpec((d, n), lambda i: (0, 0)),
            pl.BlockSpec((1, MLA_Q_LORA), lambda i: (0, 0)),
            pl.BlockSpec((1, MLA_KV_LORA), lambda i: (0, 0)),
        ],
        out_specs=(
            pl.BlockSpec((tm, MLA_Q_LORA), lambda i: (i, 0)),
            pl.BlockSpec((tm, MLA_KV_LORA), lambda i: (i, 0)),
            pl.BlockSpec((tm, LANES), lambda i: (i, 0)),
        ),
        out_shape=(
            jax.ShapeDtypeStruct((t, MLA_Q_LORA), BF16),
            jax.ShapeDtypeStruct((t, MLA_KV_LORA), BF16),
            jax.ShapeDtypeStruct((t, LANES), F32),
        ),
        scratch_shapes=[pltpu.VMEM((d, n_pad), BF16)],
        compiler_params=_cparams(("arbitrary",)),
        name="mla_in",
    )(h, w_in, cq_gain.reshape(1, -1), ckv_gain.reshape(1, -1))


def _mla_q_kernel(a_ref, w_ref, g0_ref, g1_ref, tab_ref, o_ref, wb_ref):
    @pl.when(pl.program_id(0) == 0)
    def _():
        wb_ref[...] = w_ref[...].astype(BF16)

    acc = jnp.dot(a_ref[...], wb_ref[...], preferred_element_type=F32)
    tab = tab_ref[...]
    scale = 1.0 / math.sqrt(MLA_QK)
    g0 = g0_ref[...] * scale
    g1 = g1_ref[...] * scale
    for h in range(MLA_HEADS):
        base = h * MLA_QK_PAD
        x0 = acc[:, base:base + LANES]
        x1 = acc[:, base + LANES:base + 2 * LANES]
        ss = jnp.sum(x0 * x0, axis=-1, keepdims=True) + jnp.sum(x1 * x1, axis=-1, keepdims=True)
        rs = lax.rsqrt(ss / MLA_QK + EPS)
        o_ref[:, base:base + LANES] = (x0 * rs * g0).astype(BF16)
        o_ref[:, base + LANES:base + 2 * LANES] = _rope_lanes(x1 * rs * g1, tab).astype(BF16)


def _mla_q(cq, w_q_pad, q_gain, tab, seq, tm=512):
    t, k = cq.shape
    n = w_q_pad.shape[1]
    per = seq // tm
    g0 = q_gain[:MLA_NOPE].reshape(1, LANES)
    g1 = jnp.pad(q_gain[MLA_NOPE:], (0, LANES - MLA_ROPE)).reshape(1, LANES)
    return pl.pallas_call(
        _mla_q_kernel,
        grid=(t // tm,),
        in_specs=[
            pl.BlockSpec((tm, k), lambda i: (i, 0)),
            pl.BlockSpec((k, n), lambda i: (0, 0)),
            pl.BlockSpec((1, LANES), lambda i: (0, 0)),
            pl.BlockSpec((1, LANES), lambda i: (0, 0)),
            pl.BlockSpec((tm, 3 * LANES), lambda i: (i % per, 0)),
        ],
        out_specs=pl.BlockSpec((tm, n), lambda i: (i, 0)),
        out_shape=jax.ShapeDtypeStruct((t, n), BF16),
        scratch_shapes=[pltpu.VMEM((k, n), BF16)],
        compiler_params=_cparams(("arbitrary",)),
        name="mla_q_up",
    )(cq, w_q_pad, g0, g1, tab)


def _mla_kv_kernel(a_ref, w_ref, kpe_ref, g0_ref, g1_ref, tab_ref, k_ref, v_ref, wb_ref):
    @pl.when(pl.program_id(0) == 0)
    def _():
        wb_ref[...] = w_ref[...].astype(BF16)

    acc = jnp.dot(a_ref[...], wb_ref[...], preferred_element_type=F32)
    tab = tab_ref[...]
    kpe = kpe_ref[...]
    ss_pe = jnp.sum(kpe * kpe, axis=-1, keepdims=True)
    g0 = g0_ref[...]
    g1 = g1_ref[...]
    for h in range(MLA_HEADS):
        base = h * (MLA_NOPE + MLA_V)
        kn = acc[:, base:base + MLA_NOPE]
        ss = jnp.sum(kn * kn, axis=-1, keepdims=True) + ss_pe
        rs = lax.rsqrt(ss / MLA_QK + EPS)
        kb = h * MLA_QK_PAD
        k_ref[:, kb:kb + LANES] = (kn * rs * g0).astype(BF16)
        k_ref[:, kb + LANES:kb + 2 * LANES] = _rope_lanes(kpe * rs * g1, tab).astype(BF16)
        vb = h * 2 * MLA_V
        v_ref[:, vb:vb + MLA_V] = acc[:, base + MLA_NOPE:base + MLA_NOPE + MLA_V].astype(BF16)
        v_ref[:, vb + MLA_V:vb + 2 * MLA_V] = jnp.ones((acc.shape[0], MLA_V), BF16)


def _mla_kv(ckv, w_kv_up, kpe, k_gain, tab, seq, tm=512):
    t, k = ckv.shape
    n = w_kv_up.shape[1]
    per = seq // tm
    g0 = k_gain[:MLA_NOPE].reshape(1, LANES)
    g1 = jnp.pad(k_gain[MLA_NOPE:], (0, LANES - MLA_ROPE)).reshape(1, LANES)
    return pl.pallas_call(
        _mla_kv_kernel,
        grid=(t // tm,),
        in_specs=[
            pl.BlockSpec((tm, k), lambda i: (i, 0)),
            pl.BlockSpec((k, n), lambda i: (0, 0)),
            pl.BlockSpec((tm, LANES), lambda i: (i, 0)),
            pl.BlockSpec((1, LANES), lambda i: (0, 0)),
            pl.BlockSpec((1, LANES), lambda i: (0, 0)),
            pl.BlockSpec((tm, 3 * LANES), lambda i: (i % per, 0)),
        ],
        out_specs=(
            pl.BlockSpec((tm, MLA_HEADS * MLA_QK_PAD), lambda i: (i, 0)),
            pl.BlockSpec((tm, MLA_HEADS * 2 * MLA_V), lambda i: (i, 0)),
        ),
        out_shape=(
            jax.ShapeDtypeStruct((t, MLA_HEADS * MLA_QK_PAD), BF16),
            jax.ShapeDtypeStruct((t, MLA_HEADS * 2 * MLA_V), BF16),
        ),
        scratch_shapes=[pltpu.VMEM((k, n), BF16)],
        compiler_params=_cparams(("arbitrary",)),
        name="mla_kv_up",
    )(ckv, w_kv_up, kpe, g0, g1, tab)


def _mla_attn_kernel(q_ref, k_ref, v_ref, o_ref, *, tq):
    seq = q_ref.shape[0]
    nt = (((1,), (1,)), ((), ()))
    r = lax.broadcasted_iota(jnp.int32, (tq, tq), 0)
    c = lax.broadcasted_iota(jnp.int32, (tq, tq), 1)
    causal = c <= r
    for qi in range(seq // tq):
        q = q_ref[qi * tq:(qi + 1) * tq, :]
        scores = []
        for j in range(qi + 1):
            s = lax.dot_general(q, k_ref[j * tq:(j + 1) * tq, :], nt, preferred_element_type=F32)
            if j == qi:
                s = jnp.where(causal, s, NEG_BIG)
            scores.append(s)
        top = scores[0]
        for s in scores[1:]:
            top = jnp.maximum(top, s)
        m = jnp.max(top, axis=-1, keepdims=True)
        acc = None
        for j, s in enumerate(scores):
            p = jnp.exp(s - m).astype(BF16)
            pv = jnp.dot(p, v_ref[j * tq:(j + 1) * tq, :], preferred_element_type=F32)
            acc = pv if acc is None else acc + pv
        o_ref[qi * tq:(qi + 1) * tq, :] = (acc[:, :MLA_V] / acc[:, MLA_V:]).astype(o_ref.dtype)


def _mla_attn(q, k, v, tq=256):
    b, s, _ = q.shape
    return pl.pallas_call(
        functools.partial(_mla_attn_kernel, tq=tq),
        grid=(b, MLA_HEADS),
        in_specs=[
            pl.BlockSpec((None, s, MLA_QK_PAD), lambda bi, h: (bi, 0, h)),
            pl.BlockSpec((None, s, MLA_QK_PAD), lambda bi, h: (bi, 0, h)),
            pl.BlockSpec((None, s, 2 * MLA_V), lambda bi, h: (bi, 0, h)),
        ],
        out_specs=pl.BlockSpec((None, s, MLA_V), lambda bi, h: (bi, 0, h)),
        out_shape=jax.ShapeDtypeStruct((b, s, MLA_HEADS * MLA_V), BF16),
        compiler_params=_cparams(("arbitrary", "arbitrary")),
        name="mla_attn",
    )(q, k, v)


PLAN_COLS = 256
PLAN_EXP, PLAN_FIRST, PLAN_NEXT, PLAN_NUSED, PLAN_NVALID, PLAN_BSTART, PLAN_COUNT = range(7)


def _dispatch_plan_tile(cnt):
    nblk = jnp.floor((cnt + (EXPERT_BLOCK - 1.0)) * (1.0 / EXPERT_BLOCK))
    e_r = lax.broadcasted_iota(jnp.int32, (LANES, LANES), 0)
    e_c = lax.broadcasted_iota(jnp.int32, (LANES, LANES), 1)
    before = jnp.where(e_r < e_c, 1.0, 0.0).astype(BF16)
    bstart = jnp.dot(jnp.broadcast_to(nblk, (8, LANES)).astype(BF16), before,
                     preferred_element_type=F32)[0:1]
    bend = bstart + nblk
    n_used = jnp.max(bend, axis=-1, keepdims=True)
    row = lax.broadcasted_iota(jnp.int32, (PLAN_COLS, LANES), 0).astype(F32)
    lane = lax.broadcasted_iota(jnp.int32, (PLAN_COLS, LANES), 1).astype(F32)
    is_exp = lane < N_EXPERTS
    row1 = row[:, 0:1]
    done = jnp.logical_and(bend <= row, is_exp)
    blk_exp = jnp.minimum(jnp.sum(jnp.where(done, 1.0, 0.0), axis=-1, keepdims=True), N_EXPERTS - 1.0)
    mine = lane == blk_exp
    bstart_of = jnp.sum(jnp.where(mine, bstart, 0.0), axis=-1, keepdims=True)
    cnt_of = jnp.sum(jnp.where(mine, cnt, 0.0), axis=-1, keepdims=True)
    valid = row1 < n_used
    first = jnp.where(jnp.logical_and(valid, row1 == bstart_of), 1.0, 0.0)
    nvalid = jnp.clip(cnt_of - EXPERT_BLOCK * (row1 - bstart_of), 0.0, float(EXPERT_BLOCK))
    nvalid = jnp.where(valid, nvalid, 0.0)
    later = jnp.logical_and(jnp.logical_and(lane > blk_exp, nblk > 0.0), is_exp)
    nxt = jnp.min(jnp.where(later, lane, 999.0), axis=-1, keepdims=True)
    nxt = jnp.where(nxt > 998.0, -1.0, nxt)
    bstart_col = jnp.sum(jnp.where(lane < row, nblk, 0.0), axis=-1, keepdims=True)
    cnt_col = jnp.sum(jnp.where(lane == row, cnt, 0.0), axis=-1, keepdims=True)
    tile = jnp.zeros((PLAN_COLS, LANES), F32)
    cols = {PLAN_EXP: blk_exp, PLAN_FIRST: first, PLAN_NEXT: nxt, PLAN_NUSED: n_used,
            PLAN_NVALID: nvalid, PLAN_BSTART: bstart_col, PLAN_COUNT: cnt_col}
    for k, val in cols.items():
        tile = jnp.where(lane == k, val, tile)
    return tile


def _router_kernel(x_ref, g_ref, sh_ref, sc_ref, wr_ref, br_ref, h_ref, info_ref, idx_ref, plan_ref, carry_ref):
    i = pl.program_id(0)
    tm = x_ref.shape[0]

    @pl.when(i == 0)
    def _():
        carry_ref[...] = jnp.zeros(carry_ref.shape, F32)

    h = _norm_mod(x_ref[...], g_ref[...], sh_ref[...], sc_ref[...])
    h_ref[...] = _pack_halves(h)
    w = wr_ref[...]
    w_hi = w.astype(BF16)
    w_lo = (w - w_hi.astype(F32)).astype(BF16)
    h_hi = h.astype(BF16)
    h_lo = (h - h_hi.astype(F32)).astype(BF16)
    lg = (jnp.dot(h_hi, w_hi, preferred_element_type=F32)
          + jnp.dot(h_lo, w_hi, preferred_element_type=F32)
          + jnp.dot(h_hi, w_lo, preferred_element_type=F32)) + br_ref[...]

    lane = lax.broadcasted_iota(jnp.int32, (tm, LANES), 1).astype(F32)
    no_lane = float(LANES)
    gl = jnp.where(lane < N_GROUPS, lg, NEG_BIG)
    gmax = jnp.max(gl, axis=-1, keepdims=True)
    g_idx = jnp.min(jnp.where(gl == gmax, lane, no_lane), axis=-1, keepdims=True)
    g_p = 1.0 / jnp.sum(jnp.exp(gl - gmax), axis=-1, keepdims=True)
    lo_lane = N_GROUPS + g_idx * EXPERTS_PER_GROUP
    in_grp = jnp.logical_and(lane >= lo_lane, lane < lo_lane + EXPERTS_PER_GROUP)
    ev = jnp.where(in_grp, lg, NEG_BIG)
    v1 = jnp.max(ev, axis=-1, keepdims=True)
    i1 = jnp.min(jnp.where(ev == v1, lane, no_lane), axis=-1, keepdims=True)
    ev2 = jnp.where(lane == i1, NEG_BIG, ev)
    v2 = jnp.max(ev2, axis=-1, keepdims=True)
    i2 = jnp.min(jnp.where(ev2 == v2, lane, no_lane), axis=-1, keepdims=True)
    e2 = jnp.exp(v2 - v1)
    den = 1.0 + e2
    w1 = (1.0 / den) * g_p
    w2 = (e2 / den) * g_p
    id1 = i1 - N_GROUPS
    id2 = i2 - N_GROUPS

    oh1 = lane == id1
    oh2 = lane == id2
    both = jnp.where(jnp.logical_or(oh1, oh2), 1.0, 0.0)
    r = lax.broadcasted_iota(jnp.int32, (tm, tm), 0)
    c = lax.broadcasted_iota(jnp.int32, (tm, tm), 1)
    tril = jnp.where(c < r, 1.0, 0.0).astype(BF16)
    before = jnp.dot(tril, both.astype(BF16), preferred_element_type=F32) + carry_ref[...]
    rank1 = jnp.sum(jnp.where(oh1, before, 0.0), axis=-1, keepdims=True)
    rank2 = jnp.sum(jnp.where(oh2, before, 0.0), axis=-1, keepdims=True)
    carry_ref[...] = carry_ref[...] + jnp.sum(both, axis=0, keepdims=True)

    info = jnp.zeros((tm, LANES), F32)
    for col, val in enumerate((id1, id2, rank1, rank2, w1, w2)):
        info = jnp.where(lane == col, val, info)
    info_ref[...] = info
    idx_ref[...] = jnp.transpose(info)[0:8].astype(jnp.int32)

    @pl.when(i == pl.num_programs(0) - 1)
    def _():
        plan = _dispatch_plan_tile(carry_ref[...])
        plan_ref[...] = jnp.transpose(plan)[0:8].astype(jnp.int32)


def _router(x2, g, mod, seq, wr, br, tm=256):
    t, d = x2.shape
    per = seq // tm
    return pl.pallas_call(
        _router_kernel,
        grid=(t // tm,),
        in_specs=[
            pl.BlockSpec((tm, d), lambda i: (i, 0)),
            pl.BlockSpec((1, d), lambda i: (0, 0)),
            pl.BlockSpec((None, 1, d), lambda i: ((i // per) * 6 + 3, 0, 0)),
            pl.BlockSpec((None, 1, d), lambda i: ((i // per) * 6 + 4, 0, 0)),
            pl.BlockSpec((d, LANES), lambda i: (0, 0)),
            pl.BlockSpec((1, LANES), lambda i: (0, 0)),
        ],
        out_specs=(
            pl.BlockSpec((tm, d // 2), lambda i: (i, 0)),
            pl.BlockSpec((tm, LANES), lambda i: (i, 0)),
            pl.BlockSpec((8, tm), lambda i: (0, i)),
            pl.BlockSpec((8, PLAN_COLS), lambda i: (0, 0)),
        ),
        out_shape=(
            jax.ShapeDtypeStruct((t, d // 2), jnp.uint32),
            jax.ShapeDtypeStruct((t, LANES), F32),
            jax.ShapeDtypeStruct((8, t), jnp.int32),
            jax.ShapeDtypeStruct((8, PLAN_COLS), jnp.int32),
        ),
        scratch_shapes=[pltpu.VMEM((1, LANES), F32)],
        compiler_params=_cparams(("arbitrary",)),
        name="moe_router",
    )(x2, g.reshape(1, d), mod, mod, wr, br)


GATHER_GROUP = 8


def _slot_of(plan, idx, n_tok, k, t):
    e = idx[k * n_tok + t]
    return plan[PLAN_BSTART * PLAN_COLS + e] * EXPERT_BLOCK + idx[(2 + k) * n_tok + t]


def _expert_kernel(plan, idx, h_hbm, wg_hbm, wu_hbm, wd_hbm, ys_ref,
                   row_tok, xbuf, xsem, sg, su, sd, wsem, wgb, wub, wdb, *, layer, n_tok):
    i = pl.program_id(0)
    nu = plan[PLAN_NUSED * PLAN_COLS]
    n_rows = row_tok.shape[0]

    def weight_copies(e):
        return (pltpu.make_async_copy(wg_hbm.at[layer, e], sg, wsem.at[0]),
                pltpu.make_async_copy(wu_hbm.at[layer, e], su, wsem.at[1]),
                pltpu.make_async_copy(wd_hbm.at[layer, e], sd, wsem.at[2]))

    def n_groups(blk):
        return (plan[PLAN_NVALID * PLAN_COLS + blk] + GATHER_GROUP - 1) // GATHER_GROUP

    def start_gather(blk, slot):
        base = blk * EXPERT_BLOCK

        def body(g, carry):
            for k in range(GATHER_GROUP):
                r = g * GATHER_GROUP + k
                tok = row_tok[base + r]
                pltpu.make_async_copy(h_hbm.at[pl.ds(tok, 1), :],
                                      xbuf.at[slot, pl.ds(r, 1), :], xsem.at[slot]).start()
            return carry

        lax.fori_loop(0, n_groups(blk), body, 0)

    def wait_gather(blk, slot):
        def body(g, carry):
            pltpu.make_async_copy(h_hbm.at[pl.ds(0, GATHER_GROUP), :],
                                  xbuf.at[slot, pl.ds(0, GATHER_GROUP), :], xsem.at[slot]).wait()
            return carry

        lax.fori_loop(0, n_groups(blk), body, 0)

    @pl.when(i == 0)
    def _():
        for cp in weight_copies(plan[PLAN_EXP * PLAN_COLS]):
            cp.start(priority=1)
        xbuf[...] = jnp.zeros(xbuf.shape, xbuf.dtype)

        def pad_body(e, carry):
            end = plan[PLAN_BSTART * PLAN_COLS + e] * EXPERT_BLOCK + plan[PLAN_COUNT * PLAN_COLS + e]
            for k in range(GATHER_GROUP - 1):
                row_tok[jnp.minimum(end + k, n_rows - 1)] = 0
            return carry

        lax.fori_loop(0, N_EXPERTS, pad_body, 0)

        def fill_body(t, carry):
            row_tok[_slot_of(plan, idx, n_tok, 0, t)] = t
            row_tok[_slot_of(plan, idx, n_tok, 1, t)] = t
            return carry

        lax.fori_loop(0, n_tok, fill_body, 0, unroll=8)
        start_gather(0, 0)

    @pl.when(i < nu)
    def _():
        slot = i % 2

        @pl.when(i + 1 < nu)
        def _():
            start_gather(i + 1, 1 - slot)

        @pl.when(plan[PLAN_FIRST * PLAN_COLS + i] == 1)
        def _():
            nxt = plan[PLAN_NEXT * PLAN_COLS + i]
            cps = weight_copies(plan[PLAN_EXP * PLAN_COLS + i])
            nxt_cps = weight_copies(jnp.maximum(nxt, 0))
            for cp, ncp, stage, dst in zip(cps, nxt_cps, (sg, su, sd), (wgb, wub, wdb)):
                cp.wait()
                dst[...] = stage[...].astype(BF16)

                @pl.when(nxt >= 0)
                def _():
                    ncp.start(priority=1)

        wait_gather(i, slot)
        x_lo, x_hi = _unpack_halves(xbuf[slot])
        x_lo = x_lo.astype(BF16)
        x_hi = x_hi.astype(BF16)
        half = wgb.shape[0] // 2
        g = (jnp.dot(x_lo, wgb[:half], preferred_element_type=F32)
             + jnp.dot(x_hi, wgb[half:], preferred_element_type=F32))
        u = (jnp.dot(x_lo, wub[:half], preferred_element_type=F32)
             + jnp.dot(x_hi, wub[half:], preferred_element_type=F32))
        hid = (_silu(g) * u).astype(BF16)
        ys_ref[...] = _pack_halves(jnp.dot(hid, wdb[...], preferred_element_type=F32))

    @pl.when(i >= nu)
    def _():
        ys_ref[...] = jnp.zeros(ys_ref.shape, ys_ref.dtype)


def _expert_ffn(h2, w_gate, w_up, w_down, layer, plan, idx, n_blocks):
    t, dp = h2.shape
    d, f = w_gate.shape[2], w_gate.shape[3]
    n_rows = n_blocks * EXPERT_BLOCK
    grid_spec = pltpu.PrefetchScalarGridSpec(
        num_scalar_prefetch=2,
        grid=(n_blocks,),
        in_specs=[pl.BlockSpec(memory_space=pl.ANY)] * 4,
        out_specs=pl.BlockSpec((EXPERT_BLOCK, dp), lambda i, *_: (i, 0)),
        scratch_shapes=[
            pltpu.SMEM((n_rows,), jnp.int32),
            pltpu.VMEM((2, EXPERT_BLOCK, dp), jnp.uint32),
            pltpu.SemaphoreType.DMA((2,)),
            pltpu.VMEM((d, f), F32),
            pltpu.VMEM((d, f), F32),
            pltpu.VMEM((f, d), F32),
            pltpu.SemaphoreType.DMA((3,)),
            pltpu.VMEM((d, f), BF16),
            pltpu.VMEM((d, f), BF16),
            pltpu.VMEM((f, d), BF16),
        ],
    )
    return pl.pallas_call(
        functools.partial(_expert_kernel, layer=layer, n_tok=t),
        grid_spec=grid_spec,
        out_shape=jax.ShapeDtypeStruct((n_rows, dp), jnp.uint32),
        compiler_params=_cparams(("arbitrary",)),
        name="moe_experts",
    )(plan, idx, h2, w_gate, w_up, w_down)


def _combine_kernel(plan, idx, x_ref, info_ref, gate_ref, ys_hbm, o_ref, buf, sem, *, tm, n_tok):
    i = pl.program_id(0)
    nsteps = pl.num_programs(0)

    def start_gather(step, slot):
        base = step * tm

        def body(r, carry):
            for k in range(2):
                p = _slot_of(plan, idx, n_tok, k, base + r)
                pltpu.make_async_copy(ys_hbm.at[pl.ds(p, 1), :],
                                      buf.at[slot, k, pl.ds(r, 1), :], sem.at[slot]).start()
            return carry

        lax.fori_loop(0, tm, body, 0, unroll=8)

    @pl.when(i == 0)
    def _():
        start_gather(0, 0)

    slot = i % 2

    @pl.when(i + 1 < nsteps)
    def _():
        start_gather(i + 1, 1 - slot)

    for k in range(2):
        pltpu.make_async_copy(ys_hbm.at[pl.ds(0, tm), :], buf.at[slot, k], sem.at[slot]).wait()
    info = info_ref[...]
    w0 = info[:, 4:5]
    w1 = info[:, 5:6]
    a_lo, a_hi = _unpack_halves(buf[slot, 0])
    b_lo, b_hi = _unpack_halves(buf[slot, 1])
    half = a_lo.shape[1]
    o_ref[:, :half] = x_ref[:, :half] + gate_ref[:, :half] * (w0 * a_lo + w1 * b_lo)
    o_ref[:, half:] = x_ref[:, half:] + gate_ref[:, half:] * (w0 * a_hi + w1 * b_hi)


def _combine(x2, info, mod, seq, ys, plan, idx, tm=128):
    t, d = x2.shape
    per = seq // tm
    grid_spec = pltpu.PrefetchScalarGridSpec(
        num_scalar_prefetch=2,
        grid=(t // tm,),
        in_specs=[
            pl.BlockSpec((tm, d), lambda i, *_: (i, 0)),
            pl.BlockSpec((tm, LANES), lambda i, *_: (i, 0)),
            pl.BlockSpec((None, 1, d), lambda i, *_: ((i // per) * 6 + 5, 0, 0)),
            pl.BlockSpec(memory_space=pl.ANY),
        ],
        out_specs=pl.BlockSpec((tm, d), lambda i, *_: (i, 0)),
        scratch_shapes=[
            pltpu.VMEM((2, 2, tm, d // 2), jnp.uint32),
            pltpu.SemaphoreType.DMA((2,)),
        ],
    )
    return pl.pallas_call(
        functools.partial(_combine_kernel, tm=tm, n_tok=t),
        grid_spec=grid_spec,
        out_shape=jax.ShapeDtypeStruct((t, d), F32),
        compiler_params=_cparams(("arbitrary",)),
        name="moe_combine",
    )(plan, idx, x2, info, mod, ys)


def _hier_moe(x2, norm_g, mod, seq, w_rg, b_rg, w_re, b_re, w_gate, w_up, w_down, layer):
    t, d = x2.shape
    pad = LANES - N_GROUPS - N_EXPERTS
    wr = jnp.concatenate([w_rg, w_re, jnp.zeros((d, pad), F32)], axis=1)
    br = jnp.concatenate([b_rg, b_re, jnp.zeros((pad,), F32)]).reshape(1, LANES)
    h2, info, idx, plan = _router(x2, norm_g, mod, seq, wr, br)
    n_assign = 2 * t
    n_blocks = (n_assign + N_EXPERTS * (EXPERT_BLOCK - 1) + EXPERT_BLOCK - 1) // EXPERT_BLOCK
    assert n_blocks <= PLAN_COLS
    plan = plan.reshape(-1)
    idx = idx.reshape(-1)
    ys = _expert_ffn(h2, w_gate, w_up, w_down, layer, plan, idx, n_blocks)
    return _combine(x2, info, mod, seq, ys, plan, idx)


def kernel(x, c, ada_w, ada_b, norm1_g, norm2_g, dsa_w_in, dsa_q_gain, dsa_k_gain, dsa_w_out, mla_w_in, mla_cq_gain, mla_ckv_gain, mla_w_q_up, mla_w_kv_up, mla_q_gain, mla_k_gain, mla_w_out, router_group_w, router_group_b, router_expert_w, router_expert_b, expert_w_gate, expert_w_up, expert_w_down):
    b, s, d = x.shape
    t = b * s
    mods = _ada_mod(c, ada_w, ada_b)
    x2 = x.reshape(t, d)

    mod = mods[0]
    h = _normmod(x2, norm1_g[0], mod, s, 0, 1)
    gains = (dsa_q_gain[0], dsa_k_gain[0], dsa_k_gain[0])
    qkv = [[_dsa_proj(h, dsa_w_in[0], gains[w], g, w, b) for w in range(3)] for g in range(len(DIL_PAIRS))]
    og2 = _dsa_attn(*qkv[2], 2)
    og1 = _dsa_attn(*qkv[1], 1)
    o = _dsa_attn(*qkv[0], 0, others=(og1, og2))
    x2 = _resid_mm(o.reshape(t, DSA_WIDTH), dsa_w_out[0], x2, mod, s, 2)
    x2 = _hier_moe(x2, norm2_g[0], mod, s, router_group_w[0], router_group_b[0],
                   router_expert_w[0], router_expert_b[0],
                   expert_w_gate, expert_w_up, expert_w_down, 0)

    mod = mods[1]
    h = _normmod(x2, norm1_g[1], mod, s, 0, 1)
    cq, ckv, kpe = _mla_in(h, mla_w_in[0], mla_cq_gain[0], mla_ckv_gain[0])
    tab = _rope_tables(s)
    w_q_pad = jnp.pad(mla_w_q_up[0].reshape(MLA_Q_LORA, MLA_HEADS, MLA_QK),
                      ((0, 0), (0, 0), (0, MLA_QK_PAD - MLA_QK))).reshape(MLA_Q_LORA, MLA_HEADS * MLA_QK_PAD)
    q = _mla_q(cq, w_q_pad, mla_q_gain[0], tab, s)
    k, v = _mla_kv(ckv, mla_w_kv_up[0], kpe, mla_k_gain[0], tab, s)
    o = _mla_attn(q.reshape(b, s, -1), k.reshape(b, s, -1), v.reshape(b, s, -1))
    x2 = _resid_mm(o.reshape(t, MLA_HEADS * MLA_V), mla_w_out[0], x2, mod, s, 2)
    x2 = _hier_moe(x2, norm2_g[1], mod, s, router_group_w[1], router_group_b[1],
                   router_expert_w[1], router_expert_b[1],
                   expert_w_gate, expert_w_up, expert_w_down, 1)
    return x2.reshape(b, s, d)
```

```python
import functools
import math

import jax
import jax.numpy as jnp
import numpy as np
from jax import lax
from jax.experimental import pallas as pl
from jax.experimental.pallas import tpu as pltpu

F32 = jnp.float32
BF16 = jnp.bfloat16

D_MODEL = 2048
EPS = 1e-6
LANES = 128
NEG_BIG = -1e30

DIL_PAIRS = ((128, 1), (512, 4), (2048, 16))
DSA_HEADS = 8
DSA_HEAD_DIM = 128
DSA_WIDTH = DSA_HEADS * DSA_HEAD_DIM
BAND = 128

MLA_HEADS = 16
MLA_Q_LORA = 512
MLA_KV_LORA = 512
MLA_NOPE = 128
MLA_ROPE = 64
MLA_V = 128
MLA_QK = MLA_NOPE + MLA_ROPE
MLA_QK_PAD = 256
ROPE_THETA = 10000.0

N_GROUPS = 4
EXPERTS_PER_GROUP = 16
N_EXPERTS = N_GROUPS * EXPERTS_PER_GROUP
D_EXPERT = 768
EXPERT_BLOCK = 128

VMEM_LIMIT = 48 * 1024 * 1024


def _cparams(sem, vmem=VMEM_LIMIT):
    return pltpu.CompilerParams(dimension_semantics=sem, vmem_limit_bytes=vmem)


def _silu(x):
    return x * (1.0 / (1.0 + jnp.exp(-x)))


def _pack_halves(x):
    n = x.shape[1] // 2
    xb = x.astype(BF16).astype(F32)
    lo = lax.bitcast_convert_type(xb[:, :n], jnp.uint32) >> 16
    hi = lax.bitcast_convert_type(xb[:, n:], jnp.uint32) & jnp.uint32(0xFFFF0000)
    return hi | lo


def _unpack_halves(w):
    lo = lax.bitcast_convert_type(w << 16, F32)
    hi = lax.bitcast_convert_type(w & jnp.uint32(0xFFFF0000), F32)
    return lo, hi


ROW_TILE = 8


def _store_row_tiles(ref, words):
    rows = words.shape[0]
    for s in range(ROW_TILE):
        ref[pl.ds(s, rows, stride=ROW_TILE), :] = words[:, s * LANES:(s + 1) * LANES]


def _load_row_tiles(ref, rows):
    return [ref[pl.ds(s, rows, stride=ROW_TILE), :] for s in range(ROW_TILE)]


def _norm_mod(x, g, shift, scale):
    ms = jnp.mean(x * x, axis=-1, keepdims=True)
    y = x * lax.rsqrt(ms + EPS) * g
    return y * (1.0 + scale) + shift


def _ada_kernel(c_ref, w_ref, b_ref, o_ref):
    ca = _silu(c_ref[...])
    hi = ca.astype(BF16)
    lo = (ca - hi.astype(F32)).astype(BF16)
    lhs = jnp.concatenate([hi, lo], axis=0)
    res = jnp.dot(lhs, w_ref[...].astype(BF16), preferred_element_type=F32)
    o_ref[...] = res[:8] + res[8:] + b_ref[...]


def _ada_mod(c, ada_w, ada_b):
    depth, d, n = ada_w.shape
    b = c.shape[0]
    c8 = jnp.pad(c, ((0, 8 - b), (0, 0)))
    tn = 1024
    out = pl.pallas_call(
        _ada_kernel,
        grid=(depth, n // tn),
        in_specs=[
            pl.BlockSpec((8, d), lambda i, j: (0, 0)),
            pl.BlockSpec((None, d, tn), lambda i, j: (i, 0, j)),
            pl.BlockSpec((None, 1, tn), lambda i, j: (i, 0, j)),
        ],
        out_specs=pl.BlockSpec((None, 8, tn), lambda i, j: (i, 0, j)),
        out_shape=jax.ShapeDtypeStruct((depth, 8, n), F32),
        compiler_params=_cparams(("arbitrary", "arbitrary")),
        name="ada_mod",
    )(c8, ada_w, ada_b.reshape(depth, 1, n))
    return out[:, :b].reshape(depth, b * 6, 1, d)


def _normmod_kernel(x_ref, g_ref, sh_ref, sc_ref, o_ref):
    o_ref[...] = _norm_mod(x_ref[...], g_ref[...], sh_ref[...], sc_ref[...]).astype(o_ref.dtype)


def _normmod(x2, g, mod, seq, k_shift, k_scale, tm=512):
    t, d = x2.shape
    per = seq // tm
    return pl.pallas_call(
        _normmod_kernel,
        grid=(t // tm,),
        in_specs=[
            pl.BlockSpec((tm, d), lambda i: (i, 0)),
            pl.BlockSpec((1, d), lambda i: (0, 0)),
            pl.BlockSpec((None, 1, d), lambda i: ((i // per) * 6 + k_shift, 0, 0)),
            pl.BlockSpec((None, 1, d), lambda i: ((i // per) * 6 + k_scale, 0, 0)),
        ],
        out_specs=pl.BlockSpec((tm, d), lambda i: (i, 0)),
        out_shape=jax.ShapeDtypeStruct((t, d), BF16),
        compiler_params=_cparams(("arbitrary",)),
        name="normmod",
    )(x2, g.reshape(1, d), mod, mod)


def _cast_weight_once(w_ref, wb_ref):
    @pl.when(pl.program_id(1) == 0)
    def _():
        wb_ref[...] = w_ref[...].astype(BF16)


def _dsa_proj_kernel(a_ref, w_ref, g_ref, o_ref, wb_ref, res_ref, *, dilation, normed, gain_scale):
    @pl.when(pl.program_id(0) == 0)
    def _():
        wb_ref[...] = w_ref[...].astype(BF16)

    acc = jnp.dot(a_ref[...], wb_ref[...], preferred_element_type=F32)
    gain = g_ref[...] * gain_scale
    rows = res_ref.shape[1] // dilation
    for h in range(DSA_HEADS):
        blk = acc[:, h * LANES:(h + 1) * LANES]
        if normed:
            ms = jnp.mean(blk * blk, axis=-1, keepdims=True)
            blk = blk * lax.rsqrt(ms + EPS) * gain
        if dilation == 1:
            o_ref[0, :, h * LANES:(h + 1) * LANES] = blk.astype(o_ref.dtype)
        else:
            res_ref[h] = blk
            for r in range(dilation):
                sub = res_ref[h, pl.ds(r, rows, stride=dilation), :]
                o_ref[r, :, h * LANES:(h + 1) * LANES] = sub.astype(o_ref.dtype)


def _dsa_proj(h, w_in, gain, group, which, batch, tm=512, tn=DSA_WIDTH):
    t, d = h.shape
    dil = DIL_PAIRS[group][1]
    seq = t // batch
    per = seq // tm
    gain_scale = 1.0 / math.sqrt(DSA_HEAD_DIM) if which == 0 else 1.0
    return pl.pallas_call(
        functools.partial(_dsa_proj_kernel, dilation=dil, normed=which != 2, gain_scale=gain_scale),
        grid=(t // tm,),
        in_specs=[
            pl.BlockSpec((tm, d), lambda i: (i, 0)),
            pl.BlockSpec((d, tn), lambda i: (0, group * 3 + which)),
            pl.BlockSpec((1, LANES), lambda i: (0, 0)),
        ],
        out_specs=pl.BlockSpec((None, dil, tm // dil, tn), lambda i: (i // per, 0, i % per, 0)),
        out_shape=jax.ShapeDtypeStruct((batch, dil, seq // dil, tn), BF16),
        scratch_shapes=[pltpu.VMEM((d, tn), BF16), pltpu.VMEM((DSA_HEADS, tm, LANES), F32)],
        compiler_params=_cparams(("arbitrary",)),
        name=f"dsa_proj_g{group}_{'qkv'[which]}",
    )(h, w_in, gain.reshape(1, LANES))


def _resid_mm_kernel(a_ref, w_ref, x_ref, gate_ref, o_ref, wb_ref):
    _cast_weight_once(w_ref, wb_ref)
    y = jnp.dot(a_ref[...], wb_ref[...], preferred_element_type=F32)
    o_ref[...] = x_ref[...] + gate_ref[...] * y


def _resid_mm(a, w, x2, mod, seq, k_gate, tm=512, tn=1024):
    t, k = a.shape
    n = w.shape[1]
    per = seq // tm
    return pl.pallas_call(
        _resid_mm_kernel,
        grid=(n // tn, t // tm),
        in_specs=[
            pl.BlockSpec((tm, k), lambda j, i: (i, 0)),
            pl.BlockSpec((k, tn), lambda j, i: (0, j)),
            pl.BlockSpec((tm, tn), lambda j, i: (i, j)),
            pl.BlockSpec((None, 1, tn), lambda j, i: ((i // per) * 6 + k_gate, 0, j)),
        ],
        out_specs=pl.BlockSpec((tm, tn), lambda j, i: (i, j)),
        out_shape=jax.ShapeDtypeStruct((t, n), F32),
        scratch_shapes=[pltpu.VMEM((k, tn), BF16)],
        compiler_params=_cparams(("arbitrary", "arbitrary")),
        name="resid_mm",
    )(a, w, x2, mod)


def _alibi_slope(head_slot, group):
    n = len(DIL_PAIRS) * DSA_HEADS
    return 2.0 ** (-8.0 * (head_slot * len(DIL_PAIRS) + group + 1.0) / n)


def _dsa_bias_table(group, has_prev):
    window, d = DIL_PAIRS[group]
    steps = window // d
    qi = np.arange(BAND)[:, None]
    if has_prev:
        kj = np.arange(2 * BAND)[None, :]
        delta = qi + BAND - kj
        prev_key = np.broadcast_to(kj < BAND, delta.shape)
    else:
        kj = np.arange(BAND)[None, :]
        delta = qi - kj
        prev_key = np.zeros(delta.shape, bool)
    inside = (delta >= 0) & (delta <= steps)
    tabs = []
    for first in (True, False):
        valid = inside & ~(prev_key & first)
        per_head = [np.where(valid, -_alibi_slope(h, group) * (delta * d), NEG_BIG) for h in range(DSA_HEADS)]
        tabs.append(np.stack(per_head))
    return jnp.asarray(np.stack(tabs), F32)


def _dsa_attn_kernel(*refs, dilation, has_prev, merge):
    refs = list(refs)
    q_ref, kc_ref, vc_ref, bias_ref = refs[:4]
    pos = 4
    if has_prev:
        kp_ref, vp_ref = refs[pos:pos + 2]
        pos += 2
    if merge:
        other = refs[pos:pos + 4]
        pos += 4
        o_ref = refs[pos]
    else:
        o_ref, lse_ref = refs[pos:pos + 2]

    r = pl.program_id(2)
    lane = lax.broadcasted_iota(jnp.int32, (BAND, LANES), 1)
    nt = (((1,), (1,)), ((), ()))
    lse_tile = jnp.zeros((BAND, LANES), F32)
    nk = 2 * BAND if has_prev else BAND
    ones = jnp.ones((nk, LANES), BF16)

    def head_scores(h):
        hs = slice(h * LANES, (h + 1) * LANES)
        q = q_ref[:, hs]
        if has_prev:
            k = jnp.concatenate([kp_ref[:, hs], kc_ref[:, hs]], axis=0)
        else:
            k = kc_ref[:, hs]
        return lax.dot_general(q, k, nt, preferred_element_type=F32) + bias_ref[h]

    def head_values(h):
        hs = slice(h * LANES, (h + 1) * LANES)
        if has_prev:
            v = jnp.concatenate([vp_ref[:, hs], vc_ref[:, hs]], axis=0)
        else:
            v = vc_ref[:, hs]
        return jnp.concatenate([v, ones], axis=1)

    scores = [head_scores(h) for h in range(DSA_HEADS)]
    maxes = [jnp.max(s, axis=-1, keepdims=True) for s in scores]
    probs = [jnp.exp(s - m).astype(BF16) for s, m in zip(scores, maxes)]
    accs = [jnp.dot(p, head_values(h), preferred_element_type=F32) for h, p in enumerate(probs)]

    for h in range(DSA_HEADS):
        hs = slice(h * LANES, (h + 1) * LANES)
        l = accs[h][:, LANES:]
        o = accs[h][:, :LANES] / l
        lse = maxes[h] + jnp.log(l)
        if merge:
            lses = [lse] + [other[2 * g + 1][:, h * 16:h * 16 + 1] for g in range(2)]
            outs = [o] + [other[2 * g][h] for g in range(2)]
            top = jnp.maximum(jnp.maximum(lses[0], lses[1]), lses[2])
            es = [jnp.exp(x - top) for x in lses]
            den = es[0] + es[1] + es[2]
            o = (es[0] * outs[0] + es[1] * outs[1] + es[2] * outs[2]) / den
            o_ref[:, hs] = o.astype(o_ref.dtype)
        else:
            in_head = jnp.logical_and(lane >= h * 16, lane < (h + 1) * 16)
            lse_tile = jnp.where(in_head, lse, lse_tile)
            o_ref[h, pl.ds(r, BAND, stride=dilation), :] = o
    if not merge:
        lse_ref[pl.ds(r, BAND, stride=dilation), :] = lse_tile


def _dsa_attn(q_g, k_g, v_g, group, others=None):
    b, d, sub_len, _ = q_g.shape
    s = d * sub_len
    nb = sub_len // BAND
    has_prev = nb > 1
    merge = others is not None
    assert not merge or d == 1
    bias = _dsa_bias_table(group, has_prev)

    cur = lambda bi, n, r: (bi, r, n, 0)
    prev = lambda bi, n, r: (bi, r, jnp.maximum(n - 1, 0), 0)
    blk = (None, None, BAND, DSA_WIDTH)
    in_specs = [pl.BlockSpec(blk, cur), pl.BlockSpec(blk, cur), pl.BlockSpec(blk, cur),
                pl.BlockSpec((None,) + bias.shape[1:], lambda bi, n, r: (jnp.minimum(n, 1), 0, 0, 0))]
    args = [q_g, k_g, v_g, bias]
    if has_prev:
        in_specs += [pl.BlockSpec(blk, prev), pl.BlockSpec(blk, prev)]
        args += [k_g, v_g]
    span = BAND * d
    nat = lambda bi, n, r: (bi, n, 0)
    nat_heads = lambda bi, n, r: (bi, 0, n, 0)
    if merge:
        for o_g, lse_g in others:
            in_specs += [pl.BlockSpec((None, DSA_HEADS, BAND, LANES), nat_heads),
                         pl.BlockSpec((None, BAND, LANES), nat)]
            args += [o_g, lse_g]
        out_shape = jax.ShapeDtypeStruct((b, s, DSA_WIDTH), BF16)
        out_specs = pl.BlockSpec((None, BAND, DSA_WIDTH), nat)
    else:
        out_shape = (jax.ShapeDtypeStruct((b, DSA_HEADS, s, LANES), F32), jax.ShapeDtypeStruct((b, s, LANES), F32))
        out_specs = (pl.BlockSpec((None, DSA_HEADS, span, LANES), nat_heads),
                     pl.BlockSpec((None, span, LANES), nat))
    return pl.pallas_call(
        functools.partial(_dsa_attn_kernel, dilation=d, has_prev=has_prev, merge=merge),
        grid=(b, nb, d),
        in_specs=in_specs,
        out_specs=out_specs,
        out_shape=out_shape,
        compiler_params=_cparams(("arbitrary", "arbitrary", "arbitrary")),
        name=f"dsa_attn_g{group}",
    )(*args)


def _rope_tables(seq):
    half = MLA_ROPE // 2
    inv = ROPE_THETA ** (-jnp.arange(half, dtype=F32) / half)
    ang = jnp.arange(seq, dtype=F32)[:, None] * inv[None, :]
    cos, sin = jnp.cos(ang), jnp.sin(ang)
    z = jnp.zeros((seq, LANES - MLA_ROPE), F32)
    zh = jnp.zeros((seq, half), F32)
    cos_t = jnp.concatenate([cos, cos, z], axis=1)
    sin_a = jnp.concatenate([zh, sin, z], axis=1)
    sin_b = jnp.concatenate([-sin, zh, z], axis=1)
    return jnp.concatenate([cos_t, sin_a, sin_b], axis=1)


def _rope_lanes(x, tab):
    half = MLA_ROPE // 2
    cos_t = tab[:, 0:LANES]
    sin_a = tab[:, LANES:2 * LANES]
    sin_b = tab[:, 2 * LANES:3 * LANES]
    return (x * cos_t + pltpu.roll(x, half, 1) * sin_a
            + pltpu.roll(x, LANES - half, 1) * sin_b)


def _mla_in_kernel(a_ref, w_ref, cqg_ref, ckvg_ref, cq_ref, ckv_ref, kpe_ref, wb_ref):
    n = w_ref.shape[1]

    @pl.when(pl.program_id(0) == 0)
    def _():
        wb_ref[...] = jnp.zeros(wb_ref.shape, BF16)
        wb_ref[:, :n] = w_ref[...].astype(BF16)

    acc = jnp.dot(a_ref[...], wb_ref[...], preferred_element_type=F32)
    cq = acc[:, :MLA_Q_LORA]
    cq_ref[...] = (cq * lax.rsqrt(jnp.mean(cq * cq, axis=-1, keepdims=True) + EPS)
                   * cqg_ref[...]).astype(BF16)
    ckv = acc[:, MLA_Q_LORA:MLA_Q_LORA + MLA_KV_LORA]
    ckv_ref[...] = (ckv * lax.rsqrt(jnp.mean(ckv * ckv, axis=-1, keepdims=True) + EPS)
                    * ckvg_ref[...]).astype(BF16)
    kpe_ref[...] = acc[:, MLA_Q_LORA + MLA_KV_LORA:]


def _mla_in(h, w_in, cq_gain, ckv_gain, tm=512):
    t, d = h.shape
    n = w_in.shape[1]
    n_pad = MLA_Q_LORA + MLA_KV_LORA + LANES
    return pl.pallas_call(
        _mla_in_kernel,
        grid=(t // tm,),
        in_specs=[
            pl.BlockSpec((tm, d), lambda i: (i, 0)),
            pl.BlockSpec((d, n), lambda i: (0, 0)),
            pl.BlockSpec((1, MLA_Q_LORA), lambda i: (0, 0)),
            pl.BlockSpec((1, MLA_KV_LORA), lambda i: (0, 0)),
        ],
        out_specs=(
            pl.BlockSpec((tm, MLA_Q_LORA), lambda i: (i, 0)),
            pl.BlockSpec((tm, MLA_KV_LORA), lambda i: (i, 0)),
            pl.BlockSpec((tm, LANES), lambda i: (i, 0)),
        ),
        out_shape=(
            jax.ShapeDtypeStruct((t, MLA_Q_LORA), BF16),
            jax.ShapeDtypeStruct((t, MLA_KV_LORA), BF16),
            jax.ShapeDtypeStruct((t, LANES), F32),
        ),
        scratch_shapes=[pltpu.VMEM((d, n_pad), BF16)],
        compiler_params=_cparams(("arbitrary",)),
        name="mla_in",
    )(h, w_in, cq_gain.reshape(1, -1), ckv_gain.reshape(1, -1))


def _mla_q_kernel(a_ref, w_ref, g0_ref, g1_ref, tab_ref, o_ref, wb_ref):
    @pl.when(pl.program_id(0) == 0)
    def _():
        wb_ref[...] = w_ref[...].astype(BF16)

    acc = jnp.dot(a_ref[...], wb_ref[...], preferred_element_type=F32)
    tab = tab_ref[...]
    scale = 1.0 / math.sqrt(MLA_QK)
    g0 = g0_ref[...] * scale
    g1 = g1_ref[...] * scale
    for h in range(MLA_HEADS):
        base = h * MLA_QK_PAD
        x0 = acc[:, base:base + LANES]
        x1 = acc[:, base + LANES:base + 2 * LANES]
        ss = jnp.sum(x0 * x0, axis=-1, keepdims=True) + jnp.sum(x1 * x1, axis=-1, keepdims=True)
        rs = lax.rsqrt(ss / MLA_QK + EPS)
        o_ref[:, base:base + LANES] = (x0 * rs * g0).astype(BF16)
        o_ref[:, base + LANES:base + 2 * LANES] = _rope_lanes(x1 * rs * g1, tab).astype(BF16)


def _mla_q(cq, w_q_pad, q_gain, tab, seq, tm=512):
    t, k = cq.shape
    n = w_q_pad.shape[1]
    per = seq // tm
    g0 = q_gain[:MLA_NOPE].reshape(1, LANES)
    g1 = jnp.pad(q_gain[MLA_NOPE:], (0, LANES - MLA_ROPE)).reshape(1, LANES)
    return pl.pallas_call(
        _mla_q_kernel,
        grid=(t // tm,),
        in_specs=[
            pl.BlockSpec((tm, k), lambda i: (i, 0)),
            pl.BlockSpec((k, n), lambda i: (0, 0)),
            pl.BlockSpec((1, LANES), lambda i: (0, 0)),
            pl.BlockSpec((1, LANES), lambda i: (0, 0)),
            pl.BlockSpec((tm, 3 * LANES), lambda i: (i % per, 0)),
        ],
        out_specs=pl.BlockSpec((tm, n), lambda i: (i, 0)),
        out_shape=jax.ShapeDtypeStruct((t, n), BF16),
        scratch_shapes=[pltpu.VMEM((k, n), BF16)],
        compiler_params=_cparams(("arbitrary",)),
        name="mla_q_up",
    )(cq, w_q_pad, g0, g1, tab)


def _mla_kv_kernel(a_ref, w_ref, kpe_ref, g0_ref, g1_ref, tab_ref, k_ref, v_ref, wb_ref):
    @pl.when(pl.program_id(0) == 0)
    def _():
        wb_ref[...] = w_ref[...].astype(BF16)

    acc = jnp.dot(a_ref[...], wb_ref[...], preferred_element_type=F32)
    tab = tab_ref[...]
    kpe = kpe_ref[...]
    ss_pe = jnp.sum(kpe * kpe, axis=-1, keepdims=True)
    g0 = g0_ref[...]
    g1 = g1_ref[...]
    for h in range(MLA_HEADS):
        base = h * (MLA_NOPE + MLA_V)
        kn = acc[:, base:base + MLA_NOPE]
        ss = jnp.sum(kn * kn, axis=-1, keepdims=True) + ss_pe
        rs = lax.rsqrt(ss / MLA_QK + EPS)
        kb = h * MLA_QK_PAD
        k_ref[:, kb:kb + LANES] = (kn * rs * g0).astype(BF16)
        k_ref[:, kb + LANES:kb + 2 * LANES] = _rope_lanes(kpe * rs * g1, tab).astype(BF16)
        vb = h * 2 * MLA_V
        v_ref[:, vb:vb + MLA_V] = acc[:, base + MLA_NOPE:base + MLA_NOPE + MLA_V].astype(BF16)
        v_ref[:, vb + MLA_V:vb + 2 * MLA_V] = jnp.ones((acc.shape[0], MLA_V), BF16)


def _mla_kv(ckv, w_kv_up, kpe, k_gain, tab, seq, tm=512):
    t, k = ckv.shape
    n = w_kv_up.shape[1]
    per = seq // tm
    g0 = k_gain[:MLA_NOPE].reshape(1, LANES)
    g1 = jnp.pad(k_gain[MLA_NOPE:], (0, LANES - MLA_ROPE)).reshape(1, LANES)
    return pl.pallas_call(
        _mla_kv_kernel,
        grid=(t // tm,),
        in_specs=[
            pl.BlockSpec((tm, k), lambda i: (i, 0)),
            pl.BlockSpec((k, n), lambda i: (0, 0)),
            pl.BlockSpec((tm, LANES), lambda i: (i, 0)),
            pl.BlockSpec((1, LANES), lambda i: (0, 0)),
            pl.BlockSpec((1, LANES), lambda i: (0, 0)),
            pl.BlockSpec((tm, 3 * LANES), lambda i: (i % per, 0)),
        ],
        out_specs=(
            pl.BlockSpec((tm, MLA_HEADS * MLA_QK_PAD), lambda i: (i, 0)),
            pl.BlockSpec((tm, MLA_HEADS * 2 * MLA_V), lambda i: (i, 0)),
        ),
        out_shape=(
            jax.ShapeDtypeStruct((t, MLA_HEADS * MLA_QK_PAD), BF16),
            jax.ShapeDtypeStruct((t, MLA_HEADS * 2 * MLA_V), BF16),
        ),
        scratch_shapes=[pltpu.VMEM((k, n), BF16)],
        compiler_params=_cparams(("arbitrary",)),
        name="mla_kv_up",
    )(ckv, w_kv_up, kpe, g0, g1, tab)


def _mla_attn_kernel(q_ref, k_ref, v_ref, o_ref, *, tq):
    seq = q_ref.shape[0]
    nt = (((1,), (1,)), ((), ()))
    r = lax.broadcasted_iota(jnp.int32, (tq, tq), 0)
    c = lax.broadcasted_iota(jnp.int32, (tq, tq), 1)
    causal = c <= r
    for qi in range(seq // tq):
        q = q_ref[qi * tq:(qi + 1) * tq, :]
        scores = []
        for j in range(qi + 1):
            s = lax.dot_general(q, k_ref[j * tq:(j + 1) * tq, :], nt, preferred_element_type=F32)
            if j == qi:
                s = jnp.where(causal, s, NEG_BIG)
            scores.append(s)
        top = scores[0]
        for s in scores[1:]:
            top = jnp.maximum(top, s)
        m = jnp.max(top, axis=-1, keepdims=True)
        acc = None
        for j, s in enumerate(scores):
            p = jnp.exp(s - m).astype(BF16)
            pv = jnp.dot(p, v_ref[j * tq:(j + 1) * tq, :], preferred_element_type=F32)
            acc = pv if acc is None else acc + pv
        o_ref[qi * tq:(qi + 1) * tq, :] = (acc[:, :MLA_V] / acc[:, MLA_V:]).astype(o_ref.dtype)


def _mla_attn(q, k, v, tq=256):
    b, s, _ = q.shape
    return pl.pallas_call(
        functools.partial(_mla_attn_kernel, tq=tq),
        grid=(b, MLA_HEADS),
        in_specs=[
            pl.BlockSpec((None, s, MLA_QK_PAD), lambda bi, h: (bi, 0, h)),
            pl.BlockSpec((None, s, MLA_QK_PAD), lambda bi, h: (bi, 0, h)),
            pl.BlockSpec((None, s, 2 * MLA_V), lambda bi, h: (bi, 0, h)),
        ],
        out_specs=pl.BlockSpec((None, s, MLA_V), lambda bi, h: (bi, 0, h)),
        out_shape=jax.ShapeDtypeStruct((b, s, MLA_HEADS * MLA_V), BF16),
        compiler_params=_cparams(("arbitrary", "arbitrary")),
        name="mla_attn",
    )(q, k, v)


PLAN_COLS = 256
PLAN_EXP, PLAN_FIRST, PLAN_NEXT, PLAN_NUSED, PLAN_NVALID, PLAN_BSTART, PLAN_COUNT = range(7)


def _dispatch_plan_tile(cnt):
    nblk = jnp.floor((cnt + (EXPERT_BLOCK - 1.0)) * (1.0 / EXPERT_BLOCK))
    e_r = lax.broadcasted_iota(jnp.int32, (LANES, LANES), 0)
    e_c = lax.broadcasted_iota(jnp.int32, (LANES, LANES), 1)
    before = jnp.where(e_r < e_c, 1.0, 0.0).astype(BF16)
    bstart = jnp.dot(jnp.broadcast_to(nblk, (8, LANES)).astype(BF16), before,
                     preferred_element_type=F32)[0:1]
    bend = bstart + nblk
    n_used = jnp.max(bend, axis=-1, keepdims=True)
    row = lax.broadcasted_iota(jnp.int32, (PLAN_COLS, LANES), 0).astype(F32)
    lane = lax.broadcasted_iota(jnp.int32, (PLAN_COLS, LANES), 1).astype(F32)
    is_exp = lane < N_EXPERTS
    row1 = row[:, 0:1]
    done = jnp.logical_and(bend <= row, is_exp)
    blk_exp = jnp.minimum(jnp.sum(jnp.where(done, 1.0, 0.0), axis=-1, keepdims=True), N_EXPERTS - 1.0)
    mine = lane == blk_exp
    bstart_of = jnp.sum(jnp.where(mine, bstart, 0.0), axis=-1, keepdims=True)
    cnt_of = jnp.sum(jnp.where(mine, cnt, 0.0), axis=-1, keepdims=True)
    valid = row1 < n_used
    first = jnp.where(jnp.logical_and(valid, row1 == bstart_of), 1.0, 0.0)
    nvalid = jnp.clip(cnt_of - EXPERT_BLOCK * (row1 - bstart_of), 0.0, float(EXPERT_BLOCK))
    nvalid = jnp.where(valid, nvalid, 0.0)
    later = jnp.logical_and(jnp.logical_and(lane > blk_exp, nblk > 0.0), is_exp)
    nxt = jnp.min(jnp.where(later, lane, 999.0), axis=-1, keepdims=True)
    nxt = jnp.where(nxt > 998.0, -1.0, nxt)
    bstart_col = jnp.sum(jnp.where(lane < row, nblk, 0.0), axis=-1, keepdims=True)
    cnt_col = jnp.sum(jnp.where(lane == row, cnt, 0.0), axis=-1, keepdims=True)
    tile = jnp.zeros((PLAN_COLS, LANES), F32)
    cols = {PLAN_EXP: blk_exp, PLAN_FIRST: first, PLAN_NEXT: nxt, PLAN_NUSED: n_used,
            PLAN_NVALID: nvalid, PLAN_BSTART: bstart_col, PLAN_COUNT: cnt_col}
    for k, val in cols.items():
        tile = jnp.where(lane == k, val, tile)
    return tile


def _router_kernel(x_ref, g_ref, sh_ref, sc_ref, wr_ref, br_ref, h_ref, info_ref, idx_ref, plan_ref, carry_ref):
    i = pl.program_id(0)
    tm = x_ref.shape[0]

    @pl.when(i == 0)
    def _():
        carry_ref[...] = jnp.zeros(carry_ref.shape, F32)

    h = _norm_mod(x_ref[...], g_ref[...], sh_ref[...], sc_ref[...])
    _store_row_tiles(h_ref, _pack_halves(h))
    w = wr_ref[...]
    w_hi = w.astype(BF16)
    w_lo = (w - w_hi.astype(F32)).astype(BF16)
    h_hi = h.astype(BF16)
    h_lo = (h - h_hi.astype(F32)).astype(BF16)
    lg = (jnp.dot(h_hi, w_hi, preferred_element_type=F32)
          + jnp.dot(h_lo, w_hi, preferred_element_type=F32)
          + jnp.dot(h_hi, w_lo, preferred_element_type=F32)) + br_ref[...]

    lane = lax.broadcasted_iota(jnp.int32, (tm, LANES), 1).astype(F32)
    no_lane = float(LANES)
    gl = jnp.where(lane < N_GROUPS, lg, NEG_BIG)
    gmax = jnp.max(gl, axis=-1, keepdims=True)
    g_idx = jnp.min(jnp.where(gl == gmax, lane, no_lane), axis=-1, keepdims=True)
    g_p = 1.0 / jnp.sum(jnp.exp(gl - gmax), axis=-1, keepdims=True)
    lo_lane = N_GROUPS + g_idx * EXPERTS_PER_GROUP
    in_grp = jnp.logical_and(lane >= lo_lane, lane < lo_lane + EXPERTS_PER_GROUP)
    ev = jnp.where(in_grp, lg, NEG_BIG)
    v1 = jnp.max(ev, axis=-1, keepdims=True)
    i1 = jnp.min(jnp.where(ev == v1, lane, no_lane), axis=-1, keepdims=True)
    ev2 = jnp.where(lane == i1, NEG_BIG, ev)
    v2 = jnp.max(ev2, axis=-1, keepdims=True)
    i2 = jnp.min(jnp.where(ev2 == v2, lane, no_lane), axis=-1, keepdims=True)
    e2 = jnp.exp(v2 - v1)
    den = 1.0 + e2
    w1 = (1.0 / den) * g_p
    w2 = (e2 / den) * g_p
    id1 = i1 - N_GROUPS
    id2 = i2 - N_GROUPS

    oh1 = lane == id1
    oh2 = lane == id2
    both = jnp.where(jnp.logical_or(oh1, oh2), 1.0, 0.0)
    r = lax.broadcasted_iota(jnp.int32, (tm, tm), 0)
    c = lax.broadcasted_iota(jnp.int32, (tm, tm), 1)
    tril = jnp.where(c < r, 1.0, 0.0).astype(BF16)
    before = jnp.dot(tril, both.astype(BF16), preferred_element_type=F32) + carry_ref[...]
    rank1 = jnp.sum(jnp.where(oh1, before, 0.0), axis=-1, keepdims=True)
    rank2 = jnp.sum(jnp.where(oh2, before, 0.0), axis=-1, keepdims=True)
    carry_ref[...] = carry_ref[...] + jnp.sum(both, axis=0, keepdims=True)

    info = jnp.zeros((tm, LANES), F32)
    for col, val in enumerate((id1, id2, rank1, rank2, w1, w2)):
        info = jnp.where(lane == col, val, info)
    info_ref[...] = info
    idx_ref[...] = jnp.transpose(info)[0:8].astype(jnp.int32)

    @pl.when(i == pl.num_programs(0) - 1)
    def _():
        plan = _dispatch_plan_tile(carry_ref[...])
        plan_ref[...] = jnp.transpose(plan)[0:8].astype(jnp.int32)


def _router(x2, g, mod, seq, wr, br, tm=256):
    assert x2.shape[1] == 2 * ROW_TILE * LANES
    t, d = x2.shape
    per = seq // tm
    return pl.pallas_call(
        _router_kernel,
        grid=(t // tm,),
        in_specs=[
            pl.BlockSpec((tm, d), lambda i: (i, 0)),
            pl.BlockSpec((1, d), lambda i: (0, 0)),
            pl.BlockSpec((None, 1, d), lambda i: ((i // per) * 6 + 3, 0, 0)),
            pl.BlockSpec((None, 1, d), lambda i: ((i // per) * 6 + 4, 0, 0)),
            pl.BlockSpec((d, LANES), lambda i: (0, 0)),
            pl.BlockSpec((1, LANES), lambda i: (0, 0)),
        ],
        out_specs=(
            pl.BlockSpec((tm * ROW_TILE, LANES), lambda i: (i, 0)),
            pl.BlockSpec((tm, LANES), lambda i: (i, 0)),
            pl.BlockSpec((8, tm), lambda i: (0, i)),
            pl.BlockSpec((8, PLAN_COLS), lambda i: (0, 0)),
        ),
        out_shape=(
            jax.ShapeDtypeStruct((t * ROW_TILE, LANES), jnp.uint32),
            jax.ShapeDtypeStruct((t, LANES), F32),
            jax.ShapeDtypeStruct((8, t), jnp.int32),
            jax.ShapeDtypeStruct((8, PLAN_COLS), jnp.int32),
        ),
        scratch_shapes=[pltpu.VMEM((1, LANES), F32)],
        compiler_params=_cparams(("arbitrary",)),
        name="moe_router",
    )(x2, g.reshape(1, d), mod, mod, wr, br)


GATHER_GROUP = 8


def _slot_of(plan, idx, n_tok, k, t):
    e = idx[k * n_tok + t]
    return plan[PLAN_BSTART * PLAN_COLS + e] * EXPERT_BLOCK + idx[(2 + k) * n_tok + t]


def _expert_kernel(plan, idx, h_hbm, wg_hbm, wu_hbm, wd_hbm, ys_ref,
                   row_tok, xbuf, xsem, sg, su, sd, wsem, wgb, wub, wdb, *, layer, n_tok):
    i = pl.program_id(0)
    nu = plan[PLAN_NUSED * PLAN_COLS]
    n_rows = row_tok.shape[0]

    def weight_copies(e):
        return (pltpu.make_async_copy(wg_hbm.at[layer, e], sg, wsem.at[0]),
                pltpu.make_async_copy(wu_hbm.at[layer, e], su, wsem.at[1]),
                pltpu.make_async_copy(wd_hbm.at[layer, e], sd, wsem.at[2]))

    def n_groups(blk):
        return (plan[PLAN_NVALID * PLAN_COLS + blk] + GATHER_GROUP - 1) // GATHER_GROUP

    def start_gather(blk, slot):
        base = blk * EXPERT_BLOCK

        def body(g, carry):
            for k in range(GATHER_GROUP):
                r = g * GATHER_GROUP + k
                tok = row_tok[base + r]
                src = pl.multiple_of(tok * ROW_TILE, ROW_TILE)
                dst = pl.multiple_of(r * ROW_TILE, ROW_TILE)
                pltpu.make_async_copy(h_hbm.at[pl.ds(src, ROW_TILE), :],
                                      xbuf.at[slot, pl.ds(dst, ROW_TILE), :], xsem.at[slot]).start()
            return carry

        lax.fori_loop(0, n_groups(blk), body, 0)

    def wait_gather(blk, slot):
        span = GATHER_GROUP * ROW_TILE

        def body(g, carry):
            pltpu.make_async_copy(h_hbm.at[pl.ds(0, span), :],
                                  xbuf.at[slot, pl.ds(0, span), :], xsem.at[slot]).wait()
            return carry

        lax.fori_loop(0, n_groups(blk), body, 0)

    @pl.when(i == 0)
    def _():
        for cp in weight_copies(plan[PLAN_EXP * PLAN_COLS]):
            cp.start(priority=1)
        xbuf[...] = jnp.zeros(xbuf.shape, xbuf.dtype)

        def pad_body(e, carry):
            end = plan[PLAN_BSTART * PLAN_COLS + e] * EXPERT_BLOCK + plan[PLAN_COUNT * PLAN_COLS + e]
            for k in range(GATHER_GROUP - 1):
                row_tok[jnp.minimum(end + k, n_rows - 1)] = 0
            return carry

        lax.fori_loop(0, N_EXPERTS, pad_body, 0)

        def fill_body(t, carry):
            row_tok[_slot_of(plan, idx, n_tok, 0, t)] = t
            row_tok[_slot_of(plan, idx, n_tok, 1, t)] = t
            return carry

        lax.fori_loop(0, n_tok, fill_body, 0, unroll=8)
        start_gather(0, 0)

    @pl.when(i < nu)
    def _():
        slot = i % 2

        @pl.when(i + 1 < nu)
        def _():
            start_gather(i + 1, 1 - slot)

        @pl.when(plan[PLAN_FIRST * PLAN_COLS + i] == 1)
        def _():
            nxt = plan[PLAN_NEXT * PLAN_COLS + i]
            cps = weight_copies(plan[PLAN_EXP * PLAN_COLS + i])
            nxt_cps = weight_copies(jnp.maximum(nxt, 0))
            for cp, ncp, stage, dst in zip(cps, nxt_cps, (sg, su, sd), (wgb, wub, wdb)):
                cp.wait()
                dst[...] = stage[...].astype(BF16)

                @pl.when(nxt >= 0)
                def _():
                    ncp.start(priority=1)

        wait_gather(i, slot)
        parts = [_unpack_halves(w) for w in _load_row_tiles(xbuf.at[slot], EXPERT_BLOCK)]
        x_lo = jnp.concatenate([lo.astype(BF16) for lo, _ in parts], axis=1)
        x_hi = jnp.concatenate([hi.astype(BF16) for _, hi in parts], axis=1)
        half = wgb.shape[0] // 2
        g = (jnp.dot(x_lo, wgb[:half], preferred_element_type=F32)
             + jnp.dot(x_hi, wgb[half:], preferred_element_type=F32))
        u = (jnp.dot(x_lo, wub[:half], preferred_element_type=F32)
             + jnp.dot(x_hi, wub[half:], preferred_element_type=F32))
        hid = (_silu(g) * u).astype(BF16)
        _store_row_tiles(ys_ref, _pack_halves(jnp.dot(hid, wdb[...], preferred_element_type=F32)))

    @pl.when(i >= nu)
    def _():
        ys_ref[...] = jnp.zeros(ys_ref.shape, ys_ref.dtype)


def _expert_ffn(h2, w_gate, w_up, w_down, layer, plan, idx, n_blocks):
    t = h2.shape[0] // ROW_TILE
    d, f = w_gate.shape[2], w_gate.shape[3]
    n_rows = n_blocks * EXPERT_BLOCK
    grid_spec = pltpu.PrefetchScalarGridSpec(
        num_scalar_prefetch=2,
        grid=(n_blocks,),
        in_specs=[pl.BlockSpec(memory_space=pl.ANY)] * 4,
        out_specs=pl.BlockSpec((EXPERT_BLOCK * ROW_TILE, LANES), lambda i, *_: (i, 0)),
        scratch_shapes=[
            pltpu.SMEM((n_rows,), jnp.int32),
            pltpu.VMEM((2, EXPERT_BLOCK * ROW_TILE, LANES), jnp.uint32),
            pltpu.SemaphoreType.DMA((2,)),
            pltpu.VMEM((d, f), F32),
            pltpu.VMEM((d, f), F32),
            pltpu.VMEM((f, d), F32),
            pltpu.SemaphoreType.DMA((3,)),
            pltpu.VMEM((d, f), BF16),
            pltpu.VMEM((d, f), BF16),
            pltpu.VMEM((f, d), BF16),
        ],
    )
    return pl.pallas_call(
        functools.partial(_expert_kernel, layer=layer, n_tok=t),
        grid_spec=grid_spec,
        out_shape=jax.ShapeDtypeStruct((n_rows * ROW_TILE, LANES), jnp.uint32),
        compiler_params=_cparams(("arbitrary",)),
        name="moe_experts",
    )(plan, idx, h2, w_gate, w_up, w_down)


def _combine_kernel(plan, idx, x_ref, info_ref, gate_ref, ys_hbm, o_ref, buf, sem, *, tm, n_tok):
    i = pl.program_id(0)
    nsteps = pl.num_programs(0)

    def start_gather(step, slot):
        base = step * tm

        def body(r, carry):
            dst = pl.multiple_of(r * ROW_TILE, ROW_TILE)
            for k in range(2):
                src = pl.multiple_of(_slot_of(plan, idx, n_tok, k, base + r) * ROW_TILE, ROW_TILE)
                pltpu.make_async_copy(ys_hbm.at[pl.ds(src, ROW_TILE), :],
                                      buf.at[slot, k, pl.ds(dst, ROW_TILE), :], sem.at[slot]).start(priority=k)
            return carry

        lax.fori_loop(0, tm, body, 0, unroll=8)

    @pl.when(i == 0)
    def _():
        start_gather(0, 0)

    slot = i % 2

    @pl.when(i + 1 < nsteps)
    def _():
        start_gather(i + 1, 1 - slot)

    for k in range(2):
        pltpu.make_async_copy(ys_hbm.at[pl.ds(0, tm * ROW_TILE), :], buf.at[slot, k], sem.at[slot]).wait()
    info = info_ref[...]
    w0 = info[:, 4:5]
    w1 = info[:, 5:6]
    half = x_ref.shape[1] // 2
    a_tiles = _load_row_tiles(buf.at[slot, 0], tm)
    b_tiles = _load_row_tiles(buf.at[slot, 1], tm)
    for s in range(ROW_TILE):
        a_lo, a_hi = _unpack_halves(a_tiles[s])
        b_lo, b_hi = _unpack_halves(b_tiles[s])
        lo = slice(s * LANES, (s + 1) * LANES)
        hi = slice(half + s * LANES, half + (s + 1) * LANES)
        o_ref[:, lo] = x_ref[:, lo] + gate_ref[:, lo] * (w0 * a_lo + w1 * b_lo)
        o_ref[:, hi] = x_ref[:, hi] + gate_ref[:, hi] * (w0 * a_hi + w1 * b_hi)


def _combine(x2, info, mod, seq, ys, plan, idx, tm=128):
    t, d = x2.shape
    per = seq // tm
    grid_spec = pltpu.PrefetchScalarGridSpec(
        num_scalar_prefetch=2,
        grid=(t // tm,),
        in_specs=[
            pl.BlockSpec((tm, d), lambda i, *_: (i, 0)),
            pl.BlockSpec((tm, LANES), lambda i, *_: (i, 0)),
            pl.BlockSpec((None, 1, d), lambda i, *_: ((i // per) * 6 + 5, 0, 0)),
            pl.BlockSpec(memory_space=pl.ANY),
        ],
        out_specs=pl.BlockSpec((tm, d), lambda i, *_: (i, 0)),
        scratch_shapes=[
            pltpu.VMEM((2, 2, tm * ROW_TILE, LANES), jnp.uint32),
            pltpu.SemaphoreType.DMA((2,)),
        ],
    )
    return pl.pallas_call(
        functools.partial(_combine_kernel, tm=tm, n_tok=t),
        grid_spec=grid_spec,
        out_shape=jax.ShapeDtypeStruct((t, d), F32),
        compiler_params=_cparams(("arbitrary",)),
        name="moe_combine",
    )(plan, idx, x2, info, mod, ys)


def _hier_moe(x2, norm_g, mod, seq, w_rg, b_rg, w_re, b_re, w_gate, w_up, w_down, layer):
    t, d = x2.shape
    pad = LANES - N_GROUPS - N_EXPERTS
    wr = jnp.concatenate([w_rg, w_re, jnp.zeros((d, pad), F32)], axis=1)
    br = jnp.concatenate([b_rg, b_re, jnp.zeros((pad,), F32)]).reshape(1, LANES)
    h2, info, idx, plan = _router(x2, norm_g, mod, seq, wr, br)
    n_assign = 2 * t
    n_blocks = (n_assign + N_EXPERTS * (EXPERT_BLOCK - 1) + EXPERT_BLOCK - 1) // EXPERT_BLOCK
    assert n_blocks <= PLAN_COLS
    plan = plan.reshape(-1)
    idx = idx.reshape(-1)
    ys = _expert_ffn(h2, w_gate, w_up, w_down, layer, plan, idx, n_blocks)
    return _combine(x2, info, mod, seq, ys, plan, idx)


def kernel(x, c, ada_w, ada_b, norm1_g, norm2_g, dsa_w_in, dsa_q_gain, dsa_k_gain, dsa_w_out, mla_w_in, mla_cq_gain, mla_ckv_gain, mla_w_q_up, mla_w_kv_up, mla_q_gain, mla_k_gain, mla_w_out, router_group_w, router_group_b, router_expert_w, router_expert_b, expert_w_gate, expert_w_up, expert_w_down):
    b, s, d = x.shape
    t = b * s
    mods = _ada_mod(c, ada_w, ada_b)
    x2 = x.reshape(t, d)

    mod = mods[0]
    h = _normmod(x2, norm1_g[0], mod, s, 0, 1)
    gains = (dsa_q_gain[0], dsa_k_gain[0], dsa_k_gain[0])
    qkv = [[_dsa_proj(h, dsa_w_in[0], gains[w], g, w, b) for w in range(3)] for g in range(len(DIL_PAIRS))]
    og2 = _dsa_attn(*qkv[2], 2)
    og1 = _dsa_attn(*qkv[1], 1)
    o = _dsa_attn(*qkv[0], 0, others=(og1, og2))
    x2 = _resid_mm(o.reshape(t, DSA_WIDTH), dsa_w_out[0], x2, mod, s, 2)
    x2 = _hier_moe(x2, norm2_g[0], mod, s, router_group_w[0], router_group_b[0],
                   router_expert_w[0], router_expert_b[0],
                   expert_w_gate, expert_w_up, expert_w_down, 0)

    mod = mods[1]
    h = _normmod(x2, norm1_g[1], mod, s, 0, 1)
    cq, ckv, kpe = _mla_in(h, mla_w_in[0], mla_cq_gain[0], mla_ckv_gain[0])
    tab = _rope_tables(s)
    w_q_pad = jnp.pad(mla_w_q_up[0].reshape(MLA_Q_LORA, MLA_HEADS, MLA_QK),
                      ((0, 0), (0, 0), (0, MLA_QK_PAD - MLA_QK))).reshape(MLA_Q_LORA, MLA_HEADS * MLA_QK_PAD)
    q = _mla_q(cq, w_q_pad, mla_q_gain[0], tab, s)
    k, v = _mla_kv(ckv, mla_w_kv_up[0], kpe, mla_k_gain[0], tab, s)
    o = _mla_attn(q.reshape(b, s, -1), k.reshape(b, s, -1), v.reshape(b, s, -1))
    x2 = _resid_mm(o.reshape(t, MLA_HEADS * MLA_V), mla_w_out[0], x2, mod, s, 2)
    x2 = _hier_moe(x2, norm2_g[1], mod, s, router_group_w[1], router_group_b[1],
                   router_expert_w[1], router_expert_b[1],
                   expert_w_gate, expert_w_up, expert_w_down, 1)
    return x2.reshape(b, s, d)
```

```python
import functools
import math

import jax
import jax.numpy as jnp
import numpy as np
from jax import lax
from jax.experimental import pallas as pl
from jax.experimental.pallas import tpu as pltpu

F32 = jnp.float32
BF16 = jnp.bfloat16

D_MODEL = 2048
EPS = 1e-6
LANES = 128
NEG_BIG = -1e30

DIL_PAIRS = ((128, 1), (512, 4), (2048, 16))
DSA_HEADS = 8
DSA_HEAD_DIM = 128
DSA_WIDTH = DSA_HEADS * DSA_HEAD_DIM
BAND = 128

MLA_HEADS = 16
MLA_Q_LORA = 512
MLA_KV_LORA = 512
MLA_NOPE = 128
MLA_ROPE = 64
MLA_V = 128
MLA_QK = MLA_NOPE + MLA_ROPE
MLA_QK_PAD = 256
ROPE_THETA = 10000.0

N_GROUPS = 4
EXPERTS_PER_GROUP = 16
N_EXPERTS = N_GROUPS * EXPERTS_PER_GROUP
D_EXPERT = 768
EXPERT_BLOCK = 128

VMEM_LIMIT = 48 * 1024 * 1024


def _cparams(sem, vmem=VMEM_LIMIT):
    return pltpu.CompilerParams(dimension_semantics=sem, vmem_limit_bytes=vmem)


def _silu(x):
    return x * (1.0 / (1.0 + jnp.exp(-x)))


def _pack_halves(x):
    n = x.shape[1] // 2
    xb = x.astype(BF16).astype(F32)
    lo = lax.bitcast_convert_type(xb[:, :n], jnp.uint32) >> 16
    hi = lax.bitcast_convert_type(xb[:, n:], jnp.uint32) & jnp.uint32(0xFFFF0000)
    return hi | lo


def _unpack_halves(w):
    lo = lax.bitcast_convert_type(w << 16, F32)
    hi = lax.bitcast_convert_type(w & jnp.uint32(0xFFFF0000), F32)
    return lo, hi


ROW_TILE = 8


def _store_row_tiles(ref, words):
    rows = words.shape[0]
    for s in range(ROW_TILE):
        ref[pl.ds(s, rows, stride=ROW_TILE), :] = words[:, s * LANES:(s + 1) * LANES]


def _load_row_tiles(ref, rows):
    return [ref[pl.ds(s, rows, stride=ROW_TILE), :] for s in range(ROW_TILE)]


def _norm_mod(x, g, shift, scale):
    ms = jnp.mean(x * x, axis=-1, keepdims=True)
    y = x * lax.rsqrt(ms + EPS) * g
    return y * (1.0 + scale) + shift


def _ada_kernel(c_ref, w_ref, b_ref, o_ref):
    ca = _silu(c_ref[...])
    hi = ca.astype(BF16)
    lo = (ca - hi.astype(F32)).astype(BF16)
    lhs = jnp.concatenate([hi, lo], axis=0)
    res = jnp.dot(lhs, w_ref[...].astype(BF16), preferred_element_type=F32)
    o_ref[...] = res[:8] + res[8:] + b_ref[...]


def _ada_mod(c, ada_w, ada_b):
    depth, d, n = ada_w.shape
    b = c.shape[0]
    c8 = jnp.pad(c, ((0, 8 - b), (0, 0)))
    tn = 1024
    out = pl.pallas_call(
        _ada_kernel,
        grid=(depth, n // tn),
        in_specs=[
            pl.BlockSpec((8, d), lambda i, j: (0, 0)),
            pl.BlockSpec((None, d, tn), lambda i, j: (i, 0, j)),
            pl.BlockSpec((None, 1, tn), lambda i, j: (i, 0, j)),
        ],
        out_specs=pl.BlockSpec((None, 8, tn), lambda i, j: (i, 0, j)),
        out_shape=jax.ShapeDtypeStruct((depth, 8, n), F32),
        compiler_params=_cparams(("arbitrary", "arbitrary")),
        name="ada_mod",
    )(c8, ada_w, ada_b.reshape(depth, 1, n))
    return out[:, :b].reshape(depth, b * 6, 1, d)


def _normmod_kernel(x_ref, g_ref, sh_ref, sc_ref, o_ref):
    o_ref[...] = _norm_mod(x_ref[...], g_ref[...], sh_ref[...], sc_ref[...]).astype(o_ref.dtype)


def _normmod(x2, g, mod, seq, k_shift, k_scale, tm=512):
    t, d = x2.shape
    per = seq // tm
    return pl.pallas_call(
        _normmod_kernel,
        grid=(t // tm,),
        in_specs=[
            pl.BlockSpec((tm, d), lambda i: (i, 0)),
            pl.BlockSpec((1, d), lambda i: (0, 0)),
            pl.BlockSpec((None, 1, d), lambda i: ((i // per) * 6 + k_shift, 0, 0)),
            pl.BlockSpec((None, 1, d), lambda i: ((i // per) * 6 + k_scale, 0, 0)),
        ],
        out_specs=pl.BlockSpec((tm, d), lambda i: (i, 0)),
        out_shape=jax.ShapeDtypeStruct((t, d), BF16),
        compiler_params=_cparams(("arbitrary",)),
        name="normmod",
    )(x2, g.reshape(1, d), mod, mod)


def _cast_weight_once(w_ref, wb_ref):
    @pl.when(pl.program_id(1) == 0)
    def _():
        wb_ref[...] = w_ref[...].astype(BF16)


def _dsa_proj_kernel(a_ref, w_ref, g_ref, o_ref, wb_ref, res_ref, *, dilation, normed, gain_scale):
    @pl.when(pl.program_id(0) == 0)
    def _():
        wb_ref[...] = w_ref[...].astype(BF16)

    acc = jnp.dot(a_ref[...], wb_ref[...], preferred_element_type=F32)
    gain = g_ref[...] * gain_scale
    rows = res_ref.shape[1] // dilation
    for h in range(DSA_HEADS):
        blk = acc[:, h * LANES:(h + 1) * LANES]
        if normed:
            ms = jnp.mean(blk * blk, axis=-1, keepdims=True)
            blk = blk * lax.rsqrt(ms + EPS) * gain
        if dilation == 1:
            o_ref[0, :, h * LANES:(h + 1) * LANES] = blk.astype(o_ref.dtype)
        else:
            res_ref[h] = blk
            for r in range(dilation):
                sub = res_ref[h, pl.ds(r, rows, stride=dilation), :]
                o_ref[r, :, h * LANES:(h + 1) * LANES] = sub.astype(o_ref.dtype)


def _dsa_proj(h, w_in, gain, group, which, batch, tm=512, tn=DSA_WIDTH):
    t, d = h.shape
    dil = DIL_PAIRS[group][1]
    seq = t // batch
    per = seq // tm
    gain_scale = 1.0 / math.sqrt(DSA_HEAD_DIM) if which == 0 else 1.0
    return pl.pallas_call(
        functools.partial(_dsa_proj_kernel, dilation=dil, normed=which != 2, gain_scale=gain_scale),
        grid=(t // tm,),
        in_specs=[
            pl.BlockSpec((tm, d), lambda i: (i, 0)),
            pl.BlockSpec((d, tn), lambda i: (0, group * 3 + which)),
            pl.BlockSpec((1, LANES), lambda i: (0, 0)),
        ],
        out_specs=pl.BlockSpec((None, dil, tm // dil, tn), lambda i: (i // per, 0, i % per, 0)),
        out_shape=jax.ShapeDtypeStruct((batch, dil, seq // dil, tn), BF16),
        scratch_shapes=[pltpu.VMEM((d, tn), BF16), pltpu.VMEM((DSA_HEADS, tm, LANES), F32)],
        compiler_params=_cparams(("arbitrary",)),
        name=f"dsa_proj_g{group}_{'qkv'[which]}",
    )(h, w_in, gain.reshape(1, LANES))


def _resid_mm_kernel(a_ref, w_ref, x_ref, gate_ref, o_ref, wb_ref):
    _cast_weight_once(w_ref, wb_ref)
    y = jnp.dot(a_ref[...], wb_ref[...], preferred_element_type=F32)
    o_ref[...] = x_ref[...] + gate_ref[...] * y


def _resid_mm(a, w, x2, mod, seq, k_gate, tm=512, tn=1024):
    t, k = a.shape
    n = w.shape[1]
    per = seq // tm
    return pl.pallas_call(
        _resid_mm_kernel,
        grid=(n // tn, t // tm),
        in_specs=[
            pl.BlockSpec((tm, k), lambda j, i: (i, 0)),
            pl.BlockSpec((k, tn), lambda j, i: (0, j)),
            pl.BlockSpec((tm, tn), lambda j, i: (i, j)),
            pl.BlockSpec((None, 1, tn), lambda j, i: ((i // per) * 6 + k_gate, 0, j)),
        ],
        out_specs=pl.BlockSpec((tm, tn), lambda j, i: (i, j)),
        out_shape=jax.ShapeDtypeStruct((t, n), F32),
        scratch_shapes=[pltpu.VMEM((k, tn), BF16)],
        compiler_params=_cparams(("arbitrary", "arbitrary")),
        name="resid_mm",
    )(a, w, x2, mod)


def _alibi_slope(head_slot, group):
    n = len(DIL_PAIRS) * DSA_HEADS
    return 2.0 ** (-8.0 * (head_slot * len(DIL_PAIRS) + group + 1.0) / n)


def _dsa_bias_table(group, has_prev):
    window, d = DIL_PAIRS[group]
    steps = window // d
    qi = np.arange(BAND)[:, None]
    if has_prev:
        kj = np.arange(2 * BAND)[None, :]
        delta = qi + BAND - kj
        prev_key = np.broadcast_to(kj < BAND, delta.shape)
    else:
        kj = np.arange(BAND)[None, :]
        delta = qi - kj
        prev_key = np.zeros(delta.shape, bool)
    inside = (delta >= 0) & (delta <= steps)
    tabs = []
    for first in (True, False):
        valid = inside & ~(prev_key & first)
        per_head = [np.where(valid, -_alibi_slope(h, group) * (delta * d), NEG_BIG) for h in range(DSA_HEADS)]
        tabs.append(np.stack(per_head))
    return jnp.asarray(np.stack(tabs), F32)


def _dsa_attn_kernel(*refs, dilation, has_prev, merge):
    refs = list(refs)
    q_ref, kc_ref, vc_ref, bias_ref = refs[:4]
    pos = 4
    if has_prev:
        kp_ref, vp_ref = refs[pos:pos + 2]
        pos += 2
    if merge:
        other = refs[pos:pos + 4]
        pos += 4
        o_ref = refs[pos]
    else:
        o_ref, lse_ref = refs[pos:pos + 2]

    r = pl.program_id(2)
    lane = lax.broadcasted_iota(jnp.int32, (BAND, LANES), 1)
    nt = (((1,), (1,)), ((), ()))
    lse_tile = jnp.zeros((BAND, LANES), F32)
    nk = 2 * BAND if has_prev else BAND
    ones = jnp.ones((nk, LANES), BF16)

    def head_scores(h):
        hs = slice(h * LANES, (h + 1) * LANES)
        q = q_ref[:, hs]
        if has_prev:
            k = jnp.concatenate([kp_ref[:, hs], kc_ref[:, hs]], axis=0)
        else:
            k = kc_ref[:, hs]
        return lax.dot_general(q, k, nt, preferred_element_type=F32) + bias_ref[h]

    def head_values(h):
        hs = slice(h * LANES, (h + 1) * LANES)
        if has_prev:
            v = jnp.concatenate([vp_ref[:, hs], vc_ref[:, hs]], axis=0)
        else:
            v = vc_ref[:, hs]
        return jnp.concatenate([v, ones], axis=1)

    scores = [head_scores(h) for h in range(DSA_HEADS)]
    maxes = [jnp.max(s, axis=-1, keepdims=True) for s in scores]
    probs = [jnp.exp(s - m).astype(BF16) for s, m in zip(scores, maxes)]
    accs = [jnp.dot(p, head_values(h), preferred_element_type=F32) for h, p in enumerate(probs)]

    for h in range(DSA_HEADS):
        hs = slice(h * LANES, (h + 1) * LANES)
        l = accs[h][:, LANES:]
        o = accs[h][:, :LANES] / l
        lse = maxes[h] + jnp.log(l)
        if merge:
            lses = [lse] + [other[2 * g + 1][:, h * 16:h * 16 + 1] for g in range(2)]
            outs = [o] + [other[2 * g][h] for g in range(2)]
            top = jnp.maximum(jnp.maximum(lses[0], lses[1]), lses[2])
            es = [jnp.exp(x - top) for x in lses]
            den = es[0] + es[1] + es[2]
            o = (es[0] * outs[0] + es[1] * outs[1] + es[2] * outs[2]) / den
            o_ref[:, hs] = o.astype(o_ref.dtype)
        else:
            in_head = jnp.logical_and(lane >= h * 16, lane < (h + 1) * 16)
            lse_tile = jnp.where(in_head, lse, lse_tile)
            o_ref[h, pl.ds(r, BAND, stride=dilation), :] = o
    if not merge:
        lse_ref[pl.ds(r, BAND, stride=dilation), :] = lse_tile


def _dsa_attn(q_g, k_g, v_g, group, others=None):
    b, d, sub_len, _ = q_g.shape
    s = d * sub_len
    nb = sub_len // BAND
    has_prev = nb > 1
    merge = others is not None
    assert not merge or d == 1
    bias = _dsa_bias_table(group, has_prev)

    cur = lambda bi, n, r: (bi, r, n, 0)
    prev = lambda bi, n, r: (bi, r, jnp.maximum(n - 1, 0), 0)
    blk = (None, None, BAND, DSA_WIDTH)
    in_specs = [pl.BlockSpec(blk, cur), pl.BlockSpec(blk, cur), pl.BlockSpec(blk, cur),
                pl.BlockSpec((None,) + bias.shape[1:], lambda bi, n, r: (jnp.minimum(n, 1), 0, 0, 0))]
    args = [q_g, k_g, v_g, bias]
    if has_prev:
        in_specs += [pl.BlockSpec(blk, prev), pl.BlockSpec(blk, prev)]
        args += [k_g, v_g]
    span = BAND * d
    nat = lambda bi, n, r: (bi, n, 0)
    nat_heads = lambda bi, n, r: (bi, 0, n, 0)
    if merge:
        for o_g, lse_g in others:
            in_specs += [pl.BlockSpec((None, DSA_HEADS, BAND, LANES), nat_heads),
                         pl.BlockSpec((None, BAND, LANES), nat)]
            args += [o_g, lse_g]
        out_shape = jax.ShapeDtypeStruct((b, s, DSA_WIDTH), BF16)
        out_specs = pl.BlockSpec((None, BAND, DSA_WIDTH), nat)
    else:
        out_shape = (jax.ShapeDtypeStruct((b, DSA_HEADS, s, LANES), F32), jax.ShapeDtypeStruct((b, s, LANES), F32))
        out_specs = (pl.BlockSpec((None, DSA_HEADS, span, LANES), nat_heads),
                     pl.BlockSpec((None, span, LANES), nat))
    return pl.pallas_call(
        functools.partial(_dsa_attn_kernel, dilation=d, has_prev=has_prev, merge=merge),
        grid=(b, nb, d),
        in_specs=in_specs,
        out_specs=out_specs,
        out_shape=out_shape,
        compiler_params=_cparams(("arbitrary", "arbitrary", "arbitrary")),
        name=f"dsa_attn_g{group}",
    )(*args)


def _rope_tables(seq):
    half = MLA_ROPE // 2
    inv = ROPE_THETA ** (-jnp.arange(half, dtype=F32) / half)
    ang = jnp.arange(seq, dtype=F32)[:, None] * inv[None, :]
    cos, sin = jnp.cos(ang), jnp.sin(ang)
    z = jnp.zeros((seq, LANES - MLA_ROPE), F32)
    zh = jnp.zeros((seq, half), F32)
    cos_t = jnp.concatenate([cos, cos, z], axis=1)
    sin_a = jnp.concatenate([zh, sin, z], axis=1)
    sin_b = jnp.concatenate([-sin, zh, z], axis=1)
    return jnp.concatenate([cos_t, sin_a, sin_b], axis=1)


def _rope_lanes(x, tab):
    half = MLA_ROPE // 2
    cos_t = tab[:, 0:LANES]
    sin_a = tab[:, LANES:2 * LANES]
    sin_b = tab[:, 2 * LANES:3 * LANES]
    return (x * cos_t + pltpu.roll(x, half, 1) * sin_a
            + pltpu.roll(x, LANES - half, 1) * sin_b)


def _mla_in_kernel(a_ref, w_ref, cqg_ref, ckvg_ref, cq_ref, ckv_ref, kpe_ref, wb_ref):
    n = w_ref.shape[1]

    @pl.when(pl.program_id(0) == 0)
    def _():
        wb_ref[...] = jnp.zeros(wb_ref.shape, BF16)
        wb_ref[:, :n] = w_ref[...].astype(BF16)

    acc = jnp.dot(a_ref[...], wb_ref[...], preferred_element_type=F32)
    cq = acc[:, :MLA_Q_LORA]
    cq_ref[...] = (cq * lax.rsqrt(jnp.mean(cq * cq, axis=-1, keepdims=True) + EPS)
                   * cqg_ref[...]).astype(BF16)
    ckv = acc[:, MLA_Q_LORA:MLA_Q_LORA + MLA_KV_LORA]
    ckv_ref[...] = (ckv * lax.rsqrt(jnp.mean(ckv * ckv, axis=-1, keepdims=True) + EPS)
                    * ckvg_ref[...]).astype(BF16)
    kpe_ref[...] = acc[:, MLA_Q_LORA + MLA_KV_LORA:]


def _mla_in(h, w_in, cq_gain, ckv_gain, tm=512):
    t, d = h.shape
    n = w_in.shape[1]
    n_pad = MLA_Q_LORA + MLA_KV_LORA + LANES
    return pl.pallas_call(
        _mla_in_kernel,
        grid=(t // tm,),
        in_specs=[
            pl.BlockSpec((tm, d), lambda i: (i, 0)),
            pl.BlockSpec((d, n), lambda i: (0, 0)),
            pl.BlockSpec((1, MLA_Q_LORA), lambda i: (0, 0)),
            pl.BlockSpec((1, MLA_KV_LORA), lambda i: (0, 0)),
        ],
        out_specs=(
            pl.BlockSpec((tm, MLA_Q_LORA), lambda i: (i, 0)),
            pl.BlockSpec((tm, MLA_KV_LORA), lambda i: (i, 0)),
            pl.BlockSpec((tm, LANES), lambda i: (i, 0)),
        ),
        out_shape=(
            jax.ShapeDtypeStruct((t, MLA_Q_LORA), BF16),
            jax.ShapeDtypeStruct((t, MLA_KV_LORA), BF16),
            jax.ShapeDtypeStruct((t, LANES), F32),
        ),
        scratch_shapes=[pltpu.VMEM((d, n_pad), BF16)],
        compiler_params=_cparams(("arbitrary",)),
        name="mla_in",
    )(h, w_in, cq_gain.reshape(1, -1), ckv_gain.reshape(1, -1))


def _mla_q_kernel(a_ref, w_ref, g0_ref, g1_ref, tab_ref, o_ref, wb_ref):
    @pl.when(pl.program_id(0) == 0)
    def _():
        wb_ref[...] = w_ref[...].astype(BF16)

    acc = jnp.dot(a_ref[...], wb_ref[...], preferred_element_type=F32)
    tab = tab_ref[...]
    scale = 1.0 / math.sqrt(MLA_QK)
    g0 = g0_ref[...] * scale
    g1 = g1_ref[...] * scale
    for h in range(MLA_HEADS):
        base = h * MLA_QK_PAD
        x0 = acc[:, base:base + LANES]
        x1 = acc[:, base + LANES:base + 2 * LANES]
        ss = jnp.sum(x0 * x0, axis=-1, keepdims=True) + jnp.sum(x1 * x1, axis=-1, keepdims=True)
        rs = lax.rsqrt(ss / MLA_QK + EPS)
        o_ref[:, base:base + LANES] = (x0 * rs * g0).astype(BF16)
        o_ref[:, base + LANES:base + 2 * LANES] = _rope_lanes(x1 * rs * g1, tab).astype(BF16)


def _mla_q(cq, w_q_pad, q_gain, tab, seq, tm=512):
    t, k = cq.shape
    n = w_q_pad.shape[1]
    per = seq // tm
    g0 = q_gain[:MLA_NOPE].reshape(1, LANES)
    g1 = jnp.pad(q_gain[MLA_NOPE:], (0, LANES - MLA_ROPE)).reshape(1, LANES)
    return pl.pallas_call(
        _mla_q_kernel,
        grid=(t // tm,),
        in_specs=[
            pl.BlockSpec((tm, k), lambda i: (i, 0)),
            pl.BlockSpec((k, n), lambda i: (0, 0)),
            pl.BlockSpec((1, LANES), lambda i: (0, 0)),
            pl.BlockSpec((1, LANES), lambda i: (0, 0)),
            pl.BlockSpec((tm, 3 * LANES), lambda i: (i % per, 0)),
        ],
        out_specs=pl.BlockSpec((tm, n), lambda i: (i, 0)),
        out_shape=jax.ShapeDtypeStruct((t, n), BF16),
        scratch_shapes=[pltpu.VMEM((k, n), BF16)],
        compiler_params=_cparams(("arbitrary",)),
        name="mla_q_up",
    )(cq, w_q_pad, g0, g1, tab)


def _mla_kv_kernel(a_ref, w_ref, kpe_ref, g0_ref, g1_ref, tab_ref, k_ref, v_ref, wb_ref):
    @pl.when(pl.program_id(0) == 0)
    def _():
        wb_ref[...] = w_ref[...].astype(BF16)

    acc = jnp.dot(a_ref[...], wb_ref[...], preferred_element_type=F32)
    tab = tab_ref[...]
    kpe = kpe_ref[...]
    ss_pe = jnp.sum(kpe * kpe, axis=-1, keepdims=True)
    g0 = g0_ref[...]
    g1 = g1_ref[...]
    for h in range(MLA_HEADS):
        base = h * (MLA_NOPE + MLA_V)
        kn = acc[:, base:base + MLA_NOPE]
        ss = jnp.sum(kn * kn, axis=-1, keepdims=True) + ss_pe
        rs = lax.rsqrt(ss / MLA_QK + EPS)
        kb = h * MLA_QK_PAD
        k_ref[:, kb:kb + LANES] = (kn * rs * g0).astype(BF16)
        k_ref[:, kb + LANES:kb + 2 * LANES] = _rope_lanes(kpe * rs * g1, tab).astype(BF16)
        vb = h * 2 * MLA_V
        v_ref[:, vb:vb + MLA_V] = acc[:, base + MLA_NOPE:base + MLA_NOPE + MLA_V].astype(BF16)
        v_ref[:, vb + MLA_V:vb + 2 * MLA_V] = jnp.ones((acc.shape[0], MLA_V), BF16)


def _mla_kv(ckv, w_kv_up, kpe, k_gain, tab, seq, tm=512):
    t, k = ckv.shape
    n = w_kv_up.shape[1]
    per = seq // tm
    g0 = k_gain[:MLA_NOPE].reshape(1, LANES)
    g1 = jnp.pad(k_gain[MLA_NOPE:], (0, LANES - MLA_ROPE)).reshape(1, LANES)
    return pl.pallas_call(
        _mla_kv_kernel,
        grid=(t // tm,),
        in_specs=[
            pl.BlockSpec((tm, k), lambda i: (i, 0)),
            pl.BlockSpec((k, n), lambda i: (0, 0)),
            pl.BlockSpec((tm, LANES), lambda i: (i, 0)),
            pl.BlockSpec((1, LANES), lambda i: (0, 0)),
            pl.BlockSpec((1, LANES), lambda i: (0, 0)),
            pl.BlockSpec((tm, 3 * LANES), lambda i: (i % per, 0)),
        ],
        out_specs=(
            pl.BlockSpec((tm, MLA_HEADS * MLA_QK_PAD), lambda i: (i, 0)),
            pl.BlockSpec((tm, MLA_HEADS * 2 * MLA_V), lambda i: (i, 0)),
        ),
        out_shape=(
            jax.ShapeDtypeStruct((t, MLA_HEADS * MLA_QK_PAD), BF16),
            jax.ShapeDtypeStruct((t, MLA_HEADS * 2 * MLA_V), BF16),
        ),
        scratch_shapes=[pltpu.VMEM((k, n), BF16)],
        compiler_params=_cparams(("arbitrary",)),
        name="mla_kv_up",
    )(ckv, w_kv_up, kpe, g0, g1, tab)


def _mla_attn_kernel(q_ref, k_ref, v_ref, o_ref, *, tq):
    seq = q_ref.shape[0]
    nt = (((1,), (1,)), ((), ()))
    r = lax.broadcasted_iota(jnp.int32, (tq, tq), 0)
    c = lax.broadcasted_iota(jnp.int32, (tq, tq), 1)
    causal = c <= r
    for qi in range(seq // tq):
        q = q_ref[qi * tq:(qi + 1) * tq, :]
        scores = []
        for j in range(qi + 1):
            s = lax.dot_general(q, k_ref[j * tq:(j + 1) * tq, :], nt, preferred_element_type=F32)
            if j == qi:
                s = jnp.where(causal, s, NEG_BIG)
            scores.append(s)
        top = scores[0]
        for s in scores[1:]:
            top = jnp.maximum(top, s)
        m = jnp.max(top, axis=-1, keepdims=True)
        acc = None
        for j, s in enumerate(scores):
            p = jnp.exp(s - m).astype(BF16)
            pv = jnp.dot(p, v_ref[j * tq:(j + 1) * tq, :], preferred_element_type=F32)
            acc = pv if acc is None else acc + pv
        o_ref[qi * tq:(qi + 1) * tq, :] = (acc[:, :MLA_V] / acc[:, MLA_V:]).astype(o_ref.dtype)


def _mla_attn(q, k, v, tq=256):
    b, s, _ = q.shape
    return pl.pallas_call(
        functools.partial(_mla_attn_kernel, tq=tq),
        grid=(b, MLA_HEADS),
        in_specs=[
            pl.BlockSpec((None, s, MLA_QK_PAD), lambda bi, h: (bi, 0, h)),
            pl.BlockSpec((None, s, MLA_QK_PAD), lambda bi, h: (bi, 0, h)),
            pl.BlockSpec((None, s, 2 * MLA_V), lambda bi, h: (bi, 0, h)),
        ],
        out_specs=pl.BlockSpec((None, s, MLA_V), lambda bi, h: (bi, 0, h)),
        out_shape=jax.ShapeDtypeStruct((b, s, MLA_HEADS * MLA_V), BF16),
        compiler_params=_cparams(("arbitrary", "arbitrary")),
        name="mla_attn",
    )(q, k, v)


PLAN_COLS = 256
(PLAN_EXP, PLAN_FIRST, PLAN_NEXT, PLAN_NUSED, PLAN_NVALID, PLAN_BSTART, PLAN_COUNT,
 PLAN_NEXT2) = range(8)


def _dispatch_plan_tile(cnt):
    nblk = jnp.floor((cnt + (EXPERT_BLOCK - 1.0)) * (1.0 / EXPERT_BLOCK))
    e_r = lax.broadcasted_iota(jnp.int32, (LANES, LANES), 0)
    e_c = lax.broadcasted_iota(jnp.int32, (LANES, LANES), 1)
    before = jnp.where(e_r < e_c, 1.0, 0.0).astype(BF16)
    used = jnp.where(nblk > 0.0, 1.0, 0.0)
    sub = lax.broadcasted_iota(jnp.int32, (8, LANES), 0)
    prefix = jnp.dot(jnp.where(sub == 0, nblk, used).astype(BF16), before, preferred_element_type=F32)
    bstart = prefix[0:1]
    ordinal = prefix[1:2]
    bend = bstart + nblk
    n_used = jnp.max(bend, axis=-1, keepdims=True)
    row = lax.broadcasted_iota(jnp.int32, (PLAN_COLS, LANES), 0).astype(F32)
    lane = lax.broadcasted_iota(jnp.int32, (PLAN_COLS, LANES), 1).astype(F32)
    is_exp = lane < N_EXPERTS
    row1 = row[:, 0:1]
    done = jnp.logical_and(bend <= row, is_exp)
    blk_exp = jnp.minimum(jnp.sum(jnp.where(done, 1.0, 0.0), axis=-1, keepdims=True), N_EXPERTS - 1.0)
    mine = lane == blk_exp
    bstart_of = jnp.sum(jnp.where(mine, bstart, 0.0), axis=-1, keepdims=True)
    cnt_of = jnp.sum(jnp.where(mine, cnt, 0.0), axis=-1, keepdims=True)
    ord_of = jnp.sum(jnp.where(mine, ordinal, 0.0), axis=-1, keepdims=True)
    valid = row1 < n_used
    first = jnp.where(jnp.logical_and(valid, row1 == bstart_of), ord_of + 1.0, 0.0)
    nvalid = jnp.clip(cnt_of - EXPERT_BLOCK * (row1 - bstart_of), 0.0, float(EXPERT_BLOCK))
    nvalid = jnp.where(valid, nvalid, 0.0)
    usable = jnp.logical_and(nblk > 0.0, is_exp)

    def next_used(after):
        nxt = jnp.min(jnp.where(jnp.logical_and(lane > after, usable), lane, 999.0), axis=-1, keepdims=True)
        return jnp.where(nxt > 998.0, -1.0, nxt)

    nxt = next_used(blk_exp)
    nxt2 = jnp.where(nxt < 0.0, -1.0, next_used(nxt))
    bstart_col = jnp.sum(jnp.where(lane < row, nblk, 0.0), axis=-1, keepdims=True)
    cnt_col = jnp.sum(jnp.where(lane == row, cnt, 0.0), axis=-1, keepdims=True)
    tile = jnp.zeros((PLAN_COLS, LANES), F32)
    cols = {PLAN_EXP: blk_exp, PLAN_FIRST: first, PLAN_NEXT: nxt, PLAN_NUSED: n_used,
            PLAN_NVALID: nvalid, PLAN_BSTART: bstart_col, PLAN_COUNT: cnt_col, PLAN_NEXT2: nxt2}
    for k, val in cols.items():
        tile = jnp.where(lane == k, val, tile)
    return tile


def _router_kernel(x_ref, g_ref, sh_ref, sc_ref, wr_ref, br_ref, h_ref, info_ref, idx_ref, plan_ref, carry_ref):
    i = pl.program_id(0)
    tm = x_ref.shape[0]

    @pl.when(i == 0)
    def _():
        carry_ref[...] = jnp.zeros(carry_ref.shape, F32)

    h = _norm_mod(x_ref[...], g_ref[...], sh_ref[...], sc_ref[...])
    _store_row_tiles(h_ref, _pack_halves(h))
    w = wr_ref[...]
    w_hi = w.astype(BF16)
    w_lo = (w - w_hi.astype(F32)).astype(BF16)
    h_hi = h.astype(BF16)
    h_lo = (h - h_hi.astype(F32)).astype(BF16)
    lg = (jnp.dot(h_hi, w_hi, preferred_element_type=F32)
          + jnp.dot(h_lo, w_hi, preferred_element_type=F32)
          + jnp.dot(h_hi, w_lo, preferred_element_type=F32)) + br_ref[...]

    lane = lax.broadcasted_iota(jnp.int32, (tm, LANES), 1).astype(F32)
    no_lane = float(LANES)
    gl = jnp.where(lane < N_GROUPS, lg, NEG_BIG)
    gmax = jnp.max(gl, axis=-1, keepdims=True)
    g_idx = jnp.min(jnp.where(gl == gmax, lane, no_lane), axis=-1, keepdims=True)
    g_p = 1.0 / jnp.sum(jnp.exp(gl - gmax), axis=-1, keepdims=True)
    lo_lane = N_GROUPS + g_idx * EXPERTS_PER_GROUP
    in_grp = jnp.logical_and(lane >= lo_lane, lane < lo_lane + EXPERTS_PER_GROUP)
    ev = jnp.where(in_grp, lg, NEG_BIG)
    v1 = jnp.max(ev, axis=-1, keepdims=True)
    i1 = jnp.min(jnp.where(ev == v1, lane, no_lane), axis=-1, keepdims=True)
    ev2 = jnp.where(lane == i1, NEG_BIG, ev)
    v2 = jnp.max(ev2, axis=-1, keepdims=True)
    i2 = jnp.min(jnp.where(ev2 == v2, lane, no_lane), axis=-1, keepdims=True)
    e2 = jnp.exp(v2 - v1)
    den = 1.0 + e2
    w1 = (1.0 / den) * g_p
    w2 = (e2 / den) * g_p
    id1 = i1 - N_GROUPS
    id2 = i2 - N_GROUPS

    oh1 = lane == id1
    oh2 = lane == id2
    both = jnp.where(jnp.logical_or(oh1, oh2), 1.0, 0.0)
    r = lax.broadcasted_iota(jnp.int32, (tm, tm), 0)
    c = lax.broadcasted_iota(jnp.int32, (tm, tm), 1)
    tril = jnp.where(c < r, 1.0, 0.0).astype(BF16)
    before = jnp.dot(tril, both.astype(BF16), preferred_element_type=F32) + carry_ref[...]
    rank1 = jnp.sum(jnp.where(oh1, before, 0.0), axis=-1, keepdims=True)
    rank2 = jnp.sum(jnp.where(oh2, before, 0.0), axis=-1, keepdims=True)
    carry_ref[...] = carry_ref[...] + jnp.sum(both, axis=0, keepdims=True)

    info = jnp.zeros((tm, LANES), F32)
    for col, val in enumerate((id1, id2, rank1, rank2, w1, w2)):
        info = jnp.where(lane == col, val, info)
    info_ref[...] = info
    idx_ref[...] = jnp.transpose(info)[0:8].astype(jnp.int32)

    @pl.when(i == pl.num_programs(0) - 1)
    def _():
        plan = _dispatch_plan_tile(carry_ref[...])
        plan_ref[...] = jnp.transpose(plan)[0:8].astype(jnp.int32)


def _router(x2, g, mod, seq, wr, br, tm=256):
    assert x2.shape[1] == 2 * ROW_TILE * LANES
    t, d = x2.shape
    per = seq // tm
    return pl.pallas_call(
        _router_kernel,
        grid=(t // tm,),
        in_specs=[
            pl.BlockSpec((tm, d), lambda i: (i, 0)),
            pl.BlockSpec((1, d), lambda i: (0, 0)),
            pl.BlockSpec((None, 1, d), lambda i: ((i // per) * 6 + 3, 0, 0)),
            pl.BlockSpec((None, 1, d), lambda i: ((i // per) * 6 + 4, 0, 0)),
            pl.BlockSpec((d, LANES), lambda i: (0, 0)),
            pl.BlockSpec((1, LANES), lambda i: (0, 0)),
        ],
        out_specs=(
            pl.BlockSpec((tm * ROW_TILE, LANES), lambda i: (i, 0)),
            pl.BlockSpec((tm, LANES), lambda i: (i, 0)),
            pl.BlockSpec((8, tm), lambda i: (0, i)),
            pl.BlockSpec((8, PLAN_COLS), lambda i: (0, 0)),
        ),
        out_shape=(
            jax.ShapeDtypeStruct((t * ROW_TILE, LANES), jnp.uint32),
            jax.ShapeDtypeStruct((t, LANES), F32),
            jax.ShapeDtypeStruct((8, t), jnp.int32),
            jax.ShapeDtypeStruct((8, PLAN_COLS), jnp.int32),
        ),
        scratch_shapes=[pltpu.VMEM((1, LANES), F32)],
        compiler_params=_cparams(("arbitrary",)),
        name="moe_router",
    )(x2, g.reshape(1, d), mod, mod, wr, br)


GATHER_GROUP = 8


def _slot_of(plan, idx, n_tok, k, t):
    e = idx[k * n_tok + t]
    return plan[PLAN_BSTART * PLAN_COLS + e] * EXPERT_BLOCK + idx[(2 + k) * n_tok + t]


def _expert_kernel(plan, idx, h_hbm, wg_hbm, wu_hbm, wd_hbm, ys_ref,
                   row_tok, xbuf, xsem, sg, su, sd, wsem, wgb, wub, wdb, *, layer, n_tok):
    i = pl.program_id(0)
    nu = plan[PLAN_NUSED * PLAN_COLS]
    n_rows = row_tok.shape[0]

    def weight_copies(e, st):
        return (pltpu.make_async_copy(wg_hbm.at[layer, e], sg.at[st], wsem.at[st, 0]),
                pltpu.make_async_copy(wu_hbm.at[layer, e], su.at[st], wsem.at[st, 1]),
                pltpu.make_async_copy(wd_hbm.at[layer, e], sd.at[st], wsem.at[st, 2]))

    def n_groups(blk):
        return (plan[PLAN_NVALID * PLAN_COLS + blk] + GATHER_GROUP - 1) // GATHER_GROUP

    def start_gather(blk, slot):
        base = blk * EXPERT_BLOCK

        def body(g, carry):
            for k in range(GATHER_GROUP):
                r = g * GATHER_GROUP + k
                tok = row_tok[base + r]
                src = pl.multiple_of(tok * ROW_TILE, ROW_TILE)
                dst = pl.multiple_of(r * ROW_TILE, ROW_TILE)
                pltpu.make_async_copy(h_hbm.at[pl.ds(src, ROW_TILE), :],
                                      xbuf.at[slot, pl.ds(dst, ROW_TILE), :], xsem.at[slot]).start()
            return carry

        lax.fori_loop(0, n_groups(blk), body, 0)

    def wait_gather(blk, slot):
        span = GATHER_GROUP * ROW_TILE

        def body(g, carry):
            pltpu.make_async_copy(h_hbm.at[pl.ds(0, span), :],
                                  xbuf.at[slot, pl.ds(0, span), :], xsem.at[slot]).wait()
            return carry

        lax.fori_loop(0, n_groups(blk), body, 0)

    @pl.when(i == 0)
    def _():
        for cp in weight_copies(plan[PLAN_EXP * PLAN_COLS], 0):
            cp.start(priority=1)
        second = plan[PLAN_NEXT * PLAN_COLS]

        @pl.when(second >= 0)
        def _():
            for cp in weight_copies(second, 1):
                cp.start(priority=1)

        xbuf[...] = jnp.zeros(xbuf.shape, xbuf.dtype)

        def pad_body(e, carry):
            end = plan[PLAN_BSTART * PLAN_COLS + e] * EXPERT_BLOCK + plan[PLAN_COUNT * PLAN_COLS + e]
            for k in range(GATHER_GROUP - 1):
                row_tok[jnp.minimum(end + k, n_rows - 1)] = 0
            return carry

        lax.fori_loop(0, N_EXPERTS, pad_body, 0)

        def fill_body(t, carry):
            row_tok[_slot_of(plan, idx, n_tok, 0, t)] = t
            row_tok[_slot_of(plan, idx, n_tok, 1, t)] = t
            return carry

        lax.fori_loop(0, n_tok, fill_body, 0, unroll=8)
        start_gather(0, 0)

    @pl.when(i < nu)
    def _():
        slot = i % 2

        @pl.when(i + 1 < nu)
        def _():
            start_gather(i + 1, 1 - slot)

        first = plan[PLAN_FIRST * PLAN_COLS + i]

        @pl.when(first > 0)
        def _():
            st = (first - 1) % 2
            nxt2 = plan[PLAN_NEXT2 * PLAN_COLS + i]
            cps = weight_copies(plan[PLAN_EXP * PLAN_COLS + i], st)
            nxt_cps = weight_copies(jnp.maximum(nxt2, 0), st)
            for cp, ncp, stage, dst in zip(cps, nxt_cps, (sg, su, sd), (wgb, wub, wdb)):
                cp.wait()
                dst[...] = stage[st].astype(BF16)

                @pl.when(nxt2 >= 0)
                def _():
                    ncp.start(priority=1)

        wait_gather(i, slot)
        parts = [_unpack_halves(w) for w in _load_row_tiles(xbuf.at[slot], EXPERT_BLOCK)]
        x_lo = jnp.concatenate([lo.astype(BF16) for lo, _ in parts], axis=1)
        x_hi = jnp.concatenate([hi.astype(BF16) for _, hi in parts], axis=1)
        half = wgb.shape[0] // 2
        g = (jnp.dot(x_lo, wgb[:half], preferred_element_type=F32)
             + jnp.dot(x_hi, wgb[half:], preferred_element_type=F32))
        u = (jnp.dot(x_lo, wub[:half], preferred_element_type=F32)
             + jnp.dot(x_hi, wub[half:], preferred_element_type=F32))
        hid = (_silu(g) * u).astype(BF16)
        _store_row_tiles(ys_ref, _pack_halves(jnp.dot(hid, wdb[...], preferred_element_type=F32)))

    @pl.when(i >= nu)
    def _():
        ys_ref[...] = jnp.zeros(ys_ref.shape, ys_ref.dtype)


def _expert_ffn(h2, w_gate, w_up, w_down, layer, plan, idx, n_blocks):
    t = h2.shape[0] // ROW_TILE
    d, f = w_gate.shape[2], w_gate.shape[3]
    n_rows = n_blocks * EXPERT_BLOCK
    grid_spec = pltpu.PrefetchScalarGridSpec(
        num_scalar_prefetch=2,
        grid=(n_blocks,),
        in_specs=[pl.BlockSpec(memory_space=pl.ANY)] * 4,
        out_specs=pl.BlockSpec((EXPERT_BLOCK * ROW_TILE, LANES), lambda i, *_: (i, 0)),
        scratch_shapes=[
            pltpu.SMEM((n_rows,), jnp.int32),
            pltpu.VMEM((2, EXPERT_BLOCK * ROW_TILE, LANES), jnp.uint32),
            pltpu.SemaphoreType.DMA((2,)),
            pltpu.VMEM((2, d, f), F32),
            pltpu.VMEM((2, d, f), F32),
            pltpu.VMEM((2, f, d), F32),
            pltpu.SemaphoreType.DMA((2, 3)),
            pltpu.VMEM((d, f), BF16),
            pltpu.VMEM((d, f), BF16),
            pltpu.VMEM((f, d), BF16),
        ],
    )
    weight_bytes = 3 * d * f * (2 * 4 + 2)
    vmem = weight_bytes + 8 * 1024 * 1024
    return pl.pallas_call(
        functools.partial(_expert_kernel, layer=layer, n_tok=t),
        grid_spec=grid_spec,
        out_shape=jax.ShapeDtypeStruct((n_rows * ROW_TILE, LANES), jnp.uint32),
        compiler_params=_cparams(("arbitrary",), vmem),
        name="moe_experts",
    )(plan, idx, h2, w_gate, w_up, w_down)


def _combine_kernel(plan, idx, x_ref, info_ref, gate_ref, ys_hbm, *rest, tm, n_tok, with_next):
    if with_next:
        ng_ref, nsh_ref, nsc_ref, o_ref, h_ref, buf, sem = rest
    else:
        o_ref, buf, sem = rest
    i = pl.program_id(0)
    nsteps = pl.num_programs(0)

    def start_gather(step, slot):
        base = step * tm

        def body(r, carry):
            dst = pl.multiple_of(r * ROW_TILE, ROW_TILE)
            for k in range(2):
                src = pl.multiple_of(_slot_of(plan, idx, n_tok, k, base + r) * ROW_TILE, ROW_TILE)
                pltpu.make_async_copy(ys_hbm.at[pl.ds(src, ROW_TILE), :],
                                      buf.at[slot, k, pl.ds(dst, ROW_TILE), :], sem.at[slot]).start(priority=k)
            return carry

        lax.fori_loop(0, tm, body, 0, unroll=8)

    @pl.when(i == 0)
    def _():
        start_gather(0, 0)

    slot = i % 2

    @pl.when(i + 1 < nsteps)
    def _():
        start_gather(i + 1, 1 - slot)

    for k in range(2):
        pltpu.make_async_copy(ys_hbm.at[pl.ds(0, tm * ROW_TILE), :], buf.at[slot, k], sem.at[slot]).wait()
    info = info_ref[...]
    w0 = info[:, 4:5]
    w1 = info[:, 5:6]
    half = x_ref.shape[1] // 2
    a_tiles = _load_row_tiles(buf.at[slot, 0], tm)
    b_tiles = _load_row_tiles(buf.at[slot, 1], tm)
    for s in range(ROW_TILE):
        a_lo, a_hi = _unpack_halves(a_tiles[s])
        b_lo, b_hi = _unpack_halves(b_tiles[s])
        lo = slice(s * LANES, (s + 1) * LANES)
        hi = slice(half + s * LANES, half + (s + 1) * LANES)
        o_ref[:, lo] = x_ref[:, lo] + gate_ref[:, lo] * (w0 * a_lo + w1 * b_lo)
        o_ref[:, hi] = x_ref[:, hi] + gate_ref[:, hi] * (w0 * a_hi + w1 * b_hi)
    if with_next:
        h_ref[...] = _norm_mod(o_ref[...], ng_ref[...], nsh_ref[...], nsc_ref[...]).astype(h_ref.dtype)


def _combine(x2, info, mod, seq, ys, plan, idx, next_norm=None, tm=128):
    t, d = x2.shape
    per = seq // tm
    row = lambda i, *_: (i, 0)
    mod_row = lambda k: (lambda i, *_: ((i // per) * 6 + k, 0, 0))
    in_specs = [
        pl.BlockSpec((tm, d), row),
        pl.BlockSpec((tm, LANES), row),
        pl.BlockSpec((None, 1, d), mod_row(5)),
        pl.BlockSpec(memory_space=pl.ANY),
    ]
    args = [plan, idx, x2, info, mod, ys]
    out_specs = pl.BlockSpec((tm, d), row)
    out_shape = jax.ShapeDtypeStruct((t, d), F32)
    if next_norm is not None:
        next_g, next_mod = next_norm
        in_specs += [pl.BlockSpec((1, d), lambda i, *_: (0, 0)),
                     pl.BlockSpec((None, 1, d), mod_row(0)), pl.BlockSpec((None, 1, d), mod_row(1))]
        args += [next_g.reshape(1, d), next_mod, next_mod]
        out_specs = (out_specs, pl.BlockSpec((tm, d), row))
        out_shape = (out_shape, jax.ShapeDtypeStruct((t, d), BF16))
    grid_spec = pltpu.PrefetchScalarGridSpec(
        num_scalar_prefetch=2,
        grid=(t // tm,),
        in_specs=in_specs,
        out_specs=out_specs,
        scratch_shapes=[
            pltpu.VMEM((2, 2, tm * ROW_TILE, LANES), jnp.uint32),
            pltpu.SemaphoreType.DMA((2,)),
        ],
    )
    return pl.pallas_call(
        functools.partial(_combine_kernel, tm=tm, n_tok=t, with_next=next_norm is not None),
        grid_spec=grid_spec,
        out_shape=out_shape,
        compiler_params=_cparams(("arbitrary",)),
        name="moe_combine",
    )(*args)


def _hier_moe(x2, norm_g, mod, seq, w_rg, b_rg, w_re, b_re, w_gate, w_up, w_down, layer, next_norm=None):
    t, d = x2.shape
    pad = LANES - N_GROUPS - N_EXPERTS
    wr = jnp.concatenate([w_rg, w_re, jnp.zeros((d, pad), F32)], axis=1)
    br = jnp.concatenate([b_rg, b_re, jnp.zeros((pad,), F32)]).reshape(1, LANES)
    h2, info, idx, plan = _router(x2, norm_g, mod, seq, wr, br)
    n_assign = 2 * t
    n_blocks = (n_assign + N_EXPERTS * (EXPERT_BLOCK - 1) + EXPERT_BLOCK - 1) // EXPERT_BLOCK
    assert n_blocks <= PLAN_COLS
    plan = plan.reshape(-1)
    idx = idx.reshape(-1)
    ys = _expert_ffn(h2, w_gate, w_up, w_down, layer, plan, idx, n_blocks)
    return _combine(x2, info, mod, seq, ys, plan, idx, next_norm)


def kernel(x, c, ada_w, ada_b, norm1_g, norm2_g, dsa_w_in, dsa_q_gain, dsa_k_gain, dsa_w_out, mla_w_in, mla_cq_gain, mla_ckv_gain, mla_w_q_up, mla_w_kv_up, mla_q_gain, mla_k_gain, mla_w_out, router_group_w, router_group_b, router_expert_w, router_expert_b, expert_w_gate, expert_w_up, expert_w_down):
    b, s, d = x.shape
    t = b * s
    mods = _ada_mod(c, ada_w, ada_b)
    x2 = x.reshape(t, d)

    mod = mods[0]
    h = _normmod(x2, norm1_g[0], mod, s, 0, 1)
    gains = (dsa_q_gain[0], dsa_k_gain[0], dsa_k_gain[0])
    qkv = [[_dsa_proj(h, dsa_w_in[0], gains[w], g, w, b) for w in range(3)] for g in range(len(DIL_PAIRS))]
    og2 = _dsa_attn(*qkv[2], 2)
    og1 = _dsa_attn(*qkv[1], 1)
    o = _dsa_attn(*qkv[0], 0, others=(og1, og2))
    x2 = _resid_mm(o.reshape(t, DSA_WIDTH), dsa_w_out[0], x2, mod, s, 2)
    x2, h = _hier_moe(x2, norm2_g[0], mod, s, router_group_w[0], router_group_b[0],
                      router_expert_w[0], router_expert_b[0],
                      expert_w_gate, expert_w_up, expert_w_down, 0, next_norm=(norm1_g[1], mods[1]))

    mod = mods[1]
    cq, ckv, kpe = _mla_in(h, mla_w_in[0], mla_cq_gain[0], mla_ckv_gain[0])
    tab = _rope_tables(s)
    w_q_pad = jnp.pad(mla_w_q_up[0].reshape(MLA_Q_LORA, MLA_HEADS, MLA_QK),
                      ((0, 0), (0, 0), (0, MLA_QK_PAD - MLA_QK))).reshape(MLA_Q_LORA, MLA_HEADS * MLA_QK_PAD)
    q = _mla_q(cq, w_q_pad, mla_q_gain[0], tab, s)
    k, v = _mla_kv(ckv, mla_w_kv_up[0], kpe, mla_k_gain[0], tab, s)
    o = _mla_attn(q.reshape(b, s, -1), k.reshape(b, s, -1), v.reshape(b, s, -1))
    x2 = _resid_mm(o.reshape(t, MLA_HEADS * MLA_V), mla_w_out[0], x2, mod, s, 2)
    x2 = _hier_moe(x2, norm2_g[1], mod, s, router_group_w[1], router_group_b[1],
                   router_expert_w[1], router_expert_b[1],
                   expert_w_gate, expert_w_up, expert_w_down, 1)
    return x2.reshape(b, s, d)
```

```python
import functools
import math

import jax
import jax.numpy as jnp
import numpy as np
from jax import lax
from jax.experimental import pallas as pl
from jax.experimental.pallas import tpu as pltpu

F32 = jnp.float32
BF16 = jnp.bfloat16

D_MODEL = 2048
EPS = 1e-6
LANES = 128
NEG_BIG = -1e30

DIL_PAIRS = ((128, 1), (512, 4), (2048, 16))
DSA_HEADS = 8
DSA_HEAD_DIM = 128
DSA_WIDTH = DSA_HEADS * DSA_HEAD_DIM
BAND = 128

MLA_HEADS = 16
MLA_Q_LORA = 512
MLA_KV_LORA = 512
MLA_NOPE = 128
MLA_ROPE = 64
MLA_V = 128
MLA_QK = MLA_NOPE + MLA_ROPE
MLA_QK_PAD = 256
ROPE_THETA = 10000.0

N_GROUPS = 4
EXPERTS_PER_GROUP = 16
N_EXPERTS = N_GROUPS * EXPERTS_PER_GROUP
D_EXPERT = 768
EXPERT_BLOCK = 128

VMEM_LIMIT = 48 * 1024 * 1024


def _cparams(sem, vmem=VMEM_LIMIT):
    return pltpu.CompilerParams(dimension_semantics=sem, vmem_limit_bytes=vmem)


def _silu(x):
    return x * (1.0 / (1.0 + jnp.exp(-x)))


def _pack_halves(x):
    n = x.shape[1] // 2
    xb = x.astype(BF16).astype(F32)
    lo = lax.bitcast_convert_type(xb[:, :n], jnp.uint32) >> 16
    hi = lax.bitcast_convert_type(xb[:, n:], jnp.uint32) & jnp.uint32(0xFFFF0000)
    return hi | lo


def _unpack_halves(w):
    lo = lax.bitcast_convert_type(w << 16, F32)
    hi = lax.bitcast_convert_type(w & jnp.uint32(0xFFFF0000), F32)
    return lo, hi


ROW_TILE = 8


def _store_row_tiles(ref, words):
    rows = words.shape[0]
    for s in range(ROW_TILE):
        ref[pl.ds(s, rows, stride=ROW_TILE), :] = words[:, s * LANES:(s + 1) * LANES]


def _load_row_tiles(ref, rows):
    return [ref[pl.ds(s, rows, stride=ROW_TILE), :] for s in range(ROW_TILE)]


def _norm_mod(x, g, shift, scale):
    ms = jnp.mean(x * x, axis=-1, keepdims=True)
    y = x * lax.rsqrt(ms + EPS) * g
    return y * (1.0 + scale) + shift


def _ada_kernel(c_ref, w_ref, b_ref, o_ref):
    ca = _silu(c_ref[...])
    hi = ca.astype(BF16)
    lo = (ca - hi.astype(F32)).astype(BF16)
    lhs = jnp.concatenate([hi, lo], axis=0)
    res = jnp.dot(lhs, w_ref[...].astype(BF16), preferred_element_type=F32)
    o_ref[...] = res[:8] + res[8:] + b_ref[...]


def _ada_mod(c, ada_w, ada_b):
    depth, d, n = ada_w.shape
    b = c.shape[0]
    c8 = jnp.pad(c, ((0, 8 - b), (0, 0)))
    tn = 1024
    out = pl.pallas_call(
        _ada_kernel,
        grid=(depth, n // tn),
        in_specs=[
            pl.BlockSpec((8, d), lambda i, j: (0, 0)),
            pl.BlockSpec((None, d, tn), lambda i, j: (i, 0, j)),
            pl.BlockSpec((None, 1, tn), lambda i, j: (i, 0, j)),
        ],
        out_specs=pl.BlockSpec((None, 8, tn), lambda i, j: (i, 0, j)),
        out_shape=jax.ShapeDtypeStruct((depth, 8, n), F32),
        compiler_params=_cparams(("arbitrary", "arbitrary")),
        name="ada_mod",
    )(c8, ada_w, ada_b.reshape(depth, 1, n))
    return out[:, :b].reshape(depth, b * 6, 1, d)


def _normmod_kernel(x_ref, g_ref, sh_ref, sc_ref, o_ref):
    o_ref[...] = _norm_mod(x_ref[...], g_ref[...], sh_ref[...], sc_ref[...]).astype(o_ref.dtype)


def _normmod(x2, g, mod, seq, k_shift, k_scale, tm=512):
    t, d = x2.shape
    per = seq // tm
    return pl.pallas_call(
        _normmod_kernel,
        grid=(t // tm,),
        in_specs=[
            pl.BlockSpec((tm, d), lambda i: (i, 0)),
            pl.BlockSpec((1, d), lambda i: (0, 0)),
            pl.BlockSpec((None, 1, d), lambda i: ((i // per) * 6 + k_shift, 0, 0)),
            pl.BlockSpec((None, 1, d), lambda i: ((i // per) * 6 + k_scale, 0, 0)),
        ],
        out_specs=pl.BlockSpec((tm, d), lambda i: (i, 0)),
        out_shape=jax.ShapeDtypeStruct((t, d), BF16),
        compiler_params=_cparams(("arbitrary",)),
        name="normmod",
    )(x2, g.reshape(1, d), mod, mod)


def _cast_weight_once(w_ref, wb_ref):
    @pl.when(pl.program_id(1) == 0)
    def _():
        wb_ref[...] = w_ref[...].astype(BF16)


def _dsa_proj_kernel(a_ref, w_ref, g_ref, o_ref, wb_ref, res_ref, *, dilation, normed, gain_scale):
    @pl.when(pl.program_id(0) == 0)
    def _():
        wb_ref[...] = w_ref[...].astype(BF16)

    gain = g_ref[...] * gain_scale
    tm = a_ref.shape[0]
    n_parts = 2
    pr = tm // n_parts
    accs = [jnp.dot(a_ref[p * pr:(p + 1) * pr, :], wb_ref[...], preferred_element_type=F32)
            for p in range(n_parts)]
    for p, acc in enumerate(accs):
        blks = []
        for h in range(DSA_HEADS):
            blk = acc[:, h * LANES:(h + 1) * LANES]
            if normed:
                ms = jnp.mean(blk * blk, axis=-1, keepdims=True)
                blk = blk * lax.rsqrt(ms + EPS) * gain
            blks.append(blk)
        for h, blk in enumerate(blks):
            hs = slice(h * LANES, (h + 1) * LANES)
            if dilation == 1:
                o_ref[0, p * pr:(p + 1) * pr, hs] = blk.astype(o_ref.dtype)
            else:
                res_ref[h, p * pr:(p + 1) * pr, :] = blk
                sub_rows = pr // dilation
                for r in range(dilation):
                    sub = res_ref[h, pl.ds(p * pr + r, sub_rows, stride=dilation), :]
                    o_ref[r, p * sub_rows:(p + 1) * sub_rows, hs] = sub.astype(o_ref.dtype)


def _dsa_proj(h, w_in, gain, group, which, batch, tm=512, tn=DSA_WIDTH):
    t, d = h.shape
    dil = DIL_PAIRS[group][1]
    seq = t // batch
    per = seq // tm
    gain_scale = 1.0 / math.sqrt(DSA_HEAD_DIM) if which == 0 else 1.0
    return pl.pallas_call(
        functools.partial(_dsa_proj_kernel, dilation=dil, normed=which != 2, gain_scale=gain_scale),
        grid=(t // tm,),
        in_specs=[
            pl.BlockSpec((tm, d), lambda i: (i, 0)),
            pl.BlockSpec((d, tn), lambda i: (0, group * 3 + which)),
            pl.BlockSpec((1, LANES), lambda i: (0, 0)),
        ],
        out_specs=pl.BlockSpec((None, dil, tm // dil, tn), lambda i: (i // per, 0, i % per, 0)),
        out_shape=jax.ShapeDtypeStruct((batch, dil, seq // dil, tn), BF16),
        scratch_shapes=[pltpu.VMEM((d, tn), BF16), pltpu.VMEM((DSA_HEADS, tm, LANES), F32)],
        compiler_params=_cparams(("arbitrary",)),
        name=f"dsa_proj_g{group}_{'qkv'[which]}",
    )(h, w_in, gain.reshape(1, LANES))


def _resid_mm_kernel(a_ref, w_ref, x_ref, gate_ref, o_ref, wb_ref):
    _cast_weight_once(w_ref, wb_ref)
    y = jnp.dot(a_ref[...], wb_ref[...], preferred_element_type=F32)
    o_ref[...] = x_ref[...] + gate_ref[...] * y


def _resid_mm(a, w, x2, mod, seq, k_gate, tm=512, tn=1024):
    t, k = a.shape
    n = w.shape[1]
    per = seq // tm
    return pl.pallas_call(
        _resid_mm_kernel,
        grid=(n // tn, t // tm),
        in_specs=[
            pl.BlockSpec((tm, k), lambda j, i: (i, 0)),
            pl.BlockSpec((k, tn), lambda j, i: (0, j)),
            pl.BlockSpec((tm, tn), lambda j, i: (i, j)),
            pl.BlockSpec((None, 1, tn), lambda j, i: ((i // per) * 6 + k_gate, 0, j)),
        ],
        out_specs=pl.BlockSpec((tm, tn), lambda j, i: (i, j)),
        out_shape=jax.ShapeDtypeStruct((t, n), F32),
        scratch_shapes=[pltpu.VMEM((k, tn), BF16)],
        compiler_params=_cparams(("arbitrary", "arbitrary")),
        name="resid_mm",
    )(a, w, x2, mod)


def _alibi_slope(head_slot, group):
    n = len(DIL_PAIRS) * DSA_HEADS
    return 2.0 ** (-8.0 * (head_slot * len(DIL_PAIRS) + group + 1.0) / n)


def _dsa_bias_table(group, has_prev):
    window, d = DIL_PAIRS[group]
    steps = window // d
    qi = np.arange(BAND)[:, None]
    if has_prev:
        kj = np.arange(2 * BAND)[None, :]
        delta = qi + BAND - kj
        prev_key = np.broadcast_to(kj < BAND, delta.shape)
    else:
        kj = np.arange(BAND)[None, :]
        delta = qi - kj
        prev_key = np.zeros(delta.shape, bool)
    inside = (delta >= 0) & (delta <= steps)
    tabs = []
    for first in (True, False):
        valid = inside & ~(prev_key & first)
        per_head = [np.where(valid, -_alibi_slope(h, group) * (delta * d), NEG_BIG) for h in range(DSA_HEADS)]
        tabs.append(np.stack(per_head))
    return jnp.asarray(np.stack(tabs), F32)


def _dsa_attn_kernel(*refs, dilation, has_prev, merge):
    refs = list(refs)
    q_ref, kc_ref, vc_ref, bias_ref = refs[:4]
    pos = 4
    if has_prev:
        kp_ref, vp_ref = refs[pos:pos + 2]
        pos += 2
    if merge:
        other = refs[pos:pos + 4]
        pos += 4
        o_ref = refs[pos]
    else:
        o_ref, lse_ref = refs[pos:pos + 2]

    r = pl.program_id(2)
    lane = lax.broadcasted_iota(jnp.int32, (BAND, LANES), 1)
    nt = (((1,), (1,)), ((), ()))
    nk = 2 * BAND if has_prev else BAND
    ones = jnp.ones((nk, LANES), BF16)
    n_batch = q_ref.shape[0]

    def head_scores(bi, h):
        hs = slice(h * LANES, (h + 1) * LANES)
        q = q_ref[bi, :, hs]
        if has_prev:
            k = jnp.concatenate([kp_ref[bi, :, hs], kc_ref[bi, :, hs]], axis=0)
        else:
            k = kc_ref[bi, :, hs]
        return lax.dot_general(q, k, nt, preferred_element_type=F32) + bias_ref[h]

    def head_values(bi, h):
        hs = slice(h * LANES, (h + 1) * LANES)
        if has_prev:
            v = jnp.concatenate([vp_ref[bi, :, hs], vc_ref[bi, :, hs]], axis=0)
        else:
            v = vc_ref[bi, :, hs]
        return jnp.concatenate([v, ones], axis=1)

    results = []
    for bi in range(n_batch):
        scores = [head_scores(bi, h) for h in range(DSA_HEADS)]
        maxes = [jnp.max(s, axis=-1, keepdims=True) for s in scores]
        probs = [jnp.exp(s - m).astype(BF16) for s, m in zip(scores, maxes)]
        accs = [jnp.dot(p, head_values(bi, h), preferred_element_type=F32) for h, p in enumerate(probs)]
        outs = []
        lse_tile = jnp.zeros((BAND, LANES), F32)
        for h in range(DSA_HEADS):
            l = accs[h][:, LANES:]
            o = accs[h][:, :LANES] / l
            lse = maxes[h] + jnp.log(l)
            if merge:
                lses = [lse] + [other[2 * g + 1][bi, :, h * 16:h * 16 + 1] for g in range(2)]
                parts = [o] + [other[2 * g][bi, h] for g in range(2)]
                top = jnp.maximum(jnp.maximum(lses[0], lses[1]), lses[2])
                es = [jnp.exp(x - top) for x in lses]
                den = es[0] + es[1] + es[2]
                o = ((es[0] * parts[0] + es[1] * parts[1] + es[2] * parts[2]) / den).astype(o_ref.dtype)
            else:
                in_head = jnp.logical_and(lane >= h * 16, lane < (h + 1) * 16)
                lse_tile = jnp.where(in_head, lse, lse_tile)
            outs.append(o)
        results.append((outs, lse_tile))

    for bi, (outs, lse_tile) in enumerate(results):
        for h, o in enumerate(outs):
            if merge:
                o_ref[bi, :, h * LANES:(h + 1) * LANES] = o
            else:
                o_ref[bi, h, pl.ds(r, BAND, stride=dilation), :] = o
        if not merge:
            lse_ref[bi, pl.ds(r, BAND, stride=dilation), :] = lse_tile


def _dsa_attn(q_g, k_g, v_g, group, others=None):
    b, d, sub_len, _ = q_g.shape
    s = d * sub_len
    nb = sub_len // BAND
    has_prev = nb > 1
    merge = others is not None
    assert not merge or d == 1
    bias = _dsa_bias_table(group, has_prev)

    bb = 2 if b % 2 == 0 else 1
    cur = lambda bi, n, r: (bi, r, n, 0)
    prev = lambda bi, n, r: (bi, r, jnp.maximum(n - 1, 0), 0)
    blk = (bb, None, BAND, DSA_WIDTH)
    in_specs = [pl.BlockSpec(blk, cur), pl.BlockSpec(blk, cur), pl.BlockSpec(blk, cur),
                pl.BlockSpec((None,) + bias.shape[1:], lambda bi, n, r: (jnp.minimum(n, 1), 0, 0, 0))]
    args = [q_g, k_g, v_g, bias]
    if has_prev:
        in_specs += [pl.BlockSpec(blk, prev), pl.BlockSpec(blk, prev)]
        args += [k_g, v_g]
    span = BAND * d
    nat = lambda bi, n, r: (bi, n, 0)
    nat_heads = lambda bi, n, r: (bi, 0, n, 0)
    if merge:
        for o_g, lse_g in others:
            in_specs += [pl.BlockSpec((bb, DSA_HEADS, BAND, LANES), nat_heads),
                         pl.BlockSpec((bb, BAND, LANES), nat)]
            args += [o_g, lse_g]
        out_shape = jax.ShapeDtypeStruct((b, s, DSA_WIDTH), BF16)
        out_specs = pl.BlockSpec((bb, BAND, DSA_WIDTH), nat)
    else:
        out_shape = (jax.ShapeDtypeStruct((b, DSA_HEADS, s, LANES), F32), jax.ShapeDtypeStruct((b, s, LANES), F32))
        out_specs = (pl.BlockSpec((bb, DSA_HEADS, span, LANES), nat_heads),
                     pl.BlockSpec((bb, span, LANES), nat))
    return pl.pallas_call(
        functools.partial(_dsa_attn_kernel, dilation=d, has_prev=has_prev, merge=merge),
        grid=(b // bb, nb, d),
        in_specs=in_specs,
        out_specs=out_specs,
        out_shape=out_shape,
        compiler_params=_cparams(("arbitrary", "arbitrary", "arbitrary")),
        name=f"dsa_attn_g{group}",
    )(*args)


def _rope_tables(seq):
    half = MLA_ROPE // 2
    inv = ROPE_THETA ** (-jnp.arange(half, dtype=F32) / half)
    ang = jnp.arange(seq, dtype=F32)[:, None] * inv[None, :]
    cos, sin = jnp.cos(ang), jnp.sin(ang)
    z = jnp.zeros((seq, LANES - MLA_ROPE), F32)
    zh = jnp.zeros((seq, half), F32)
    cos_t = jnp.concatenate([cos, cos, z], axis=1)
    sin_a = jnp.concatenate([zh, sin, z], axis=1)
    sin_b = jnp.concatenate([-sin, zh, z], axis=1)
    return jnp.concatenate([cos_t, sin_a, sin_b], axis=1)


def _rope_lanes(x, tab):
    half = MLA_ROPE // 2
    cos_t = tab[:, 0:LANES]
    sin_a = tab[:, LANES:2 * LANES]
    sin_b = tab[:, 2 * LANES:3 * LANES]
    return (x * cos_t + pltpu.roll(x, half, 1) * sin_a
            + pltpu.roll(x, LANES - half, 1) * sin_b)


def _mla_in_kernel(a_ref, w_ref, cqg_ref, ckvg_ref, cq_ref, ckv_ref, kpe_ref, wb_ref):
    n = w_ref.shape[1]

    @pl.when(pl.program_id(0) == 0)
    def _():
        wb_ref[...] = jnp.zeros(wb_ref.shape, BF16)
        wb_ref[:, :n] = w_ref[...].astype(BF16)

    acc = jnp.dot(a_ref[...], wb_ref[...], preferred_element_type=F32)
    cq = acc[:, :MLA_Q_LORA]
    cq_ref[...] = (cq * lax.rsqrt(jnp.mean(cq * cq, axis=-1, keepdims=True) + EPS)
                   * cqg_ref[...]).astype(BF16)
    ckv = acc[:, MLA_Q_LORA:MLA_Q_LORA + MLA_KV_LORA]
    ckv_ref[...] = (ckv * lax.rsqrt(jnp.mean(ckv * ckv, axis=-1, keepdims=True) + EPS)
                    * ckvg_ref[...]).astype(BF16)
    kpe_ref[...] = acc[:, MLA_Q_LORA + MLA_KV_LORA:]


def _mla_in(h, w_in, cq_gain, ckv_gain, tm=512):
    t, d = h.shape
    n = w_in.shape[1]
    n_pad = MLA_Q_LORA + MLA_KV_LORA + LANES
    return pl.pallas_call(
        _mla_in_kernel,
        grid=(t // tm,),
        in_specs=[
            pl.BlockSpec((tm, d), lambda i: (i, 0)),
            pl.BlockSpec((d, n), lambda i: (0, 0)),
            pl.BlockSpec((1, MLA_Q_LORA), lambda i: (0, 0)),
            pl.BlockSpec((1, MLA_KV_LORA), lambda i: (0, 0)),
        ],
        out_specs=(
            pl.BlockSpec((tm, MLA_Q_LORA), lambda i: (i, 0)),
            pl.BlockSpec((tm, MLA_KV_LORA), lambda i: (i, 0)),
            pl.BlockSpec((tm, LANES), lambda i: (i, 0)),
        ),
        out_shape=(
            jax.ShapeDtypeStruct((t, MLA_Q_LORA), BF16),
            jax.ShapeDtypeStruct((t, MLA_KV_LORA), BF16),
            jax.ShapeDtypeStruct((t, LANES), F32),
        ),
        scratch_shapes=[pltpu.VMEM((d, n_pad), BF16)],
        compiler_params=_cparams(("arbitrary",)),
        name="mla_in",
    )(h, w_in, cq_gain.reshape(1, -1), ckv_gain.reshape(1, -1))


def _mla_q_kernel(a_ref, w_ref, g0_ref, g1_ref, tab_ref, o_ref, wb_ref):
    @pl.when(pl.program_id(0) == 0)
    def _():
        wb_ref[...] = w_ref[...].astype(BF16)

    acc = jnp.dot(a_ref[...], wb_ref[...], preferred_element_type=F32)
    tab = tab_ref[...]
    scale = 1.0 / math.sqrt(MLA_QK)
    g0 = g0_ref[...] * scale
    g1 = g1_ref[...] * scale
    for h in range(MLA_HEADS):
        base = h * MLA_QK_PAD
        x0 = acc[:, base:base + LANES]
        x1 = acc[:, base + LANES:base + 2 * LANES]
        ss = jnp.sum(x0 * x0, axis=-1, keepdims=True) + jnp.sum(x1 * x1, axis=-1, keepdims=True)
        rs = lax.rsqrt(ss / MLA_QK + EPS)
        o_ref[:, base:base + LANES] = (x0 * rs * g0).astype(BF16)
        o_ref[:, base + LANES:base + 2 * LANES] = _rope_lanes(x1 * rs * g1, tab).astype(BF16)


def _mla_q(cq, w_q_pad, q_gain, tab, seq, tm=512):
    t, k = cq.shape
    n = w_q_pad.shape[1]
    per = seq // tm
    g0 = q_gain[:MLA_NOPE].reshape(1, LANES)
    g1 = jnp.pad(q_gain[MLA_NOPE:], (0, LANES - MLA_ROPE)).reshape(1, LANES)
    return pl.pallas_call(
        _mla_q_kernel,
        grid=(t // tm,),
        in_specs=[
            pl.BlockSpec((tm, k), lambda i: (i, 0)),
            pl.BlockSpec((k, n), lambda i: (0, 0)),
            pl.BlockSpec((1, LANES), lambda i: (0, 0)),
            pl.BlockSpec((1, LANES), lambda i: (0, 0)),
            pl.BlockSpec((tm, 3 * LANES), lambda i: (i % per, 0)),
        ],
        out_specs=pl.BlockSpec((tm, n), lambda i: (i, 0)),
        out_shape=jax.ShapeDtypeStruct((t, n), BF16),
        scratch_shapes=[pltpu.VMEM((k, n), BF16)],
        compiler_params=_cparams(("arbitrary",)),
        name="mla_q_up",
    )(cq, w_q_pad, g0, g1, tab)


def _mla_kv_kernel(a_ref, w_ref, kpe_ref, g0_ref, g1_ref, tab_ref, k_ref, v_ref, wb_ref):
    @pl.when(pl.program_id(0) == 0)
    def _():
        wb_ref[...] = w_ref[...].astype(BF16)

    acc = jnp.dot(a_ref[...], wb_ref[...], preferred_element_type=F32)
    tab = tab_ref[...]
    kpe = kpe_ref[...]
    ss_pe = jnp.sum(kpe * kpe, axis=-1, keepdims=True)
    g0 = g0_ref[...]
    g1 = g1_ref[...]
    for h in range(MLA_HEADS):
        base = h * (MLA_NOPE + MLA_V)
        kn = acc[:, base:base + MLA_NOPE]
        ss = jnp.sum(kn * kn, axis=-1, keepdims=True) + ss_pe
        rs = lax.rsqrt(ss / MLA_QK + EPS)
        kb = h * MLA_QK_PAD
        k_ref[:, kb:kb + LANES] = (kn * rs * g0).astype(BF16)
        k_ref[:, kb + LANES:kb + 2 * LANES] = _rope_lanes(kpe * rs * g1, tab).astype(BF16)
        vb = h * 2 * MLA_V
        v_ref[:, vb:vb + MLA_V] = acc[:, base + MLA_NOPE:base + MLA_NOPE + MLA_V].astype(BF16)
        v_ref[:, vb + MLA_V:vb + 2 * MLA_V] = jnp.ones((acc.shape[0], MLA_V), BF16)


def _mla_kv(ckv, w_kv_up, kpe, k_gain, tab, seq, tm=512):
    t, k = ckv.shape
    n = w_kv_up.shape[1]
    per = seq // tm
    g0 = k_gain[:MLA_NOPE].reshape(1, LANES)
    g1 = jnp.pad(k_gain[MLA_NOPE:], (0, LANES - MLA_ROPE)).reshape(1, LANES)
    return pl.pallas_call(
        _mla_kv_kernel,
        grid=(t // tm,),
        in_specs=[
            pl.BlockSpec((tm, k), lambda i: (i, 0)),
            pl.BlockSpec((k, n), lambda i: (0, 0)),
            pl.BlockSpec((tm, LANES), lambda i: (i, 0)),
            pl.BlockSpec((1, LANES), lambda i: (0, 0)),
            pl.BlockSpec((1, LANES), lambda i: (0, 0)),
            pl.BlockSpec((tm, 3 * LANES), lambda i: (i % per, 0)),
        ],
        out_specs=(
            pl.BlockSpec((tm, MLA_HEADS * MLA_QK_PAD), lambda i: (i, 0)),
            pl.BlockSpec((tm, MLA_HEADS * 2 * MLA_V), lambda i: (i, 0)),
        ),
        out_shape=(
            jax.ShapeDtypeStruct((t, MLA_HEADS * MLA_QK_PAD), BF16),
            jax.ShapeDtypeStruct((t, MLA_HEADS * 2 * MLA_V), BF16),
        ),
        scratch_shapes=[pltpu.VMEM((k, n), BF16)],
        compiler_params=_cparams(("arbitrary",)),
        name="mla_kv_up",
    )(ckv, w_kv_up, kpe, g0, g1, tab)


def _mla_attn_kernel(q_ref, k_ref, v_ref, o_ref, *, tq, heads):
    seq = q_ref.shape[0]
    nt = (((1,), (1,)), ((), ()))
    r = lax.broadcasted_iota(jnp.int32, (tq, tq), 0)
    c = lax.broadcasted_iota(jnp.int32, (tq, tq), 1)
    causal = c <= r
    for qi in range(seq // tq):
        rows = slice(qi * tq, (qi + 1) * tq)
        outs = []
        for hh in range(heads):
            qk_cols = slice(hh * MLA_QK_PAD, (hh + 1) * MLA_QK_PAD)
            v_cols = slice(hh * 2 * MLA_V, (hh + 1) * 2 * MLA_V)
            q = q_ref[rows, qk_cols]
            scores = []
            for j in range(qi + 1):
                s = lax.dot_general(q, k_ref[j * tq:(j + 1) * tq, qk_cols], nt, preferred_element_type=F32)
                if j == qi:
                    s = jnp.where(causal, s, NEG_BIG)
                scores.append(s)
            top = scores[0]
            for s in scores[1:]:
                top = jnp.maximum(top, s)
            m = jnp.max(top, axis=-1, keepdims=True)
            acc = None
            for j, s in enumerate(scores):
                p = jnp.exp(s - m).astype(BF16)
                pv = jnp.dot(p, v_ref[j * tq:(j + 1) * tq, v_cols], preferred_element_type=F32)
                acc = pv if acc is None else acc + pv
            outs.append((acc[:, :MLA_V] / acc[:, MLA_V:]).astype(o_ref.dtype))
        for hh in range(heads):
            o_ref[rows, hh * MLA_V:(hh + 1) * MLA_V] = outs[hh]


def _mla_attn(q, k, v, tq=256, heads=4):
    b, s, _ = q.shape
    return pl.pallas_call(
        functools.partial(_mla_attn_kernel, tq=tq, heads=heads),
        grid=(b, MLA_HEADS // heads),
        in_specs=[
            pl.BlockSpec((None, s, heads * MLA_QK_PAD), lambda bi, h: (bi, 0, h)),
            pl.BlockSpec((None, s, heads * MLA_QK_PAD), lambda bi, h: (bi, 0, h)),
            pl.BlockSpec((None, s, heads * 2 * MLA_V), lambda bi, h: (bi, 0, h)),
        ],
        out_specs=pl.BlockSpec((None, s, heads * MLA_V), lambda bi, h: (bi, 0, h)),
        out_shape=jax.ShapeDtypeStruct((b, s, MLA_HEADS * MLA_V), BF16),
        compiler_params=_cparams(("arbitrary", "arbitrary")),
        name="mla_attn",
    )(q, k, v)


PLAN_COLS = 256
(PLAN_EXP, PLAN_FIRST, PLAN_NEXT, PLAN_NUSED, PLAN_NVALID, PLAN_BSTART, PLAN_COUNT,
 PLAN_NEXT2) = range(8)


def _dispatch_plan_tile(cnt):
    nblk = jnp.floor((cnt + (EXPERT_BLOCK - 1.0)) * (1.0 / EXPERT_BLOCK))
    e_r = lax.broadcasted_iota(jnp.int32, (LANES, LANES), 0)
    e_c = lax.broadcasted_iota(jnp.int32, (LANES, LANES), 1)
    before = jnp.where(e_r < e_c, 1.0, 0.0).astype(BF16)
    used = jnp.where(nblk > 0.0, 1.0, 0.0)
    sub = lax.broadcasted_iota(jnp.int32, (8, LANES), 0)
    prefix = jnp.dot(jnp.where(sub == 0, nblk, used).astype(BF16), before, preferred_element_type=F32)
    bstart = prefix[0:1]
    ordinal = prefix[1:2]
    bend = bstart + nblk
    n_used = jnp.max(bend, axis=-1, keepdims=True)
    row = lax.broadcasted_iota(jnp.int32, (PLAN_COLS, LANES), 0).astype(F32)
    lane = lax.broadcasted_iota(jnp.int32, (PLAN_COLS, LANES), 1).astype(F32)
    is_exp = lane < N_EXPERTS
    row1 = row[:, 0:1]
    done = jnp.logical_and(bend <= row, is_exp)
    blk_exp = jnp.minimum(jnp.sum(jnp.where(done, 1.0, 0.0), axis=-1, keepdims=True), N_EXPERTS - 1.0)
    mine = lane == blk_exp
    bstart_of = jnp.sum(jnp.where(mine, bstart, 0.0), axis=-1, keepdims=True)
    cnt_of = jnp.sum(jnp.where(mine, cnt, 0.0), axis=-1, keepdims=True)
    ord_of = jnp.sum(jnp.where(mine, ordinal, 0.0), axis=-1, keepdims=True)
    valid = row1 < n_used
    first = jnp.where(jnp.logical_and(valid, row1 == bstart_of), ord_of + 1.0, 0.0)
    nvalid = jnp.clip(cnt_of - EXPERT_BLOCK * (row1 - bstart_of), 0.0, float(EXPERT_BLOCK))
    nvalid = jnp.where(valid, nvalid, 0.0)
    usable = jnp.logical_and(nblk > 0.0, is_exp)

    def next_used(after):
        nxt = jnp.min(jnp.where(jnp.logical_and(lane > after, usable), lane, 999.0), axis=-1, keepdims=True)
        return jnp.where(nxt > 998.0, -1.0, nxt)

    nxt = next_used(blk_exp)
    nxt2 = jnp.where(nxt < 0.0, -1.0, next_used(nxt))
    bstart_col = jnp.sum(jnp.where(lane < row, nblk, 0.0), axis=-1, keepdims=True)
    cnt_col = jnp.sum(jnp.where(lane == row, cnt, 0.0), axis=-1, keepdims=True)
    tile = jnp.zeros((PLAN_COLS, LANES), F32)
    cols = {PLAN_EXP: blk_exp, PLAN_FIRST: first, PLAN_NEXT: nxt, PLAN_NUSED: n_used,
            PLAN_NVALID: nvalid, PLAN_BSTART: bstart_col, PLAN_COUNT: cnt_col, PLAN_NEXT2: nxt2}
    for k, val in cols.items():
        tile = jnp.where(lane == k, val, tile)
    return tile


def _router_kernel(x_ref, g_ref, sh_ref, sc_ref, wr_ref, br_ref, h_ref, info_ref, idx_ref, plan_ref, carry_ref):
    i = pl.program_id(0)
    tm = x_ref.shape[0]

    @pl.when(i == 0)
    def _():
        carry_ref[...] = jnp.zeros(carry_ref.shape, F32)

    h = _norm_mod(x_ref[...], g_ref[...], sh_ref[...], sc_ref[...])
    _store_row_tiles(h_ref, _pack_halves(h))
    w = wr_ref[...]
    w_hi = w.astype(BF16)
    w_lo = (w - w_hi.astype(F32)).astype(BF16)
    h_hi = h.astype(BF16)
    h_lo = (h - h_hi.astype(F32)).astype(BF16)
    lg = (jnp.dot(h_hi, w_hi, preferred_element_type=F32)
          + jnp.dot(h_lo, w_hi, preferred_element_type=F32)
          + jnp.dot(h_hi, w_lo, preferred_element_type=F32)) + br_ref[...]

    lane = lax.broadcasted_iota(jnp.int32, (tm, LANES), 1).astype(F32)
    no_lane = float(LANES)
    gl = jnp.where(lane < N_GROUPS, lg, NEG_BIG)
    gmax = jnp.max(gl, axis=-1, keepdims=True)
    g_idx = jnp.min(jnp.where(gl == gmax, lane, no_lane), axis=-1, keepdims=True)
    g_p = 1.0 / jnp.sum(jnp.exp(gl - gmax), axis=-1, keepdims=True)
    lo_lane = N_GROUPS + g_idx * EXPERTS_PER_GROUP
    in_grp = jnp.logical_and(lane >= lo_lane, lane < lo_lane + EXPERTS_PER_GROUP)
    ev = jnp.where(in_grp, lg, NEG_BIG)
    v1 = jnp.max(ev, axis=-1, keepdims=True)
    i1 = jnp.min(jnp.where(ev == v1, lane, no_lane), axis=-1, keepdims=True)
    ev2 = jnp.where(lane == i1, NEG_BIG, ev)
    v2 = jnp.max(ev2, axis=-1, keepdims=True)
    i2 = jnp.min(jnp.where(ev2 == v2, lane, no_lane), axis=-1, keepdims=True)
    e2 = jnp.exp(v2 - v1)
    den = 1.0 + e2
    w1 = (1.0 / den) * g_p
    w2 = (e2 / den) * g_p
    id1 = i1 - N_GROUPS
    id2 = i2 - N_GROUPS

    oh1 = lane == id1
    oh2 = lane == id2
    both = jnp.where(jnp.logical_or(oh1, oh2), 1.0, 0.0)
    r = lax.broadcasted_iota(jnp.int32, (tm, tm), 0)
    c = lax.broadcasted_iota(jnp.int32, (tm, tm), 1)
    tril = jnp.where(c < r, 1.0, 0.0).astype(BF16)
    before = jnp.dot(tril, both.astype(BF16), preferred_element_type=F32) + carry_ref[...]
    rank1 = jnp.sum(jnp.where(oh1, before, 0.0), axis=-1, keepdims=True)
    rank2 = jnp.sum(jnp.where(oh2, before, 0.0), axis=-1, keepdims=True)
    carry_ref[...] = carry_ref[...] + jnp.sum(both, axis=0, keepdims=True)

    info = jnp.zeros((tm, LANES), F32)
    for col, val in enumerate((id1, id2, rank1, rank2, w1, w2)):
        info = jnp.where(lane == col, val, info)
    info_ref[...] = info
    idx_ref[...] = jnp.transpose(info)[0:8].astype(jnp.int32)

    @pl.when(i == pl.num_programs(0) - 1)
    def _():
        plan = _dispatch_plan_tile(carry_ref[...])
        plan_ref[...] = jnp.transpose(plan)[0:8].astype(jnp.int32)


def _router(x2, g, mod, seq, wr, br, tm=256):
    assert x2.shape[1] == 2 * ROW_TILE * LANES
    t, d = x2.shape
    per = seq // tm
    return pl.pallas_call(
        _router_kernel,
        grid=(t // tm,),
        in_specs=[
            pl.BlockSpec((tm, d), lambda i: (i, 0)),
            pl.BlockSpec((1, d), lambda i: (0, 0)),
            pl.BlockSpec((None, 1, d), lambda i: ((i // per) * 6 + 3, 0, 0)),
            pl.BlockSpec((None, 1, d), lambda i: ((i // per) * 6 + 4, 0, 0)),
            pl.BlockSpec((d, LANES), lambda i: (0, 0)),
            pl.BlockSpec((1, LANES), lambda i: (0, 0)),
        ],
        out_specs=(
            pl.BlockSpec((tm * ROW_TILE, LANES), lambda i: (i, 0)),
            pl.BlockSpec((tm, LANES), lambda i: (i, 0)),
            pl.BlockSpec((8, tm), lambda i: (0, i)),
            pl.BlockSpec((8, PLAN_COLS), lambda i: (0, 0)),
        ),
        out_shape=(
            jax.ShapeDtypeStruct((t * ROW_TILE, LANES), jnp.uint32),
            jax.ShapeDtypeStruct((t, LANES), F32),
            jax.ShapeDtypeStruct((8, t), jnp.int32),
            jax.ShapeDtypeStruct((8, PLAN_COLS), jnp.int32),
        ),
        scratch_shapes=[pltpu.VMEM((1, LANES), F32)],
        compiler_params=_cparams(("arbitrary",)),
        name="moe_router",
    )(x2, g.reshape(1, d), mod, mod, wr, br)


GATHER_GROUP = 8


def _slot_of(plan, idx, k, t):
    e = idx[k, t]
    return plan[PLAN_BSTART, e] * EXPERT_BLOCK + idx[2 + k, t]


def _expert_kernel(plan, idx, h_hbm, wg_hbm, wu_hbm, wd_hbm, ys_ref,
                   row_tok, xbuf, xsem, sg, su, sd, wsem, wgb, wub, wdb, *, layer, n_tok):
    i = pl.program_id(0)
    nu = plan[PLAN_NUSED, 0]
    n_rows = row_tok.shape[0]

    def weight_copies(e, st):
        return (pltpu.make_async_copy(wg_hbm.at[layer, e], sg.at[st], wsem.at[st, 0]),
                pltpu.make_async_copy(wu_hbm.at[layer, e], su.at[st], wsem.at[st, 1]),
                pltpu.make_async_copy(wd_hbm.at[layer, e], sd.at[st], wsem.at[st, 2]))

    def n_groups(blk):
        return (plan[PLAN_NVALID, blk] + GATHER_GROUP - 1) // GATHER_GROUP

    def start_gather(blk, slot):
        base = blk * EXPERT_BLOCK

        def body(g, carry):
            for k in range(GATHER_GROUP):
                r = g * GATHER_GROUP + k
                tok = row_tok[base + r]
                src = pl.multiple_of(tok * ROW_TILE, ROW_TILE)
                dst = pl.multiple_of(r * ROW_TILE, ROW_TILE)
                pltpu.make_async_copy(h_hbm.at[pl.ds(src, ROW_TILE), :],
                                      xbuf.at[slot, pl.ds(dst, ROW_TILE), :], xsem.at[slot]).start()
            return carry

        lax.fori_loop(0, n_groups(blk), body, 0)

    def wait_gather(blk, slot):
        span = GATHER_GROUP * ROW_TILE

        def body(g, carry):
            pltpu.make_async_copy(h_hbm.at[pl.ds(0, span), :],
                                  xbuf.at[slot, pl.ds(0, span), :], xsem.at[slot]).wait()
            return carry

        lax.fori_loop(0, n_groups(blk), body, 0)

    @pl.when(i == 0)
    def _():
        for cp in weight_copies(plan[PLAN_EXP, 0], 0):
            cp.start(priority=1)
        second = plan[PLAN_NEXT, 0]

        @pl.when(second >= 0)
        def _():
            for cp in weight_copies(second, 1):
                cp.start(priority=1)

        xbuf[...] = jnp.zeros(xbuf.shape, xbuf.dtype)

        def pad_body(e, carry):
            end = plan[PLAN_BSTART, e] * EXPERT_BLOCK + plan[PLAN_COUNT, e]
            for k in range(GATHER_GROUP - 1):
                row_tok[jnp.minimum(end + k, n_rows - 1)] = 0
            return carry

        lax.fori_loop(0, N_EXPERTS, pad_body, 0)

        def fill_body(t, carry):
            row_tok[_slot_of(plan, idx, 0, t)] = t
            row_tok[_slot_of(plan, idx, 1, t)] = t
            return carry

        lax.fori_loop(0, n_tok, fill_body, 0, unroll=8)
        start_gather(0, 0)

    @pl.when(i < nu)
    def _():
        slot = i % 2

        @pl.when(i + 1 < nu)
        def _():
            start_gather(i + 1, 1 - slot)

        first = plan[PLAN_FIRST, i]

        @pl.when(first > 0)
        def _():
            st = (first - 1) % 2
            nxt2 = plan[PLAN_NEXT2, i]
            cps = weight_copies(plan[PLAN_EXP, i], st)
            nxt_cps = weight_copies(jnp.maximum(nxt2, 0), st)
            for cp, ncp, stage, dst in zip(cps, nxt_cps, (sg, su, sd), (wgb, wub, wdb)):
                cp.wait()
                dst[...] = stage[st].astype(BF16)

                @pl.when(nxt2 >= 0)
                def _():
                    ncp.start(priority=1)

        wait_gather(i, slot)
        parts = [_unpack_halves(w) for w in _load_row_tiles(xbuf.at[slot], EXPERT_BLOCK)]
        x_lo = jnp.concatenate([lo.astype(BF16) for lo, _ in parts], axis=1)
        x_hi = jnp.concatenate([hi.astype(BF16) for _, hi in parts], axis=1)
        half = wgb.shape[0] // 2
        g = (jnp.dot(x_lo, wgb[:half], preferred_element_type=F32)
             + jnp.dot(x_hi, wgb[half:], preferred_element_type=F32))
        u = (jnp.dot(x_lo, wub[:half], preferred_element_type=F32)
             + jnp.dot(x_hi, wub[half:], preferred_element_type=F32))
        hid = (_silu(g) * u).astype(BF16)
        _store_row_tiles(ys_ref, _pack_halves(jnp.dot(hid, wdb[...], preferred_element_type=F32)))

    @pl.when(i >= nu)
    def _():
        ys_ref[...] = jnp.zeros(ys_ref.shape, ys_ref.dtype)


def _expert_ffn(h2, w_gate, w_up, w_down, layer, plan, idx, n_blocks):
    t = h2.shape[0] // ROW_TILE
    d, f = w_gate.shape[2], w_gate.shape[3]
    n_rows = n_blocks * EXPERT_BLOCK
    grid_spec = pltpu.PrefetchScalarGridSpec(
        num_scalar_prefetch=2,
        grid=(n_blocks,),
        in_specs=[pl.BlockSpec(memory_space=pl.ANY)] * 4,
        out_specs=pl.BlockSpec((EXPERT_BLOCK * ROW_TILE, LANES), lambda i, *_: (i, 0)),
        scratch_shapes=[
            pltpu.SMEM((n_rows,), jnp.int32),
            pltpu.VMEM((2, EXPERT_BLOCK * ROW_TILE, LANES), jnp.uint32),
            pltpu.SemaphoreType.DMA((2,)),
            pltpu.VMEM((2, d, f), F32),
            pltpu.VMEM((2, d, f), F32),
            pltpu.VMEM((2, f, d), F32),
            pltpu.SemaphoreType.DMA((2, 3)),
            pltpu.VMEM((d, f), BF16),
            pltpu.VMEM((d, f), BF16),
            pltpu.VMEM((f, d), BF16),
        ],
    )
    weight_bytes = 3 * d * f * (2 * 4 + 2)
    vmem = weight_bytes + 8 * 1024 * 1024
    return pl.pallas_call(
        functools.partial(_expert_kernel, layer=layer, n_tok=t),
        grid_spec=grid_spec,
        out_shape=jax.ShapeDtypeStruct((n_rows * ROW_TILE, LANES), jnp.uint32),
        compiler_params=_cparams(("arbitrary",), vmem),
        name="moe_experts",
    )(plan, idx, h2, w_gate, w_up, w_down)


def _combine_kernel(plan, idx, x_ref, info_ref, gate_ref, ys_hbm, *rest, tm, with_next):
    if with_next:
        ng_ref, nsh_ref, nsc_ref, o_ref, h_ref, buf, sem = rest
    else:
        o_ref, buf, sem = rest
    i = pl.program_id(0)
    nsteps = pl.num_programs(0)

    def start_gather(step, slot):
        base = step * tm

        def body(r, carry):
            dst = pl.multiple_of(r * ROW_TILE, ROW_TILE)
            for k in range(2):
                src = pl.multiple_of(_slot_of(plan, idx, k, base + r) * ROW_TILE, ROW_TILE)
                pltpu.make_async_copy(ys_hbm.at[pl.ds(src, ROW_TILE), :],
                                      buf.at[slot, k, pl.ds(dst, ROW_TILE), :], sem.at[slot]).start(priority=k)
            return carry

        lax.fori_loop(0, tm, body, 0, unroll=8)

    @pl.when(i == 0)
    def _():
        start_gather(0, 0)

    slot = i % 2

    @pl.when(i + 1 < nsteps)
    def _():
        start_gather(i + 1, 1 - slot)

    for k in range(2):
        pltpu.make_async_copy(ys_hbm.at[pl.ds(0, tm * ROW_TILE), :], buf.at[slot, k], sem.at[slot]).wait()
    info = info_ref[...]
    w0 = info[:, 4:5]
    w1 = info[:, 5:6]
    half = x_ref.shape[1] // 2
    a_tiles = _load_row_tiles(buf.at[slot, 0], tm)
    b_tiles = _load_row_tiles(buf.at[slot, 1], tm)
    for s in range(ROW_TILE):
        a_lo, a_hi = _unpack_halves(a_tiles[s])
        b_lo, b_hi = _unpack_halves(b_tiles[s])
        lo = slice(s * LANES, (s + 1) * LANES)
        hi = slice(half + s * LANES, half + (s + 1) * LANES)
        o_ref[:, lo] = x_ref[:, lo] + gate_ref[:, lo] * (w0 * a_lo + w1 * b_lo)
        o_ref[:, hi] = x_ref[:, hi] + gate_ref[:, hi] * (w0 * a_hi + w1 * b_hi)
    if with_next:
        h_ref[...] = _norm_mod(o_ref[...], ng_ref[...], nsh_ref[...], nsc_ref[...]).astype(h_ref.dtype)


def _combine(x2, info, mod, seq, ys, plan, idx, next_norm=None, tm=128):
    t, d = x2.shape
    per = seq // tm
    row = lambda i, *_: (i, 0)
    mod_row = lambda k: (lambda i, *_: ((i // per) * 6 + k, 0, 0))
    in_specs = [
        pl.BlockSpec((tm, d), row),
        pl.BlockSpec((tm, LANES), row),
        pl.BlockSpec((None, 1, d), mod_row(5)),
        pl.BlockSpec(memory_space=pl.ANY),
    ]
    args = [plan, idx, x2, info, mod, ys]
    out_specs = pl.BlockSpec((tm, d), row)
    out_shape = jax.ShapeDtypeStruct((t, d), F32)
    if next_norm is not None:
        next_g, next_mod = next_norm
        in_specs += [pl.BlockSpec((1, d), lambda i, *_: (0, 0)),
                     pl.BlockSpec((None, 1, d), mod_row(0)), pl.BlockSpec((None, 1, d), mod_row(1))]
        args += [next_g.reshape(1, d), next_mod, next_mod]
        out_specs = (out_specs, pl.BlockSpec((tm, d), row))
        out_shape = (out_shape, jax.ShapeDtypeStruct((t, d), BF16))
    grid_spec = pltpu.PrefetchScalarGridSpec(
        num_scalar_prefetch=2,
        grid=(t // tm,),
        in_specs=in_specs,
        out_specs=out_specs,
        scratch_shapes=[
            pltpu.VMEM((2, 2, tm * ROW_TILE, LANES), jnp.uint32),
            pltpu.SemaphoreType.DMA((2,)),
        ],
    )
    return pl.pallas_call(
        functools.partial(_combine_kernel, tm=tm, with_next=next_norm is not None),
        grid_spec=grid_spec,
        out_shape=out_shape,
        compiler_params=_cparams(("arbitrary",)),
        name="moe_combine",
    )(*args)


def _hier_moe(x2, norm_g, mod, seq, w_rg, b_rg, w_re, b_re, w_gate, w_up, w_down, layer, next_norm=None):
    t, d = x2.shape
    pad = LANES - N_GROUPS - N_EXPERTS
    wr = jnp.concatenate([w_rg, w_re, jnp.zeros((d, pad), F32)], axis=1)
    br = jnp.concatenate([b_rg, b_re, jnp.zeros((pad,), F32)]).reshape(1, LANES)
    h2, info, idx, plan = _router(x2, norm_g, mod, seq, wr, br)
    n_assign = 2 * t
    n_blocks = (n_assign + N_EXPERTS * (EXPERT_BLOCK - 1) + EXPERT_BLOCK - 1) // EXPERT_BLOCK
    assert n_blocks <= PLAN_COLS
    ys = _expert_ffn(h2, w_gate, w_up, w_down, layer, plan, idx, n_blocks)
    return _combine(x2, info, mod, seq, ys, plan, idx, next_norm)


def kernel(x, c, ada_w, ada_b, norm1_g, norm2_g, dsa_w_in, dsa_q_gain, dsa_k_gain, dsa_w_out, mla_w_in, mla_cq_gain, mla_ckv_gain, mla_w_q_up, mla_w_kv_up, mla_q_gain, mla_k_gain, mla_w_out, router_group_w, router_group_b, router_expert_w, router_expert_b, expert_w_gate, expert_w_up, expert_w_down):
    b, s, d = x.shape
    t = b * s
    mods = _ada_mod(c, ada_w, ada_b)
    x2 = x.reshape(t, d)

    mod = mods[0]
    h = _normmod(x2, norm1_g[0], mod, s, 0, 1)
    gains = (dsa_q_gain[0], dsa_k_gain[0], dsa_k_gain[0])
    qkv = [[_dsa_proj(h, dsa_w_in[0], gains[w], g, w, b) for w in range(3)] for g in range(len(DIL_PAIRS))]
    og2 = _dsa_attn(*qkv[2], 2)
    og1 = _dsa_attn(*qkv[1], 1)
    o = _dsa_attn(*qkv[0], 0, others=(og1, og2))
    x2 = _resid_mm(o.reshape(t, DSA_WIDTH), dsa_w_out[0], x2, mod, s, 2)
    x2, h = _hier_moe(x2, norm2_g[0], mod, s, router_group_w[0], router_group_b[0],
                      router_expert_w[0], router_expert_b[0],
                      expert_w_gate, expert_w_up, expert_w_down, 0, next_norm=(norm1_g[1], mods[1]))

    mod = mods[1]
    cq, ckv, kpe = _mla_in(h, mla_w_in[0], mla_cq_gain[0], mla_ckv_gain[0])
    tab = _rope_tables(s)
    w_q_pad = jnp.pad(mla_w_q_up[0].reshape(MLA_Q_LORA, MLA_HEADS, MLA_QK),
                      ((0, 0), (0, 0), (0, MLA_QK_PAD - MLA_QK))).reshape(MLA_Q_LORA, MLA_HEADS * MLA_QK_PAD)
    q = _mla_q(cq, w_q_pad, mla_q_gain[0], tab, s)
    k, v = _mla_kv(ckv, mla_w_kv_up[0], kpe, mla_k_gain[0], tab, s)
    o = _mla_attn(q.reshape(b, s, -1), k.reshape(b, s, -1), v.reshape(b, s, -1))
    x2 = _resid_mm(o.reshape(t, MLA_HEADS * MLA_V), mla_w_out[0], x2, mod, s, 2)
    x2 = _hier_moe(x2, norm2_g[1], mod, s, router_group_w[1], router_group_b[1],
                   router_expert_w[1], router_expert_b[1],
                   expert_w_gate, expert_w_up, expert_w_down, 1)
    return x2.reshape(b, s, d)
```

```python
import functools
import math

import jax
import jax.numpy as jnp
import numpy as np
from jax import lax
from jax.experimental import pallas as pl
from jax.experimental.pallas import tpu as pltpu

F32 = jnp.float32
BF16 = jnp.bfloat16

D_MODEL = 2048
EPS = 1e-6
LANES = 128
NEG_BIG = -1e30

DIL_PAIRS = ((128, 1), (512, 4), (2048, 16))
DSA_HEADS = 8
DSA_HEAD_DIM = 128
DSA_WIDTH = DSA_HEADS * DSA_HEAD_DIM
BAND = 128

MLA_HEADS = 16
MLA_Q_LORA = 512
MLA_KV_LORA = 512
MLA_NOPE = 128
MLA_ROPE = 64
MLA_V = 128
MLA_QK = MLA_NOPE + MLA_ROPE
MLA_QK_PAD = 256
ROPE_THETA = 10000.0

N_GROUPS = 4
EXPERTS_PER_GROUP = 16
N_EXPERTS = N_GROUPS * EXPERTS_PER_GROUP
D_EXPERT = 768
EXPERT_BLOCK = 128

VMEM_LIMIT = 48 * 1024 * 1024


def _cparams(sem, vmem=VMEM_LIMIT):
    return pltpu.CompilerParams(dimension_semantics=sem, vmem_limit_bytes=vmem)


def _silu(x):
    return x * (1.0 / (1.0 + jnp.exp(-x)))


def _pack_halves(x):
    n = x.shape[1] // 2
    xb = x.astype(BF16).astype(F32)
    lo = lax.bitcast_convert_type(xb[:, :n], jnp.uint32) >> 16
    hi = lax.bitcast_convert_type(xb[:, n:], jnp.uint32) & jnp.uint32(0xFFFF0000)
    return hi | lo


def _unpack_halves(w):
    lo = lax.bitcast_convert_type(w << 16, F32)
    hi = lax.bitcast_convert_type(w & jnp.uint32(0xFFFF0000), F32)
    return lo, hi


ROW_TILE = 8


def _store_row_tiles(ref, words):
    rows = words.shape[0]
    for s in range(ROW_TILE):
        ref[pl.ds(s, rows, stride=ROW_TILE), :] = words[:, s * LANES:(s + 1) * LANES]


def _load_row_tiles(ref, rows):
    return [ref[pl.ds(s, rows, stride=ROW_TILE), :] for s in range(ROW_TILE)]


def _norm_mod(x, g, shift, scale):
    ms = jnp.mean(x * x, axis=-1, keepdims=True)
    y = x * lax.rsqrt(ms + EPS) * g
    return y * (1.0 + scale) + shift


def _ada_kernel(c_ref, w_ref, b_ref, o_ref):
    ca = _silu(c_ref[...])
    hi = ca.astype(BF16)
    lo = (ca - hi.astype(F32)).astype(BF16)
    lhs = jnp.concatenate([hi, lo], axis=0)
    res = jnp.dot(lhs, w_ref[...].astype(BF16), preferred_element_type=F32)
    o_ref[...] = res[:8] + res[8:] + b_ref[...]


def _ada_mod(c, ada_w, ada_b):
    depth, d, n = ada_w.shape
    b = c.shape[0]
    c8 = jnp.pad(c, ((0, 8 - b), (0, 0)))
    tn = 1024
    out = pl.pallas_call(
        _ada_kernel,
        grid=(depth, n // tn),
        in_specs=[
            pl.BlockSpec((8, d), lambda i, j: (0, 0)),
            pl.BlockSpec((None, d, tn), lambda i, j: (i, 0, j)),
            pl.BlockSpec((None, 1, tn), lambda i, j: (i, 0, j)),
        ],
        out_specs=pl.BlockSpec((None, 8, tn), lambda i, j: (i, 0, j)),
        out_shape=jax.ShapeDtypeStruct((depth, 8, n), F32),
        compiler_params=_cparams(("arbitrary", "arbitrary")),
        name="ada_mod",
    )(c8, ada_w, ada_b.reshape(depth, 1, n))
    return out[:, :b].reshape(depth, b * 6, 1, d)


def _normmod_kernel(x_ref, g_ref, sh_ref, sc_ref, o_ref):
    o_ref[...] = _norm_mod(x_ref[...], g_ref[...], sh_ref[...], sc_ref[...]).astype(o_ref.dtype)


def _normmod(x2, g, mod, seq, k_shift, k_scale, tm=512):
    t, d = x2.shape
    per = seq // tm
    return pl.pallas_call(
        _normmod_kernel,
        grid=(t // tm,),
        in_specs=[
            pl.BlockSpec((tm, d), lambda i: (i, 0)),
            pl.BlockSpec((1, d), lambda i: (0, 0)),
            pl.BlockSpec((None, 1, d), lambda i: ((i // per) * 6 + k_shift, 0, 0)),
            pl.BlockSpec((None, 1, d), lambda i: ((i // per) * 6 + k_scale, 0, 0)),
        ],
        out_specs=pl.BlockSpec((tm, d), lambda i: (i, 0)),
        out_shape=jax.ShapeDtypeStruct((t, d), BF16),
        compiler_params=_cparams(("arbitrary",)),
        name="normmod",
    )(x2, g.reshape(1, d), mod, mod)


def _cast_weight_once(w_ref, wb_ref):
    @pl.when(pl.program_id(1) == 0)
    def _():
        wb_ref[...] = w_ref[...].astype(BF16)


def _dsa_proj_kernel(a_ref, w_ref, g_ref, o_ref, wb_ref, res_ref, *, dilation, normed, gain_scale):
    @pl.when(pl.program_id(0) == 0)
    def _():
        wb_ref[...] = w_ref[...].astype(BF16)

    gain = g_ref[...] * gain_scale
    tm = a_ref.shape[0]
    n_parts = 2
    pr = tm // n_parts
    accs = [jnp.dot(a_ref[p * pr:(p + 1) * pr, :], wb_ref[...], preferred_element_type=F32)
            for p in range(n_parts)]
    for p, acc in enumerate(accs):
        blks = []
        for h in range(DSA_HEADS):
            blk = acc[:, h * LANES:(h + 1) * LANES]
            if normed:
                ms = jnp.mean(blk * blk, axis=-1, keepdims=True)
                blk = blk * lax.rsqrt(ms + EPS) * gain
            blks.append(blk)
        for h, blk in enumerate(blks):
            hs = slice(h * LANES, (h + 1) * LANES)
            if dilation == 1:
                o_ref[0, p * pr:(p + 1) * pr, hs] = blk.astype(o_ref.dtype)
            else:
                res_ref[h, p * pr:(p + 1) * pr, :] = blk
                sub_rows = pr // dilation
                for r in range(dilation):
                    sub = res_ref[h, pl.ds(p * pr + r, sub_rows, stride=dilation), :]
                    o_ref[r, p * sub_rows:(p + 1) * sub_rows, hs] = sub.astype(o_ref.dtype)


def _dsa_proj(h, w_in, gain, group, which, batch, tm=512, tn=DSA_WIDTH):
    t, d = h.shape
    dil = DIL_PAIRS[group][1]
    seq = t // batch
    per = seq // tm
    gain_scale = 1.0 / math.sqrt(DSA_HEAD_DIM) if which == 0 else 1.0
    return pl.pallas_call(
        functools.partial(_dsa_proj_kernel, dilation=dil, normed=which != 2, gain_scale=gain_scale),
        grid=(t // tm,),
        in_specs=[
            pl.BlockSpec((tm, d), lambda i: (i, 0)),
            pl.BlockSpec((d, tn), lambda i: (0, group * 3 + which)),
            pl.BlockSpec((1, LANES), lambda i: (0, 0)),
        ],
        out_specs=pl.BlockSpec((None, dil, tm // dil, tn), lambda i: (i // per, 0, i % per, 0)),
        out_shape=jax.ShapeDtypeStruct((batch, dil, seq // dil, tn), BF16),
        scratch_shapes=[pltpu.VMEM((d, tn), BF16), pltpu.VMEM((DSA_HEADS, tm, LANES), F32)],
        compiler_params=_cparams(("arbitrary",)),
        name=f"dsa_proj_g{group}_{'qkv'[which]}",
    )(h, w_in, gain.reshape(1, LANES))


def _resid_mm_kernel(a_ref, w_ref, x_ref, gate_ref, o_ref, wb_ref):
    _cast_weight_once(w_ref, wb_ref)
    y = jnp.dot(a_ref[...], wb_ref[...], preferred_element_type=F32)
    o_ref[...] = x_ref[...] + gate_ref[...] * y


def _resid_mm(a, w, x2, mod, seq, k_gate, tm=512, tn=1024):
    t, k = a.shape
    n = w.shape[1]
    per = seq // tm
    return pl.pallas_call(
        _resid_mm_kernel,
        grid=(n // tn, t // tm),
        in_specs=[
            pl.BlockSpec((tm, k), lambda j, i: (i, 0)),
            pl.BlockSpec((k, tn), lambda j, i: (0, j)),
            pl.BlockSpec((tm, tn), lambda j, i: (i, j)),
            pl.BlockSpec((None, 1, tn), lambda j, i: ((i // per) * 6 + k_gate, 0, j)),
        ],
        out_specs=pl.BlockSpec((tm, tn), lambda j, i: (i, j)),
        out_shape=jax.ShapeDtypeStruct((t, n), F32),
        scratch_shapes=[pltpu.VMEM((k, tn), BF16)],
        compiler_params=_cparams(("arbitrary", "arbitrary")),
        name="resid_mm",
    )(a, w, x2, mod)


def _alibi_slope(head_slot, group):
    n = len(DIL_PAIRS) * DSA_HEADS
    return 2.0 ** (-8.0 * (head_slot * len(DIL_PAIRS) + group + 1.0) / n)


def _dsa_bias_table(group, has_prev):
    window, d = DIL_PAIRS[group]
    steps = window // d
    qi = np.arange(BAND)[:, None]
    if has_prev:
        kj = np.arange(2 * BAND)[None, :]
        delta = qi + BAND - kj
        prev_key = np.broadcast_to(kj < BAND, delta.shape)
    else:
        kj = np.arange(BAND)[None, :]
        delta = qi - kj
        prev_key = np.zeros(delta.shape, bool)
    inside = (delta >= 0) & (delta <= steps)
    tabs = []
    for first in (True, False):
        valid = inside & ~(prev_key & first)
        per_head = [np.where(valid, -_alibi_slope(h, group) * (delta * d), NEG_BIG) for h in range(DSA_HEADS)]
        tabs.append(np.stack(per_head))
    return jnp.asarray(np.stack(tabs), F32)


def _dsa_attn_kernel(*refs, dilation, has_prev, merge):
    refs = list(refs)
    q_ref, kc_ref, vc_ref, bias_ref = refs[:4]
    pos = 4
    if has_prev:
        kp_ref, vp_ref = refs[pos:pos + 2]
        pos += 2
    if merge:
        other = refs[pos:pos + 4]
        pos += 4
        o_ref = refs[pos]
    else:
        o_ref, lse_ref = refs[pos:pos + 2]

    r = pl.program_id(2)
    lane = lax.broadcasted_iota(jnp.int32, (BAND, LANES), 1)
    nt = (((1,), (1,)), ((), ()))
    nk = 2 * BAND if has_prev else BAND
    ones = jnp.ones((nk, LANES), BF16)
    n_batch = q_ref.shape[0]

    def head_scores(bi, h):
        hs = slice(h * LANES, (h + 1) * LANES)
        q = q_ref[bi, :, hs]
        if has_prev:
            k = jnp.concatenate([kp_ref[bi, :, hs], kc_ref[bi, :, hs]], axis=0)
        else:
            k = kc_ref[bi, :, hs]
        return lax.dot_general(q, k, nt, preferred_element_type=F32) + bias_ref[h]

    def head_values(bi, h):
        hs = slice(h * LANES, (h + 1) * LANES)
        if has_prev:
            v = jnp.concatenate([vp_ref[bi, :, hs], vc_ref[bi, :, hs]], axis=0)
        else:
            v = vc_ref[bi, :, hs]
        return jnp.concatenate([v, ones], axis=1)

    results = []
    for bi in range(n_batch):
        scores = [head_scores(bi, h) for h in range(DSA_HEADS)]
        maxes = [jnp.max(s, axis=-1, keepdims=True) for s in scores]
        probs = [jnp.exp(s - m).astype(BF16) for s, m in zip(scores, maxes)]
        accs = [jnp.dot(p, head_values(bi, h), preferred_element_type=F32) for h, p in enumerate(probs)]
        outs = []
        lse_tile = jnp.zeros((BAND, LANES), F32)
        for h in range(DSA_HEADS):
            l = accs[h][:, LANES:]
            o = accs[h][:, :LANES] / l
            lse = maxes[h] + jnp.log(l)
            if merge:
                lses = [lse] + [other[2 * g + 1][bi, :, h * 16:h * 16 + 1] for g in range(2)]
                parts = [o] + [other[2 * g][bi, h] for g in range(2)]
                top = jnp.maximum(jnp.maximum(lses[0], lses[1]), lses[2])
                es = [jnp.exp(x - top) for x in lses]
                den = es[0] + es[1] + es[2]
                o = ((es[0] * parts[0] + es[1] * parts[1] + es[2] * parts[2]) / den).astype(o_ref.dtype)
            else:
                in_head = jnp.logical_and(lane >= h * 16, lane < (h + 1) * 16)
                lse_tile = jnp.where(in_head, lse, lse_tile)
            outs.append(o)
        results.append((outs, lse_tile))

    for bi, (outs, lse_tile) in enumerate(results):
        for h, o in enumerate(outs):
            if merge:
                o_ref[bi, :, h * LANES:(h + 1) * LANES] = o
            else:
                o_ref[bi, h, pl.ds(r, BAND, stride=dilation), :] = o
        if not merge:
            lse_ref[bi, pl.ds(r, BAND, stride=dilation), :] = lse_tile


def _dsa_attn(q_g, k_g, v_g, group, others=None):
    b, d, sub_len, _ = q_g.shape
    s = d * sub_len
    nb = sub_len // BAND
    has_prev = nb > 1
    merge = others is not None
    assert not merge or d == 1
    bias = _dsa_bias_table(group, has_prev)

    bb = 2 if b % 2 == 0 else 1
    cur = lambda bi, n, r: (bi, r, n, 0)
    prev = lambda bi, n, r: (bi, r, jnp.maximum(n - 1, 0), 0)
    blk = (bb, None, BAND, DSA_WIDTH)
    in_specs = [pl.BlockSpec(blk, cur), pl.BlockSpec(blk, cur), pl.BlockSpec(blk, cur),
                pl.BlockSpec((None,) + bias.shape[1:], lambda bi, n, r: (jnp.minimum(n, 1), 0, 0, 0))]
    args = [q_g, k_g, v_g, bias]
    if has_prev:
        in_specs += [pl.BlockSpec(blk, prev), pl.BlockSpec(blk, prev)]
        args += [k_g, v_g]
    span = BAND * d
    nat = lambda bi, n, r: (bi, n, 0)
    nat_heads = lambda bi, n, r: (bi, 0, n, 0)
    if merge:
        for o_g, lse_g in others:
            in_specs += [pl.BlockSpec((bb, DSA_HEADS, BAND, LANES), nat_heads),
                         pl.BlockSpec((bb, BAND, LANES), nat)]
            args += [o_g, lse_g]
        out_shape = jax.ShapeDtypeStruct((b, s, DSA_WIDTH), BF16)
        out_specs = pl.BlockSpec((bb, BAND, DSA_WIDTH), nat)
    else:
        out_shape = (jax.ShapeDtypeStruct((b, DSA_HEADS, s, LANES), F32), jax.ShapeDtypeStruct((b, s, LANES), F32))
        out_specs = (pl.BlockSpec((bb, DSA_HEADS, span, LANES), nat_heads),
                     pl.BlockSpec((bb, span, LANES), nat))
    return pl.pallas_call(
        functools.partial(_dsa_attn_kernel, dilation=d, has_prev=has_prev, merge=merge),
        grid=(b // bb, nb, d),
        in_specs=in_specs,
        out_specs=out_specs,
        out_shape=out_shape,
        compiler_params=_cparams(("arbitrary", "arbitrary", "arbitrary")),
        name=f"dsa_attn_g{group}",
    )(*args)


def _rope_tables(seq):
    half = MLA_ROPE // 2
    inv = ROPE_THETA ** (-jnp.arange(half, dtype=F32) / half)
    ang = jnp.arange(seq, dtype=F32)[:, None] * inv[None, :]
    cos, sin = jnp.cos(ang), jnp.sin(ang)
    z = jnp.zeros((seq, LANES - MLA_ROPE), F32)
    zh = jnp.zeros((seq, half), F32)
    cos_t = jnp.concatenate([cos, cos, z], axis=1)
    sin_a = jnp.concatenate([zh, sin, z], axis=1)
    sin_b = jnp.concatenate([-sin, zh, z], axis=1)
    return jnp.concatenate([cos_t, sin_a, sin_b], axis=1)


def _rope_lanes(x, tab):
    half = MLA_ROPE // 2
    cos_t = tab[:, 0:LANES]
    sin_a = tab[:, LANES:2 * LANES]
    sin_b = tab[:, 2 * LANES:3 * LANES]
    return (x * cos_t + pltpu.roll(x, half, 1) * sin_a
            + pltpu.roll(x, LANES - half, 1) * sin_b)


def _mla_in_kernel(a_ref, w_ref, cqg_ref, ckvg_ref, cq_ref, ckv_ref, kpe_ref, wb_ref):
    n = w_ref.shape[1]

    @pl.when(pl.program_id(0) == 0)
    def _():
        wb_ref[...] = jnp.zeros(wb_ref.shape, BF16)
        wb_ref[:, :n] = w_ref[...].astype(BF16)

    acc = jnp.dot(a_ref[...], wb_ref[...], preferred_element_type=F32)
    cq = acc[:, :MLA_Q_LORA]
    cq_ref[...] = (cq * lax.rsqrt(jnp.mean(cq * cq, axis=-1, keepdims=True) + EPS)
                   * cqg_ref[...]).astype(BF16)
    ckv = acc[:, MLA_Q_LORA:MLA_Q_LORA + MLA_KV_LORA]
    ckv_ref[...] = (ckv * lax.rsqrt(jnp.mean(ckv * ckv, axis=-1, keepdims=True) + EPS)
                    * ckvg_ref[...]).astype(BF16)
    kpe_ref[...] = acc[:, MLA_Q_LORA + MLA_KV_LORA:]


def _mla_in(h, w_in, cq_gain, ckv_gain, tm=512):
    t, d = h.shape
    n = w_in.shape[1]
    n_pad = MLA_Q_LORA + MLA_KV_LORA + LANES
    return pl.pallas_call(
        _mla_in_kernel,
        grid=(t // tm,),
        in_specs=[
            pl.BlockSpec((tm, d), lambda i: (i, 0)),
            pl.BlockSpec((d, n), lambda i: (0, 0)),
            pl.BlockSpec((1, MLA_Q_LORA), lambda i: (0, 0)),
            pl.BlockSpec((1, MLA_KV_LORA), lambda i: (0, 0)),
        ],
        out_specs=(
            pl.BlockSpec((tm, MLA_Q_LORA), lambda i: (i, 0)),
            pl.BlockSpec((tm, MLA_KV_LORA), lambda i: (i, 0)),
            pl.BlockSpec((tm, LANES), lambda i: (i, 0)),
        ),
        out_shape=(
            jax.ShapeDtypeStruct((t, MLA_Q_LORA), BF16),
            jax.ShapeDtypeStruct((t, MLA_KV_LORA), BF16),
            jax.ShapeDtypeStruct((t, LANES), F32),
        ),
        scratch_shapes=[pltpu.VMEM((d, n_pad), BF16)],
        compiler_params=_cparams(("arbitrary",)),
        name="mla_in",
    )(h, w_in, cq_gain.reshape(1, -1), ckv_gain.reshape(1, -1))


def _mla_q_kernel(a_ref, w_ref, g0_ref, g1_ref, tab_ref, o_ref, wb_ref):
    @pl.when(pl.program_id(0) == 0)
    def _():
        wb_ref[...] = w_ref[...].astype(BF16)

    acc = jnp.dot(a_ref[...], wb_ref[...], preferred_element_type=F32)
    tab = tab_ref[...]
    scale = 1.0 / math.sqrt(MLA_QK)
    g0 = g0_ref[...] * scale
    g1 = g1_ref[...] * scale
    for h in range(MLA_HEADS):
        base = h * MLA_QK_PAD
        x0 = acc[:, base:base + LANES]
        x1 = acc[:, base + LANES:base + 2 * LANES]
        ss = jnp.sum(x0 * x0, axis=-1, keepdims=True) + jnp.sum(x1 * x1, axis=-1, keepdims=True)
        rs = lax.rsqrt(ss / MLA_QK + EPS)
        o_ref[:, base:base + LANES] = (x0 * rs * g0).astype(BF16)
        o_ref[:, base + LANES:base + 2 * LANES] = _rope_lanes(x1 * rs * g1, tab).astype(BF16)


def _mla_q(cq, w_q_pad, q_gain, tab, seq, tm=512):
    t, k = cq.shape
    n = w_q_pad.shape[1]
    per = seq // tm
    g0 = q_gain[:MLA_NOPE].reshape(1, LANES)
    g1 = jnp.pad(q_gain[MLA_NOPE:], (0, LANES - MLA_ROPE)).reshape(1, LANES)
    return pl.pallas_call(
        _mla_q_kernel,
        grid=(t // tm,),
        in_specs=[
            pl.BlockSpec((tm, k), lambda i: (i, 0)),
            pl.BlockSpec((k, n), lambda i: (0, 0)),
            pl.BlockSpec((1, LANES), lambda i: (0, 0)),
            pl.BlockSpec((1, LANES), lambda i: (0, 0)),
            pl.BlockSpec((tm, 3 * LANES), lambda i: (i % per, 0)),
        ],
        out_specs=pl.BlockSpec((tm, n), lambda i: (i, 0)),
        out_shape=jax.ShapeDtypeStruct((t, n), BF16),
        scratch_shapes=[pltpu.VMEM((k, n), BF16)],
        compiler_params=_cparams(("arbitrary",)),
        name="mla_q_up",
    )(cq, w_q_pad, g0, g1, tab)


def _mla_kv_kernel(a_ref, w_ref, kpe_ref, g0_ref, g1_ref, tab_ref, k_ref, v_ref, wb_ref):
    @pl.when(pl.program_id(0) == 0)
    def _():
        wb_ref[...] = w_ref[...].astype(BF16)

    acc = jnp.dot(a_ref[...], wb_ref[...], preferred_element_type=F32)
    tab = tab_ref[...]
    kpe = kpe_ref[...]
    ss_pe = jnp.sum(kpe * kpe, axis=-1, keepdims=True)
    g0 = g0_ref[...]
    g1 = g1_ref[...]
    for h in range(MLA_HEADS):
        base = h * (MLA_NOPE + MLA_V)
        kn = acc[:, base:base + MLA_NOPE]
        ss = jnp.sum(kn * kn, axis=-1, keepdims=True) + ss_pe
        rs = lax.rsqrt(ss / MLA_QK + EPS)
        kb = h * MLA_QK_PAD
        k_ref[:, kb:kb + LANES] = (kn * rs * g0).astype(BF16)
        k_ref[:, kb + LANES:kb + 2 * LANES] = _rope_lanes(kpe * rs * g1, tab).astype(BF16)
        vb = h * 2 * MLA_V
        v_ref[:, vb:vb + MLA_V] = acc[:, base + MLA_NOPE:base + MLA_NOPE + MLA_V].astype(BF16)
        v_ref[:, vb + MLA_V:vb + 2 * MLA_V] = jnp.ones((acc.shape[0], MLA_V), BF16)


def _mla_kv(ckv, w_kv_up, kpe, k_gain, tab, seq, tm=512):
    t, k = ckv.shape
    n = w_kv_up.shape[1]
    per = seq // tm
    g0 = k_gain[:MLA_NOPE].reshape(1, LANES)
    g1 = jnp.pad(k_gain[MLA_NOPE:], (0, LANES - MLA_ROPE)).reshape(1, LANES)
    return pl.pallas_call(
        _mla_kv_kernel,
        grid=(t // tm,),
        in_specs=[
            pl.BlockSpec((tm, k), lambda i: (i, 0)),
            pl.BlockSpec((k, n), lambda i: (0, 0)),
            pl.BlockSpec((tm, LANES), lambda i: (i, 0)),
            pl.BlockSpec((1, LANES), lambda i: (0, 0)),
            pl.BlockSpec((1, LANES), lambda i: (0, 0)),
            pl.BlockSpec((tm, 3 * LANES), lambda i: (i % per, 0)),
        ],
        out_specs=(
            pl.BlockSpec((tm, MLA_HEADS * MLA_QK_PAD), lambda i: (i, 0)),
            pl.BlockSpec((tm, MLA_HEADS * 2 * MLA_V), lambda i: (i, 0)),
        ),
        out_shape=(
            jax.ShapeDtypeStruct((t, MLA_HEADS * MLA_QK_PAD), BF16),
            jax.ShapeDtypeStruct((t, MLA_HEADS * 2 * MLA_V), BF16),
        ),
        scratch_shapes=[pltpu.VMEM((k, n), BF16)],
        compiler_params=_cparams(("arbitrary",)),
        name="mla_kv_up",
    )(ckv, w_kv_up, kpe, g0, g1, tab)


def _mla_attn_kernel(q_ref, k_ref, v_ref, o_ref, *, tq, heads):
    seq = q_ref.shape[0]
    nt = (((1,), (1,)), ((), ()))
    r = lax.broadcasted_iota(jnp.int32, (tq, tq), 0)
    c = lax.broadcasted_iota(jnp.int32, (tq, tq), 1)
    causal = c <= r
    for qi in range(seq // tq):
        rows = slice(qi * tq, (qi + 1) * tq)
        outs = []
        for hh in range(heads):
            qk_cols = slice(hh * MLA_QK_PAD, (hh + 1) * MLA_QK_PAD)
            v_cols = slice(hh * 2 * MLA_V, (hh + 1) * 2 * MLA_V)
            q = q_ref[rows, qk_cols]
            scores = []
            for j in range(qi + 1):
                s = lax.dot_general(q, k_ref[j * tq:(j + 1) * tq, qk_cols], nt, preferred_element_type=F32)
                if j == qi:
                    s = jnp.where(causal, s, NEG_BIG)
                scores.append(s)
            top = scores[0]
            for s in scores[1:]:
                top = jnp.maximum(top, s)
            m = jnp.max(top, axis=-1, keepdims=True)
            acc = None
            for j, s in enumerate(scores):
                p = jnp.exp(s - m).astype(BF16)
                pv = jnp.dot(p, v_ref[j * tq:(j + 1) * tq, v_cols], preferred_element_type=F32)
                acc = pv if acc is None else acc + pv
            outs.append((acc[:, :MLA_V] / acc[:, MLA_V:]).astype(o_ref.dtype))
        for hh in range(heads):
            o_ref[rows, hh * MLA_V:(hh + 1) * MLA_V] = outs[hh]


def _mla_attn(q, k, v, tq=256, heads=4):
    b, s, _ = q.shape
    return pl.pallas_call(
        functools.partial(_mla_attn_kernel, tq=tq, heads=heads),
        grid=(b, MLA_HEADS // heads),
        in_specs=[
            pl.BlockSpec((None, s, heads * MLA_QK_PAD), lambda bi, h: (bi, 0, h)),
            pl.BlockSpec((None, s, heads * MLA_QK_PAD), lambda bi, h: (bi, 0, h)),
            pl.BlockSpec((None, s, heads * 2 * MLA_V), lambda bi, h: (bi, 0, h)),
        ],
        out_specs=pl.BlockSpec((None, s, heads * MLA_V), lambda bi, h: (bi, 0, h)),
        out_shape=jax.ShapeDtypeStruct((b, s, MLA_HEADS * MLA_V), BF16),
        compiler_params=_cparams(("arbitrary", "arbitrary")),
        name="mla_attn",
    )(q, k, v)


PLAN_COLS = 256
(PLAN_EXP, PLAN_FIRST, PLAN_NEXT, PLAN_NUSED, PLAN_NVALID, PLAN_BSTART, PLAN_COUNT,
 PLAN_NEXT2) = range(8)


def _dispatch_plan_tile(cnt):
    nblk = jnp.floor((cnt + (EXPERT_BLOCK - 1.0)) * (1.0 / EXPERT_BLOCK))
    e_r = lax.broadcasted_iota(jnp.int32, (LANES, LANES), 0)
    e_c = lax.broadcasted_iota(jnp.int32, (LANES, LANES), 1)
    before = jnp.where(e_r < e_c, 1.0, 0.0).astype(BF16)
    used = jnp.where(nblk > 0.0, 1.0, 0.0)
    sub = lax.broadcasted_iota(jnp.int32, (8, LANES), 0)
    prefix = jnp.dot(jnp.where(sub == 0, nblk, used).astype(BF16), before, preferred_element_type=F32)
    bstart = prefix[0:1]
    ordinal = prefix[1:2]
    bend = bstart + nblk
    n_used = jnp.max(bend, axis=-1, keepdims=True)
    row = lax.broadcasted_iota(jnp.int32, (PLAN_COLS, LANES), 0).astype(F32)
    lane = lax.broadcasted_iota(jnp.int32, (PLAN_COLS, LANES), 1).astype(F32)
    is_exp = lane < N_EXPERTS
    row1 = row[:, 0:1]
    done = jnp.logical_and(bend <= row, is_exp)
    blk_exp = jnp.minimum(jnp.sum(jnp.where(done, 1.0, 0.0), axis=-1, keepdims=True), N_EXPERTS - 1.0)
    mine = lane == blk_exp
    bstart_of = jnp.sum(jnp.where(mine, bstart, 0.0), axis=-1, keepdims=True)
    cnt_of = jnp.sum(jnp.where(mine, cnt, 0.0), axis=-1, keepdims=True)
    ord_of = jnp.sum(jnp.where(mine, ordinal, 0.0), axis=-1, keepdims=True)
    valid = row1 < n_used
    first = jnp.where(jnp.logical_and(valid, row1 == bstart_of), ord_of + 1.0, 0.0)
    nvalid = jnp.clip(cnt_of - EXPERT_BLOCK * (row1 - bstart_of), 0.0, float(EXPERT_BLOCK))
    nvalid = jnp.where(valid, nvalid, 0.0)
    usable = jnp.logical_and(nblk > 0.0, is_exp)

    def next_used(after):
        nxt = jnp.min(jnp.where(jnp.logical_and(lane > after, usable), lane, 999.0), axis=-1, keepdims=True)
        return jnp.where(nxt > 998.0, -1.0, nxt)

    nxt = next_used(blk_exp)
    nxt2 = jnp.where(nxt < 0.0, -1.0, next_used(nxt))
    bstart_col = jnp.sum(jnp.where(lane < row, nblk, 0.0), axis=-1, keepdims=True)
    cnt_col = jnp.sum(jnp.where(lane == row, cnt, 0.0), axis=-1, keepdims=True)
    tile = jnp.zeros((PLAN_COLS, LANES), F32)
    cols = {PLAN_EXP: blk_exp, PLAN_FIRST: first, PLAN_NEXT: nxt, PLAN_NUSED: n_used,
            PLAN_NVALID: nvalid, PLAN_BSTART: bstart_col, PLAN_COUNT: cnt_col, PLAN_NEXT2: nxt2}
    for k, val in cols.items():
        tile = jnp.where(lane == k, val, tile)
    return tile


def _router_kernel(x_ref, g_ref, sh_ref, sc_ref, wr_ref, br_ref, h_ref, info_ref, slots_ref, plan_ref,
                   carry_ref, idx_all):
    i = pl.program_id(0)
    tm = x_ref.shape[0]

    @pl.when(i == 0)
    def _():
        carry_ref[...] = jnp.zeros(carry_ref.shape, F32)

    h = _norm_mod(x_ref[...], g_ref[...], sh_ref[...], sc_ref[...])
    _store_row_tiles(h_ref, _pack_halves(h))
    w = wr_ref[...]
    w_hi = w.astype(BF16)
    w_lo = (w - w_hi.astype(F32)).astype(BF16)
    h_hi = h.astype(BF16)
    h_lo = (h - h_hi.astype(F32)).astype(BF16)
    lg = (jnp.dot(h_hi, w_hi, preferred_element_type=F32)
          + jnp.dot(h_lo, w_hi, preferred_element_type=F32)
          + jnp.dot(h_hi, w_lo, preferred_element_type=F32)) + br_ref[...]

    lane = lax.broadcasted_iota(jnp.int32, (tm, LANES), 1).astype(F32)
    no_lane = float(LANES)
    gl = jnp.where(lane < N_GROUPS, lg, NEG_BIG)
    gmax = jnp.max(gl, axis=-1, keepdims=True)
    g_idx = jnp.min(jnp.where(gl == gmax, lane, no_lane), axis=-1, keepdims=True)
    g_p = 1.0 / jnp.sum(jnp.exp(gl - gmax), axis=-1, keepdims=True)
    lo_lane = N_GROUPS + g_idx * EXPERTS_PER_GROUP
    in_grp = jnp.logical_and(lane >= lo_lane, lane < lo_lane + EXPERTS_PER_GROUP)
    ev = jnp.where(in_grp, lg, NEG_BIG)
    v1 = jnp.max(ev, axis=-1, keepdims=True)
    i1 = jnp.min(jnp.where(ev == v1, lane, no_lane), axis=-1, keepdims=True)
    ev2 = jnp.where(lane == i1, NEG_BIG, ev)
    v2 = jnp.max(ev2, axis=-1, keepdims=True)
    i2 = jnp.min(jnp.where(ev2 == v2, lane, no_lane), axis=-1, keepdims=True)
    e2 = jnp.exp(v2 - v1)
    den = 1.0 + e2
    w1 = (1.0 / den) * g_p
    w2 = (e2 / den) * g_p
    id1 = i1 - N_GROUPS
    id2 = i2 - N_GROUPS

    oh1 = lane == id1
    oh2 = lane == id2
    both = jnp.where(jnp.logical_or(oh1, oh2), 1.0, 0.0)
    r = lax.broadcasted_iota(jnp.int32, (tm, tm), 0)
    c = lax.broadcasted_iota(jnp.int32, (tm, tm), 1)
    tril = jnp.where(c < r, 1.0, 0.0).astype(BF16)
    before = jnp.dot(tril, both.astype(BF16), preferred_element_type=F32) + carry_ref[...]
    rank1 = jnp.sum(jnp.where(oh1, before, 0.0), axis=-1, keepdims=True)
    rank2 = jnp.sum(jnp.where(oh2, before, 0.0), axis=-1, keepdims=True)
    carry_ref[...] = carry_ref[...] + jnp.sum(both, axis=0, keepdims=True)

    info = jnp.zeros((tm, LANES), F32)
    for col, val in enumerate((id1, id2, rank1, rank2, w1, w2)):
        info = jnp.where(lane == col, val, info)
    info_ref[...] = info
    idx_all[:, pl.ds(pl.multiple_of(i * tm, tm), tm)] = jnp.transpose(info)[0:8]

    @pl.when(i == pl.num_programs(0) - 1)
    def _():
        plan = _dispatch_plan_tile(carry_ref[...])
        plan_ref[...] = jnp.transpose(plan)[0:8].astype(jnp.int32)
        first_row = plan[0:N_EXPERTS, PLAN_BSTART:PLAN_BSTART + 1] * float(EXPERT_BLOCK)
        chunk = 1024
        expert = lax.broadcasted_iota(jnp.int32, (N_EXPERTS, chunk), 0).astype(F32)
        for c0 in range(0, idx_all.shape[1], chunk):
            cols = slice(c0, c0 + chunk)
            rows = []
            for k in range(2):
                ids = idx_all[k:k + 1, cols]
                base = jnp.sum(jnp.where(expert == ids, first_row, 0.0), axis=0, keepdims=True)
                rows.append(base + idx_all[2 + k:3 + k, cols])
            slots_ref[:, cols] = jnp.concatenate(rows, axis=0).astype(jnp.int32)


def _router(x2, g, mod, seq, wr, br, tm=256):
    assert x2.shape[1] == 2 * ROW_TILE * LANES
    t, d = x2.shape
    per = seq // tm
    return pl.pallas_call(
        _router_kernel,
        grid=(t // tm,),
        in_specs=[
            pl.BlockSpec((tm, d), lambda i: (i, 0)),
            pl.BlockSpec((1, d), lambda i: (0, 0)),
            pl.BlockSpec((None, 1, d), lambda i: ((i // per) * 6 + 3, 0, 0)),
            pl.BlockSpec((None, 1, d), lambda i: ((i // per) * 6 + 4, 0, 0)),
            pl.BlockSpec((d, LANES), lambda i: (0, 0)),
            pl.BlockSpec((1, LANES), lambda i: (0, 0)),
        ],
        out_specs=(
            pl.BlockSpec((tm * ROW_TILE, LANES), lambda i: (i, 0)),
            pl.BlockSpec((tm, LANES), lambda i: (i, 0)),
            pl.BlockSpec((2, t), lambda i: (0, 0)),
            pl.BlockSpec((8, PLAN_COLS), lambda i: (0, 0)),
        ),
        out_shape=(
            jax.ShapeDtypeStruct((t * ROW_TILE, LANES), jnp.uint32),
            jax.ShapeDtypeStruct((t, LANES), F32),
            jax.ShapeDtypeStruct((2, t), jnp.int32),
            jax.ShapeDtypeStruct((8, PLAN_COLS), jnp.int32),
        ),
        scratch_shapes=[pltpu.VMEM((1, LANES), F32), pltpu.VMEM((8, t), F32)],
        compiler_params=_cparams(("arbitrary",)),
        name="moe_router",
    )(x2, g.reshape(1, d), mod, mod, wr, br)


GATHER_GROUP = 8


def _plan(plan, row, col):
    return plan[row * PLAN_COLS + col]


def _expert_kernel(plan, slots, h_hbm, wg_hbm, wu_hbm, wd_hbm, ys_ref,
                   row_tok, xbuf, xsem, sg, su, sd, wsem, wgb, wub, wdb, *, layer, n_tok):
    i = pl.program_id(0)
    nu = _plan(plan, PLAN_NUSED, 0)
    n_rows = row_tok.shape[0]

    def weight_copies(e, st):
        return (pltpu.make_async_copy(wg_hbm.at[layer, e], sg.at[st], wsem.at[st, 0]),
                pltpu.make_async_copy(wu_hbm.at[layer, e], su.at[st], wsem.at[st, 1]),
                pltpu.make_async_copy(wd_hbm.at[layer, e], sd.at[st], wsem.at[st, 2]))

    def n_groups(blk):
        return (_plan(plan, PLAN_NVALID, blk) + GATHER_GROUP - 1) // GATHER_GROUP

    def start_gather(blk, slot):
        base = blk * EXPERT_BLOCK

        def body(g, carry):
            for k in range(GATHER_GROUP):
                r = g * GATHER_GROUP + k
                tok = row_tok[base + r]
                src = pl.multiple_of(tok * ROW_TILE, ROW_TILE)
                dst = pl.multiple_of(r * ROW_TILE, ROW_TILE)
                pltpu.make_async_copy(h_hbm.at[pl.ds(src, ROW_TILE), :],
                                      xbuf.at[slot, pl.ds(dst, ROW_TILE), :], xsem.at[slot]).start()
            return carry

        lax.fori_loop(0, n_groups(blk), body, 0)

    def wait_gather(blk, slot):
        span = GATHER_GROUP * ROW_TILE

        def body(g, carry):
            pltpu.make_async_copy(h_hbm.at[pl.ds(0, span), :],
                                  xbuf.at[slot, pl.ds(0, span), :], xsem.at[slot]).wait()
            return carry

        lax.fori_loop(0, n_groups(blk), body, 0)

    @pl.when(i == 0)
    def _():
        for cp in weight_copies(_plan(plan, PLAN_EXP, 0), 0):
            cp.start(priority=1)
        second = _plan(plan, PLAN_NEXT, 0)

        @pl.when(second >= 0)
        def _():
            for cp in weight_copies(second, 1):
                cp.start(priority=1)

        xbuf[...] = jnp.zeros(xbuf.shape, xbuf.dtype)

        def pad_body(e, carry):
            end = _plan(plan, PLAN_BSTART, e) * EXPERT_BLOCK + _plan(plan, PLAN_COUNT, e)
            for k in range(GATHER_GROUP - 1):
                row_tok[jnp.minimum(end + k, n_rows - 1)] = 0
            return carry

        lax.fori_loop(0, N_EXPERTS, pad_body, 0)

        def fill_body(t, carry):
            row_tok[slots[0, t]] = t
            row_tok[slots[1, t]] = t
            return carry

        lax.fori_loop(0, n_tok, fill_body, 0, unroll=8)
        start_gather(0, 0)

    @pl.when(i < nu)
    def _():
        slot = i % 2

        @pl.when(i + 1 < nu)
        def _():
            start_gather(i + 1, 1 - slot)

        first = _plan(plan, PLAN_FIRST, i)

        @pl.when(first > 0)
        def _():
            st = (first - 1) % 2
            nxt2 = _plan(plan, PLAN_NEXT2, i)
            cps = weight_copies(_plan(plan, PLAN_EXP, i), st)
            nxt_cps = weight_copies(jnp.maximum(nxt2, 0), st)
            for cp, ncp, stage, dst in zip(cps, nxt_cps, (sg, su, sd), (wgb, wub, wdb)):
                cp.wait()
                dst[...] = stage[st].astype(BF16)

                @pl.when(nxt2 >= 0)
                def _():
                    ncp.start(priority=1)

        wait_gather(i, slot)
        parts = [_unpack_halves(w) for w in _load_row_tiles(xbuf.at[slot], EXPERT_BLOCK)]
        x_lo = jnp.concatenate([lo.astype(BF16) for lo, _ in parts], axis=1)
        x_hi = jnp.concatenate([hi.astype(BF16) for _, hi in parts], axis=1)
        half = wgb.shape[0] // 2
        g = (jnp.dot(x_lo, wgb[:half], preferred_element_type=F32)
             + jnp.dot(x_hi, wgb[half:], preferred_element_type=F32))
        u = (jnp.dot(x_lo, wub[:half], preferred_element_type=F32)
             + jnp.dot(x_hi, wub[half:], preferred_element_type=F32))
        hid = (_silu(g) * u).astype(BF16)
        _store_row_tiles(ys_ref, _pack_halves(jnp.dot(hid, wdb[...], preferred_element_type=F32)))

    @pl.when(i >= nu)
    def _():
        ys_ref[...] = jnp.zeros(ys_ref.shape, ys_ref.dtype)


def _expert_ffn(h2, w_gate, w_up, w_down, layer, plan, slots, n_blocks):
    t = h2.shape[0] // ROW_TILE
    d, f = w_gate.shape[2], w_gate.shape[3]
    n_rows = n_blocks * EXPERT_BLOCK
    grid_spec = pltpu.PrefetchScalarGridSpec(
        num_scalar_prefetch=2,
        grid=(n_blocks,),
        in_specs=[pl.BlockSpec(memory_space=pl.ANY)] * 4,
        out_specs=pl.BlockSpec((EXPERT_BLOCK * ROW_TILE, LANES), lambda i, *_: (i, 0)),
        scratch_shapes=[
            pltpu.SMEM((n_rows,), jnp.int32),
            pltpu.VMEM((2, EXPERT_BLOCK * ROW_TILE, LANES), jnp.uint32),
            pltpu.SemaphoreType.DMA((2,)),
            pltpu.VMEM((2, d, f), F32),
            pltpu.VMEM((2, d, f), F32),
            pltpu.VMEM((2, f, d), F32),
            pltpu.SemaphoreType.DMA((2, 3)),
            pltpu.VMEM((d, f), BF16),
            pltpu.VMEM((d, f), BF16),
            pltpu.VMEM((f, d), BF16),
        ],
    )
    weight_bytes = 3 * d * f * (2 * 4 + 2)
    vmem = weight_bytes + 8 * 1024 * 1024
    return pl.pallas_call(
        functools.partial(_expert_kernel, layer=layer, n_tok=t),
        grid_spec=grid_spec,
        out_shape=jax.ShapeDtypeStruct((n_rows * ROW_TILE, LANES), jnp.uint32),
        compiler_params=_cparams(("arbitrary",), vmem),
        name="moe_experts",
    )(plan, slots, h2, w_gate, w_up, w_down)


def _combine_kernel(slots, x_ref, info_ref, gate_ref, ys_hbm, *rest, tm, with_next):
    if with_next:
        ng_ref, nsh_ref, nsc_ref, o_ref, h_ref, buf, sem = rest
    else:
        o_ref, buf, sem = rest
    i = pl.program_id(0)
    nsteps = pl.num_programs(0)

    def start_gather(step, slot):
        base = step * tm

        def body(r, carry):
            dst = pl.multiple_of(r * ROW_TILE, ROW_TILE)
            for k in range(2):
                src = pl.multiple_of(slots[k, base + r] * ROW_TILE, ROW_TILE)
                pltpu.make_async_copy(ys_hbm.at[pl.ds(src, ROW_TILE), :],
                                      buf.at[slot, k, pl.ds(dst, ROW_TILE), :], sem.at[slot]).start(priority=k)
            return carry

        lax.fori_loop(0, tm, body, 0, unroll=8)

    @pl.when(i == 0)
    def _():
        start_gather(0, 0)

    slot = i % 2

    @pl.when(i + 1 < nsteps)
    def _():
        start_gather(i + 1, 1 - slot)

    for k in range(2):
        pltpu.make_async_copy(ys_hbm.at[pl.ds(0, tm * ROW_TILE), :], buf.at[slot, k], sem.at[slot]).wait()
    info = info_ref[...]
    w0 = info[:, 4:5]
    w1 = info[:, 5:6]
    half = x_ref.shape[1] // 2
    a_tiles = _load_row_tiles(buf.at[slot, 0], tm)
    b_tiles = _load_row_tiles(buf.at[slot, 1], tm)
    for s in range(ROW_TILE):
        a_lo, a_hi = _unpack_halves(a_tiles[s])
        b_lo, b_hi = _unpack_halves(b_tiles[s])
        lo = slice(s * LANES, (s + 1) * LANES)
        hi = slice(half + s * LANES, half + (s + 1) * LANES)
        o_ref[:, lo] = x_ref[:, lo] + gate_ref[:, lo] * (w0 * a_lo + w1 * b_lo)
        o_ref[:, hi] = x_ref[:, hi] + gate_ref[:, hi] * (w0 * a_hi + w1 * b_hi)
    if with_next:
        h_ref[...] = _norm_mod(o_ref[...], ng_ref[...], nsh_ref[...], nsc_ref[...]).astype(h_ref.dtype)


def _combine(x2, info, mod, seq, ys, slots, next_norm=None, tm=128):
    t, d = x2.shape
    per = seq // tm
    row = lambda i, *_: (i, 0)
    mod_row = lambda k: (lambda i, *_: ((i // per) * 6 + k, 0, 0))
    in_specs = [
        pl.BlockSpec((tm, d), row),
        pl.BlockSpec((tm, LANES), row),
        pl.BlockSpec((None, 1, d), mod_row(5)),
        pl.BlockSpec(memory_space=pl.ANY),
    ]
    args = [slots, x2, info, mod, ys]
    out_specs = pl.BlockSpec((tm, d), row)
    out_shape = jax.ShapeDtypeStruct((t, d), F32)
    if next_norm is not None:
        next_g, next_mod = next_norm
        in_specs += [pl.BlockSpec((1, d), lambda i, *_: (0, 0)),
                     pl.BlockSpec((None, 1, d), mod_row(0)), pl.BlockSpec((None, 1, d), mod_row(1))]
        args += [next_g.reshape(1, d), next_mod, next_mod]
        out_specs = (out_specs, pl.BlockSpec((tm, d), row))
        out_shape = (out_shape, jax.ShapeDtypeStruct((t, d), BF16))
    grid_spec = pltpu.PrefetchScalarGridSpec(
        num_scalar_prefetch=1,
        grid=(t // tm,),
        in_specs=in_specs,
        out_specs=out_specs,
        scratch_shapes=[
            pltpu.VMEM((2, 2, tm * ROW_TILE, LANES), jnp.uint32),
            pltpu.SemaphoreType.DMA((2,)),
        ],
    )
    return pl.pallas_call(
        functools.partial(_combine_kernel, tm=tm, with_next=next_norm is not None),
        grid_spec=grid_spec,
        out_shape=out_shape,
        compiler_params=_cparams(("arbitrary",)),
        name="moe_combine",
    )(*args)


def _hier_moe(x2, norm_g, mod, seq, w_rg, b_rg, w_re, b_re, w_gate, w_up, w_down, layer, next_norm=None):
    t, d = x2.shape
    pad = LANES - N_GROUPS - N_EXPERTS
    wr = jnp.concatenate([w_rg, w_re, jnp.zeros((d, pad), F32)], axis=1)
    br = jnp.concatenate([b_rg, b_re, jnp.zeros((pad,), F32)]).reshape(1, LANES)
    h2, info, slots, plan = _router(x2, norm_g, mod, seq, wr, br)
    n_assign = 2 * t
    n_blocks = (n_assign + N_EXPERTS * (EXPERT_BLOCK - 1) + EXPERT_BLOCK - 1) // EXPERT_BLOCK
    assert n_blocks <= PLAN_COLS
    plan = plan.reshape(-1)
    ys = _expert_ffn(h2, w_gate, w_up, w_down, layer, plan, slots, n_blocks)
    return _combine(x2, info, mod, seq, ys, slots, next_norm)


def kernel(x, c, ada_w, ada_b, norm1_g, norm2_g, dsa_w_in, dsa_q_gain, dsa_k_gain, dsa_w_out, mla_w_in, mla_cq_gain, mla_ckv_gain, mla_w_q_up, mla_w_kv_up, mla_q_gain, mla_k_gain, mla_w_out, router_group_w, router_group_b, router_expert_w, router_expert_b, expert_w_gate, expert_w_up, expert_w_down):
    b, s, d = x.shape
    t = b * s
    mods = _ada_mod(c, ada_w, ada_b)
    x2 = x.reshape(t, d)

    mod = mods[0]
    h = _normmod(x2, norm1_g[0], mod, s, 0, 1)
    gains = (dsa_q_gain[0], dsa_k_gain[0], dsa_k_gain[0])
    qkv = [[_dsa_proj(h, dsa_w_in[0], gains[w], g, w, b) for w in range(3)] for g in range(len(DIL_PAIRS))]
    og2 = _dsa_attn(*qkv[2], 2)
    og1 = _dsa_attn(*qkv[1], 1)
    o = _dsa_attn(*qkv[0], 0, others=(og1, og2))
    x2 = _resid_mm(o.reshape(t, DSA_WIDTH), dsa_w_out[0], x2, mod, s, 2)
    x2, h = _hier_moe(x2, norm2_g[0], mod, s, router_group_w[0], router_group_b[0],
                      router_expert_w[0], router_expert_b[0],
                      expert_w_gate, expert_w_up, expert_w_down, 0, next_norm=(norm1_g[1], mods[1]))

    mod = mods[1]
    cq, ckv, kpe = _mla_in(h, mla_w_in[0], mla_cq_gain[0], mla_ckv_gain[0])
    tab = _rope_tables(s)
    w_q_pad = jnp.pad(mla_w_q_up[0].reshape(MLA_Q_LORA, MLA_HEADS, MLA_QK),
                      ((0, 0), (0, 0), (0, MLA_QK_PAD - MLA_QK))).reshape(MLA_Q_LORA, MLA_HEADS * MLA_QK_PAD)
    q = _mla_q(cq, w_q_pad, mla_q_gain[0], tab, s)
    k, v = _mla_kv(ckv, mla_w_kv_up[0], kpe, mla_k_gain[0], tab, s)
    o = _mla_attn(q.reshape(b, s, -1), k.reshape(b, s, -1), v.reshape(b, s, -1))
    x2 = _resid_mm(o.reshape(t, MLA_HEADS * MLA_V), mla_w_out[0], x2, mod, s, 2)
    x2 = _hier_moe(x2, norm2_g[1], mod, s, router_group_w[1], router_group_b[1],
                   router_expert_w[1], router_expert_b[1],
                   expert_w_gate, expert_w_up, expert_w_down, 1)
    return x2.reshape(b, s, d)
```

```python
import functools
import math

import jax
import jax.numpy as jnp
import numpy as np
from jax import lax
from jax.experimental import pallas as pl
from jax.experimental.pallas import tpu as pltpu

F32 = jnp.float32
BF16 = jnp.bfloat16

D_MODEL = 2048
EPS = 1e-6
LANES = 128
NEG_BIG = -1e30

DIL_PAIRS = ((128, 1), (512, 4), (2048, 16))
DSA_HEADS = 8
DSA_HEAD_DIM = 128
DSA_WIDTH = DSA_HEADS * DSA_HEAD_DIM
BAND = 128

MLA_HEADS = 16
MLA_Q_LORA = 512
MLA_KV_LORA = 512
MLA_NOPE = 128
MLA_ROPE = 64
MLA_V = 128
MLA_QK = MLA_NOPE + MLA_ROPE
MLA_QK_PAD = 256
ROPE_THETA = 10000.0

N_GROUPS = 4
EXPERTS_PER_GROUP = 16
N_EXPERTS = N_GROUPS * EXPERTS_PER_GROUP
D_EXPERT = 768
EXPERT_BLOCK = 128

VMEM_LIMIT = 48 * 1024 * 1024


def _cparams(sem, vmem=VMEM_LIMIT):
    return pltpu.CompilerParams(dimension_semantics=sem, vmem_limit_bytes=vmem)


def _silu(x):
    return x * (1.0 / (1.0 + jnp.exp(-x)))


def _pack_halves(x):
    n = x.shape[1] // 2
    xb = x.astype(BF16).astype(F32)
    lo = lax.bitcast_convert_type(xb[:, :n], jnp.uint32) >> 16
    hi = lax.bitcast_convert_type(xb[:, n:], jnp.uint32) & jnp.uint32(0xFFFF0000)
    return hi | lo


def _unpack_halves(w):
    lo = lax.bitcast_convert_type(w << 16, F32)
    hi = lax.bitcast_convert_type(w & jnp.uint32(0xFFFF0000), F32)
    return lo, hi


ROW_TILE = 8


def _store_row_tiles(ref, words):
    rows = words.shape[0]
    for s in range(ROW_TILE):
        ref[pl.ds(s, rows, stride=ROW_TILE), :] = words[:, s * LANES:(s + 1) * LANES]


def _load_row_tiles(ref, rows):
    return [ref[pl.ds(s, rows, stride=ROW_TILE), :] for s in range(ROW_TILE)]


def _norm_mod(x, g, shift, scale):
    ms = jnp.mean(x * x, axis=-1, keepdims=True)
    y = x * lax.rsqrt(ms + EPS) * g
    return y * (1.0 + scale) + shift


def _ada_kernel(c_ref, w_ref, b_ref, o_ref):
    ca = _silu(c_ref[...])
    hi = ca.astype(BF16)
    lo = (ca - hi.astype(F32)).astype(BF16)
    lhs = jnp.concatenate([hi, lo], axis=0)
    res = jnp.dot(lhs, w_ref[...].astype(BF16), preferred_element_type=F32)
    o_ref[...] = res[:8] + res[8:] + b_ref[...]


def _ada_mod(c, ada_w, ada_b):
    depth, d, n = ada_w.shape
    b = c.shape[0]
    c8 = jnp.pad(c, ((0, 8 - b), (0, 0)))
    tn = 1024
    out = pl.pallas_call(
        _ada_kernel,
        grid=(depth, n // tn),
        in_specs=[
            pl.BlockSpec((8, d), lambda i, j: (0, 0)),
            pl.BlockSpec((None, d, tn), lambda i, j: (i, 0, j)),
            pl.BlockSpec((None, 1, tn), lambda i, j: (i, 0, j)),
        ],
        out_specs=pl.BlockSpec((None, 8, tn), lambda i, j: (i, 0, j)),
        out_shape=jax.ShapeDtypeStruct((depth, 8, n), F32),
        compiler_params=_cparams(("arbitrary", "arbitrary")),
        name="ada_mod",
    )(c8, ada_w, ada_b.reshape(depth, 1, n))
    return out[:, :b].reshape(depth, b * 6, 1, d)


def _normmod_kernel(x_ref, g_ref, sh_ref, sc_ref, o_ref):
    o_ref[...] = _norm_mod(x_ref[...], g_ref[...], sh_ref[...], sc_ref[...]).astype(o_ref.dtype)


def _normmod(x2, g, mod, seq, k_shift, k_scale, tm=512):
    t, d = x2.shape
    per = seq // tm
    return pl.pallas_call(
        _normmod_kernel,
        grid=(t // tm,),
        in_specs=[
            pl.BlockSpec((tm, d), lambda i: (i, 0)),
            pl.BlockSpec((1, d), lambda i: (0, 0)),
            pl.BlockSpec((None, 1, d), lambda i: ((i // per) * 6 + k_shift, 0, 0)),
            pl.BlockSpec((None, 1, d), lambda i: ((i // per) * 6 + k_scale, 0, 0)),
        ],
        out_specs=pl.BlockSpec((tm, d), lambda i: (i, 0)),
        out_shape=jax.ShapeDtypeStruct((t, d), BF16),
        compiler_params=_cparams(("arbitrary",)),
        name="normmod",
    )(x2, g.reshape(1, d), mod, mod)


def _cast_weight_once(w_ref, wb_ref):
    @pl.when(pl.program_id(1) == 0)
    def _():
        wb_ref[...] = w_ref[...].astype(BF16)


def _dsa_proj_kernel(a_ref, w_ref, g_ref, o_ref, wb_ref, res_ref, *, dilation, normed, gain_scale):
    @pl.when(pl.program_id(0) == 0)
    def _():
        wb_ref[...] = w_ref[...].astype(BF16)

    gain = g_ref[...] * gain_scale
    tm = a_ref.shape[0]
    n_parts = 2
    pr = tm // n_parts
    accs = [jnp.dot(a_ref[p * pr:(p + 1) * pr, :], wb_ref[...], preferred_element_type=F32)
            for p in range(n_parts)]
    for p, acc in enumerate(accs):
        blks = []
        for h in range(DSA_HEADS):
            blk = acc[:, h * LANES:(h + 1) * LANES]
            if normed:
                ms = jnp.mean(blk * blk, axis=-1, keepdims=True)
                blk = blk * lax.rsqrt(ms + EPS) * gain
            blks.append(blk)
        for h, blk in enumerate(blks):
            hs = slice(h * LANES, (h + 1) * LANES)
            if dilation == 1:
                o_ref[0, p * pr:(p + 1) * pr, hs] = blk.astype(o_ref.dtype)
            else:
                res_ref[h, p * pr:(p + 1) * pr, :] = blk
                sub_rows = pr // dilation
                for r in range(dilation):
                    sub = res_ref[h, pl.ds(p * pr + r, sub_rows, stride=dilation), :]
                    o_ref[r, p * sub_rows:(p + 1) * sub_rows, hs] = sub.astype(o_ref.dtype)


def _dsa_proj(h, w_in, gain, group, which, batch, tm=512, tn=DSA_WIDTH):
    t, d = h.shape
    dil = DIL_PAIRS[group][1]
    seq = t // batch
    per = seq // tm
    gain_scale = 1.0 / math.sqrt(DSA_HEAD_DIM) if which == 0 else 1.0
    return pl.pallas_call(
        functools.partial(_dsa_proj_kernel, dilation=dil, normed=which != 2, gain_scale=gain_scale),
        grid=(t // tm,),
        in_specs=[
            pl.BlockSpec((tm, d), lambda i: (i, 0)),
            pl.BlockSpec((d, tn), lambda i: (0, group * 3 + which)),
            pl.BlockSpec((1, LANES), lambda i: (0, 0)),
        ],
        out_specs=pl.BlockSpec((None, dil, tm // dil, tn), lambda i: (i // per, 0, i % per, 0)),
        out_shape=jax.ShapeDtypeStruct((batch, dil, seq // dil, tn), BF16),
        scratch_shapes=[pltpu.VMEM((d, tn), BF16), pltpu.VMEM((DSA_HEADS, tm, LANES), F32)],
        compiler_params=_cparams(("arbitrary",)),
        name=f"dsa_proj_g{group}_{'qkv'[which]}",
    )(h, w_in, gain.reshape(1, LANES))


def _resid_mm_kernel(a_ref, w_ref, x_ref, gate_ref, o_ref, wb_ref):
    _cast_weight_once(w_ref, wb_ref)
    y = jnp.dot(a_ref[...], wb_ref[...], preferred_element_type=F32)
    o_ref[...] = x_ref[...] + gate_ref[...] * y


def _resid_mm(a, w, x2, mod, seq, k_gate, tm=512, tn=1024):
    t, k = a.shape
    n = w.shape[1]
    per = seq // tm
    return pl.pallas_call(
        _resid_mm_kernel,
        grid=(n // tn, t // tm),
        in_specs=[
            pl.BlockSpec((tm, k), lambda j, i: (i, 0)),
            pl.BlockSpec((k, tn), lambda j, i: (0, j)),
            pl.BlockSpec((tm, tn), lambda j, i: (i, j)),
            pl.BlockSpec((None, 1, tn), lambda j, i: ((i // per) * 6 + k_gate, 0, j)),
        ],
        out_specs=pl.BlockSpec((tm, tn), lambda j, i: (i, j)),
        out_shape=jax.ShapeDtypeStruct((t, n), F32),
        scratch_shapes=[pltpu.VMEM((k, tn), BF16)],
        compiler_params=_cparams(("arbitrary", "arbitrary")),
        name="resid_mm",
    )(a, w, x2, mod)


def _alibi_slope(head_slot, group):
    n = len(DIL_PAIRS) * DSA_HEADS
    return 2.0 ** (-8.0 * (head_slot * len(DIL_PAIRS) + group + 1.0) / n)


def _dsa_bias_table(group, has_prev):
    window, d = DIL_PAIRS[group]
    steps = window // d
    qi = np.arange(BAND)[:, None]
    if has_prev:
        kj = np.arange(2 * BAND)[None, :]
        delta = qi + BAND - kj
        prev_key = np.broadcast_to(kj < BAND, delta.shape)
    else:
        kj = np.arange(BAND)[None, :]
        delta = qi - kj
        prev_key = np.zeros(delta.shape, bool)
    inside = (delta >= 0) & (delta <= steps)
    tabs = []
    for first in (True, False):
        valid = inside & ~(prev_key & first)
        per_head = [np.where(valid, -_alibi_slope(h, group) * (delta * d), NEG_BIG) for h in range(DSA_HEADS)]
        tabs.append(np.stack(per_head))
    return jnp.asarray(np.stack(tabs), F32)


def _dsa_attn_kernel(*refs, dilation, has_prev, merge):
    refs = list(refs)
    q_ref, kc_ref, vc_ref, bias_ref = refs[:4]
    pos = 4
    if has_prev:
        kp_ref, vp_ref = refs[pos:pos + 2]
        pos += 2
    if merge:
        other = refs[pos:pos + 4]
        pos += 4
        o_ref = refs[pos]
    else:
        o_ref, lse_ref = refs[pos:pos + 2]

    r = pl.program_id(2)
    lane = lax.broadcasted_iota(jnp.int32, (BAND, LANES), 1)
    nt = (((1,), (1,)), ((), ()))
    nk = 2 * BAND if has_prev else BAND
    ones = jnp.ones((nk, LANES), BF16)
    n_batch = q_ref.shape[0]

    def head_scores(bi, h):
        hs = slice(h * LANES, (h + 1) * LANES)
        q = q_ref[bi, :, hs]
        if has_prev:
            k = jnp.concatenate([kp_ref[bi, :, hs], kc_ref[bi, :, hs]], axis=0)
        else:
            k = kc_ref[bi, :, hs]
        return lax.dot_general(q, k, nt, preferred_element_type=F32) + bias_ref[h]

    def head_values(bi, h):
        hs = slice(h * LANES, (h + 1) * LANES)
        if has_prev:
            v = jnp.concatenate([vp_ref[bi, :, hs], vc_ref[bi, :, hs]], axis=0)
        else:
            v = vc_ref[bi, :, hs]
        return jnp.concatenate([v, ones], axis=1)

    results = []
    for bi in range(n_batch):
        scores = [head_scores(bi, h) for h in range(DSA_HEADS)]
        maxes = [jnp.max(s, axis=-1, keepdims=True) for s in scores]
        probs = [jnp.exp(s - m).astype(BF16) for s, m in zip(scores, maxes)]
        accs = [jnp.dot(p, head_values(bi, h), preferred_element_type=F32) for h, p in enumerate(probs)]
        outs = []
        lse_tile = jnp.zeros((BAND, LANES), F32)
        for h in range(DSA_HEADS):
            l = accs[h][:, LANES:]
            o = accs[h][:, :LANES] / l
            lse = maxes[h] + jnp.log(l)
            if merge:
                lses = [lse] + [other[2 * g + 1][bi, :, h * 16:h * 16 + 1] for g in range(2)]
                parts = [o] + [other[2 * g][bi, h] for g in range(2)]
                top = jnp.maximum(jnp.maximum(lses[0], lses[1]), lses[2])
                es = [jnp.exp(x - top) for x in lses]
                den = es[0] + es[1] + es[2]
                o = ((es[0] * parts[0] + es[1] * parts[1] + es[2] * parts[2]) / den).astype(o_ref.dtype)
            else:
                in_head = jnp.logical_and(lane >= h * 16, lane < (h + 1) * 16)
                lse_tile = jnp.where(in_head, lse, lse_tile)
            outs.append(o)
        results.append((outs, lse_tile))

    for bi, (outs, lse_tile) in enumerate(results):
        for h, o in enumerate(outs):
            if merge:
                o_ref[bi, :, h * LANES:(h + 1) * LANES] = o
            else:
                o_ref[bi, h, pl.ds(r, BAND, stride=dilation), :] = o
        if not merge:
            lse_ref[bi, pl.ds(r, BAND, stride=dilation), :] = lse_tile


def _dsa_attn(q_g, k_g, v_g, group, others=None):
    b, d, sub_len, _ = q_g.shape
    s = d * sub_len
    nb = sub_len // BAND
    has_prev = nb > 1
    merge = others is not None
    assert not merge or d == 1
    bias = _dsa_bias_table(group, has_prev)

    bb = 2 if b % 2 == 0 else 1
    cur = lambda bi, n, r: (bi, r, n, 0)
    prev = lambda bi, n, r: (bi, r, jnp.maximum(n - 1, 0), 0)
    blk = (bb, None, BAND, DSA_WIDTH)
    in_specs = [pl.BlockSpec(blk, cur), pl.BlockSpec(blk, cur), pl.BlockSpec(blk, cur),
                pl.BlockSpec((None,) + bias.shape[1:], lambda bi, n, r: (jnp.minimum(n, 1), 0, 0, 0))]
    args = [q_g, k_g, v_g, bias]
    if has_prev:
        in_specs += [pl.BlockSpec(blk, prev), pl.BlockSpec(blk, prev)]
        args += [k_g, v_g]
    span = BAND * d
    nat = lambda bi, n, r: (bi, n, 0)
    nat_heads = lambda bi, n, r: (bi, 0, n, 0)
    if merge:
        for o_g, lse_g in others:
            in_specs += [pl.BlockSpec((bb, DSA_HEADS, BAND, LANES), nat_heads),
                         pl.BlockSpec((bb, BAND, LANES), nat)]
            args += [o_g, lse_g]
        out_shape = jax.ShapeDtypeStruct((b, s, DSA_WIDTH), BF16)
        out_specs = pl.BlockSpec((bb, BAND, DSA_WIDTH), nat)
    else:
        out_shape = (jax.ShapeDtypeStruct((b, DSA_HEADS, s, LANES), F32), jax.ShapeDtypeStruct((b, s, LANES), F32))
        out_specs = (pl.BlockSpec((bb, DSA_HEADS, span, LANES), nat_heads),
                     pl.BlockSpec((bb, span, LANES), nat))
    return pl.pallas_call(
        functools.partial(_dsa_attn_kernel, dilation=d, has_prev=has_prev, merge=merge),
        grid=(b // bb, nb, d),
        in_specs=in_specs,
        out_specs=out_specs,
        out_shape=out_shape,
        compiler_params=_cparams(("arbitrary", "arbitrary", "arbitrary")),
        name=f"dsa_attn_g{group}",
    )(*args)


def _rope_tables(seq):
    half = MLA_ROPE // 2
    inv = np.float32(ROPE_THETA) ** (-np.arange(half, dtype=np.float32) / np.float32(half))
    ang = np.arange(seq, dtype=np.float32)[:, None] * inv[None, :]
    cos, sin = np.cos(ang), np.sin(ang)
    z = np.zeros((seq, LANES - MLA_ROPE), np.float32)
    zh = np.zeros((seq, half), np.float32)
    cos_t = np.concatenate([cos, cos, z], axis=1)
    sin_a = np.concatenate([zh, sin, z], axis=1)
    sin_b = np.concatenate([-sin, zh, z], axis=1)
    return jnp.asarray(np.concatenate([cos_t, sin_a, sin_b], axis=1), F32)


def _rope_lanes(x, tab):
    half = MLA_ROPE // 2
    cos_t = tab[:, 0:LANES]
    sin_a = tab[:, LANES:2 * LANES]
    sin_b = tab[:, 2 * LANES:3 * LANES]
    return (x * cos_t + pltpu.roll(x, half, 1) * sin_a
            + pltpu.roll(x, LANES - half, 1) * sin_b)


def _mla_in_kernel(a_ref, w_ref, cqg_ref, ckvg_ref, cq_ref, ckv_ref, kpe_ref, wb_ref):
    n = w_ref.shape[1]

    @pl.when(pl.program_id(0) == 0)
    def _():
        wb_ref[...] = jnp.zeros(wb_ref.shape, BF16)
        wb_ref[:, :n] = w_ref[...].astype(BF16)

    acc = jnp.dot(a_ref[...], wb_ref[...], preferred_element_type=F32)
    cq = acc[:, :MLA_Q_LORA]
    cq_ref[...] = (cq * lax.rsqrt(jnp.mean(cq * cq, axis=-1, keepdims=True) + EPS)
                   * cqg_ref[...]).astype(BF16)
    ckv = acc[:, MLA_Q_LORA:MLA_Q_LORA + MLA_KV_LORA]
    ckv_ref[...] = (ckv * lax.rsqrt(jnp.mean(ckv * ckv, axis=-1, keepdims=True) + EPS)
                    * ckvg_ref[...]).astype(BF16)
    kpe_ref[...] = acc[:, MLA_Q_LORA + MLA_KV_LORA:]


def _mla_in(h, w_in, cq_gain, ckv_gain, tm=512):
    t, d = h.shape
    n = w_in.shape[2]
    n_pad = MLA_Q_LORA + MLA_KV_LORA + LANES
    return pl.pallas_call(
        _mla_in_kernel,
        grid=(t // tm,),
        in_specs=[
            pl.BlockSpec((tm, d), lambda i: (i, 0)),
            pl.BlockSpec((None, d, n), lambda i: (0, 0, 0)),
            pl.BlockSpec((1, MLA_Q_LORA), lambda i: (0, 0)),
            pl.BlockSpec((1, MLA_KV_LORA), lambda i: (0, 0)),
        ],
        out_specs=(
            pl.BlockSpec((tm, MLA_Q_LORA), lambda i: (i, 0)),
            pl.BlockSpec((tm, MLA_KV_LORA), lambda i: (i, 0)),
            pl.BlockSpec((tm, LANES), lambda i: (i, 0)),
        ),
        out_shape=(
            jax.ShapeDtypeStruct((t, MLA_Q_LORA), BF16),
            jax.ShapeDtypeStruct((t, MLA_KV_LORA), BF16),
            jax.ShapeDtypeStruct((t, LANES), F32),
        ),
        scratch_shapes=[pltpu.VMEM((d, n_pad), BF16)],
        compiler_params=_cparams(("arbitrary",)),
        name="mla_in",
    )(h, w_in, cq_gain.reshape(1, -1), ckv_gain.reshape(1, -1))


def _mla_q_kernel(a_ref, w_ref, g0_ref, g1_ref, tab_ref, o_ref, wb_ref):
    @pl.when(pl.program_id(0) == 0)
    def _():
        wb_ref[...] = jnp.zeros(wb_ref.shape, BF16)
        for h in range(MLA_HEADS):
            src = h * MLA_QK
            dst = h * MLA_QK_PAD
            wb_ref[:, dst:dst + MLA_NOPE] = w_ref[:, src:src + MLA_NOPE].astype(BF16)
            wb_ref[:, dst + MLA_NOPE:dst + MLA_QK] = w_ref[:, src + MLA_NOPE:src + MLA_QK].astype(BF16)

    acc = jnp.dot(a_ref[...], wb_ref[...], preferred_element_type=F32)
    tab = tab_ref[...]
    scale = 1.0 / math.sqrt(MLA_QK)
    g0 = g0_ref[...] * scale
    g1 = g1_ref[...] * scale
    for h in range(MLA_HEADS):
        base = h * MLA_QK_PAD
        x0 = acc[:, base:base + LANES]
        x1 = acc[:, base + LANES:base + 2 * LANES]
        ss = jnp.sum(x0 * x0, axis=-1, keepdims=True) + jnp.sum(x1 * x1, axis=-1, keepdims=True)
        rs = lax.rsqrt(ss / MLA_QK + EPS)
        o_ref[:, base:base + LANES] = (x0 * rs * g0).astype(BF16)
        o_ref[:, base + LANES:base + 2 * LANES] = _rope_lanes(x1 * rs * g1, tab).astype(BF16)


def _mla_q(cq, w_q_up, q_gain, tab, seq, tm=512):
    t, k = cq.shape
    n = MLA_HEADS * MLA_QK_PAD
    per = seq // tm
    g0 = q_gain[:MLA_NOPE].reshape(1, LANES)
    g1 = jnp.pad(q_gain[MLA_NOPE:], (0, LANES - MLA_ROPE)).reshape(1, LANES)
    return pl.pallas_call(
        _mla_q_kernel,
        grid=(t // tm,),
        in_specs=[
            pl.BlockSpec((tm, k), lambda i: (i, 0)),
            pl.BlockSpec((None, k, MLA_HEADS * MLA_QK), lambda i: (0, 0, 0)),
            pl.BlockSpec((1, LANES), lambda i: (0, 0)),
            pl.BlockSpec((1, LANES), lambda i: (0, 0)),
            pl.BlockSpec((tm, 3 * LANES), lambda i: (i % per, 0)),
        ],
        out_specs=pl.BlockSpec((tm, n), lambda i: (i, 0)),
        out_shape=jax.ShapeDtypeStruct((t, n), BF16),
        scratch_shapes=[pltpu.VMEM((k, n), BF16)],
        compiler_params=_cparams(("arbitrary",)),
        name="mla_q_up",
    )(cq, w_q_up, g0, g1, tab)


def _mla_kv_kernel(a_ref, w_ref, kpe_ref, g0_ref, g1_ref, tab_ref, k_ref, v_ref, wb_ref):
    @pl.when(pl.program_id(0) == 0)
    def _():
        wb_ref[...] = w_ref[...].astype(BF16)

    acc = jnp.dot(a_ref[...], wb_ref[...], preferred_element_type=F32)
    tab = tab_ref[...]
    kpe = kpe_ref[...]
    ss_pe = jnp.sum(kpe * kpe, axis=-1, keepdims=True)
    g0 = g0_ref[...]
    kpe_rot = _rope_lanes(kpe * g1_ref[...], tab)
    for h in range(MLA_HEADS):
        base = h * (MLA_NOPE + MLA_V)
        kn = acc[:, base:base + MLA_NOPE]
        ss = jnp.sum(kn * kn, axis=-1, keepdims=True) + ss_pe
        rs = lax.rsqrt(ss / MLA_QK + EPS)
        kb = h * MLA_QK_PAD
        k_ref[:, kb:kb + LANES] = (kn * rs * g0).astype(BF16)
        k_ref[:, kb + LANES:kb + 2 * LANES] = (kpe_rot * rs).astype(BF16)
        vb = h * 2 * MLA_V
        v_ref[:, vb:vb + MLA_V] = acc[:, base + MLA_NOPE:base + MLA_NOPE + MLA_V].astype(BF16)
        v_ref[:, vb + MLA_V:vb + 2 * MLA_V] = jnp.ones((acc.shape[0], MLA_V), BF16)


def _mla_kv(ckv, w_kv_up, kpe, k_gain, tab, seq, tm=512):
    t, k = ckv.shape
    n = w_kv_up.shape[1]
    per = seq // tm
    g0 = k_gain[:MLA_NOPE].reshape(1, LANES)
    g1 = jnp.pad(k_gain[MLA_NOPE:], (0, LANES - MLA_ROPE)).reshape(1, LANES)
    return pl.pallas_call(
        _mla_kv_kernel,
        grid=(t // tm,),
        in_specs=[
            pl.BlockSpec((tm, k), lambda i: (i, 0)),
            pl.BlockSpec((k, n), lambda i: (0, 0)),
            pl.BlockSpec((tm, LANES), lambda i: (i, 0)),
            pl.BlockSpec((1, LANES), lambda i: (0, 0)),
            pl.BlockSpec((1, LANES), lambda i: (0, 0)),
            pl.BlockSpec((tm, 3 * LANES), lambda i: (i % per, 0)),
        ],
        out_specs=(
            pl.BlockSpec((tm, MLA_HEADS * MLA_QK_PAD), lambda i: (i, 0)),
            pl.BlockSpec((tm, MLA_HEADS * 2 * MLA_V), lambda i: (i, 0)),
        ),
        out_shape=(
            jax.ShapeDtypeStruct((t, MLA_HEADS * MLA_QK_PAD), BF16),
            jax.ShapeDtypeStruct((t, MLA_HEADS * 2 * MLA_V), BF16),
        ),
        scratch_shapes=[pltpu.VMEM((k, n), BF16)],
        compiler_params=_cparams(("arbitrary",)),
        name="mla_kv_up",
    )(ckv, w_kv_up, kpe, g0, g1, tab)


def _mla_attn_kernel(q_ref, k_ref, v_ref, o_ref, *, tq, heads):
    seq = q_ref.shape[0]
    nt = (((1,), (1,)), ((), ()))
    r = lax.broadcasted_iota(jnp.int32, (tq, tq), 0)
    c = lax.broadcasted_iota(jnp.int32, (tq, tq), 1)
    causal = c <= r
    for qi in range(seq // tq):
        rows = slice(qi * tq, (qi + 1) * tq)
        outs = []
        for hh in range(heads):
            qk_cols = slice(hh * MLA_QK_PAD, (hh + 1) * MLA_QK_PAD)
            v_cols = slice(hh * 2 * MLA_V, (hh + 1) * 2 * MLA_V)
            q = q_ref[rows, qk_cols]
            scores = []
            for j in range(qi + 1):
                s = lax.dot_general(q, k_ref[j * tq:(j + 1) * tq, qk_cols], nt, preferred_element_type=F32)
                if j == qi:
                    s = jnp.where(causal, s, NEG_BIG)
                scores.append(s)
            top = scores[0]
            for s in scores[1:]:
                top = jnp.maximum(top, s)
            m = jnp.max(top, axis=-1, keepdims=True)
            acc = None
            for j, s in enumerate(scores):
                p = jnp.exp(s - m).astype(BF16)
                pv = jnp.dot(p, v_ref[j * tq:(j + 1) * tq, v_cols], preferred_element_type=F32)
                acc = pv if acc is None else acc + pv
            outs.append((acc[:, :MLA_V] / acc[:, MLA_V:]).astype(o_ref.dtype))
        for hh in range(heads):
            o_ref[rows, hh * MLA_V:(hh + 1) * MLA_V] = outs[hh]


def _mla_attn(q, k, v, tq=256, heads=4):
    b, s, _ = q.shape
    return pl.pallas_call(
        functools.partial(_mla_attn_kernel, tq=tq, heads=heads),
        grid=(b, MLA_HEADS // heads),
        in_specs=[
            pl.BlockSpec((None, s, heads * MLA_QK_PAD), lambda bi, h: (bi, 0, h)),
            pl.BlockSpec((None, s, heads * MLA_QK_PAD), lambda bi, h: (bi, 0, h)),
            pl.BlockSpec((None, s, heads * 2 * MLA_V), lambda bi, h: (bi, 0, h)),
        ],
        out_specs=pl.BlockSpec((None, s, heads * MLA_V), lambda bi, h: (bi, 0, h)),
        out_shape=jax.ShapeDtypeStruct((b, s, MLA_HEADS * MLA_V), BF16),
        compiler_params=_cparams(("arbitrary", "arbitrary")),
        name="mla_attn",
    )(q, k, v)


PLAN_COLS = 256
(PLAN_EXP, PLAN_FIRST, PLAN_NEXT, PLAN_NUSED, PLAN_NVALID, PLAN_BSTART, PLAN_COUNT,
 PLAN_NEXT2) = range(8)


def _dispatch_plan_tile(cnt):
    nblk = jnp.floor((cnt + (EXPERT_BLOCK - 1.0)) * (1.0 / EXPERT_BLOCK))
    e_r = lax.broadcasted_iota(jnp.int32, (LANES, LANES), 0)
    e_c = lax.broadcasted_iota(jnp.int32, (LANES, LANES), 1)
    before = jnp.where(e_r < e_c, 1.0, 0.0).astype(BF16)
    used = jnp.where(nblk > 0.0, 1.0, 0.0)
    sub = lax.broadcasted_iota(jnp.int32, (8, LANES), 0)
    prefix = jnp.dot(jnp.where(sub == 0, nblk, used).astype(BF16), before, preferred_element_type=F32)
    bstart = prefix[0:1]
    ordinal = prefix[1:2]
    bend = bstart + nblk
    n_used = jnp.max(bend, axis=-1, keepdims=True)
    row = lax.broadcasted_iota(jnp.int32, (PLAN_COLS, LANES), 0).astype(F32)
    lane = lax.broadcasted_iota(jnp.int32, (PLAN_COLS, LANES), 1).astype(F32)
    is_exp = lane < N_EXPERTS
    row1 = row[:, 0:1]
    done = jnp.logical_and(bend <= row, is_exp)
    blk_exp = jnp.minimum(jnp.sum(jnp.where(done, 1.0, 0.0), axis=-1, keepdims=True), N_EXPERTS - 1.0)
    mine = lane == blk_exp
    bstart_of = jnp.sum(jnp.where(mine, bstart, 0.0), axis=-1, keepdims=True)
    cnt_of = jnp.sum(jnp.where(mine, cnt, 0.0), axis=-1, keepdims=True)
    ord_of = jnp.sum(jnp.where(mine, ordinal, 0.0), axis=-1, keepdims=True)
    valid = row1 < n_used
    first = jnp.where(jnp.logical_and(valid, row1 == bstart_of), ord_of + 1.0, 0.0)
    nvalid = jnp.clip(cnt_of - EXPERT_BLOCK * (row1 - bstart_of), 0.0, float(EXPERT_BLOCK))
    nvalid = jnp.where(valid, nvalid, 0.0)
    usable = jnp.logical_and(nblk > 0.0, is_exp)

    def next_used(after):
        nxt = jnp.min(jnp.where(jnp.logical_and(lane > after, usable), lane, 999.0), axis=-1, keepdims=True)
        return jnp.where(nxt > 998.0, -1.0, nxt)

    nxt = next_used(blk_exp)
    nxt2 = jnp.where(nxt < 0.0, -1.0, next_used(nxt))
    bstart_col = jnp.sum(jnp.where(lane < row, nblk, 0.0), axis=-1, keepdims=True)
    cnt_col = jnp.sum(jnp.where(lane == row, cnt, 0.0), axis=-1, keepdims=True)
    tile = jnp.zeros((PLAN_COLS, LANES), F32)
    cols = {PLAN_EXP: blk_exp, PLAN_FIRST: first, PLAN_NEXT: nxt, PLAN_NUSED: n_used,
            PLAN_NVALID: nvalid, PLAN_BSTART: bstart_col, PLAN_COUNT: cnt_col, PLAN_NEXT2: nxt2}
    for k, val in cols.items():
        tile = jnp.where(lane == k, val, tile)
    return tile


def _router_kernel(x_ref, g_ref, sh_ref, sc_ref, wr_ref, br_ref, h_ref, info_ref, slots_ref, plan_ref,
                   carry_ref, idx_all):
    i = pl.program_id(0)
    tm = x_ref.shape[0]

    @pl.when(i == 0)
    def _():
        carry_ref[...] = jnp.zeros(carry_ref.shape, F32)

    h = _norm_mod(x_ref[...], g_ref[...], sh_ref[...], sc_ref[...])
    _store_row_tiles(h_ref, _pack_halves(h))
    w = wr_ref[...]
    w_hi = w.astype(BF16)
    w_lo = (w - w_hi.astype(F32)).astype(BF16)
    h_hi = h.astype(BF16)
    h_lo = (h - h_hi.astype(F32)).astype(BF16)
    lg = (jnp.dot(h_hi, w_hi, preferred_element_type=F32)
          + jnp.dot(h_lo, w_hi, preferred_element_type=F32)
          + jnp.dot(h_hi, w_lo, preferred_element_type=F32)) + br_ref[...]

    lane = lax.broadcasted_iota(jnp.int32, (tm, LANES), 1).astype(F32)
    no_lane = float(LANES)
    gl = jnp.where(lane < N_GROUPS, lg, NEG_BIG)
    gmax = jnp.max(gl, axis=-1, keepdims=True)
    g_idx = jnp.min(jnp.where(gl == gmax, lane, no_lane), axis=-1, keepdims=True)
    g_p = 1.0 / jnp.sum(jnp.exp(gl - gmax), axis=-1, keepdims=True)
    lo_lane = N_GROUPS + g_idx * EXPERTS_PER_GROUP
    in_grp = jnp.logical_and(lane >= lo_lane, lane < lo_lane + EXPERTS_PER_GROUP)
    ev = jnp.where(in_grp, lg, NEG_BIG)
    v1 = jnp.max(ev, axis=-1, keepdims=True)
    i1 = jnp.min(jnp.where(ev == v1, lane, no_lane), axis=-1, keepdims=True)
    ev2 = jnp.where(lane == i1, NEG_BIG, ev)
    v2 = jnp.max(ev2, axis=-1, keepdims=True)
    i2 = jnp.min(jnp.where(ev2 == v2, lane, no_lane), axis=-1, keepdims=True)
    e2 = jnp.exp(v2 - v1)
    den = 1.0 + e2
    w1 = (1.0 / den) * g_p
    w2 = (e2 / den) * g_p
    id1 = i1 - N_GROUPS
    id2 = i2 - N_GROUPS

    oh1 = lane == id1
    oh2 = lane == id2
    both = jnp.where(jnp.logical_or(oh1, oh2), 1.0, 0.0)
    r = lax.broadcasted_iota(jnp.int32, (tm, tm), 0)
    c = lax.broadcasted_iota(jnp.int32, (tm, tm), 1)
    tril = jnp.where(c < r, 1.0, 0.0).astype(BF16)
    before = jnp.dot(tril, both.astype(BF16), preferred_element_type=F32) + carry_ref[...]
    rank1 = jnp.sum(jnp.where(oh1, before, 0.0), axis=-1, keepdims=True)
    rank2 = jnp.sum(jnp.where(oh2, before, 0.0), axis=-1, keepdims=True)
    carry_ref[...] = carry_ref[...] + jnp.sum(both, axis=0, keepdims=True)

    info = jnp.zeros((tm, LANES), F32)
    for col, val in enumerate((id1, id2, rank1, rank2, w1, w2)):
        info = jnp.where(lane == col, val, info)
    info_ref[...] = info
    idx_all[:, pl.ds(pl.multiple_of(i * tm, tm), tm)] = jnp.transpose(info)[0:8]

    @pl.when(i == pl.num_programs(0) - 1)
    def _():
        plan = _dispatch_plan_tile(carry_ref[...])
        plan_ref[...] = jnp.transpose(plan)[0:8].astype(jnp.int32)
        first_row = plan[0:N_EXPERTS, PLAN_BSTART:PLAN_BSTART + 1] * float(EXPERT_BLOCK)
        chunk = 1024
        expert = lax.broadcasted_iota(jnp.int32, (N_EXPERTS, chunk), 0).astype(F32)
        for c0 in range(0, idx_all.shape[1], chunk):
            cols = slice(c0, c0 + chunk)
            rows = []
            for k in range(2):
                ids = idx_all[k:k + 1, cols]
                base = jnp.sum(jnp.where(expert == ids, first_row, 0.0), axis=0, keepdims=True)
                rows.append(base + idx_all[2 + k:3 + k, cols])
            slots_ref[:, cols] = jnp.concatenate(rows, axis=0).astype(jnp.int32)


def _router(x2, g, mod, seq, wr, br, tm=256):
    assert x2.shape[1] == 2 * ROW_TILE * LANES
    t, d = x2.shape
    per = seq // tm
    return pl.pallas_call(
        _router_kernel,
        grid=(t // tm,),
        in_specs=[
            pl.BlockSpec((tm, d), lambda i: (i, 0)),
            pl.BlockSpec((1, d), lambda i: (0, 0)),
            pl.BlockSpec((None, 1, d), lambda i: ((i // per) * 6 + 3, 0, 0)),
            pl.BlockSpec((None, 1, d), lambda i: ((i // per) * 6 + 4, 0, 0)),
            pl.BlockSpec((d, LANES), lambda i: (0, 0)),
            pl.BlockSpec((1, LANES), lambda i: (0, 0)),
        ],
        out_specs=(
            pl.BlockSpec((tm * ROW_TILE, LANES), lambda i: (i, 0)),
            pl.BlockSpec((tm, LANES), lambda i: (i, 0)),
            pl.BlockSpec((2, t), lambda i: (0, 0)),
            pl.BlockSpec((8, PLAN_COLS), lambda i: (0, 0)),
        ),
        out_shape=(
            jax.ShapeDtypeStruct((t * ROW_TILE, LANES), jnp.uint32),
            jax.ShapeDtypeStruct((t, LANES), F32),
            jax.ShapeDtypeStruct((2, t), jnp.int32),
            jax.ShapeDtypeStruct((8, PLAN_COLS), jnp.int32),
        ),
        scratch_shapes=[pltpu.VMEM((1, LANES), F32), pltpu.VMEM((8, t), F32)],
        compiler_params=_cparams(("arbitrary",)),
        name="moe_router",
    )(x2, g.reshape(1, d), mod, mod, wr, br)


GATHER_GROUP = 8


def _plan(plan, row, col):
    return plan[row * PLAN_COLS + col]


def _expert_kernel(plan, slots, h_hbm, wg_hbm, wu_hbm, wd_hbm, ys_ref,
                   row_tok, xbuf, xsem, sg, su, sd, wsem, wgb, wub, wdb, *, layer, n_tok):
    i = pl.program_id(0)
    nu = _plan(plan, PLAN_NUSED, 0)
    n_rows = row_tok.shape[0]

    def weight_copies(e, st):
        return (pltpu.make_async_copy(wg_hbm.at[layer, e], sg.at[st], wsem.at[st, 0]),
                pltpu.make_async_copy(wu_hbm.at[layer, e], su.at[st], wsem.at[st, 1]),
                pltpu.make_async_copy(wd_hbm.at[layer, e], sd.at[st], wsem.at[st, 2]))

    def n_groups(blk):
        return (_plan(plan, PLAN_NVALID, blk) + GATHER_GROUP - 1) // GATHER_GROUP

    def start_gather(blk, slot):
        base = blk * EXPERT_BLOCK

        def body(g, carry):
            for k in range(GATHER_GROUP):
                r = g * GATHER_GROUP + k
                tok = row_tok[base + r]
                src = pl.multiple_of(tok * ROW_TILE, ROW_TILE)
                dst = pl.multiple_of(r * ROW_TILE, ROW_TILE)
                pltpu.make_async_copy(h_hbm.at[pl.ds(src, ROW_TILE), :],
                                      xbuf.at[slot, pl.ds(dst, ROW_TILE), :], xsem.at[slot]).start()
            return carry

        lax.fori_loop(0, n_groups(blk), body, 0)

    def wait_gather(blk, slot):
        span = GATHER_GROUP * ROW_TILE

        def body(g, carry):
            pltpu.make_async_copy(h_hbm.at[pl.ds(0, span), :],
                                  xbuf.at[slot, pl.ds(0, span), :], xsem.at[slot]).wait()
            return carry

        lax.fori_loop(0, n_groups(blk), body, 0)

    @pl.when(i == 0)
    def _():
        for cp in weight_copies(_plan(plan, PLAN_EXP, 0), 0):
            cp.start(priority=1)
        second = _plan(plan, PLAN_NEXT, 0)

        @pl.when(second >= 0)
        def _():
            for cp in weight_copies(second, 1):
                cp.start(priority=1)

        xbuf[...] = jnp.zeros(xbuf.shape, xbuf.dtype)

        def pad_body(e, carry):
            end = _plan(plan, PLAN_BSTART, e) * EXPERT_BLOCK + _plan(plan, PLAN_COUNT, e)
            for k in range(GATHER_GROUP - 1):
                row_tok[jnp.minimum(end + k, n_rows - 1)] = 0
            return carry

        lax.fori_loop(0, N_EXPERTS, pad_body, 0)

        def fill_body(t, carry):
            row_tok[slots[0, t]] = t
            row_tok[slots[1, t]] = t
            return carry

        lax.fori_loop(0, n_tok, fill_body, 0, unroll=8)
        start_gather(0, 0)

    @pl.when(i < nu)
    def _():
        slot = i % 2

        @pl.when(i + 1 < nu)
        def _():
            start_gather(i + 1, 1 - slot)

        first = _plan(plan, PLAN_FIRST, i)

        @pl.when(first > 0)
        def _():
            st = (first - 1) % 2
            nxt2 = _plan(plan, PLAN_NEXT2, i)
            cps = weight_copies(_plan(plan, PLAN_EXP, i), st)
            nxt_cps = weight_copies(jnp.maximum(nxt2, 0), st)
            for cp, ncp, stage, dst in zip(cps, nxt_cps, (sg, su, sd), (wgb, wub, wdb)):
                cp.wait()
                dst[...] = stage[st].astype(BF16)

                @pl.when(nxt2 >= 0)
                def _():
                    ncp.start(priority=1)

        wait_gather(i, slot)
        parts = [_unpack_halves(w) for w in _load_row_tiles(xbuf.at[slot], EXPERT_BLOCK)]
        x_lo = jnp.concatenate([lo.astype(BF16) for lo, _ in parts], axis=1)
        x_hi = jnp.concatenate([hi.astype(BF16) for _, hi in parts], axis=1)
        half = wgb.shape[0] // 2
        g = (jnp.dot(x_lo, wgb[:half], preferred_element_type=F32)
             + jnp.dot(x_hi, wgb[half:], preferred_element_type=F32))
        u = (jnp.dot(x_lo, wub[:half], preferred_element_type=F32)
             + jnp.dot(x_hi, wub[half:], preferred_element_type=F32))
        hid = (_silu(g) * u).astype(BF16)
        _store_row_tiles(ys_ref, _pack_halves(jnp.dot(hid, wdb[...], preferred_element_type=F32)))

    @pl.when(i >= nu)
    def _():
        ys_ref[...] = jnp.zeros(ys_ref.shape, ys_ref.dtype)


def _expert_ffn(h2, w_gate, w_up, w_down, layer, plan, slots, n_blocks):
    t = h2.shape[0] // ROW_TILE
    d, f = w_gate.shape[2], w_gate.shape[3]
    n_rows = n_blocks * EXPERT_BLOCK
    grid_spec = pltpu.PrefetchScalarGridSpec(
        num_scalar_prefetch=2,
        grid=(n_blocks,),
        in_specs=[pl.BlockSpec(memory_space=pl.ANY)] * 4,
        out_specs=pl.BlockSpec((EXPERT_BLOCK * ROW_TILE, LANES), lambda i, *_: (i, 0)),
        scratch_shapes=[
            pltpu.SMEM((n_rows,), jnp.int32),
            pltpu.VMEM((2, EXPERT_BLOCK * ROW_TILE, LANES), jnp.uint32),
            pltpu.SemaphoreType.DMA((2,)),
            pltpu.VMEM((2, d, f), F32),
            pltpu.VMEM((2, d, f), F32),
            pltpu.VMEM((2, f, d), F32),
            pltpu.SemaphoreType.DMA((2, 3)),
            pltpu.VMEM((d, f), BF16),
            pltpu.VMEM((d, f), BF16),
            pltpu.VMEM((f, d), BF16),
        ],
    )
    weight_bytes = 3 * d * f * (2 * 4 + 2)
    vmem = weight_bytes + 8 * 1024 * 1024
    return pl.pallas_call(
        functools.partial(_expert_kernel, layer=layer, n_tok=t),
        grid_spec=grid_spec,
        out_shape=jax.ShapeDtypeStruct((n_rows * ROW_TILE, LANES), jnp.uint32),
        compiler_params=_cparams(("arbitrary",), vmem),
        name="moe_experts",
    )(plan, slots, h2, w_gate, w_up, w_down)


def _combine_kernel(slots, x_ref, info_ref, gate_ref, ys_hbm, *rest, tm, with_next):
    if with_next:
        ng_ref, nsh_ref, nsc_ref, o_ref, h_ref, buf, sem = rest
    else:
        o_ref, buf, sem = rest
    i = pl.program_id(0)
    last = pl.num_programs(0) - 1

    def row_copy(slot, k, src_row, r):
        src = pl.multiple_of(src_row * ROW_TILE, ROW_TILE)
        return pltpu.make_async_copy(ys_hbm.at[pl.ds(src, ROW_TILE), :],
                                     buf.at[slot, k, pl.ds(r * ROW_TILE, ROW_TILE), :], sem.at[slot])

    def wait_rows(slot):
        for k in range(2):
            pltpu.make_async_copy(ys_hbm.at[pl.ds(0, tm * ROW_TILE), :], buf.at[slot, k], sem.at[slot]).wait()

    @pl.when(i == 0)
    def _():
        def body(r, carry):
            for k in range(2):
                row_copy(0, k, slots[k, r], r).start(priority=k)
            return carry

        lax.fori_loop(0, tm, body, 0, unroll=8)

    def step(slot):
        wait_rows(slot)
        base = jnp.minimum(i + 1, last) * tm
        for r in range(tm):
            for k in range(2):
                row_copy(1 - slot, k, slots[k, base + r], r).start(priority=k)
        info = info_ref[...]
        w0 = info[:, 4:5]
        w1 = info[:, 5:6]
        half = x_ref.shape[1] // 2
        a_tiles = _load_row_tiles(buf.at[slot, 0], tm)
        b_tiles = _load_row_tiles(buf.at[slot, 1], tm)
        for s in range(ROW_TILE):
            a_lo, a_hi = _unpack_halves(a_tiles[s])
            b_lo, b_hi = _unpack_halves(b_tiles[s])
            lo = slice(s * LANES, (s + 1) * LANES)
            hi = slice(half + s * LANES, half + (s + 1) * LANES)
            o_ref[:, lo] = x_ref[:, lo] + gate_ref[:, lo] * (w0 * a_lo + w1 * b_lo)
            o_ref[:, hi] = x_ref[:, hi] + gate_ref[:, hi] * (w0 * a_hi + w1 * b_hi)
        if with_next:
            h_ref[...] = _norm_mod(o_ref[...], ng_ref[...], nsh_ref[...], nsc_ref[...]).astype(h_ref.dtype)

        @pl.when(i == last)
        def _():
            wait_rows(1 - slot)

    for parity in range(2):
        pl.when(i % 2 == parity)(functools.partial(step, parity))


def _combine(x2, info, mod, seq, ys, slots, next_norm=None, tm=128):
    t, d = x2.shape
    per = seq // tm
    row = lambda i, *_: (i, 0)
    mod_row = lambda k: (lambda i, *_: ((i // per) * 6 + k, 0, 0))
    in_specs = [
        pl.BlockSpec((tm, d), row),
        pl.BlockSpec((tm, LANES), row),
        pl.BlockSpec((None, 1, d), mod_row(5)),
        pl.BlockSpec(memory_space=pl.ANY),
    ]
    args = [slots, x2, info, mod, ys]
    out_specs = pl.BlockSpec((tm, d), row)
    out_shape = jax.ShapeDtypeStruct((t, d), F32)
    if next_norm is not None:
        next_g, next_mod = next_norm
        in_specs += [pl.BlockSpec((1, d), lambda i, *_: (0, 0)),
                     pl.BlockSpec((None, 1, d), mod_row(0)), pl.BlockSpec((None, 1, d), mod_row(1))]
        args += [next_g.reshape(1, d), next_mod, next_mod]
        out_specs = (out_specs, pl.BlockSpec((tm, d), row))
        out_shape = (out_shape, jax.ShapeDtypeStruct((t, d), BF16))
    grid_spec = pltpu.PrefetchScalarGridSpec(
        num_scalar_prefetch=1,
        grid=(t // tm,),
        in_specs=in_specs,
        out_specs=out_specs,
        scratch_shapes=[
            pltpu.VMEM((2, 2, tm * ROW_TILE, LANES), jnp.uint32),
            pltpu.SemaphoreType.DMA((2,)),
        ],
    )
    return pl.pallas_call(
        functools.partial(_combine_kernel, tm=tm, with_next=next_norm is not None),
        grid_spec=grid_spec,
        out_shape=out_shape,
        compiler_params=_cparams(("arbitrary",)),
        name="moe_combine",
    )(*args)


def _hier_moe(x2, norm_g, mod, seq, w_rg, b_rg, w_re, b_re, w_gate, w_up, w_down, layer, next_norm=None):
    t, d = x2.shape
    pad = LANES - N_GROUPS - N_EXPERTS
    wr = jnp.concatenate([w_rg, w_re, jnp.zeros((d, pad), F32)], axis=1)
    br = jnp.concatenate([b_rg, b_re, jnp.zeros((pad,), F32)]).reshape(1, LANES)
    h2, info, slots, plan = _router(x2, norm_g, mod, seq, wr, br)
    n_assign = 2 * t
    n_blocks = (n_assign + N_EXPERTS * (EXPERT_BLOCK - 1) + EXPERT_BLOCK - 1) // EXPERT_BLOCK
    assert n_blocks <= PLAN_COLS
    plan = plan.reshape(-1)
    ys = _expert_ffn(h2, w_gate, w_up, w_down, layer, plan, slots, n_blocks)
    return _combine(x2, info, mod, seq, ys, slots, next_norm)


def kernel(x, c, ada_w, ada_b, norm1_g, norm2_g, dsa_w_in, dsa_q_gain, dsa_k_gain, dsa_w_out, mla_w_in, mla_cq_gain, mla_ckv_gain, mla_w_q_up, mla_w_kv_up, mla_q_gain, mla_k_gain, mla_w_out, router_group_w, router_group_b, router_expert_w, router_expert_b, expert_w_gate, expert_w_up, expert_w_down):
    b, s, d = x.shape
    t = b * s
    mods = _ada_mod(c, ada_w, ada_b)
    x2 = x.reshape(t, d)

    mod = mods[0]
    h = _normmod(x2, norm1_g[0], mod, s, 0, 1)
    gains = (dsa_q_gain[0], dsa_k_gain[0], dsa_k_gain[0])
    qkv = [[_dsa_proj(h, dsa_w_in[0], gains[w], g, w, b) for w in range(3)] for g in range(len(DIL_PAIRS))]
    og2 = _dsa_attn(*qkv[2], 2)
    og1 = _dsa_attn(*qkv[1], 1)
    o = _dsa_attn(*qkv[0], 0, others=(og1, og2))
    x2 = _resid_mm(o.reshape(t, DSA_WIDTH), dsa_w_out[0], x2, mod, s, 2)
    x2, h = _hier_moe(x2, norm2_g[0], mod, s, router_group_w[0], router_group_b[0],
                      router_expert_w[0], router_expert_b[0],
                      expert_w_gate, expert_w_up, expert_w_down, 0, next_norm=(norm1_g[1], mods[1]))

    mod = mods[1]
    cq, ckv, kpe = _mla_in(h, mla_w_in, mla_cq_gain[0], mla_ckv_gain[0])
    tab = _rope_tables(s)
    q = _mla_q(cq, mla_w_q_up, mla_q_gain[0], tab, s)
    k, v = _mla_kv(ckv, mla_w_kv_up[0], kpe, mla_k_gain[0], tab, s)
    o = _mla_attn(q.reshape(b, s, -1), k.reshape(b, s, -1), v.reshape(b, s, -1))
    x2 = _resid_mm(o.reshape(t, MLA_HEADS * MLA_V), mla_w_out[0], x2, mod, s, 2)
    x2 = _hier_moe(x2, norm2_g[1], mod, s, router_group_w[1], router_group_b[1],
                   router_expert_w[1], router_expert_b[1],
                   expert_w_gate, expert_w_up, expert_w_down, 1)
    return x2.reshape(b, s, d)
```

```python
import functools
import math

import jax
import jax.numpy as jnp
import numpy as np
from jax import lax
from jax.experimental import pallas as pl
from jax.experimental.pallas import tpu as pltpu

F32 = jnp.float32
BF16 = jnp.bfloat16

D_MODEL = 2048
EPS = 1e-6
LANES = 128
NEG_BIG = -1e30

DIL_PAIRS = ((128, 1), (512, 4), (2048, 16))
DSA_HEADS = 8
DSA_HEAD_DIM = 128
DSA_WIDTH = DSA_HEADS * DSA_HEAD_DIM
BAND = 128

MLA_HEADS = 16
MLA_Q_LORA = 512
MLA_KV_LORA = 512
MLA_NOPE = 128
MLA_ROPE = 64
MLA_V = 128
MLA_QK = MLA_NOPE + MLA_ROPE
MLA_QK_PAD = 256
ROPE_THETA = 10000.0

N_GROUPS = 4
EXPERTS_PER_GROUP = 16
N_EXPERTS = N_GROUPS * EXPERTS_PER_GROUP
D_EXPERT = 768
EXPERT_BLOCK = 128

VMEM_LIMIT = 48 * 1024 * 1024


def _cparams(sem, vmem=VMEM_LIMIT):
    return pltpu.CompilerParams(dimension_semantics=sem, vmem_limit_bytes=vmem)


def _silu(x):
    return x * (1.0 / (1.0 + jnp.exp(-x)))


def _pack_halves(x):
    n = x.shape[1] // 2
    xb = x.astype(BF16).astype(F32)
    lo = lax.bitcast_convert_type(xb[:, :n], jnp.uint32) >> 16
    hi = lax.bitcast_convert_type(xb[:, n:], jnp.uint32) & jnp.uint32(0xFFFF0000)
    return hi | lo


def _unpack_halves(w):
    lo = lax.bitcast_convert_type(w << 16, F32)
    hi = lax.bitcast_convert_type(w & jnp.uint32(0xFFFF0000), F32)
    return lo, hi


ROW_TILE = 8


def _store_row_tiles(ref, words):
    rows = words.shape[0]
    for s in range(ROW_TILE):
        ref[pl.ds(s, rows, stride=ROW_TILE), :] = words[:, s * LANES:(s + 1) * LANES]


def _load_row_tiles(ref, rows):
    return [ref[pl.ds(s, rows, stride=ROW_TILE), :] for s in range(ROW_TILE)]


def _norm_mod(x, g, shift, scale):
    ms = jnp.mean(x * x, axis=-1, keepdims=True)
    y = x * lax.rsqrt(ms + EPS) * g
    return y * (1.0 + scale) + shift


def _ada_kernel(c_ref, w_ref, b_ref, o_ref):
    ca = _silu(c_ref[...])
    hi = ca.astype(BF16)
    lo = (ca - hi.astype(F32)).astype(BF16)
    lhs = jnp.concatenate([hi, lo], axis=0)
    res = jnp.dot(lhs, w_ref[...].astype(BF16), preferred_element_type=F32)
    o_ref[...] = res[:8] + res[8:] + b_ref[...]


def _ada_mod(c, ada_w, ada_b):
    depth, d, n = ada_w.shape
    b = c.shape[0]
    c8 = jnp.pad(c, ((0, 8 - b), (0, 0)))
    tn = 1024
    out = pl.pallas_call(
        _ada_kernel,
        grid=(depth, n // tn),
        in_specs=[
            pl.BlockSpec((8, d), lambda i, j: (0, 0)),
            pl.BlockSpec((None, d, tn), lambda i, j: (i, 0, j)),
            pl.BlockSpec((None, 1, tn), lambda i, j: (i, 0, j)),
        ],
        out_specs=pl.BlockSpec((None, 8, tn), lambda i, j: (i, 0, j)),
        out_shape=jax.ShapeDtypeStruct((depth, 8, n), F32),
        compiler_params=_cparams(("arbitrary", "arbitrary")),
        name="ada_mod",
    )(c8, ada_w, ada_b.reshape(depth, 1, n))
    return out[:, :b].reshape(depth, b * 6, 1, d)


def _normmod_kernel(x_ref, g_ref, sh_ref, sc_ref, o_ref):
    o_ref[...] = _norm_mod(x_ref[...], g_ref[...], sh_ref[...], sc_ref[...]).astype(o_ref.dtype)


def _normmod(x2, g, mod, seq, k_shift, k_scale, tm=512):
    t, d = x2.shape
    per = seq // tm
    return pl.pallas_call(
        _normmod_kernel,
        grid=(t // tm,),
        in_specs=[
            pl.BlockSpec((tm, d), lambda i: (i, 0)),
            pl.BlockSpec((1, d), lambda i: (0, 0)),
            pl.BlockSpec((None, 1, d), lambda i: ((i // per) * 6 + k_shift, 0, 0)),
            pl.BlockSpec((None, 1, d), lambda i: ((i // per) * 6 + k_scale, 0, 0)),
        ],
        out_specs=pl.BlockSpec((tm, d), lambda i: (i, 0)),
        out_shape=jax.ShapeDtypeStruct((t, d), BF16),
        compiler_params=_cparams(("arbitrary",)),
        name="normmod",
    )(x2, g.reshape(1, d), mod, mod)


def _cast_weight_once(w_ref, wb_ref):
    @pl.when(pl.program_id(1) == 0)
    def _():
        wb_ref[...] = w_ref[...].astype(BF16)


def _dsa_proj_kernel(a_ref, w_ref, g_ref, o_ref, wb_ref, res_ref, *, dilation, normed, gain_scale):
    @pl.when(pl.program_id(0) == 0)
    def _():
        wb_ref[...] = w_ref[...].astype(BF16)

    gain = g_ref[...] * gain_scale
    tm = a_ref.shape[0]
    pr = 512
    n_parts = tm // pr
    accs = [jnp.dot(a_ref[p * pr:(p + 1) * pr, :], wb_ref[...], preferred_element_type=F32)
            for p in range(n_parts)]
    for p, acc in enumerate(accs):
        blks = []
        for h in range(DSA_HEADS):
            blk = acc[:, h * LANES:(h + 1) * LANES]
            if normed:
                ms = jnp.mean(blk * blk, axis=-1, keepdims=True)
                blk = blk * lax.rsqrt(ms + EPS) * gain
            blks.append(blk)
        for h, blk in enumerate(blks):
            hs = slice(h * LANES, (h + 1) * LANES)
            if dilation == 1:
                o_ref[0, p * pr:(p + 1) * pr, hs] = blk.astype(o_ref.dtype)
            else:
                res_ref[h, p * pr:(p + 1) * pr, :] = blk
                sub_rows = pr // dilation
                for r in range(dilation):
                    sub = res_ref[h, pl.ds(p * pr + r, sub_rows, stride=dilation), :]
                    o_ref[r, p * sub_rows:(p + 1) * sub_rows, hs] = sub.astype(o_ref.dtype)


def _dsa_proj(h, w_in, gain, group, which, batch, tm=1024, tn=DSA_WIDTH):
    t, d = h.shape
    dil = DIL_PAIRS[group][1]
    seq = t // batch
    per = seq // tm
    gain_scale = 1.0 / math.sqrt(DSA_HEAD_DIM) if which == 0 else 1.0
    return pl.pallas_call(
        functools.partial(_dsa_proj_kernel, dilation=dil, normed=which != 2, gain_scale=gain_scale),
        grid=(t // tm,),
        in_specs=[
            pl.BlockSpec((tm, d), lambda i: (i, 0)),
            pl.BlockSpec((d, tn), lambda i: (0, group * 3 + which), pipeline_mode=pl.Buffered(1)),
            pl.BlockSpec((1, LANES), lambda i: (0, 0)),
        ],
        out_specs=pl.BlockSpec((None, dil, tm // dil, tn), lambda i: (i // per, 0, i % per, 0)),
        out_shape=jax.ShapeDtypeStruct((batch, dil, seq // dil, tn), BF16),
        scratch_shapes=[pltpu.VMEM((d, tn), BF16), pltpu.VMEM((DSA_HEADS, tm, LANES), F32)],
        compiler_params=_cparams(("arbitrary",)),
        name=f"dsa_proj_g{group}_{'qkv'[which]}",
    )(h, w_in, gain.reshape(1, LANES))


def _resid_mm_kernel(a_ref, w_ref, x_ref, gate_ref, o_ref, wb_ref):
    _cast_weight_once(w_ref, wb_ref)
    y = jnp.dot(a_ref[...], wb_ref[...], preferred_element_type=F32)
    o_ref[...] = x_ref[...] + gate_ref[...] * y


def _resid_mm(a, w, x2, mod, seq, k_gate, tm=512, tn=1024):
    t, k = a.shape
    n = w.shape[1]
    per = seq // tm
    return pl.pallas_call(
        _resid_mm_kernel,
        grid=(n // tn, t // tm),
        in_specs=[
            pl.BlockSpec((tm, k), lambda j, i: (i, 0)),
            pl.BlockSpec((k, tn), lambda j, i: (0, j)),
            pl.BlockSpec((tm, tn), lambda j, i: (i, j)),
            pl.BlockSpec((None, 1, tn), lambda j, i: ((i // per) * 6 + k_gate, 0, j)),
        ],
        out_specs=pl.BlockSpec((tm, tn), lambda j, i: (i, j)),
        out_shape=jax.ShapeDtypeStruct((t, n), F32),
        scratch_shapes=[pltpu.VMEM((k, tn), BF16)],
        compiler_params=_cparams(("arbitrary", "arbitrary")),
        name="resid_mm",
    )(a, w, x2, mod)


def _alibi_slope(head_slot, group):
    n = len(DIL_PAIRS) * DSA_HEADS
    return 2.0 ** (-8.0 * (head_slot * len(DIL_PAIRS) + group + 1.0) / n)


def _dsa_bias_table(group, has_prev):
    window, d = DIL_PAIRS[group]
    steps = window // d
    qi = np.arange(BAND)[:, None]
    if has_prev:
        kj = np.arange(2 * BAND)[None, :]
        delta = qi + BAND - kj
        prev_key = np.broadcast_to(kj < BAND, delta.shape)
    else:
        kj = np.arange(BAND)[None, :]
        delta = qi - kj
        prev_key = np.zeros(delta.shape, bool)
    inside = (delta >= 0) & (delta <= steps)
    tabs = []
    for first in (True, False):
        valid = inside & ~(prev_key & first)
        per_head = [np.where(valid, -_alibi_slope(h, group) * (delta * d), NEG_BIG) for h in range(DSA_HEADS)]
        tabs.append(np.stack(per_head))
    return jnp.asarray(np.stack(tabs), F32)


def _dsa_attn_kernel(*refs, dilation, has_prev, merge):
    refs = list(refs)
    q_ref, kc_ref, vc_ref, bias_ref = refs[:4]
    pos = 4
    if has_prev:
        kp_ref, vp_ref = refs[pos:pos + 2]
        pos += 2
    if merge:
        other = refs[pos:pos + 4]
        pos += 4
        o_ref = refs[pos]
    else:
        o_ref, lse_ref = refs[pos:pos + 2]

    r = pl.program_id(2)
    lane = lax.broadcasted_iota(jnp.int32, (BAND, LANES), 1)
    nt = (((1,), (1,)), ((), ()))
    nk = 2 * BAND if has_prev else BAND
    ones = jnp.ones((nk, LANES), BF16)
    n_batch = q_ref.shape[0]

    def head_scores(bi, h):
        hs = slice(h * LANES, (h + 1) * LANES)
        q = q_ref[bi, :, hs]
        if has_prev:
            k = jnp.concatenate([kp_ref[bi, :, hs], kc_ref[bi, :, hs]], axis=0)
        else:
            k = kc_ref[bi, :, hs]
        return lax.dot_general(q, k, nt, preferred_element_type=F32) + bias_ref[h]

    def head_values(bi, h):
        hs = slice(h * LANES, (h + 1) * LANES)
        if has_prev:
            v = jnp.concatenate([vp_ref[bi, :, hs], vc_ref[bi, :, hs]], axis=0)
        else:
            v = vc_ref[bi, :, hs]
        return jnp.concatenate([v, ones], axis=1)

    results = []
    for bi in range(n_batch):
        scores = [head_scores(bi, h) for h in range(DSA_HEADS)]
        maxes = [jnp.max(s, axis=-1, keepdims=True) for s in scores]
        probs = [jnp.exp(s - m).astype(BF16) for s, m in zip(scores, maxes)]
        accs = [jnp.dot(p, head_values(bi, h), preferred_element_type=F32) for h, p in enumerate(probs)]
        outs = []
        lse_tile = jnp.zeros((BAND, LANES), F32)
        for h in range(DSA_HEADS):
            l = accs[h][:, LANES:]
            o = accs[h][:, :LANES] / l
            lse = maxes[h] + jnp.log(l)
            if merge:
                lses = [lse] + [other[2 * g + 1][bi, :, h * 16:h * 16 + 1] for g in range(2)]
                parts = [o] + [other[2 * g][bi, h] for g in range(2)]
                top = jnp.maximum(jnp.maximum(lses[0], lses[1]), lses[2])
                es = [jnp.exp(x - top) for x in lses]
                den = es[0] + es[1] + es[2]
                o = ((es[0] * parts[0] + es[1] * parts[1] + es[2] * parts[2]) / den).astype(o_ref.dtype)
            else:
                in_head = jnp.logical_and(lane >= h * 16, lane < (h + 1) * 16)
                lse_tile = jnp.where(in_head, lse, lse_tile)
            outs.append(o)
        results.append((outs, lse_tile))

    for bi, (outs, lse_tile) in enumerate(results):
        for h, o in enumerate(outs):
            if merge:
                o_ref[bi, :, h * LANES:(h + 1) * LANES] = o
            else:
                o_ref[bi, h, pl.ds(r, BAND, stride=dilation), :] = o
        if not merge:
            lse_ref[bi, pl.ds(r, BAND, stride=dilation), :] = lse_tile


def _dsa_attn(q_g, k_g, v_g, group, others=None):
    b, d, sub_len, _ = q_g.shape
    s = d * sub_len
    nb = sub_len // BAND
    has_prev = nb > 1
    merge = others is not None
    assert not merge or d == 1
    bias = _dsa_bias_table(group, has_prev)

    bb = 2 if b % 2 == 0 else 1
    cur = lambda bi, n, r: (bi, r, n, 0)
    prev = lambda bi, n, r: (bi, r, jnp.maximum(n - 1, 0), 0)
    blk = (bb, None, BAND, DSA_WIDTH)
    in_specs = [pl.BlockSpec(blk, cur), pl.BlockSpec(blk, cur), pl.BlockSpec(blk, cur),
                pl.BlockSpec((None,) + bias.shape[1:], lambda bi, n, r: (jnp.minimum(n, 1), 0, 0, 0))]
    args = [q_g, k_g, v_g, bias]
    if has_prev:
        in_specs += [pl.BlockSpec(blk, prev), pl.BlockSpec(blk, prev)]
        args += [k_g, v_g]
    span = BAND * d
    nat = lambda bi, n, r: (bi, n, 0)
    nat_heads = lambda bi, n, r: (bi, 0, n, 0)
    if merge:
        for o_g, lse_g in others:
            in_specs += [pl.BlockSpec((bb, DSA_HEADS, BAND, LANES), nat_heads),
                         pl.BlockSpec((bb, BAND, LANES), nat)]
            args += [o_g, lse_g]
        out_shape = jax.ShapeDtypeStruct((b, s, DSA_WIDTH), BF16)
        out_specs = pl.BlockSpec((bb, BAND, DSA_WIDTH), nat)
    else:
        out_shape = (jax.ShapeDtypeStruct((b, DSA_HEADS, s, LANES), F32), jax.ShapeDtypeStruct((b, s, LANES), F32))
        out_specs = (pl.BlockSpec((bb, DSA_HEADS, span, LANES), nat_heads),
                     pl.BlockSpec((bb, span, LANES), nat))
    return pl.pallas_call(
        functools.partial(_dsa_attn_kernel, dilation=d, has_prev=has_prev, merge=merge),
        grid=(b // bb, nb, d),
        in_specs=in_specs,
        out_specs=out_specs,
        out_shape=out_shape,
        compiler_params=_cparams(("arbitrary", "arbitrary", "arbitrary")),
        name=f"dsa_attn_g{group}",
    )(*args)


def _rope_tables(seq):
    half = MLA_ROPE // 2
    inv = np.float32(ROPE_THETA) ** (-np.arange(half, dtype=np.float32) / np.float32(half))
    ang = np.arange(seq, dtype=np.float32)[:, None] * inv[None, :]
    cos, sin = np.cos(ang), np.sin(ang)
    z = np.zeros((seq, LANES - MLA_ROPE), np.float32)
    zh = np.zeros((seq, half), np.float32)
    cos_t = np.concatenate([cos, cos, z], axis=1)
    sin_a = np.concatenate([zh, sin, z], axis=1)
    sin_b = np.concatenate([-sin, zh, z], axis=1)
    return jnp.asarray(np.concatenate([cos_t, sin_a, sin_b], axis=1), F32)


def _rope_lanes(x, tab):
    half = MLA_ROPE // 2
    cos_t = tab[:, 0:LANES]
    sin_a = tab[:, LANES:2 * LANES]
    sin_b = tab[:, 2 * LANES:3 * LANES]
    return (x * cos_t + pltpu.roll(x, half, 1) * sin_a
            + pltpu.roll(x, LANES - half, 1) * sin_b)


def _mla_in_kernel(a_ref, w_ref, cqg_ref, ckvg_ref, cq_ref, ckv_ref, kpe_ref, wb_ref):
    n = w_ref.shape[1]

    @pl.when(pl.program_id(0) == 0)
    def _():
        wb_ref[...] = jnp.zeros(wb_ref.shape, BF16)
        wb_ref[:, :n] = w_ref[...].astype(BF16)

    acc = jnp.dot(a_ref[...], wb_ref[...], preferred_element_type=F32)
    cq = acc[:, :MLA_Q_LORA]
    cq_ref[...] = (cq * lax.rsqrt(jnp.mean(cq * cq, axis=-1, keepdims=True) + EPS)
                   * cqg_ref[...]).astype(BF16)
    ckv = acc[:, MLA_Q_LORA:MLA_Q_LORA + MLA_KV_LORA]
    ckv_ref[...] = (ckv * lax.rsqrt(jnp.mean(ckv * ckv, axis=-1, keepdims=True) + EPS)
                    * ckvg_ref[...]).astype(BF16)
    kpe_ref[...] = acc[:, MLA_Q_LORA + MLA_KV_LORA:]


def _mla_in(h, w_in, cq_gain, ckv_gain, tm=512):
    t, d = h.shape
    n = w_in.shape[2]
    n_pad = MLA_Q_LORA + MLA_KV_LORA + LANES
    return pl.pallas_call(
        _mla_in_kernel,
        grid=(t // tm,),
        in_specs=[
            pl.BlockSpec((tm, d), lambda i: (i, 0)),
            pl.BlockSpec((None, d, n), lambda i: (0, 0, 0)),
            pl.BlockSpec((1, MLA_Q_LORA), lambda i: (0, 0)),
            pl.BlockSpec((1, MLA_KV_LORA), lambda i: (0, 0)),
        ],
        out_specs=(
            pl.BlockSpec((tm, MLA_Q_LORA), lambda i: (i, 0)),
            pl.BlockSpec((tm, MLA_KV_LORA), lambda i: (i, 0)),
            pl.BlockSpec((tm, LANES), lambda i: (i, 0)),
        ),
        out_shape=(
            jax.ShapeDtypeStruct((t, MLA_Q_LORA), BF16),
            jax.ShapeDtypeStruct((t, MLA_KV_LORA), BF16),
            jax.ShapeDtypeStruct((t, LANES), F32),
        ),
        scratch_shapes=[pltpu.VMEM((d, n_pad), BF16)],
        compiler_params=_cparams(("arbitrary",)),
        name="mla_in",
    )(h, w_in, cq_gain.reshape(1, -1), ckv_gain.reshape(1, -1))


def _mla_q_kernel(a_ref, w_ref, g0_ref, g1_ref, tab_ref, o_ref, wb_ref):
    @pl.when(pl.program_id(0) == 0)
    def _():
        for h in range(MLA_HEADS):
            src = h * MLA_QK
            dst = h * MLA_QK_PAD
            rope_cols = w_ref[:, src + MLA_NOPE:src + MLA_QK].astype(BF16)
            wb_ref[:, dst:dst + MLA_NOPE] = w_ref[:, src:src + MLA_NOPE].astype(BF16)
            wb_ref[:, dst + MLA_NOPE:dst + MLA_QK] = rope_cols
            wb_ref[:, dst + MLA_QK:dst + MLA_QK_PAD] = rope_cols

    acc = jnp.dot(a_ref[...], wb_ref[...], preferred_element_type=F32)
    tab = tab_ref[...]
    cos_t = tab[:, 0:LANES]
    sin_t = tab[:, LANES:2 * LANES] + tab[:, 2 * LANES:3 * LANES]
    scale = 1.0 / math.sqrt(MLA_QK)
    g0 = g0_ref[...] * scale
    g1 = g1_ref[...] * scale
    for h in range(MLA_HEADS):
        base = h * MLA_QK_PAD
        x0 = acc[:, base:base + LANES]
        x1 = acc[:, base + LANES:base + 2 * LANES]
        ss = jnp.sum(x0 * x0 + 0.5 * (x1 * x1), axis=-1, keepdims=True)
        rs = lax.rsqrt(ss / MLA_QK + EPS)
        o_ref[:, base:base + LANES] = (x0 * rs * g0).astype(BF16)
        x1n = x1 * rs * g1
        q_rot = x1n * cos_t + pltpu.roll(x1n, MLA_ROPE // 2, 1) * sin_t
        o_ref[:, base + LANES:base + 2 * LANES] = q_rot.astype(BF16)


def _mla_q(cq, w_q_up, q_gain, tab, seq, tm=512):
    t, k = cq.shape
    n = MLA_HEADS * MLA_QK_PAD
    per = seq // tm
    g0 = q_gain[:MLA_NOPE].reshape(1, LANES)
    g1 = jnp.concatenate([q_gain[MLA_NOPE:], q_gain[MLA_NOPE:]]).reshape(1, LANES)
    return pl.pallas_call(
        _mla_q_kernel,
        grid=(t // tm,),
        in_specs=[
            pl.BlockSpec((tm, k), lambda i: (i, 0)),
            pl.BlockSpec((None, k, MLA_HEADS * MLA_QK), lambda i: (0, 0, 0)),
            pl.BlockSpec((1, LANES), lambda i: (0, 0)),
            pl.BlockSpec((1, LANES), lambda i: (0, 0)),
            pl.BlockSpec((tm, 3 * LANES), lambda i: (i % per, 0)),
        ],
        out_specs=pl.BlockSpec((tm, n), lambda i: (i, 0)),
        out_shape=jax.ShapeDtypeStruct((t, n), BF16),
        scratch_shapes=[pltpu.VMEM((k, n), BF16)],
        compiler_params=_cparams(("arbitrary",)),
        name="mla_q_up",
    )(cq, w_q_up, g0, g1, tab)


def _mla_kv_kernel(a_ref, w_ref, kpe_ref, g0_ref, g1_ref, tab_ref, k_ref, v_ref, wb_ref):
    @pl.when(pl.program_id(0) == 0)
    def _():
        wb_ref[...] = w_ref[...].astype(BF16)

    acc = jnp.dot(a_ref[...], wb_ref[...], preferred_element_type=F32)
    tab = tab_ref[...]
    kpe = kpe_ref[...]
    ss_pe = jnp.sum(kpe * kpe, axis=-1, keepdims=True)
    g0 = g0_ref[...]
    kpe_rot = _rope_lanes(kpe * g1_ref[...], tab)
    for h in range(MLA_HEADS):
        base = h * (MLA_NOPE + MLA_V)
        kn = acc[:, base:base + MLA_NOPE]
        ss = jnp.sum(kn * kn, axis=-1, keepdims=True) + ss_pe
        rs = lax.rsqrt(ss / MLA_QK + EPS)
        kb = h * MLA_QK_PAD
        k_ref[:, kb:kb + LANES] = (kn * rs * g0).astype(BF16)
        k_ref[:, kb + LANES:kb + 2 * LANES] = (kpe_rot * rs).astype(BF16)
        vb = h * 2 * MLA_V
        v_ref[:, vb:vb + MLA_V] = acc[:, base + MLA_NOPE:base + MLA_NOPE + MLA_V].astype(BF16)
        v_ref[:, vb + MLA_V:vb + 2 * MLA_V] = jnp.ones((acc.shape[0], MLA_V), BF16)


def _mla_kv(ckv, w_kv_up, kpe, k_gain, tab, seq, tm=512):
    t, k = ckv.shape
    n = w_kv_up.shape[1]
    per = seq // tm
    g0 = k_gain[:MLA_NOPE].reshape(1, LANES)
    g1 = jnp.pad(k_gain[MLA_NOPE:], (0, LANES - MLA_ROPE)).reshape(1, LANES)
    return pl.pallas_call(
        _mla_kv_kernel,
        grid=(t // tm,),
        in_specs=[
            pl.BlockSpec((tm, k), lambda i: (i, 0)),
            pl.BlockSpec((k, n), lambda i: (0, 0)),
            pl.BlockSpec((tm, LANES), lambda i: (i, 0)),
            pl.BlockSpec((1, LANES), lambda i: (0, 0)),
            pl.BlockSpec((1, LANES), lambda i: (0, 0)),
            pl.BlockSpec((tm, 3 * LANES), lambda i: (i % per, 0)),
        ],
        out_specs=(
            pl.BlockSpec((tm, MLA_HEADS * MLA_QK_PAD), lambda i: (i, 0)),
            pl.BlockSpec((tm, MLA_HEADS * 2 * MLA_V), lambda i: (i, 0)),
        ),
        out_shape=(
            jax.ShapeDtypeStruct((t, MLA_HEADS * MLA_QK_PAD), BF16),
            jax.ShapeDtypeStruct((t, MLA_HEADS * 2 * MLA_V), BF16),
        ),
        scratch_shapes=[pltpu.VMEM((k, n), BF16)],
        compiler_params=_cparams(("arbitrary",)),
        name="mla_kv_up",
    )(ckv, w_kv_up, kpe, g0, g1, tab)


def _mla_attn_kernel(q_ref, k_ref, v_ref, o_ref, *, tq, heads):
    seq = q_ref.shape[0]
    nt = (((1,), (1,)), ((), ()))
    r = lax.broadcasted_iota(jnp.int32, (tq, tq), 0)
    c = lax.broadcasted_iota(jnp.int32, (tq, tq), 1)
    causal = c <= r
    for qi in range(seq // tq):
        rows = slice(qi * tq, (qi + 1) * tq)
        outs = []
        for hh in range(heads):
            qk_cols = slice(hh * MLA_QK_PAD, (hh + 1) * MLA_QK_PAD)
            v_cols = slice(hh * 2 * MLA_V, (hh + 1) * 2 * MLA_V)
            q = q_ref[rows, qk_cols]
            scores = []
            for j in range(qi + 1):
                s = lax.dot_general(q, k_ref[j * tq:(j + 1) * tq, qk_cols], nt, preferred_element_type=F32)
                if j == qi:
                    s = jnp.where(causal, s, NEG_BIG)
                scores.append(s)
            top = scores[0]
            for s in scores[1:]:
                top = jnp.maximum(top, s)
            m = jnp.max(top, axis=-1, keepdims=True)
            acc = None
            for j, s in enumerate(scores):
                p = jnp.exp(s - m).astype(BF16)
                pv = jnp.dot(p, v_ref[j * tq:(j + 1) * tq, v_cols], preferred_element_type=F32)
                acc = pv if acc is None else acc + pv
            outs.append((acc[:, :MLA_V] / acc[:, MLA_V:]).astype(o_ref.dtype))
        for hh in range(heads):
            o_ref[rows, hh * MLA_V:(hh + 1) * MLA_V] = outs[hh]


def _mla_attn(q, k, v, tq=256, heads=4):
    b, s, _ = q.shape
    return pl.pallas_call(
        functools.partial(_mla_attn_kernel, tq=tq, heads=heads),
        grid=(b, MLA_HEADS // heads),
        in_specs=[
            pl.BlockSpec((None, s, heads * MLA_QK_PAD), lambda bi, h: (bi, 0, h)),
            pl.BlockSpec((None, s, heads * MLA_QK_PAD), lambda bi, h: (bi, 0, h)),
            pl.BlockSpec((None, s, heads * 2 * MLA_V), lambda bi, h: (bi, 0, h)),
        ],
        out_specs=pl.BlockSpec((None, s, heads * MLA_V), lambda bi, h: (bi, 0, h)),
        out_shape=jax.ShapeDtypeStruct((b, s, MLA_HEADS * MLA_V), BF16),
        compiler_params=_cparams(("arbitrary", "arbitrary")),
        name="mla_attn",
    )(q, k, v)


PLAN_COLS = 256
(PLAN_EXP, PLAN_FIRST, PLAN_NEXT, PLAN_NUSED, PLAN_NVALID, PLAN_BSTART, PLAN_COUNT,
 PLAN_NEXT2) = range(8)


def _dispatch_plan_tile(cnt):
    nblk = jnp.floor((cnt + (EXPERT_BLOCK - 1.0)) * (1.0 / EXPERT_BLOCK))
    e_r = lax.broadcasted_iota(jnp.int32, (LANES, LANES), 0)
    e_c = lax.broadcasted_iota(jnp.int32, (LANES, LANES), 1)
    before = jnp.where(e_r < e_c, 1.0, 0.0).astype(BF16)
    used = jnp.where(nblk > 0.0, 1.0, 0.0)
    sub = lax.broadcasted_iota(jnp.int32, (8, LANES), 0)
    prefix = jnp.dot(jnp.where(sub == 0, nblk, used).astype(BF16), before, preferred_element_type=F32)
    bstart = prefix[0:1]
    ordinal = prefix[1:2]
    bend = bstart + nblk
    n_used = jnp.max(bend, axis=-1, keepdims=True)
    row = lax.broadcasted_iota(jnp.int32, (PLAN_COLS, LANES), 0).astype(F32)
    lane = lax.broadcasted_iota(jnp.int32, (PLAN_COLS, LANES), 1).astype(F32)
    is_exp = lane < N_EXPERTS
    row1 = row[:, 0:1]
    done = jnp.logical_and(bend <= row, is_exp)
    blk_exp = jnp.minimum(jnp.sum(jnp.where(done, 1.0, 0.0), axis=-1, keepdims=True), N_EXPERTS - 1.0)
    mine = lane == blk_exp
    bstart_of = jnp.sum(jnp.where(mine, bstart, 0.0), axis=-1, keepdims=True)
    cnt_of = jnp.sum(jnp.where(mine, cnt, 0.0), axis=-1, keepdims=True)
    ord_of = jnp.sum(jnp.where(mine, ordinal, 0.0), axis=-1, keepdims=True)
    valid = row1 < n_used
    first = jnp.where(jnp.logical_and(valid, row1 == bstart_of), ord_of + 1.0, 0.0)
    nvalid = jnp.clip(cnt_of - EXPERT_BLOCK * (row1 - bstart_of), 0.0, float(EXPERT_BLOCK))
    nvalid = jnp.where(valid, nvalid, 0.0)
    usable = jnp.logical_and(nblk > 0.0, is_exp)

    def next_used(after):
        nxt = jnp.min(jnp.where(jnp.logical_and(lane > after, usable), lane, 999.0), axis=-1, keepdims=True)
        return jnp.where(nxt > 998.0, -1.0, nxt)

    nxt = next_used(blk_exp)
    nxt2 = jnp.where(nxt < 0.0, -1.0, next_used(nxt))
    bstart_col = jnp.sum(jnp.where(lane < row, nblk, 0.0), axis=-1, keepdims=True)
    cnt_col = jnp.sum(jnp.where(lane == row, cnt, 0.0), axis=-1, keepdims=True)
    tile = jnp.zeros((PLAN_COLS, LANES), F32)
    cols = {PLAN_EXP: blk_exp, PLAN_FIRST: first, PLAN_NEXT: nxt, PLAN_NUSED: n_used,
            PLAN_NVALID: nvalid, PLAN_BSTART: bstart_col, PLAN_COUNT: cnt_col, PLAN_NEXT2: nxt2}
    for k, val in cols.items():
        tile = jnp.where(lane == k, val, tile)
    return tile


def _router_kernel(x_ref, g_ref, sh_ref, sc_ref, wr_ref, br_ref, h_ref, info_ref, slots_ref, plan_ref,
                   carry_ref, idx_all):
    i = pl.program_id(0)
    tm = x_ref.shape[0]

    @pl.when(i == 0)
    def _():
        carry_ref[...] = jnp.zeros(carry_ref.shape, F32)

    h = _norm_mod(x_ref[...], g_ref[...], sh_ref[...], sc_ref[...])
    _store_row_tiles(h_ref, _pack_halves(h))
    w = wr_ref[...]
    w_hi = w.astype(BF16)
    w_lo = (w - w_hi.astype(F32)).astype(BF16)
    h_hi = h.astype(BF16)
    h_lo = (h - h_hi.astype(F32)).astype(BF16)
    lg = (jnp.dot(h_hi, w_hi, preferred_element_type=F32)
          + jnp.dot(h_lo, w_hi, preferred_element_type=F32)
          + jnp.dot(h_hi, w_lo, preferred_element_type=F32)) + br_ref[...]

    lane = lax.broadcasted_iota(jnp.int32, (tm, LANES), 1).astype(F32)
    no_lane = float(LANES)
    gl = jnp.where(lane < N_GROUPS, lg, NEG_BIG)
    gmax = jnp.max(gl, axis=-1, keepdims=True)
    g_idx = jnp.min(jnp.where(gl == gmax, lane, no_lane), axis=-1, keepdims=True)
    g_p = 1.0 / jnp.sum(jnp.exp(gl - gmax), axis=-1, keepdims=True)
    lo_lane = N_GROUPS + g_idx * EXPERTS_PER_GROUP
    in_grp = jnp.logical_and(lane >= lo_lane, lane < lo_lane + EXPERTS_PER_GROUP)
    ev = jnp.where(in_grp, lg, NEG_BIG)
    v1 = jnp.max(ev, axis=-1, keepdims=True)
    i1 = jnp.min(jnp.where(ev == v1, lane, no_lane), axis=-1, keepdims=True)
    ev2 = jnp.where(lane == i1, NEG_BIG, ev)
    v2 = jnp.max(ev2, axis=-1, keepdims=True)
    i2 = jnp.min(jnp.where(ev2 == v2, lane, no_lane), axis=-1, keepdims=True)
    e2 = jnp.exp(v2 - v1)
    den = 1.0 + e2
    w1 = (1.0 / den) * g_p
    w2 = (e2 / den) * g_p
    id1 = i1 - N_GROUPS
    id2 = i2 - N_GROUPS

    oh1 = lane == id1
    oh2 = lane == id2
    both = jnp.where(jnp.logical_or(oh1, oh2), 1.0, 0.0)
    r = lax.broadcasted_iota(jnp.int32, (tm, tm), 0)
    c = lax.broadcasted_iota(jnp.int32, (tm, tm), 1)
    tril = jnp.where(c < r, 1.0, 0.0).astype(BF16)
    before = jnp.dot(tril, both.astype(BF16), preferred_element_type=F32) + carry_ref[...]
    rank1 = jnp.sum(jnp.where(oh1, before, 0.0), axis=-1, keepdims=True)
    rank2 = jnp.sum(jnp.where(oh2, before, 0.0), axis=-1, keepdims=True)
    carry_ref[...] = carry_ref[...] + jnp.sum(both, axis=0, keepdims=True)

    info = jnp.zeros((tm, LANES), F32)
    for col, val in enumerate((id1, id2, rank1, rank2, w1, w2)):
        info = jnp.where(lane == col, val, info)
    info_ref[...] = info
    idx_all[:, pl.ds(pl.multiple_of(i * tm, tm), tm)] = jnp.transpose(info)[0:8]

    @pl.when(i == pl.num_programs(0) - 1)
    def _():
        plan = _dispatch_plan_tile(carry_ref[...])
        plan_ref[...] = jnp.transpose(plan)[0:8].astype(jnp.int32)
        first_row = plan[0:N_EXPERTS, PLAN_BSTART:PLAN_BSTART + 1] * float(EXPERT_BLOCK)
        chunk = 1024
        expert = lax.broadcasted_iota(jnp.int32, (N_EXPERTS, chunk), 0).astype(F32)
        for c0 in range(0, idx_all.shape[1], chunk):
            cols = slice(c0, c0 + chunk)
            rows = []
            for k in range(2):
                ids = idx_all[k:k + 1, cols]
                base = jnp.sum(jnp.where(expert == ids, first_row, 0.0), axis=0, keepdims=True)
                rows.append(base + idx_all[2 + k:3 + k, cols])
            slots_ref[:, cols] = jnp.concatenate(rows, axis=0).astype(jnp.int32)


def _router(x2, g, mod, seq, wr, br, tm=256):
    assert x2.shape[1] == 2 * ROW_TILE * LANES
    t, d = x2.shape
    per = seq // tm
    return pl.pallas_call(
        _router_kernel,
        grid=(t // tm,),
        in_specs=[
            pl.BlockSpec((tm, d), lambda i: (i, 0)),
            pl.BlockSpec((1, d), lambda i: (0, 0)),
            pl.BlockSpec((None, 1, d), lambda i: ((i // per) * 6 + 3, 0, 0)),
            pl.BlockSpec((None, 1, d), lambda i: ((i // per) * 6 + 4, 0, 0)),
            pl.BlockSpec((d, LANES), lambda i: (0, 0)),
            pl.BlockSpec((1, LANES), lambda i: (0, 0)),
        ],
        out_specs=(
            pl.BlockSpec((tm * ROW_TILE, LANES), lambda i: (i, 0)),
            pl.BlockSpec((tm, LANES), lambda i: (i, 0)),
            pl.BlockSpec((2, t), lambda i: (0, 0)),
            pl.BlockSpec((8, PLAN_COLS), lambda i: (0, 0)),
        ),
        out_shape=(
            jax.ShapeDtypeStruct((t * ROW_TILE, LANES), jnp.uint32),
            jax.ShapeDtypeStruct((t, LANES), F32),
            jax.ShapeDtypeStruct((2, t), jnp.int32),
            jax.ShapeDtypeStruct((8, PLAN_COLS), jnp.int32),
        ),
        scratch_shapes=[pltpu.VMEM((1, LANES), F32), pltpu.VMEM((8, t), F32)],
        compiler_params=_cparams(("arbitrary",)),
        name="moe_router",
    )(x2, g.reshape(1, d), mod, mod, wr, br)


GATHER_GROUP = 8


def _plan(plan, row, col):
    return plan[row * PLAN_COLS + col]


def _expert_kernel(plan, slots, h_hbm, wg_hbm, wu_hbm, wd_hbm, ys_ref,
                   row_tok, xbuf, xsem, sg, su, sd, wsem, wgb, wub, wdb, *, layer, n_tok):
    i = pl.program_id(0)
    nu = _plan(plan, PLAN_NUSED, 0)
    n_rows = row_tok.shape[0]

    def weight_copies(e, st):
        return (pltpu.make_async_copy(wg_hbm.at[layer, e], sg.at[st], wsem.at[st, 0]),
                pltpu.make_async_copy(wu_hbm.at[layer, e], su.at[st], wsem.at[st, 1]),
                pltpu.make_async_copy(wd_hbm.at[layer, e], sd.at[st], wsem.at[st, 2]))

    def n_groups(blk):
        return (_plan(plan, PLAN_NVALID, blk) + GATHER_GROUP - 1) // GATHER_GROUP

    def start_gather(blk, slot):
        base = blk * EXPERT_BLOCK

        def body(g, carry):
            for k in range(GATHER_GROUP):
                r = g * GATHER_GROUP + k
                tok = row_tok[base + r]
                src = pl.multiple_of(tok * ROW_TILE, ROW_TILE)
                dst = pl.multiple_of(r * ROW_TILE, ROW_TILE)
                pltpu.make_async_copy(h_hbm.at[pl.ds(src, ROW_TILE), :],
                                      xbuf.at[slot, pl.ds(dst, ROW_TILE), :], xsem.at[slot]).start()
            return carry

        lax.fori_loop(0, n_groups(blk), body, 0)

    def wait_gather(blk, slot):
        span = GATHER_GROUP * ROW_TILE

        def body(g, carry):
            pltpu.make_async_copy(h_hbm.at[pl.ds(0, span), :],
                                  xbuf.at[slot, pl.ds(0, span), :], xsem.at[slot]).wait()
            return carry

        lax.fori_loop(0, n_groups(blk), body, 0)

    @pl.when(i == 0)
    def _():
        for cp in weight_copies(_plan(plan, PLAN_EXP, 0), 0):
            cp.start(priority=1)
        second = _plan(plan, PLAN_NEXT, 0)

        @pl.when(second >= 0)
        def _():
            for cp in weight_copies(second, 1):
                cp.start(priority=1)

        xbuf[...] = jnp.zeros(xbuf.shape, xbuf.dtype)

        def pad_body(e, carry):
            end = _plan(plan, PLAN_BSTART, e) * EXPERT_BLOCK + _plan(plan, PLAN_COUNT, e)
            for k in range(GATHER_GROUP - 1):
                row_tok[jnp.minimum(end + k, n_rows - 1)] = 0
            return carry

        lax.fori_loop(0, N_EXPERTS, pad_body, 0)

        def fill_body(t, carry):
            row_tok[slots[0, t]] = t
            row_tok[slots[1, t]] = t
            return carry

        lax.fori_loop(0, n_tok, fill_body, 0, unroll=8)
        start_gather(0, 0)

    @pl.when(i < nu)
    def _():
        slot = i % 2

        @pl.when(i + 1 < nu)
        def _():
            start_gather(i + 1, 1 - slot)

        first = _plan(plan, PLAN_FIRST, i)

        @pl.when(first > 0)
        def _():
            st = (first - 1) % 2
            nxt2 = _plan(plan, PLAN_NEXT2, i)
            cps = weight_copies(_plan(plan, PLAN_EXP, i), st)
            nxt_cps = weight_copies(jnp.maximum(nxt2, 0), st)
            for cp, ncp, stage, dst in zip(cps, nxt_cps, (sg, su, sd), (wgb, wub, wdb)):
                cp.wait()
                dst[...] = stage[st].astype(BF16)

                @pl.when(nxt2 >= 0)
                def _():
                    ncp.start(priority=1)

        wait_gather(i, slot)
        parts = [_unpack_halves(w) for w in _load_row_tiles(xbuf.at[slot], EXPERT_BLOCK)]
        x_lo = jnp.concatenate([lo.astype(BF16) for lo, _ in parts], axis=1)
        x_hi = jnp.concatenate([hi.astype(BF16) for _, hi in parts], axis=1)
        half = wgb.shape[0] // 2
        g = (jnp.dot(x_lo, wgb[:half], preferred_element_type=F32)
             + jnp.dot(x_hi, wgb[half:], preferred_element_type=F32))
        u = (jnp.dot(x_lo, wub[:half], preferred_element_type=F32)
             + jnp.dot(x_hi, wub[half:], preferred_element_type=F32))
        hid = (_silu(g) * u).astype(BF16)
        _store_row_tiles(ys_ref, _pack_halves(jnp.dot(hid, wdb[...], preferred_element_type=F32)))

    @pl.when(i >= nu)
    def _():
        ys_ref[...] = jnp.zeros(ys_ref.shape, ys_ref.dtype)


def _expert_ffn(h2, w_gate, w_up, w_down, layer, plan, slots, n_blocks):
    t = h2.shape[0] // ROW_TILE
    d, f = w_gate.shape[2], w_gate.shape[3]
    n_rows = n_blocks * EXPERT_BLOCK
    grid_spec = pltpu.PrefetchScalarGridSpec(
        num_scalar_prefetch=2,
        grid=(n_blocks,),
        in_specs=[pl.BlockSpec(memory_space=pl.ANY)] * 4,
        out_specs=pl.BlockSpec((EXPERT_BLOCK * ROW_TILE, LANES), lambda i, *_: (i, 0)),
        scratch_shapes=[
            pltpu.SMEM((n_rows,), jnp.int32),
            pltpu.VMEM((2, EXPERT_BLOCK * ROW_TILE, LANES), jnp.uint32),
            pltpu.SemaphoreType.DMA((2,)),
            pltpu.VMEM((2, d, f), F32),
            pltpu.VMEM((2, d, f), F32),
            pltpu.VMEM((2, f, d), F32),
            pltpu.SemaphoreType.DMA((2, 3)),
            pltpu.VMEM((d, f), BF16),
            pltpu.VMEM((d, f), BF16),
            pltpu.VMEM((f, d), BF16),
        ],
    )
    weight_bytes = 3 * d * f * (2 * 4 + 2)
    vmem = weight_bytes + 8 * 1024 * 1024
    return pl.pallas_call(
        functools.partial(_expert_kernel, layer=layer, n_tok=t),
        grid_spec=grid_spec,
        out_shape=jax.ShapeDtypeStruct((n_rows * ROW_TILE, LANES), jnp.uint32),
        compiler_params=_cparams(("arbitrary",), vmem),
        name="moe_experts",
    )(plan, slots, h2, w_gate, w_up, w_down)


def _combine_kernel(slots, x_ref, info_ref, gate_ref, ys_hbm, *rest, tm, with_next):
    if with_next:
        ng_ref, nsh_ref, nsc_ref, o_ref, h_ref, buf, sem = rest
    else:
        o_ref, buf, sem = rest
    i = pl.program_id(0)
    last = pl.num_programs(0) - 1

    def row_copy(slot, k, src_row, r):
        src = pl.multiple_of(src_row * ROW_TILE, ROW_TILE)
        dst = pl.multiple_of(r * ROW_TILE, ROW_TILE)
        return pltpu.make_async_copy(ys_hbm.at[pl.ds(src, ROW_TILE), :],
                                     buf.at[slot, k, pl.ds(dst, ROW_TILE), :], sem.at[slot])

    def wait_rows(slot):
        for k in range(2):
            pltpu.make_async_copy(ys_hbm.at[pl.ds(0, tm * ROW_TILE), :], buf.at[slot, k], sem.at[slot]).wait()

    def start_gather(step, slot):
        base = step * tm

        def body(r, carry):
            for k in range(2):
                row_copy(slot, k, slots[k, base + r], r).start(priority=k)
            return carry

        lax.fori_loop(0, tm, body, 0, unroll=8)

    @pl.when(i == 0)
    def _():
        start_gather(0, 0)

    slot = i % 2

    @pl.when(i < last)
    def _():
        start_gather(i + 1, 1 - slot)

    wait_rows(slot)
    info = info_ref[...]
    w0 = info[:, 4:5]
    w1 = info[:, 5:6]
    half = x_ref.shape[1] // 2
    a_tiles = _load_row_tiles(buf.at[slot, 0], tm)
    b_tiles = _load_row_tiles(buf.at[slot, 1], tm)
    for s in range(ROW_TILE):
        a_lo, a_hi = _unpack_halves(a_tiles[s])
        b_lo, b_hi = _unpack_halves(b_tiles[s])
        lo = slice(s * LANES, (s + 1) * LANES)
        hi = slice(half + s * LANES, half + (s + 1) * LANES)
        o_ref[:, lo] = x_ref[:, lo] + gate_ref[:, lo] * (w0 * a_lo + w1 * b_lo)
        o_ref[:, hi] = x_ref[:, hi] + gate_ref[:, hi] * (w0 * a_hi + w1 * b_hi)
    if with_next:
        h_ref[...] = _norm_mod(o_ref[...], ng_ref[...], nsh_ref[...], nsc_ref[...]).astype(h_ref.dtype)


def _combine(x2, info, mod, seq, ys, slots, next_norm=None, tm=128):
    t, d = x2.shape
    per = seq // tm
    row = lambda i, *_: (i, 0)
    mod_row = lambda k: (lambda i, *_: ((i // per) * 6 + k, 0, 0))
    in_specs = [
        pl.BlockSpec((tm, d), row),
        pl.BlockSpec((tm, LANES), row),
        pl.BlockSpec((None, 1, d), mod_row(5)),
        pl.BlockSpec(memory_space=pl.ANY),
    ]
    args = [slots, x2, info, mod, ys]
    out_specs = pl.BlockSpec((tm, d), row)
    out_shape = jax.ShapeDtypeStruct((t, d), F32)
    if next_norm is not None:
        next_g, next_mod = next_norm
        in_specs += [pl.BlockSpec((1, d), lambda i, *_: (0, 0)),
                     pl.BlockSpec((None, 1, d), mod_row(0)), pl.BlockSpec((None, 1, d), mod_row(1))]
        args += [next_g.reshape(1, d), next_mod, next_mod]
        out_specs = (out_specs, pl.BlockSpec((tm, d), row))
        out_shape = (out_shape, jax.ShapeDtypeStruct((t, d), BF16))
    grid_spec = pltpu.PrefetchScalarGridSpec(
        num_scalar_prefetch=1,
        grid=(t // tm,),
        in_specs=in_specs,
        out_specs=out_specs,
        scratch_shapes=[
            pltpu.VMEM((2, 2, tm * ROW_TILE, LANES), jnp.uint32),
            pltpu.SemaphoreType.DMA((2,)),
        ],
    )
    return pl.pallas_call(
        functools.partial(_combine_kernel, tm=tm, with_next=next_norm is not None),
        grid_spec=grid_spec,
        out_shape=out_shape,
        compiler_params=_cparams(("arbitrary",)),
        name="moe_combine",
    )(*args)


def _hier_moe(x2, norm_g, mod, seq, w_rg, b_rg, w_re, b_re, w_gate, w_up, w_down, layer, next_norm=None):
    t, d = x2.shape
    pad = LANES - N_GROUPS - N_EXPERTS
    wr = jnp.concatenate([w_rg, w_re, jnp.zeros((d, pad), F32)], axis=1)
    br = jnp.concatenate([b_rg, b_re, jnp.zeros((pad,), F32)]).reshape(1, LANES)
    h2, info, slots, plan = _router(x2, norm_g, mod, seq, wr, br)
    n_assign = 2 * t
    n_blocks = (n_assign + N_EXPERTS * (EXPERT_BLOCK - 1) + EXPERT_BLOCK - 1) // EXPERT_BLOCK
    assert n_blocks <= PLAN_COLS
    plan = plan.reshape(-1)
    ys = _expert_ffn(h2, w_gate, w_up, w_down, layer, plan, slots, n_blocks)
    return _combine(x2, info, mod, seq, ys, slots, next_norm)


def kernel(x, c, ada_w, ada_b, norm1_g, norm2_g, dsa_w_in, dsa_q_gain, dsa_k_gain, dsa_w_out, mla_w_in, mla_cq_gain, mla_ckv_gain, mla_w_q_up, mla_w_kv_up, mla_q_gain, mla_k_gain, mla_w_out, router_group_w, router_group_b, router_expert_w, router_expert_b, expert_w_gate, expert_w_up, expert_w_down):
    b, s, d = x.shape
    t = b * s
    mods = _ada_mod(c, ada_w, ada_b)
    x2 = x.reshape(t, d)

    mod = mods[0]
    h = _normmod(x2, norm1_g[0], mod, s, 0, 1)
    gains = (dsa_q_gain[0], dsa_k_gain[0], dsa_k_gain[0])
    qkv = [[_dsa_proj(h, dsa_w_in[0], gains[w], g, w, b) for w in range(3)] for g in range(len(DIL_PAIRS))]
    og2 = _dsa_attn(*qkv[2], 2)
    og1 = _dsa_attn(*qkv[1], 1)
    o = _dsa_attn(*qkv[0], 0, others=(og1, og2))
    x2 = _resid_mm(o.reshape(t, DSA_WIDTH), dsa_w_out[0], x2, mod, s, 2)
    x2, h = _hier_moe(x2, norm2_g[0], mod, s, router_group_w[0], router_group_b[0],
                      router_expert_w[0], router_expert_b[0],
                      expert_w_gate, expert_w_up, expert_w_down, 0, next_norm=(norm1_g[1], mods[1]))

    mod = mods[1]
    cq, ckv, kpe = _mla_in(h, mla_w_in, mla_cq_gain[0], mla_ckv_gain[0])
    tab = _rope_tables(s)
    q = _mla_q(cq, mla_w_q_up, mla_q_gain[0], tab, s)
    k, v = _mla_kv(ckv, mla_w_kv_up[0], kpe, mla_k_gain[0], tab, s)
    o = _mla_attn(q.reshape(b, s, -1), k.reshape(b, s, -1), v.reshape(b, s, -1))
    x2 = _resid_mm(o.reshape(t, MLA_HEADS * MLA_V), mla_w_out[0], x2, mod, s, 2)
    x2 = _hier_moe(x2, norm2_g[1], mod, s, router_group_w[1], router_group_b[1],
                   router_expert_w[1], router_expert_b[1],
                   expert_w_gate, expert_w_up, expert_w_down, 1)
    return x2.reshape(b, s, d)
```

```python
import functools
import math

import jax
import jax.numpy as jnp
import numpy as np
from jax import lax
from jax.experimental import pallas as pl
from jax.experimental.pallas import tpu as pltpu

F32 = jnp.float32
BF16 = jnp.bfloat16

D_MODEL = 2048
EPS = 1e-6
LANES = 128
NEG_BIG = -1e30

DIL_PAIRS = ((128, 1), (512, 4), (2048, 16))
DSA_HEADS = 8
DSA_HEAD_DIM = 128
DSA_WIDTH = DSA_HEADS * DSA_HEAD_DIM
BAND = 128

MLA_HEADS = 16
MLA_Q_LORA = 512
MLA_KV_LORA = 512
MLA_NOPE = 128
MLA_ROPE = 64
MLA_V = 128
MLA_QK = MLA_NOPE + MLA_ROPE
MLA_QK_PAD = 256
ROPE_THETA = 10000.0

N_GROUPS = 4
EXPERTS_PER_GROUP = 16
N_EXPERTS = N_GROUPS * EXPERTS_PER_GROUP
D_EXPERT = 768
EXPERT_BLOCK = 128

VMEM_LIMIT = 48 * 1024 * 1024


def _cparams(sem, vmem=VMEM_LIMIT):
    return pltpu.CompilerParams(dimension_semantics=sem, vmem_limit_bytes=vmem)


def _silu(x):
    return x * (1.0 / (1.0 + jnp.exp(-x)))


def _pack_halves(x):
    n = x.shape[1] // 2
    xb = x.astype(BF16).astype(F32)
    lo = lax.bitcast_convert_type(xb[:, :n], jnp.uint32) >> 16
    hi = lax.bitcast_convert_type(xb[:, n:], jnp.uint32) & jnp.uint32(0xFFFF0000)
    return hi | lo


def _unpack_halves(w):
    lo = lax.bitcast_convert_type(w << 16, F32)
    hi = lax.bitcast_convert_type(w & jnp.uint32(0xFFFF0000), F32)
    return lo, hi


ROW_TILE = 8


def _store_row_tiles(ref, words):
    rows = words.shape[0]
    for s in range(ROW_TILE):
        ref[pl.ds(s, rows, stride=ROW_TILE), :] = words[:, s * LANES:(s + 1) * LANES]


def _load_row_tiles(ref, rows):
    return [ref[pl.ds(s, rows, stride=ROW_TILE), :] for s in range(ROW_TILE)]


def _norm_mod(x, g, shift, scale):
    ms = jnp.mean(x * x, axis=-1, keepdims=True)
    y = x * lax.rsqrt(ms + EPS) * g
    return y * (1.0 + scale) + shift


def _ada_kernel(c_ref, w_ref, b_ref, o_ref):
    ca = _silu(c_ref[...])
    hi = ca.astype(BF16)
    lo = (ca - hi.astype(F32)).astype(BF16)
    lhs = jnp.concatenate([hi, lo], axis=0)
    res = jnp.dot(lhs, w_ref[...].astype(BF16), preferred_element_type=F32)
    o_ref[...] = res[:8] + res[8:] + b_ref[...]


def _ada_mod(c, ada_w, ada_b):
    depth, d, n = ada_w.shape
    b = c.shape[0]
    c8 = jnp.pad(c, ((0, 8 - b), (0, 0)))
    tn = 1024
    out = pl.pallas_call(
        _ada_kernel,
        grid=(depth, n // tn),
        in_specs=[
            pl.BlockSpec((8, d), lambda i, j: (0, 0)),
            pl.BlockSpec((None, d, tn), lambda i, j: (i, 0, j)),
            pl.BlockSpec((None, 1, tn), lambda i, j: (i, 0, j)),
        ],
        out_specs=pl.BlockSpec((None, 8, tn), lambda i, j: (i, 0, j)),
        out_shape=jax.ShapeDtypeStruct((depth, 8, n), F32),
        compiler_params=_cparams(("arbitrary", "arbitrary")),
        name="ada_mod",
    )(c8, ada_w, ada_b.reshape(depth, 1, n))
    return out[:, :b].reshape(depth, b * 6, 1, d)


def _normmod_kernel(x_ref, g_ref, sh_ref, sc_ref, o_ref):
    o_ref[...] = _norm_mod(x_ref[...], g_ref[...], sh_ref[...], sc_ref[...]).astype(o_ref.dtype)


def _normmod(x2, g, mod, seq, k_shift, k_scale, tm=512):
    t, d = x2.shape
    per = seq // tm
    return pl.pallas_call(
        _normmod_kernel,
        grid=(t // tm,),
        in_specs=[
            pl.BlockSpec((tm, d), lambda i: (i, 0)),
            pl.BlockSpec((1, d), lambda i: (0, 0)),
            pl.BlockSpec((None, 1, d), lambda i: ((i // per) * 6 + k_shift, 0, 0)),
            pl.BlockSpec((None, 1, d), lambda i: ((i // per) * 6 + k_scale, 0, 0)),
        ],
        out_specs=pl.BlockSpec((tm, d), lambda i: (i, 0)),
        out_shape=jax.ShapeDtypeStruct((t, d), BF16),
        compiler_params=_cparams(("arbitrary",)),
        name="normmod",
    )(x2, g.reshape(1, d), mod, mod)


def _cast_weight_once(w_ref, wb_ref):
    @pl.when(pl.program_id(1) == 0)
    def _():
        wb_ref[...] = w_ref[...].astype(BF16)


def _dsa_proj_kernel(a_ref, w_ref, g_ref, o_ref, wb_ref, res_ref, *, dilation, normed, gain_scale):
    @pl.when(pl.program_id(0) == 0)
    def _():
        wb_ref[...] = w_ref[...].astype(BF16)

    gain = g_ref[...] * gain_scale
    tm = a_ref.shape[0]
    pr = 512
    n_parts = tm // pr
    accs = [jnp.dot(a_ref[p * pr:(p + 1) * pr, :], wb_ref[...], preferred_element_type=F32)
            for p in range(n_parts)]
    for p, acc in enumerate(accs):
        blks = []
        for h in range(DSA_HEADS):
            blk = acc[:, h * LANES:(h + 1) * LANES]
            if normed:
                ms = jnp.mean(blk * blk, axis=-1, keepdims=True)
                blk = blk * lax.rsqrt(ms + EPS) * gain
            blks.append(blk)
        for h, blk in enumerate(blks):
            hs = slice(h * LANES, (h + 1) * LANES)
            if dilation == 1:
                o_ref[0, p * pr:(p + 1) * pr, hs] = blk.astype(o_ref.dtype)
            else:
                res_ref[h, p * pr:(p + 1) * pr, :] = blk
                sub_rows = pr // dilation
                for r in range(dilation):
                    sub = res_ref[h, pl.ds(p * pr + r, sub_rows, stride=dilation), :]
                    o_ref[r, p * sub_rows:(p + 1) * sub_rows, hs] = sub.astype(o_ref.dtype)


def _dsa_proj(h, w_in, gain, group, which, batch, tm=1024, tn=DSA_WIDTH):
    t, d = h.shape
    dil = DIL_PAIRS[group][1]
    seq = t // batch
    per = seq // tm
    gain_scale = 1.0 / math.sqrt(DSA_HEAD_DIM) if which == 0 else 1.0
    return pl.pallas_call(
        functools.partial(_dsa_proj_kernel, dilation=dil, normed=which != 2, gain_scale=gain_scale),
        grid=(t // tm,),
        in_specs=[
            pl.BlockSpec((tm, d), lambda i: (i, 0)),
            pl.BlockSpec((d, tn), lambda i: (0, group * 3 + which), pipeline_mode=pl.Buffered(1)),
            pl.BlockSpec((1, LANES), lambda i: (0, 0)),
        ],
        out_specs=pl.BlockSpec((None, dil, tm // dil, tn), lambda i: (i // per, 0, i % per, 0)),
        out_shape=jax.ShapeDtypeStruct((batch, dil, seq // dil, tn), BF16),
        scratch_shapes=[pltpu.VMEM((d, tn), BF16), pltpu.VMEM((DSA_HEADS, tm, LANES), F32)],
        compiler_params=_cparams(("arbitrary",)),
        name=f"dsa_proj_g{group}_{'qkv'[which]}",
    )(h, w_in, gain.reshape(1, LANES))


def _resid_mm_kernel(a_ref, w_ref, x_ref, gate_ref, o_ref, wb_ref):
    _cast_weight_once(w_ref, wb_ref)
    y = jnp.dot(a_ref[...], wb_ref[...], preferred_element_type=F32)
    o_ref[...] = x_ref[...] + gate_ref[...] * y


def _resid_mm(a, w, x2, mod, seq, k_gate, tn=1024):
    t, k = a.shape
    n = w.shape[1]
    tm = 1024 if k <= 1024 else 512
    per = seq // tm
    return pl.pallas_call(
        _resid_mm_kernel,
        grid=(n // tn, t // tm),
        in_specs=[
            pl.BlockSpec((tm, k), lambda j, i: (i, 0)),
            pl.BlockSpec((k, tn), lambda j, i: (0, j)),
            pl.BlockSpec((tm, tn), lambda j, i: (i, j)),
            pl.BlockSpec((None, 1, tn), lambda j, i: ((i // per) * 6 + k_gate, 0, j)),
        ],
        out_specs=pl.BlockSpec((tm, tn), lambda j, i: (i, j)),
        out_shape=jax.ShapeDtypeStruct((t, n), F32),
        scratch_shapes=[pltpu.VMEM((k, tn), BF16)],
        compiler_params=_cparams(("arbitrary", "arbitrary")),
        name="resid_mm",
    )(a, w, x2, mod)


def _alibi_slope(head_slot, group):
    n = len(DIL_PAIRS) * DSA_HEADS
    return 2.0 ** (-8.0 * (head_slot * len(DIL_PAIRS) + group + 1.0) / n)


def _dsa_bias_table(group, has_prev):
    window, d = DIL_PAIRS[group]
    steps = window // d
    qi = np.arange(BAND)[:, None]
    if has_prev:
        kj = np.arange(2 * BAND)[None, :]
        delta = qi + BAND - kj
        prev_key = np.broadcast_to(kj < BAND, delta.shape)
    else:
        kj = np.arange(BAND)[None, :]
        delta = qi - kj
        prev_key = np.zeros(delta.shape, bool)
    inside = (delta >= 0) & (delta <= steps)
    tabs = []
    for first in (True, False):
        valid = inside & ~(prev_key & first)
        per_head = [np.where(valid, -_alibi_slope(h, group) * (delta * d), NEG_BIG) for h in range(DSA_HEADS)]
        tabs.append(np.stack(per_head))
    return jnp.asarray(np.stack(tabs), F32)


def _dsa_attn_kernel(*refs, dilation, has_prev, merge):
    refs = list(refs)
    q_ref, kc_ref, vc_ref, bias_ref = refs[:4]
    pos = 4
    if has_prev:
        kp_ref, vp_ref = refs[pos:pos + 2]
        pos += 2
    if merge:
        other = refs[pos:pos + 4]
        pos += 4
        o_ref = refs[pos]
    else:
        o_ref, lse_ref = refs[pos:pos + 2]

    r = pl.program_id(2)
    lane = lax.broadcasted_iota(jnp.int32, (BAND, LANES), 1)
    nt = (((1,), (1,)), ((), ()))
    nk = 2 * BAND if has_prev else BAND
    ones = jnp.ones((nk, LANES), BF16)
    n_batch = q_ref.shape[0]

    def head_scores(bi, h):
        hs = slice(h * LANES, (h + 1) * LANES)
        q = q_ref[bi, :, hs]
        if has_prev:
            k = jnp.concatenate([kp_ref[bi, :, hs], kc_ref[bi, :, hs]], axis=0)
        else:
            k = kc_ref[bi, :, hs]
        return lax.dot_general(q, k, nt, preferred_element_type=F32) + bias_ref[h]

    def head_values(bi, h):
        hs = slice(h * LANES, (h + 1) * LANES)
        if has_prev:
            v = jnp.concatenate([vp_ref[bi, :, hs], vc_ref[bi, :, hs]], axis=0)
        else:
            v = vc_ref[bi, :, hs]
        return jnp.concatenate([v, ones], axis=1)

    results = []
    for bi in range(n_batch):
        scores = [head_scores(bi, h) for h in range(DSA_HEADS)]
        maxes = [jnp.max(s, axis=-1, keepdims=True) for s in scores]
        probs = [jnp.exp(s - m).astype(BF16) for s, m in zip(scores, maxes)]
        accs = [jnp.dot(p, head_values(bi, h), preferred_element_type=F32) for h, p in enumerate(probs)]
        outs = []
        lse_tile = jnp.zeros((BAND, LANES), F32)
        for h in range(DSA_HEADS):
            l = accs[h][:, LANES:]
            o = accs[h][:, :LANES] / l
            lse = maxes[h] + jnp.log(l)
            if merge:
                lses = [lse] + [other[2 * g + 1][bi, :, h * 16:h * 16 + 1] for g in range(2)]
                parts = [o] + [other[2 * g][bi, h] for g in range(2)]
                top = jnp.maximum(jnp.maximum(lses[0], lses[1]), lses[2])
                es = [jnp.exp(x - top) for x in lses]
                den = es[0] + es[1] + es[2]
                o = ((es[0] * parts[0] + es[1] * parts[1] + es[2] * parts[2]) / den).astype(o_ref.dtype)
            else:
                in_head = jnp.logical_and(lane >= h * 16, lane < (h + 1) * 16)
                lse_tile = jnp.where(in_head, lse, lse_tile)
            outs.append(o)
        results.append((outs, lse_tile))

    for bi, (outs, lse_tile) in enumerate(results):
        for h, o in enumerate(outs):
            if merge:
                o_ref[bi, :, h * LANES:(h + 1) * LANES] = o
            else:
                o_ref[bi, h, pl.ds(r, BAND, stride=dilation), :] = o
        if not merge:
            lse_ref[bi, pl.ds(r, BAND, stride=dilation), :] = lse_tile


def _dsa_attn(q_g, k_g, v_g, group, others=None):
    b, d, sub_len, _ = q_g.shape
    s = d * sub_len
    nb = sub_len // BAND
    has_prev = nb > 1
    merge = others is not None
    assert not merge or d == 1
    bias = _dsa_bias_table(group, has_prev)

    bb = 2 if b % 2 == 0 else 1
    cur = lambda bi, n, r: (bi, r, n, 0)
    prev = lambda bi, n, r: (bi, r, jnp.maximum(n - 1, 0), 0)
    blk = (bb, None, BAND, DSA_WIDTH)
    in_specs = [pl.BlockSpec(blk, cur), pl.BlockSpec(blk, cur), pl.BlockSpec(blk, cur),
                pl.BlockSpec((None,) + bias.shape[1:], lambda bi, n, r: (jnp.minimum(n, 1), 0, 0, 0))]
    args = [q_g, k_g, v_g, bias]
    if has_prev:
        in_specs += [pl.BlockSpec(blk, prev), pl.BlockSpec(blk, prev)]
        args += [k_g, v_g]
    span = BAND * d
    nat = lambda bi, n, r: (bi, n, 0)
    nat_heads = lambda bi, n, r: (bi, 0, n, 0)
    if merge:
        for o_g, lse_g in others:
            in_specs += [pl.BlockSpec((bb, DSA_HEADS, BAND, LANES), nat_heads),
                         pl.BlockSpec((bb, BAND, LANES), nat)]
            args += [o_g, lse_g]
        out_shape = jax.ShapeDtypeStruct((b, s, DSA_WIDTH), BF16)
        out_specs = pl.BlockSpec((bb, BAND, DSA_WIDTH), nat)
    else:
        out_shape = (jax.ShapeDtypeStruct((b, DSA_HEADS, s, LANES), F32), jax.ShapeDtypeStruct((b, s, LANES), F32))
        out_specs = (pl.BlockSpec((bb, DSA_HEADS, span, LANES), nat_heads),
                     pl.BlockSpec((bb, span, LANES), nat))
    return pl.pallas_call(
        functools.partial(_dsa_attn_kernel, dilation=d, has_prev=has_prev, merge=merge),
        grid=(b // bb, nb, d),
        in_specs=in_specs,
        out_specs=out_specs,
        out_shape=out_shape,
        compiler_params=_cparams(("arbitrary", "arbitrary", "arbitrary")),
        name=f"dsa_attn_g{group}",
    )(*args)


def _rope_tables(seq):
    half = MLA_ROPE // 2
    inv = np.float32(ROPE_THETA) ** (-np.arange(half, dtype=np.float32) / np.float32(half))
    ang = np.arange(seq, dtype=np.float32)[:, None] * inv[None, :]
    cos, sin = np.cos(ang), np.sin(ang)
    z = np.zeros((seq, LANES - MLA_ROPE), np.float32)
    zh = np.zeros((seq, half), np.float32)
    cos_t = np.concatenate([cos, cos, z], axis=1)
    sin_a = np.concatenate([zh, sin, z], axis=1)
    sin_b = np.concatenate([-sin, zh, z], axis=1)
    return jnp.asarray(np.concatenate([cos_t, sin_a, sin_b], axis=1), F32)


def _rope_lanes(x, tab):
    half = MLA_ROPE // 2
    cos_t = tab[:, 0:LANES]
    sin_a = tab[:, LANES:2 * LANES]
    sin_b = tab[:, 2 * LANES:3 * LANES]
    return (x * cos_t + pltpu.roll(x, half, 1) * sin_a
            + pltpu.roll(x, LANES - half, 1) * sin_b)


def _mla_in_kernel(a_ref, w_ref, cqg_ref, ckvg_ref, cq_ref, ckv_ref, kpe_ref, wb_ref):
    n = w_ref.shape[1]

    @pl.when(pl.program_id(0) == 0)
    def _():
        wb_ref[...] = jnp.zeros(wb_ref.shape, BF16)
        wb_ref[:, :n] = w_ref[...].astype(BF16)

    acc = jnp.dot(a_ref[...], wb_ref[...], preferred_element_type=F32)
    cq = acc[:, :MLA_Q_LORA]
    cq_ref[...] = (cq * lax.rsqrt(jnp.mean(cq * cq, axis=-1, keepdims=True) + EPS)
                   * cqg_ref[...]).astype(BF16)
    ckv = acc[:, MLA_Q_LORA:MLA_Q_LORA + MLA_KV_LORA]
    ckv_ref[...] = (ckv * lax.rsqrt(jnp.mean(ckv * ckv, axis=-1, keepdims=True) + EPS)
                    * ckvg_ref[...]).astype(BF16)
    kpe_ref[...] = acc[:, MLA_Q_LORA + MLA_KV_LORA:]


def _mla_in(h, w_in, cq_gain, ckv_gain, tm=512):
    t, d = h.shape
    n = w_in.shape[2]
    n_pad = MLA_Q_LORA + MLA_KV_LORA + LANES
    return pl.pallas_call(
        _mla_in_kernel,
        grid=(t // tm,),
        in_specs=[
            pl.BlockSpec((tm, d), lambda i: (i, 0)),
            pl.BlockSpec((None, d, n), lambda i: (0, 0, 0)),
            pl.BlockSpec((1, MLA_Q_LORA), lambda i: (0, 0)),
            pl.BlockSpec((1, MLA_KV_LORA), lambda i: (0, 0)),
        ],
        out_specs=(
            pl.BlockSpec((tm, MLA_Q_LORA), lambda i: (i, 0)),
            pl.BlockSpec((tm, MLA_KV_LORA), lambda i: (i, 0)),
            pl.BlockSpec((tm, LANES), lambda i: (i, 0)),
        ),
        out_shape=(
            jax.ShapeDtypeStruct((t, MLA_Q_LORA), BF16),
            jax.ShapeDtypeStruct((t, MLA_KV_LORA), BF16),
            jax.ShapeDtypeStruct((t, LANES), F32),
        ),
        scratch_shapes=[pltpu.VMEM((d, n_pad), BF16)],
        compiler_params=_cparams(("arbitrary",)),
        name="mla_in",
    )(h, w_in, cq_gain.reshape(1, -1), ckv_gain.reshape(1, -1))


def _mla_q_kernel(a_ref, w_ref, g0_ref, g1_ref, tab_ref, o_ref, wb_ref):
    @pl.when(pl.program_id(0) == 0)
    def _():
        for h in range(MLA_HEADS):
            src = h * MLA_QK
            dst = h * MLA_QK_PAD
            rope_cols = w_ref[:, src + MLA_NOPE:src + MLA_QK].astype(BF16)
            wb_ref[:, dst:dst + MLA_NOPE] = w_ref[:, src:src + MLA_NOPE].astype(BF16)
            wb_ref[:, dst + MLA_NOPE:dst + MLA_QK] = rope_cols
            wb_ref[:, dst + MLA_QK:dst + MLA_QK_PAD] = rope_cols

    acc = jnp.dot(a_ref[...], wb_ref[...], preferred_element_type=F32)
    tab = tab_ref[...]
    cos_t = tab[:, 0:LANES]
    sin_t = tab[:, LANES:2 * LANES] + tab[:, 2 * LANES:3 * LANES]
    scale = math.log2(math.e) / math.sqrt(MLA_QK)
    g0 = g0_ref[...] * scale
    g1 = g1_ref[...] * scale
    outs = []
    for h in range(MLA_HEADS):
        base = h * MLA_QK_PAD
        x0 = acc[:, base:base + LANES]
        x1 = acc[:, base + LANES:base + 2 * LANES]
        ss = jnp.sum(x0 * x0 + 0.5 * (x1 * x1), axis=-1, keepdims=True)
        rs = lax.rsqrt(ss / MLA_QK + EPS)
        x1n = x1 * rs * g1
        q_rot = x1n * cos_t + pltpu.roll(x1n, MLA_ROPE // 2, 1) * sin_t
        outs.append(((x0 * rs * g0).astype(BF16), q_rot.astype(BF16)))
    for h, (q_nope, q_rope) in enumerate(outs):
        base = h * MLA_QK_PAD
        o_ref[:, base:base + LANES] = q_nope
        o_ref[:, base + LANES:base + 2 * LANES] = q_rope


def _mla_q(cq, w_q_up, q_gain, tab, seq, tm=512):
    t, k = cq.shape
    n = MLA_HEADS * MLA_QK_PAD
    per = seq // tm
    g0 = q_gain[:MLA_NOPE].reshape(1, LANES)
    g1 = jnp.concatenate([q_gain[MLA_NOPE:], q_gain[MLA_NOPE:]]).reshape(1, LANES)
    return pl.pallas_call(
        _mla_q_kernel,
        grid=(t // tm,),
        in_specs=[
            pl.BlockSpec((tm, k), lambda i: (i, 0)),
            pl.BlockSpec((None, k, MLA_HEADS * MLA_QK), lambda i: (0, 0, 0)),
            pl.BlockSpec((1, LANES), lambda i: (0, 0)),
            pl.BlockSpec((1, LANES), lambda i: (0, 0)),
            pl.BlockSpec((tm, 3 * LANES), lambda i: (i % per, 0)),
        ],
        out_specs=pl.BlockSpec((tm, n), lambda i: (i, 0)),
        out_shape=jax.ShapeDtypeStruct((t, n), BF16),
        scratch_shapes=[pltpu.VMEM((k, n), BF16)],
        compiler_params=_cparams(("arbitrary",)),
        name="mla_q_up",
    )(cq, w_q_up, g0, g1, tab)


def _mla_kv_kernel(a_ref, w_ref, kpe_ref, g0_ref, g1_ref, tab_ref, k_ref, v_ref, wb_ref):
    @pl.when(pl.program_id(0) == 0)
    def _():
        wb_ref[...] = w_ref[...].astype(BF16)

    acc = jnp.dot(a_ref[...], wb_ref[...], preferred_element_type=F32)
    tab = tab_ref[...]
    kpe = kpe_ref[...]
    ss_pe = jnp.sum(kpe * kpe, axis=-1, keepdims=True)
    g0 = g0_ref[...]
    kpe_rot = _rope_lanes(kpe * g1_ref[...], tab)
    outs = []
    for h in range(MLA_HEADS):
        base = h * (MLA_NOPE + MLA_V)
        kn = acc[:, base:base + MLA_NOPE]
        ss = jnp.sum(kn * kn, axis=-1, keepdims=True) + ss_pe
        rs = lax.rsqrt(ss / MLA_QK + EPS)
        outs.append(((kn * rs * g0).astype(BF16), (kpe_rot * rs).astype(BF16),
                     acc[:, base + MLA_NOPE:base + MLA_NOPE + MLA_V].astype(BF16)))
    ones = jnp.ones((acc.shape[0], MLA_V), BF16)
    for h, (k_nope, k_rope, v) in enumerate(outs):
        kb = h * MLA_QK_PAD
        k_ref[:, kb:kb + LANES] = k_nope
        k_ref[:, kb + LANES:kb + 2 * LANES] = k_rope
        vb = h * 2 * MLA_V
        v_ref[:, vb:vb + MLA_V] = v
        v_ref[:, vb + MLA_V:vb + 2 * MLA_V] = ones


def _mla_kv(ckv, w_kv_up, kpe, k_gain, tab, seq, tm=512):
    t, k = ckv.shape
    n = w_kv_up.shape[1]
    per = seq // tm
    g0 = k_gain[:MLA_NOPE].reshape(1, LANES)
    g1 = jnp.pad(k_gain[MLA_NOPE:], (0, LANES - MLA_ROPE)).reshape(1, LANES)
    return pl.pallas_call(
        _mla_kv_kernel,
        grid=(t // tm,),
        in_specs=[
            pl.BlockSpec((tm, k), lambda i: (i, 0)),
            pl.BlockSpec((k, n), lambda i: (0, 0)),
            pl.BlockSpec((tm, LANES), lambda i: (i, 0)),
            pl.BlockSpec((1, LANES), lambda i: (0, 0)),
            pl.BlockSpec((1, LANES), lambda i: (0, 0)),
            pl.BlockSpec((tm, 3 * LANES), lambda i: (i % per, 0)),
        ],
        out_specs=(
            pl.BlockSpec((tm, MLA_HEADS * MLA_QK_PAD), lambda i: (i, 0)),
            pl.BlockSpec((tm, MLA_HEADS * 2 * MLA_V), lambda i: (i, 0)),
        ),
        out_shape=(
            jax.ShapeDtypeStruct((t, MLA_HEADS * MLA_QK_PAD), BF16),
            jax.ShapeDtypeStruct((t, MLA_HEADS * 2 * MLA_V), BF16),
        ),
        scratch_shapes=[pltpu.VMEM((k, n), BF16)],
        compiler_params=_cparams(("arbitrary",)),
        name="mla_kv_up",
    )(ckv, w_kv_up, kpe, g0, g1, tab)


def _mla_attn_kernel(q_ref, k_ref, v_ref, o_ref, *, tq, heads):
    seq = q_ref.shape[0]
    nt = (((1,), (1,)), ((), ()))
    r = lax.broadcasted_iota(jnp.int32, (tq, tq), 0)
    c = lax.broadcasted_iota(jnp.int32, (tq, tq), 1)
    causal = c <= r
    for qi in range(seq // tq):
        rows = slice(qi * tq, (qi + 1) * tq)
        outs = []
        for hh in range(heads):
            qk_cols = slice(hh * MLA_QK_PAD, (hh + 1) * MLA_QK_PAD)
            v_cols = slice(hh * 2 * MLA_V, (hh + 1) * 2 * MLA_V)
            q = q_ref[rows, qk_cols]
            scores = []
            for j in range(qi + 1):
                s = lax.dot_general(q, k_ref[j * tq:(j + 1) * tq, qk_cols], nt, preferred_element_type=F32)
                if j == qi:
                    s = jnp.where(causal, s, NEG_BIG)
                scores.append(s)
            top = scores[0]
            for s in scores[1:]:
                top = jnp.maximum(top, s)
            m = jnp.max(top, axis=-1, keepdims=True)
            acc = None
            for j, s in enumerate(scores):
                p = jnp.exp2(s - m).astype(BF16)
                pv = jnp.dot(p, v_ref[j * tq:(j + 1) * tq, v_cols], preferred_element_type=F32)
                acc = pv if acc is None else acc + pv
            outs.append((acc[:, :MLA_V] / acc[:, MLA_V:]).astype(o_ref.dtype))
        for hh in range(heads):
            o_ref[rows, hh * MLA_V:(hh + 1) * MLA_V] = outs[hh]


def _mla_attn(q, k, v, tq=256, heads=4):
    b, s, _ = q.shape
    return pl.pallas_call(
        functools.partial(_mla_attn_kernel, tq=tq, heads=heads),
        grid=(b, MLA_HEADS // heads),
        in_specs=[
            pl.BlockSpec((None, s, heads * MLA_QK_PAD), lambda bi, h: (bi, 0, h)),
            pl.BlockSpec((None, s, heads * MLA_QK_PAD), lambda bi, h: (bi, 0, h)),
            pl.BlockSpec((None, s, heads * 2 * MLA_V), lambda bi, h: (bi, 0, h)),
        ],
        out_specs=pl.BlockSpec((None, s, heads * MLA_V), lambda bi, h: (bi, 0, h)),
        out_shape=jax.ShapeDtypeStruct((b, s, MLA_HEADS * MLA_V), BF16),
        compiler_params=_cparams(("arbitrary", "arbitrary")),
        name="mla_attn",
    )(q, k, v)


PLAN_COLS = 256
(PLAN_EXP, PLAN_FIRST, PLAN_NEXT, PLAN_NUSED, PLAN_NVALID, PLAN_BSTART, PLAN_COUNT,
 PLAN_NEXT2) = range(8)


def _dispatch_plan_tile(cnt):
    nblk = jnp.floor((cnt + (EXPERT_BLOCK - 1.0)) * (1.0 / EXPERT_BLOCK))
    e_r = lax.broadcasted_iota(jnp.int32, (LANES, LANES), 0)
    e_c = lax.broadcasted_iota(jnp.int32, (LANES, LANES), 1)
    before = jnp.where(e_r < e_c, 1.0, 0.0).astype(BF16)
    used = jnp.where(nblk > 0.0, 1.0, 0.0)
    sub = lax.broadcasted_iota(jnp.int32, (8, LANES), 0)
    prefix = jnp.dot(jnp.where(sub == 0, nblk, used).astype(BF16), before, preferred_element_type=F32)
    bstart = prefix[0:1]
    ordinal = prefix[1:2]
    bend = bstart + nblk
    n_used = jnp.max(bend, axis=-1, keepdims=True)
    row = lax.broadcasted_iota(jnp.int32, (PLAN_COLS, LANES), 0).astype(F32)
    lane = lax.broadcasted_iota(jnp.int32, (PLAN_COLS, LANES), 1).astype(F32)
    is_exp = lane < N_EXPERTS
    row1 = row[:, 0:1]
    done = jnp.logical_and(bend <= row, is_exp)
    blk_exp = jnp.minimum(jnp.sum(jnp.where(done, 1.0, 0.0), axis=-1, keepdims=True), N_EXPERTS - 1.0)
    mine = lane == blk_exp
    bstart_of = jnp.sum(jnp.where(mine, bstart, 0.0), axis=-1, keepdims=True)
    cnt_of = jnp.sum(jnp.where(mine, cnt, 0.0), axis=-1, keepdims=True)
    ord_of = jnp.sum(jnp.where(mine, ordinal, 0.0), axis=-1, keepdims=True)
    valid = row1 < n_used
    first = jnp.where(jnp.logical_and(valid, row1 == bstart_of), ord_of + 1.0, 0.0)
    nvalid = jnp.clip(cnt_of - EXPERT_BLOCK * (row1 - bstart_of), 0.0, float(EXPERT_BLOCK))
    nvalid = jnp.where(valid, nvalid, 0.0)
    usable = jnp.logical_and(nblk > 0.0, is_exp)

    def next_used(after):
        nxt = jnp.min(jnp.where(jnp.logical_and(lane > after, usable), lane, 999.0), axis=-1, keepdims=True)
        return jnp.where(nxt > 998.0, -1.0, nxt)

    nxt = next_used(blk_exp)
    nxt2 = jnp.where(nxt < 0.0, -1.0, next_used(nxt))
    bstart_col = jnp.sum(jnp.where(lane < row, nblk, 0.0), axis=-1, keepdims=True)
    cnt_col = jnp.sum(jnp.where(lane == row, cnt, 0.0), axis=-1, keepdims=True)
    tile = jnp.zeros((PLAN_COLS, LANES), F32)
    cols = {PLAN_EXP: blk_exp, PLAN_FIRST: first, PLAN_NEXT: nxt, PLAN_NUSED: n_used,
            PLAN_NVALID: nvalid, PLAN_BSTART: bstart_col, PLAN_COUNT: cnt_col, PLAN_NEXT2: nxt2}
    for k, val in cols.items():
        tile = jnp.where(lane == k, val, tile)
    return tile


def _router_kernel(x_ref, g_ref, sh_ref, sc_ref, wr_ref, br_ref, h_ref, info_ref, slots_ref, plan_ref,
                   carry_ref, idx_all):
    i = pl.program_id(0)
    tm = x_ref.shape[0]

    @pl.when(i == 0)
    def _():
        carry_ref[...] = jnp.zeros(carry_ref.shape, F32)

    h = _norm_mod(x_ref[...], g_ref[...], sh_ref[...], sc_ref[...])
    w = wr_ref[...]
    w_hi = w.astype(BF16)
    w_lo = (w - w_hi.astype(F32)).astype(BF16)
    h_hi = h.astype(BF16)
    h_lo = (h - h_hi.astype(F32)).astype(BF16)
    lg = (jnp.dot(h_hi, w_hi, preferred_element_type=F32)
          + jnp.dot(h_lo, w_hi, preferred_element_type=F32)
          + jnp.dot(h_hi, w_lo, preferred_element_type=F32)) + br_ref[...]

    lane = lax.broadcasted_iota(jnp.int32, (tm, LANES), 1).astype(F32)
    no_lane = float(LANES)
    gl = jnp.where(lane < N_GROUPS, lg, NEG_BIG)
    gmax = jnp.max(gl, axis=-1, keepdims=True)
    g_idx = jnp.min(jnp.where(gl == gmax, lane, no_lane), axis=-1, keepdims=True)
    g_p = 1.0 / jnp.sum(jnp.exp(gl - gmax), axis=-1, keepdims=True)
    lo_lane = N_GROUPS + g_idx * EXPERTS_PER_GROUP
    in_grp = jnp.logical_and(lane >= lo_lane, lane < lo_lane + EXPERTS_PER_GROUP)
    ev = jnp.where(in_grp, lg, NEG_BIG)
    v1 = jnp.max(ev, axis=-1, keepdims=True)
    i1 = jnp.min(jnp.where(ev == v1, lane, no_lane), axis=-1, keepdims=True)
    ev2 = jnp.where(lane == i1, NEG_BIG, ev)
    v2 = jnp.max(ev2, axis=-1, keepdims=True)
    i2 = jnp.min(jnp.where(ev2 == v2, lane, no_lane), axis=-1, keepdims=True)
    e2 = jnp.exp(v2 - v1)
    den = 1.0 + e2
    w1 = (1.0 / den) * g_p
    w2 = (e2 / den) * g_p
    id1 = i1 - N_GROUPS
    id2 = i2 - N_GROUPS

    oh1 = lane == id1
    oh2 = lane == id2
    both = jnp.where(jnp.logical_or(oh1, oh2), 1.0, 0.0)
    r = lax.broadcasted_iota(jnp.int32, (tm, tm), 0)
    c = lax.broadcasted_iota(jnp.int32, (tm, tm), 1)
    tril = jnp.where(c < r, 1.0, 0.0).astype(BF16)
    before = jnp.dot(tril, both.astype(BF16), preferred_element_type=F32) + carry_ref[...]
    rank1 = jnp.sum(jnp.where(oh1, before, 0.0), axis=-1, keepdims=True)
    rank2 = jnp.sum(jnp.where(oh2, before, 0.0), axis=-1, keepdims=True)
    carry_ref[...] = carry_ref[...] + jnp.sum(both, axis=0, keepdims=True)

    info = jnp.zeros((tm, LANES), F32)
    for col, val in enumerate((id1, id2, rank1, rank2, w1, w2)):
        info = jnp.where(lane == col, val, info)
    _store_row_tiles(h_ref, _pack_halves(h))
    info_ref[...] = info
    idx_all[:, pl.ds(pl.multiple_of(i * tm, tm), tm)] = jnp.transpose(info)[0:8]

    @pl.when(i == pl.num_programs(0) - 1)
    def _():
        plan = _dispatch_plan_tile(carry_ref[...])
        plan_ref[...] = jnp.transpose(plan)[0:8].astype(jnp.int32)
        first_row = plan[0:N_EXPERTS, PLAN_BSTART:PLAN_BSTART + 1] * float(EXPERT_BLOCK)
        chunk = 1024
        expert = lax.broadcasted_iota(jnp.int32, (N_EXPERTS, chunk), 0).astype(F32)
        for c0 in range(0, idx_all.shape[1], chunk):
            cols = slice(c0, c0 + chunk)
            rows = []
            for k in range(2):
                ids = idx_all[k:k + 1, cols]
                base = jnp.sum(jnp.where(expert == ids, first_row, 0.0), axis=0, keepdims=True)
                rows.append(base + idx_all[2 + k:3 + k, cols])
            slots_ref[:, cols] = jnp.concatenate(rows, axis=0).astype(jnp.int32)


def _router(x2, g, mod, seq, wr, br, tm=256):
    assert x2.shape[1] == 2 * ROW_TILE * LANES
    t, d = x2.shape
    per = seq // tm
    return pl.pallas_call(
        _router_kernel,
        grid=(t // tm,),
        in_specs=[
            pl.BlockSpec((tm, d), lambda i: (i, 0)),
            pl.BlockSpec((1, d), lambda i: (0, 0)),
            pl.BlockSpec((None, 1, d), lambda i: ((i // per) * 6 + 3, 0, 0)),
            pl.BlockSpec((None, 1, d), lambda i: ((i // per) * 6 + 4, 0, 0)),
            pl.BlockSpec((d, LANES), lambda i: (0, 0)),
            pl.BlockSpec((1, LANES), lambda i: (0, 0)),
        ],
        out_specs=(
            pl.BlockSpec((tm * ROW_TILE, LANES), lambda i: (i, 0)),
            pl.BlockSpec((tm, LANES), lambda i: (i, 0)),
            pl.BlockSpec((2, t), lambda i: (0, 0)),
            pl.BlockSpec((8, PLAN_COLS), lambda i: (0, 0)),
        ),
        out_shape=(
            jax.ShapeDtypeStruct((t * ROW_TILE, LANES), jnp.uint32),
            jax.ShapeDtypeStruct((t, LANES), F32),
            jax.ShapeDtypeStruct((2, t), jnp.int32),
            jax.ShapeDtypeStruct((8, PLAN_COLS), jnp.int32),
        ),
        scratch_shapes=[pltpu.VMEM((1, LANES), F32), pltpu.VMEM((8, t), F32)],
        compiler_params=_cparams(("arbitrary",)),
        name="moe_router",
    )(x2, g.reshape(1, d), mod, mod, wr, br)


GATHER_GROUP = 8


def _plan(plan, row, col):
    return plan[row * PLAN_COLS + col]


def _expert_kernel(plan, slots, h_hbm, wg_hbm, wu_hbm, wd_hbm, ys_ref,
                   row_tok, xbuf, xsem, sg, su, sd, wsem, wgb, wub, wdb, *, layer, n_tok):
    i = pl.program_id(0)
    nu = _plan(plan, PLAN_NUSED, 0)
    n_rows = row_tok.shape[0]

    def weight_copies(e, st):
        return (pltpu.make_async_copy(wg_hbm.at[layer, e], sg.at[st], wsem.at[st, 0]),
                pltpu.make_async_copy(wu_hbm.at[layer, e], su.at[st], wsem.at[st, 1]),
                pltpu.make_async_copy(wd_hbm.at[layer, e], sd.at[st], wsem.at[st, 2]))

    def n_groups(blk):
        return (_plan(plan, PLAN_NVALID, blk) + GATHER_GROUP - 1) // GATHER_GROUP

    def start_gather(blk, slot):
        base = blk * EXPERT_BLOCK

        def body(g, carry):
            for k in range(GATHER_GROUP):
                r = g * GATHER_GROUP + k
                tok = row_tok[base + r]
                src = pl.multiple_of(tok * ROW_TILE, ROW_TILE)
                dst = pl.multiple_of(r * ROW_TILE, ROW_TILE)
                pltpu.make_async_copy(h_hbm.at[pl.ds(src, ROW_TILE), :],
                                      xbuf.at[slot, pl.ds(dst, ROW_TILE), :], xsem.at[slot]).start()
            return carry

        lax.fori_loop(0, n_groups(blk), body, 0)

    def wait_gather(blk, slot):
        span = GATHER_GROUP * ROW_TILE

        def body(g, carry):
            pltpu.make_async_copy(h_hbm.at[pl.ds(0, span), :],
                                  xbuf.at[slot, pl.ds(0, span), :], xsem.at[slot]).wait()
            return carry

        lax.fori_loop(0, n_groups(blk), body, 0)

    @pl.when(i == 0)
    def _():
        for cp in weight_copies(_plan(plan, PLAN_EXP, 0), 0):
            cp.start(priority=1)
        second = _plan(plan, PLAN_NEXT, 0)

        @pl.when(second >= 0)
        def _():
            for cp in weight_copies(second, 1):
                cp.start(priority=1)

        xbuf[...] = jnp.zeros(xbuf.shape, xbuf.dtype)

        def pad_body(e, carry):
            end = _plan(plan, PLAN_BSTART, e) * EXPERT_BLOCK + _plan(plan, PLAN_COUNT, e)
            for k in range(GATHER_GROUP - 1):
                row_tok[jnp.minimum(end + k, n_rows - 1)] = 0
            return carry

        lax.fori_loop(0, N_EXPERTS, pad_body, 0)

        def fill_body(t, carry):
            row_tok[slots[0, t]] = t
            row_tok[slots[1, t]] = t
            return carry

        lax.fori_loop(0, n_tok, fill_body, 0, unroll=8)
        start_gather(0, 0)

    @pl.when(i < nu)
    def _():
        slot = i % 2

        @pl.when(i + 1 < nu)
        def _():
            start_gather(i + 1, 1 - slot)

        first = _plan(plan, PLAN_FIRST, i)

        @pl.when(first > 0)
        def _():
            st = (first - 1) % 2
            nxt2 = _plan(plan, PLAN_NEXT2, i)
            cps = weight_copies(_plan(plan, PLAN_EXP, i), st)
            nxt_cps = weight_copies(jnp.maximum(nxt2, 0), st)
            for cp, ncp, stage, dst in zip(cps, nxt_cps, (sg, su, sd), (wgb, wub, wdb)):
                cp.wait()
                dst[...] = stage[st].astype(BF16)

                @pl.when(nxt2 >= 0)
                def _():
                    ncp.start(priority=1)

        wait_gather(i, slot)
        parts = [_unpack_halves(w) for w in _load_row_tiles(xbuf.at[slot], EXPERT_BLOCK)]
        x_lo = jnp.concatenate([lo.astype(BF16) for lo, _ in parts], axis=1)
        x_hi = jnp.concatenate([hi.astype(BF16) for _, hi in parts], axis=1)
        half = wgb.shape[0] // 2
        g = (jnp.dot(x_lo, wgb[:half], preferred_element_type=F32)
             + jnp.dot(x_hi, wgb[half:], preferred_element_type=F32))
        u = (jnp.dot(x_lo, wub[:half], preferred_element_type=F32)
             + jnp.dot(x_hi, wub[half:], preferred_element_type=F32))
        hid = (_silu(g) * u).astype(BF16)
        _store_row_tiles(ys_ref, _pack_halves(jnp.dot(hid, wdb[...], preferred_element_type=F32)))

    @pl.when(i >= nu)
    def _():
        ys_ref[...] = jnp.zeros(ys_ref.shape, ys_ref.dtype)


def _expert_ffn(h2, w_gate, w_up, w_down, layer, plan, slots, n_blocks):
    t = h2.shape[0] // ROW_TILE
    d, f = w_gate.shape[2], w_gate.shape[3]
    n_rows = n_blocks * EXPERT_BLOCK
    grid_spec = pltpu.PrefetchScalarGridSpec(
        num_scalar_prefetch=2,
        grid=(n_blocks,),
        in_specs=[pl.BlockSpec(memory_space=pl.ANY)] * 4,
        out_specs=pl.BlockSpec((EXPERT_BLOCK * ROW_TILE, LANES), lambda i, *_: (i, 0)),
        scratch_shapes=[
            pltpu.SMEM((n_rows,), jnp.int32),
            pltpu.VMEM((2, EXPERT_BLOCK * ROW_TILE, LANES), jnp.uint32),
            pltpu.SemaphoreType.DMA((2,)),
            pltpu.VMEM((2, d, f), F32),
            pltpu.VMEM((2, d, f), F32),
            pltpu.VMEM((2, f, d), F32),
            pltpu.SemaphoreType.DMA((2, 3)),
            pltpu.VMEM((d, f), BF16),
            pltpu.VMEM((d, f), BF16),
            pltpu.VMEM((f, d), BF16),
        ],
    )
    weight_bytes = 3 * d * f * (2 * 4 + 2)
    vmem = weight_bytes + 8 * 1024 * 1024
    return pl.pallas_call(
        functools.partial(_expert_kernel, layer=layer, n_tok=t),
        grid_spec=grid_spec,
        out_shape=jax.ShapeDtypeStruct((n_rows * ROW_TILE, LANES), jnp.uint32),
        compiler_params=_cparams(("arbitrary",), vmem),
        name="moe_experts",
    )(plan, slots, h2, w_gate, w_up, w_down)


def _combine_kernel(slots, x_ref, info_ref, gate_ref, ys_hbm, *rest, tm, with_next):
    if with_next:
        ng_ref, nsh_ref, nsc_ref, o_ref, h_ref, buf, sem = rest
    else:
        o_ref, buf, sem = rest
    i = pl.program_id(0)
    last = pl.num_programs(0) - 1

    def row_copy(slot, k, src_row, r):
        src = pl.multiple_of(src_row * ROW_TILE, ROW_TILE)
        dst = pl.multiple_of(r * ROW_TILE, ROW_TILE)
        return pltpu.make_async_copy(ys_hbm.at[pl.ds(src, ROW_TILE), :],
                                     buf.at[slot, k, pl.ds(dst, ROW_TILE), :], sem.at[slot])

    def wait_rows(slot):
        for k in range(2):
            pltpu.make_async_copy(ys_hbm.at[pl.ds(0, tm * ROW_TILE), :], buf.at[slot, k], sem.at[slot]).wait()

    def start_gather(step, slot):
        base = step * tm

        def body(r, carry):
            for k in range(2):
                row_copy(slot, k, slots[k, base + r], r).start(priority=k)
            return carry

        lax.fori_loop(0, tm, body, 0, unroll=8)

    @pl.when(i == 0)
    def _():
        start_gather(0, 0)

    slot = i % 2

    @pl.when(i < last)
    def _():
        start_gather(i + 1, 1 - slot)

    wait_rows(slot)
    info = info_ref[...]
    w0 = info[:, 4:5]
    w1 = info[:, 5:6]
    half = x_ref.shape[1] // 2
    a_tiles = _load_row_tiles(buf.at[slot, 0], tm)
    b_tiles = _load_row_tiles(buf.at[slot, 1], tm)
    for s in range(ROW_TILE):
        a_lo, a_hi = _unpack_halves(a_tiles[s])
        b_lo, b_hi = _unpack_halves(b_tiles[s])
        lo = slice(s * LANES, (s + 1) * LANES)
        hi = slice(half + s * LANES, half + (s + 1) * LANES)
        o_ref[:, lo] = x_ref[:, lo] + gate_ref[:, lo] * (w0 * a_lo + w1 * b_lo)
        o_ref[:, hi] = x_ref[:, hi] + gate_ref[:, hi] * (w0 * a_hi + w1 * b_hi)
    if with_next:
        h_ref[...] = _norm_mod(o_ref[...], ng_ref[...], nsh_ref[...], nsc_ref[...]).astype(h_ref.dtype)


def _combine(x2, info, mod, seq, ys, slots, next_norm=None, tm=256):
    t, d = x2.shape
    per = seq // tm
    row = lambda i, *_: (i, 0)
    mod_row = lambda k: (lambda i, *_: ((i // per) * 6 + k, 0, 0))
    in_specs = [
        pl.BlockSpec((tm, d), row),
        pl.BlockSpec((tm, LANES), row),
        pl.BlockSpec((None, 1, d), mod_row(5)),
        pl.BlockSpec(memory_space=pl.ANY),
    ]
    args = [slots, x2, info, mod, ys]
    out_specs = pl.BlockSpec((tm, d), row)
    out_shape = jax.ShapeDtypeStruct((t, d), F32)
    if next_norm is not None:
        next_g, next_mod = next_norm
        in_specs += [pl.BlockSpec((1, d), lambda i, *_: (0, 0)),
                     pl.BlockSpec((None, 1, d), mod_row(0)), pl.BlockSpec((None, 1, d), mod_row(1))]
        args += [next_g.reshape(1, d), next_mod, next_mod]
        out_specs = (out_specs, pl.BlockSpec((tm, d), row))
        out_shape = (out_shape, jax.ShapeDtypeStruct((t, d), BF16))
    grid_spec = pltpu.PrefetchScalarGridSpec(
        num_scalar_prefetch=1,
        grid=(t // tm,),
        in_specs=in_specs,
        out_specs=out_specs,
        scratch_shapes=[
            pltpu.VMEM((2, 2, tm * ROW_TILE, LANES), jnp.uint32),
            pltpu.SemaphoreType.DMA((2,)),
        ],
    )
    return pl.pallas_call(
        functools.partial(_combine_kernel, tm=tm, with_next=next_norm is not None),
        grid_spec=grid_spec,
        out_shape=out_shape,
        compiler_params=_cparams(("arbitrary",)),
        name="moe_combine",
    )(*args)


def _hier_moe(x2, norm_g, mod, seq, w_rg, b_rg, w_re, b_re, w_gate, w_up, w_down, layer, next_norm=None):
    t, d = x2.shape
    pad = LANES - N_GROUPS - N_EXPERTS
    wr = jnp.concatenate([w_rg, w_re, jnp.zeros((d, pad), F32)], axis=1)
    br = jnp.concatenate([b_rg, b_re, jnp.zeros((pad,), F32)]).reshape(1, LANES)
    h2, info, slots, plan = _router(x2, norm_g, mod, seq, wr, br)
    n_assign = 2 * t
    n_blocks = (n_assign + N_EXPERTS * (EXPERT_BLOCK - 1) + EXPERT_BLOCK - 1) // EXPERT_BLOCK
    assert n_blocks <= PLAN_COLS
    plan = plan.reshape(-1)
    ys = _expert_ffn(h2, w_gate, w_up, w_down, layer, plan, slots, n_blocks)
    return _combine(x2, info, mod, seq, ys, slots, next_norm)


def kernel(x, c, ada_w, ada_b, norm1_g, norm2_g, dsa_w_in, dsa_q_gain, dsa_k_gain, dsa_w_out, mla_w_in, mla_cq_gain, mla_ckv_gain, mla_w_q_up, mla_w_kv_up, mla_q_gain, mla_k_gain, mla_w_out, router_group_w, router_group_b, router_expert_w, router_expert_b, expert_w_gate, expert_w_up, expert_w_down):
    b, s, d = x.shape
    t = b * s
    mods = _ada_mod(c, ada_w, ada_b)
    x2 = x.reshape(t, d)

    mod = mods[0]
    h = _normmod(x2, norm1_g[0], mod, s, 0, 1)
    gains = (dsa_q_gain[0], dsa_k_gain[0], dsa_k_gain[0])
    qkv = [[_dsa_proj(h, dsa_w_in[0], gains[w], g, w, b) for w in range(3)] for g in range(len(DIL_PAIRS))]
    og2 = _dsa_attn(*qkv[2], 2)
    og1 = _dsa_attn(*qkv[1], 1)
    o = _dsa_attn(*qkv[0], 0, others=(og1, og2))
    x2 = _resid_mm(o.reshape(t, DSA_WIDTH), dsa_w_out[0], x2, mod, s, 2)
    x2, h = _hier_moe(x2, norm2_g[0], mod, s, router_group_w[0], router_group_b[0],
                      router_expert_w[0], router_expert_b[0],
                      expert_w_gate, expert_w_up, expert_w_down, 0, next_norm=(norm1_g[1], mods[1]))

    mod = mods[1]
    cq, ckv, kpe = _mla_in(h, mla_w_in, mla_cq_gain[0], mla_ckv_gain[0])
    tab = _rope_tables(s)
    q = _mla_q(cq, mla_w_q_up, mla_q_gain[0], tab, s)
    k, v = _mla_kv(ckv, mla_w_kv_up[0], kpe, mla_k_gain[0], tab, s)
    o = _mla_attn(q.reshape(b, s, -1), k.reshape(b, s, -1), v.reshape(b, s, -1))
    x2 = _resid_mm(o.reshape(t, MLA_HEADS * MLA_V), mla_w_out[0], x2, mod, s, 2)
    x2 = _hier_moe(x2, norm2_g[1], mod, s, router_group_w[1], router_group_b[1],
                   router_expert_w[1], router_expert_b[1],
                   expert_w_gate, expert_w_up, expert_w_down, 1)
    return x2.reshape(b, s, d)
```

```python
import functools
import math

import jax
import jax.numpy as jnp
import numpy as np
from jax import lax
from jax.experimental import pallas as pl
from jax.experimental.pallas import tpu as pltpu

F32 = jnp.float32
BF16 = jnp.bfloat16

D_MODEL = 2048
EPS = 1e-6
LANES = 128
NEG_BIG = -1e30
LOG2_E = math.log2(math.e)
LN_2 = math.log(2.0)

DIL_PAIRS = ((128, 1), (512, 4), (2048, 16))
DSA_HEADS = 8
DSA_HEAD_DIM = 128
DSA_WIDTH = DSA_HEADS * DSA_HEAD_DIM
BAND = 128

MLA_HEADS = 16
MLA_Q_LORA = 512
MLA_KV_LORA = 512
MLA_NOPE = 128
MLA_ROPE = 64
MLA_V = 128
MLA_QK = MLA_NOPE + MLA_ROPE
MLA_QK_PAD = 256
ROPE_THETA = 10000.0

N_GROUPS = 4
EXPERTS_PER_GROUP = 16
N_EXPERTS = N_GROUPS * EXPERTS_PER_GROUP
D_EXPERT = 768
EXPERT_BLOCK = 128

VMEM_LIMIT = 48 * 1024 * 1024


def _cparams(sem, vmem=VMEM_LIMIT):
    return pltpu.CompilerParams(dimension_semantics=sem, vmem_limit_bytes=vmem)


def _silu(x):
    return x * (1.0 / (1.0 + jnp.exp(-x)))


def _pack_halves(x):
    n = x.shape[1] // 2
    xb = x.astype(BF16).astype(F32)
    lo = lax.bitcast_convert_type(xb[:, :n], jnp.uint32) >> 16
    hi = lax.bitcast_convert_type(xb[:, n:], jnp.uint32) & jnp.uint32(0xFFFF0000)
    return hi | lo


def _unpack_halves(w):
    lo = lax.bitcast_convert_type(w << 16, F32)
    hi = lax.bitcast_convert_type(w & jnp.uint32(0xFFFF0000), F32)
    return lo, hi


ROW_TILE = 8


def _store_row_tiles(ref, words):
    rows = words.shape[0]
    for s in range(ROW_TILE):
        ref[pl.ds(s, rows, stride=ROW_TILE), :] = words[:, s * LANES:(s + 1) * LANES]


def _load_row_tiles(ref, rows):
    return [ref[pl.ds(s, rows, stride=ROW_TILE), :] for s in range(ROW_TILE)]


def _norm_mod(x, g, shift, scale):
    ms = jnp.mean(x * x, axis=-1, keepdims=True)
    y = x * lax.rsqrt(ms + EPS) * g
    return y * (1.0 + scale) + shift


def _ada_kernel(c_ref, w_ref, b_ref, o_ref):
    ca = _silu(c_ref[...])
    hi = ca.astype(BF16)
    lo = (ca - hi.astype(F32)).astype(BF16)
    lhs = jnp.concatenate([hi, lo], axis=0)
    res = jnp.dot(lhs, w_ref[...].astype(BF16), preferred_element_type=F32)
    o_ref[...] = res[:8] + res[8:] + b_ref[...]


def _ada_mod(c, ada_w, ada_b):
    depth, d, n = ada_w.shape
    b = c.shape[0]
    c8 = jnp.pad(c, ((0, 8 - b), (0, 0)))
    tn = 1024
    out = pl.pallas_call(
        _ada_kernel,
        grid=(depth, n // tn),
        in_specs=[
            pl.BlockSpec((8, d), lambda i, j: (0, 0)),
            pl.BlockSpec((None, d, tn), lambda i, j: (i, 0, j)),
            pl.BlockSpec((None, 1, tn), lambda i, j: (i, 0, j)),
        ],
        out_specs=pl.BlockSpec((None, 8, tn), lambda i, j: (i, 0, j)),
        out_shape=jax.ShapeDtypeStruct((depth, 8, n), F32),
        compiler_params=_cparams(("arbitrary", "arbitrary")),
        name="ada_mod",
    )(c8, ada_w, ada_b.reshape(depth, 1, n))
    return out[:, :b].reshape(depth, b * 6, 1, d)


def _normmod_kernel(x_ref, g_ref, sh_ref, sc_ref, o_ref):
    o_ref[...] = _norm_mod(x_ref[...], g_ref[...], sh_ref[...], sc_ref[...]).astype(o_ref.dtype)


def _normmod(x2, g, mod, seq, k_shift, k_scale, tm=1024):
    t, d = x2.shape
    per = seq // tm
    return pl.pallas_call(
        _normmod_kernel,
        grid=(t // tm,),
        in_specs=[
            pl.BlockSpec((tm, d), lambda i: (i, 0)),
            pl.BlockSpec((1, d), lambda i: (0, 0)),
            pl.BlockSpec((None, 1, d), lambda i: ((i // per) * 6 + k_shift, 0, 0)),
            pl.BlockSpec((None, 1, d), lambda i: ((i // per) * 6 + k_scale, 0, 0)),
        ],
        out_specs=pl.BlockSpec((tm, d), lambda i: (i, 0)),
        out_shape=jax.ShapeDtypeStruct((t, d), BF16),
        compiler_params=_cparams(("arbitrary",)),
        name="normmod",
    )(x2, g.reshape(1, d), mod, mod)


def _cast_weight_once(w_ref, wb_ref):
    @pl.when(pl.program_id(1) == 0)
    def _():
        wb_ref[...] = w_ref[...].astype(BF16)


def _dsa_proj_kernel(a_ref, w_ref, g_ref, o_ref, wb_ref, res_ref, *, dilation, normed, gain_scale):
    @pl.when(pl.program_id(0) == 0)
    def _():
        wb_ref[...] = w_ref[...].astype(BF16)

    gain = g_ref[...] * gain_scale
    tm = a_ref.shape[0]
    pr = 512
    n_parts = tm // pr
    accs = [jnp.dot(a_ref[p * pr:(p + 1) * pr, :], wb_ref[...], preferred_element_type=F32)
            for p in range(n_parts)]
    for p, acc in enumerate(accs):
        blks = []
        for h in range(DSA_HEADS):
            blk = acc[:, h * LANES:(h + 1) * LANES]
            if normed:
                ms = jnp.mean(blk * blk, axis=-1, keepdims=True)
                blk = blk * lax.rsqrt(ms + EPS) * gain
            blks.append(blk)
        for h, blk in enumerate(blks):
            hs = slice(h * LANES, (h + 1) * LANES)
            if dilation == 1:
                o_ref[0, p * pr:(p + 1) * pr, hs] = blk.astype(o_ref.dtype)
            else:
                res_ref[h, p * pr:(p + 1) * pr, :] = blk
                sub_rows = pr // dilation
                for r in range(dilation):
                    sub = res_ref[h, pl.ds(p * pr + r, sub_rows, stride=dilation), :]
                    o_ref[r, p * sub_rows:(p + 1) * sub_rows, hs] = sub.astype(o_ref.dtype)


def _dsa_proj(h, w_in, gain, group, which, batch, tm=1024, tn=DSA_WIDTH):
    t, d = h.shape
    dil = DIL_PAIRS[group][1]
    seq = t // batch
    per = seq // tm
    gain_scale = LOG2_E / math.sqrt(DSA_HEAD_DIM) if which == 0 else 1.0
    return pl.pallas_call(
        functools.partial(_dsa_proj_kernel, dilation=dil, normed=which != 2, gain_scale=gain_scale),
        grid=(t // tm,),
        in_specs=[
            pl.BlockSpec((tm, d), lambda i: (i, 0)),
            pl.BlockSpec((d, tn), lambda i: (0, group * 3 + which), pipeline_mode=pl.Buffered(1)),
            pl.BlockSpec((1, LANES), lambda i: (0, 0)),
        ],
        out_specs=pl.BlockSpec((None, dil, tm // dil, tn), lambda i: (i // per, 0, i % per, 0)),
        out_shape=jax.ShapeDtypeStruct((batch, dil, seq // dil, tn), BF16),
        scratch_shapes=[pltpu.VMEM((d, tn), BF16), pltpu.VMEM((DSA_HEADS, tm, LANES), F32)],
        compiler_params=_cparams(("arbitrary",)),
        name=f"dsa_proj_g{group}_{'qkv'[which]}",
    )(h, w_in, gain.reshape(1, LANES))


def _resid_mm_kernel(a_ref, w_ref, x_ref, gate_ref, o_ref, wb_ref):
    _cast_weight_once(w_ref, wb_ref)
    y = jnp.dot(a_ref[...], wb_ref[...], preferred_element_type=F32)
    o_ref[...] = x_ref[...] + gate_ref[...] * y


def _resid_mm(a, w, x2, mod, seq, k_gate, tn=1024):
    t, k = a.shape
    n = w.shape[1]
    tm = 1024 if k <= 1024 else 512
    per = seq // tm
    return pl.pallas_call(
        _resid_mm_kernel,
        grid=(n // tn, t // tm),
        in_specs=[
            pl.BlockSpec((tm, k), lambda j, i: (i, 0)),
            pl.BlockSpec((k, tn), lambda j, i: (0, j)),
            pl.BlockSpec((tm, tn), lambda j, i: (i, j)),
            pl.BlockSpec((None, 1, tn), lambda j, i: ((i // per) * 6 + k_gate, 0, j)),
        ],
        out_specs=pl.BlockSpec((tm, tn), lambda j, i: (i, j)),
        out_shape=jax.ShapeDtypeStruct((t, n), F32),
        scratch_shapes=[pltpu.VMEM((k, tn), BF16)],
        compiler_params=_cparams(("arbitrary", "arbitrary")),
        name="resid_mm",
    )(a, w, x2, mod)


def _alibi_slope(head_slot, group):
    n = len(DIL_PAIRS) * DSA_HEADS
    return 2.0 ** (-8.0 * (head_slot * len(DIL_PAIRS) + group + 1.0) / n)


def _dsa_bias_table(group, has_prev):
    window, d = DIL_PAIRS[group]
    steps = window // d
    qi = np.arange(BAND)[:, None]
    if has_prev:
        kj = np.arange(2 * BAND)[None, :]
        delta = qi + BAND - kj
        prev_key = np.broadcast_to(kj < BAND, delta.shape)
    else:
        kj = np.arange(BAND)[None, :]
        delta = qi - kj
        prev_key = np.zeros(delta.shape, bool)
    inside = (delta >= 0) & (delta <= steps)
    tabs = []
    for first in (True, False):
        valid = inside & ~(prev_key & first)
        per_head = [np.where(valid, -_alibi_slope(h, group) * LOG2_E * (delta * d), NEG_BIG)
                    for h in range(DSA_HEADS)]
        tabs.append(np.stack(per_head))
    return jnp.asarray(np.stack(tabs), F32)


def _dsa_attn_kernel(*refs, dilation, has_prev, merge):
    refs = list(refs)
    q_ref, kc_ref, vc_ref, bias_ref = refs[:4]
    pos = 4
    if has_prev:
        kp_ref, vp_ref = refs[pos:pos + 2]
        pos += 2
    if merge:
        other = refs[pos:pos + 4]
        pos += 4
        o_ref = refs[pos]
    else:
        o_ref, lse_ref = refs[pos:pos + 2]

    r = pl.program_id(2)
    lane = lax.broadcasted_iota(jnp.int32, (BAND, LANES), 1)
    nt = (((1,), (1,)), ((), ()))
    nk = 2 * BAND if has_prev else BAND
    ones = jnp.ones((nk, LANES), BF16)
    n_batch = q_ref.shape[0]

    def head_scores(bi, h):
        hs = slice(h * LANES, (h + 1) * LANES)
        q = q_ref[bi, :, hs]
        if has_prev:
            k = jnp.concatenate([kp_ref[bi, :, hs], kc_ref[bi, :, hs]], axis=0)
        else:
            k = kc_ref[bi, :, hs]
        return lax.dot_general(q, k, nt, preferred_element_type=F32) + bias_ref[h]

    def head_values(bi, h):
        hs = slice(h * LANES, (h + 1) * LANES)
        if has_prev:
            v = jnp.concatenate([vp_ref[bi, :, hs], vc_ref[bi, :, hs]], axis=0)
        else:
            v = vc_ref[bi, :, hs]
        return jnp.concatenate([v, ones], axis=1)

    results = []
    for bi in range(n_batch):
        scores = [head_scores(bi, h) for h in range(DSA_HEADS)]
        maxes = [jnp.max(s, axis=-1, keepdims=True) for s in scores]
        probs = [jnp.exp2(s - m).astype(BF16) for s, m in zip(scores, maxes)]
        accs = [jnp.dot(p, head_values(bi, h), preferred_element_type=F32) for h, p in enumerate(probs)]
        outs = []
        lse_tile = jnp.zeros((BAND, LANES), F32)
        for h in range(DSA_HEADS):
            l = accs[h][:, LANES:]
            o = accs[h][:, :LANES] / l
            lse = maxes[h] * LN_2 + jnp.log(l)
            if merge:
                lses = [lse] + [other[2 * g + 1][bi, :, h * 16:h * 16 + 1] for g in range(2)]
                parts = [o] + [other[2 * g][bi, h] for g in range(2)]
                top = jnp.maximum(jnp.maximum(lses[0], lses[1]), lses[2])
                es = [jnp.exp(x - top) for x in lses]
                den = es[0] + es[1] + es[2]
                o = ((es[0] * parts[0] + es[1] * parts[1] + es[2] * parts[2]) / den).astype(o_ref.dtype)
            else:
                in_head = jnp.logical_and(lane >= h * 16, lane < (h + 1) * 16)
                lse_tile = jnp.where(in_head, lse, lse_tile)
            outs.append(o)
        results.append((outs, lse_tile))

    for bi, (outs, lse_tile) in enumerate(results):
        for h, o in enumerate(outs):
            if merge:
                o_ref[bi, :, h * LANES:(h + 1) * LANES] = o
            else:
                o_ref[bi, h, pl.ds(r, BAND, stride=dilation), :] = o
        if not merge:
            lse_ref[bi, pl.ds(r, BAND, stride=dilation), :] = lse_tile


def _dsa_attn(q_g, k_g, v_g, group, others=None):
    b, d, sub_len, _ = q_g.shape
    s = d * sub_len
    nb = sub_len // BAND
    has_prev = nb > 1
    merge = others is not None
    assert not merge or d == 1
    bias = _dsa_bias_table(group, has_prev)

    span = BAND * d
    bb = 4 if span <= 4 * BAND else 2
    while b % bb:
        bb //= 2
    cur = lambda bi, n, r: (bi, r, n, 0)
    prev = lambda bi, n, r: (bi, r, jnp.maximum(n - 1, 0), 0)
    blk = (bb, None, BAND, DSA_WIDTH)
    in_specs = [pl.BlockSpec(blk, cur), pl.BlockSpec(blk, cur), pl.BlockSpec(blk, cur),
                pl.BlockSpec((None,) + bias.shape[1:], lambda bi, n, r: (jnp.minimum(n, 1), 0, 0, 0))]
    args = [q_g, k_g, v_g, bias]
    if has_prev:
        in_specs += [pl.BlockSpec(blk, prev), pl.BlockSpec(blk, prev)]
        args += [k_g, v_g]
    nat = lambda bi, n, r: (bi, n, 0)
    nat_heads = lambda bi, n, r: (bi, 0, n, 0)
    if merge:
        for o_g, lse_g in others:
            in_specs += [pl.BlockSpec((bb, DSA_HEADS, BAND, LANES), nat_heads),
                         pl.BlockSpec((bb, BAND, LANES), nat)]
            args += [o_g, lse_g]
        out_shape = jax.ShapeDtypeStruct((b, s, DSA_WIDTH), BF16)
        out_specs = pl.BlockSpec((bb, BAND, DSA_WIDTH), nat)
    else:
        out_shape = (jax.ShapeDtypeStruct((b, DSA_HEADS, s, LANES), F32), jax.ShapeDtypeStruct((b, s, LANES), F32))
        out_specs = (pl.BlockSpec((bb, DSA_HEADS, span, LANES), nat_heads),
                     pl.BlockSpec((bb, span, LANES), nat))
    return pl.pallas_call(
        functools.partial(_dsa_attn_kernel, dilation=d, has_prev=has_prev, merge=merge),
        grid=(b // bb, nb, d),
        in_specs=in_specs,
        out_specs=out_specs,
        out_shape=out_shape,
        compiler_params=_cparams(("arbitrary", "arbitrary", "arbitrary")),
        name=f"dsa_attn_g{group}",
    )(*args)


def _rope_tables(seq):
    half = MLA_ROPE // 2
    inv = np.float32(ROPE_THETA) ** (-np.arange(half, dtype=np.float32) / np.float32(half))
    ang = np.arange(seq, dtype=np.float32)[:, None] * inv[None, :]
    cos, sin = np.cos(ang), np.sin(ang)
    z = np.zeros((seq, LANES - MLA_ROPE), np.float32)
    zh = np.zeros((seq, half), np.float32)
    cos_t = np.concatenate([cos, cos, z], axis=1)
    sin_a = np.concatenate([zh, sin, z], axis=1)
    sin_b = np.concatenate([-sin, zh, z], axis=1)
    return jnp.asarray(np.concatenate([cos_t, sin_a, sin_b], axis=1), F32)


def _rope_lanes(x, tab):
    half = MLA_ROPE // 2
    cos_t = tab[:, 0:LANES]
    sin_a = tab[:, LANES:2 * LANES]
    sin_b = tab[:, 2 * LANES:3 * LANES]
    return (x * cos_t + pltpu.roll(x, half, 1) * sin_a
            + pltpu.roll(x, LANES - half, 1) * sin_b)


def _mla_in_kernel(a_ref, w_ref, cqg_ref, ckvg_ref, cq_ref, ckv_ref, kpe_ref, wb_ref):
    n = w_ref.shape[1]

    @pl.when(pl.program_id(0) == 0)
    def _():
        wb_ref[...] = jnp.zeros(wb_ref.shape, BF16)
        wb_ref[:, :n] = w_ref[...].astype(BF16)

    acc = jnp.dot(a_ref[...], wb_ref[...], preferred_element_type=F32)
    cq = acc[:, :MLA_Q_LORA]
    cq_ref[...] = (cq * lax.rsqrt(jnp.mean(cq * cq, axis=-1, keepdims=True) + EPS)
                   * cqg_ref[...]).astype(BF16)
    ckv = acc[:, MLA_Q_LORA:MLA_Q_LORA + MLA_KV_LORA]
    ckv_ref[...] = (ckv * lax.rsqrt(jnp.mean(ckv * ckv, axis=-1, keepdims=True) + EPS)
                    * ckvg_ref[...]).astype(BF16)
    kpe_ref[...] = acc[:, MLA_Q_LORA + MLA_KV_LORA:]


def _mla_in(h, w_in, cq_gain, ckv_gain, tm=1024):
    t, d = h.shape
    n = w_in.shape[2]
    n_pad = MLA_Q_LORA + MLA_KV_LORA + LANES
    return pl.pallas_call(
        _mla_in_kernel,
        grid=(t // tm,),
        in_specs=[
            pl.BlockSpec((tm, d), lambda i: (i, 0)),
            pl.BlockSpec((None, d, n), lambda i: (0, 0, 0)),
            pl.BlockSpec((1, MLA_Q_LORA), lambda i: (0, 0)),
            pl.BlockSpec((1, MLA_KV_LORA), lambda i: (0, 0)),
        ],
        out_specs=(
            pl.BlockSpec((tm, MLA_Q_LORA), lambda i: (i, 0)),
            pl.BlockSpec((tm, MLA_KV_LORA), lambda i: (i, 0)),
            pl.BlockSpec((tm, LANES), lambda i: (i, 0)),
        ),
        out_shape=(
            jax.ShapeDtypeStruct((t, MLA_Q_LORA), BF16),
            jax.ShapeDtypeStruct((t, MLA_KV_LORA), BF16),
            jax.ShapeDtypeStruct((t, LANES), F32),
        ),
        scratch_shapes=[pltpu.VMEM((d, n_pad), BF16)],
        compiler_params=_cparams(("arbitrary",)),
        name="mla_in",
    )(h, w_in, cq_gain.reshape(1, -1), ckv_gain.reshape(1, -1))


def _mla_q_kernel(a_ref, w_ref, g0_ref, g1_ref, tab_ref, o_ref, wb_ref):
    @pl.when(pl.program_id(0) == 0)
    def _():
        for h in range(MLA_HEADS):
            src = h * MLA_QK
            dst = h * MLA_QK_PAD
            rope_cols = w_ref[:, src + MLA_NOPE:src + MLA_QK].astype(BF16)
            wb_ref[:, dst:dst + MLA_NOPE] = w_ref[:, src:src + MLA_NOPE].astype(BF16)
            wb_ref[:, dst + MLA_NOPE:dst + MLA_QK] = rope_cols
            wb_ref[:, dst + MLA_QK:dst + MLA_QK_PAD] = rope_cols

    acc = jnp.dot(a_ref[...], wb_ref[...], preferred_element_type=F32)
    tab = tab_ref[...]
    cos_t = tab[:, 0:LANES]
    sin_t = tab[:, LANES:2 * LANES] + tab[:, 2 * LANES:3 * LANES]
    scale = math.log2(math.e) / math.sqrt(MLA_QK)
    g0 = g0_ref[...] * scale
    g1 = g1_ref[...] * scale
    outs = []
    for h in range(MLA_HEADS):
        base = h * MLA_QK_PAD
        x0 = acc[:, base:base + LANES]
        x1 = acc[:, base + LANES:base + 2 * LANES]
        ss = jnp.sum(x0 * x0 + 0.5 * (x1 * x1), axis=-1, keepdims=True)
        rs = lax.rsqrt(ss / MLA_QK + EPS)
        x1n = x1 * rs * g1
        q_rot = x1n * cos_t + pltpu.roll(x1n, MLA_ROPE // 2, 1) * sin_t
        outs.append(((x0 * rs * g0).astype(BF16), q_rot.astype(BF16)))
    for h, (q_nope, q_rope) in enumerate(outs):
        base = h * MLA_QK_PAD
        o_ref[:, base:base + LANES] = q_nope
        o_ref[:, base + LANES:base + 2 * LANES] = q_rope


def _mla_q(cq, w_q_up, q_gain, tab, seq, tm=512):
    t, k = cq.shape
    n = MLA_HEADS * MLA_QK_PAD
    per = seq // tm
    g0 = q_gain[:MLA_NOPE].reshape(1, LANES)
    g1 = jnp.concatenate([q_gain[MLA_NOPE:], q_gain[MLA_NOPE:]]).reshape(1, LANES)
    return pl.pallas_call(
        _mla_q_kernel,
        grid=(t // tm,),
        in_specs=[
            pl.BlockSpec((tm, k), lambda i: (i, 0)),
            pl.BlockSpec((None, k, MLA_HEADS * MLA_QK), lambda i: (0, 0, 0)),
            pl.BlockSpec((1, LANES), lambda i: (0, 0)),
            pl.BlockSpec((1, LANES), lambda i: (0, 0)),
            pl.BlockSpec((tm, 3 * LANES), lambda i: (i % per, 0)),
        ],
        out_specs=pl.BlockSpec((tm, n), lambda i: (i, 0)),
        out_shape=jax.ShapeDtypeStruct((t, n), BF16),
        scratch_shapes=[pltpu.VMEM((k, n), BF16)],
        compiler_params=_cparams(("arbitrary",)),
        name="mla_q_up",
    )(cq, w_q_up, g0, g1, tab)


def _mla_kv_kernel(a_ref, w_ref, kpe_ref, g0_ref, g1_ref, tab_ref, k_ref, v_ref, wb_ref):
    @pl.when(pl.program_id(0) == 0)
    def _():
        wb_ref[...] = w_ref[...].astype(BF16)

    acc = jnp.dot(a_ref[...], wb_ref[...], preferred_element_type=F32)
    tab = tab_ref[...]
    kpe = kpe_ref[...]
    ss_pe = jnp.sum(kpe * kpe, axis=-1, keepdims=True)
    g0 = g0_ref[...]
    kpe_rot = _rope_lanes(kpe * g1_ref[...], tab)
    outs = []
    for h in range(MLA_HEADS):
        base = h * (MLA_NOPE + MLA_V)
        kn = acc[:, base:base + MLA_NOPE]
        ss = jnp.sum(kn * kn, axis=-1, keepdims=True) + ss_pe
        rs = lax.rsqrt(ss / MLA_QK + EPS)
        outs.append(((kn * rs * g0).astype(BF16), (kpe_rot * rs).astype(BF16),
                     acc[:, base + MLA_NOPE:base + MLA_NOPE + MLA_V].astype(BF16)))
    ones = jnp.ones((acc.shape[0], MLA_V), BF16)
    for h, (k_nope, k_rope, v) in enumerate(outs):
        kb = h * MLA_QK_PAD
        k_ref[:, kb:kb + LANES] = k_nope
        k_ref[:, kb + LANES:kb + 2 * LANES] = k_rope
        vb = h * 2 * MLA_V
        v_ref[:, vb:vb + MLA_V] = v
        v_ref[:, vb + MLA_V:vb + 2 * MLA_V] = ones


def _mla_kv(ckv, w_kv_up, kpe, k_gain, tab, seq, tm=512):
    t, k = ckv.shape
    n = w_kv_up.shape[1]
    per = seq // tm
    g0 = k_gain[:MLA_NOPE].reshape(1, LANES)
    g1 = jnp.pad(k_gain[MLA_NOPE:], (0, LANES - MLA_ROPE)).reshape(1, LANES)
    return pl.pallas_call(
        _mla_kv_kernel,
        grid=(t // tm,),
        in_specs=[
            pl.BlockSpec((tm, k), lambda i: (i, 0)),
            pl.BlockSpec((k, n), lambda i: (0, 0)),
            pl.BlockSpec((tm, LANES), lambda i: (i, 0)),
            pl.BlockSpec((1, LANES), lambda i: (0, 0)),
            pl.BlockSpec((1, LANES), lambda i: (0, 0)),
            pl.BlockSpec((tm, 3 * LANES), lambda i: (i % per, 0)),
        ],
        out_specs=(
            pl.BlockSpec((tm, MLA_HEADS * MLA_QK_PAD), lambda i: (i, 0)),
            pl.BlockSpec((tm, MLA_HEADS * 2 * MLA_V), lambda i: (i, 0)),
        ),
        out_shape=(
            jax.ShapeDtypeStruct((t, MLA_HEADS * MLA_QK_PAD), BF16),
            jax.ShapeDtypeStruct((t, MLA_HEADS * 2 * MLA_V), BF16),
        ),
        scratch_shapes=[pltpu.VMEM((k, n), BF16)],
        compiler_params=_cparams(("arbitrary",)),
        name="mla_kv_up",
    )(ckv, w_kv_up, kpe, g0, g1, tab)


def _mla_attn_kernel(q_ref, k_ref, v_ref, o_ref, *, tq, heads):
    seq = q_ref.shape[0]
    nt = (((1,), (1,)), ((), ()))
    r = lax.broadcasted_iota(jnp.int32, (tq, tq), 0)
    c = lax.broadcasted_iota(jnp.int32, (tq, tq), 1)
    causal = c <= r
    for qi in range(seq // tq):
        rows = slice(qi * tq, (qi + 1) * tq)
        outs = []
        for hh in range(heads):
            qk_cols = slice(hh * MLA_QK_PAD, (hh + 1) * MLA_QK_PAD)
            v_cols = slice(hh * 2 * MLA_V, (hh + 1) * 2 * MLA_V)
            q = q_ref[rows, qk_cols]
            scores = []
            for j in range(qi + 1):
                s = lax.dot_general(q, k_ref[j * tq:(j + 1) * tq, qk_cols], nt, preferred_element_type=F32)
                if j == qi:
                    s = jnp.where(causal, s, NEG_BIG)
                scores.append(s)
            top = scores[0]
            for s in scores[1:]:
                top = jnp.maximum(top, s)
            m = jnp.max(top, axis=-1, keepdims=True)
            acc = None
            for j, s in enumerate(scores):
                p = jnp.exp2(s - m).astype(BF16)
                pv = jnp.dot(p, v_ref[j * tq:(j + 1) * tq, v_cols], preferred_element_type=F32)
                acc = pv if acc is None else acc + pv
            outs.append((acc[:, :MLA_V] / acc[:, MLA_V:]).astype(o_ref.dtype))
        for hh in range(heads):
            o_ref[rows, hh * MLA_V:(hh + 1) * MLA_V] = outs[hh]


def _mla_attn(q, k, v, tq=256, heads=4):
    b, s, _ = q.shape
    return pl.pallas_call(
        functools.partial(_mla_attn_kernel, tq=tq, heads=heads),
        grid=(b, MLA_HEADS // heads),
        in_specs=[
            pl.BlockSpec((None, s, heads * MLA_QK_PAD), lambda bi, h: (bi, 0, h)),
            pl.BlockSpec((None, s, heads * MLA_QK_PAD), lambda bi, h: (bi, 0, h)),
            pl.BlockSpec((None, s, heads * 2 * MLA_V), lambda bi, h: (bi, 0, h)),
        ],
        out_specs=pl.BlockSpec((None, s, heads * MLA_V), lambda bi, h: (bi, 0, h)),
        out_shape=jax.ShapeDtypeStruct((b, s, MLA_HEADS * MLA_V), BF16),
        compiler_params=_cparams(("arbitrary", "arbitrary")),
        name="mla_attn",
    )(q, k, v)


PLAN_COLS = 256
(PLAN_EXP, PLAN_FIRST, PLAN_NEXT, PLAN_NUSED, PLAN_NVALID, PLAN_BSTART, PLAN_COUNT,
 PLAN_NEXT2) = range(8)


def _dispatch_plan_tile(cnt):
    nblk = jnp.floor((cnt + (EXPERT_BLOCK - 1.0)) * (1.0 / EXPERT_BLOCK))
    e_r = lax.broadcasted_iota(jnp.int32, (LANES, LANES), 0)
    e_c = lax.broadcasted_iota(jnp.int32, (LANES, LANES), 1)
    before = jnp.where(e_r < e_c, 1.0, 0.0).astype(BF16)
    used = jnp.where(nblk > 0.0, 1.0, 0.0)
    sub = lax.broadcasted_iota(jnp.int32, (8, LANES), 0)
    prefix = jnp.dot(jnp.where(sub == 0, nblk, used).astype(BF16), before, preferred_element_type=F32)
    bstart = prefix[0:1]
    ordinal = prefix[1:2]
    bend = bstart + nblk
    n_used = jnp.max(bend, axis=-1, keepdims=True)
    row = lax.broadcasted_iota(jnp.int32, (PLAN_COLS, LANES), 0).astype(F32)
    lane = lax.broadcasted_iota(jnp.int32, (PLAN_COLS, LANES), 1).astype(F32)
    is_exp = lane < N_EXPERTS
    row1 = row[:, 0:1]
    done = jnp.logical_and(bend <= row, is_exp)
    blk_exp = jnp.minimum(jnp.sum(jnp.where(done, 1.0, 0.0), axis=-1, keepdims=True), N_EXPERTS - 1.0)
    mine = lane == blk_exp
    bstart_of = jnp.sum(jnp.where(mine, bstart, 0.0), axis=-1, keepdims=True)
    cnt_of = jnp.sum(jnp.where(mine, cnt, 0.0), axis=-1, keepdims=True)
    ord_of = jnp.sum(jnp.where(mine, ordinal, 0.0), axis=-1, keepdims=True)
    valid = row1 < n_used
    first = jnp.where(jnp.logical_and(valid, row1 == bstart_of), ord_of + 1.0, 0.0)
    nvalid = jnp.clip(cnt_of - EXPERT_BLOCK * (row1 - bstart_of), 0.0, float(EXPERT_BLOCK))
    nvalid = jnp.where(valid, nvalid, 0.0)
    usable = jnp.logical_and(nblk > 0.0, is_exp)

    def next_used(after):
        nxt = jnp.min(jnp.where(jnp.logical_and(lane > after, usable), lane, 999.0), axis=-1, keepdims=True)
        return jnp.where(nxt > 998.0, -1.0, nxt)

    nxt = next_used(blk_exp)
    nxt2 = jnp.where(nxt < 0.0, -1.0, next_used(nxt))
    bstart_col = jnp.sum(jnp.where(lane < row, nblk, 0.0), axis=-1, keepdims=True)
    cnt_col = jnp.sum(jnp.where(lane == row, cnt, 0.0), axis=-1, keepdims=True)
    tile = jnp.zeros((PLAN_COLS, LANES), F32)
    cols = {PLAN_EXP: blk_exp, PLAN_FIRST: first, PLAN_NEXT: nxt, PLAN_NUSED: n_used,
            PLAN_NVALID: nvalid, PLAN_BSTART: bstart_col, PLAN_COUNT: cnt_col, PLAN_NEXT2: nxt2}
    for k, val in cols.items():
        tile = jnp.where(lane == k, val, tile)
    return tile


def _router_kernel(x_ref, g_ref, sh_ref, sc_ref, wr_ref, br_ref, h_ref, info_ref, slots_ref, plan_ref,
                   carry_ref, idx_all):
    i = pl.program_id(0)
    tm = x_ref.shape[0]

    @pl.when(i == 0)
    def _():
        carry_ref[...] = jnp.zeros(carry_ref.shape, F32)

    h = _norm_mod(x_ref[...], g_ref[...], sh_ref[...], sc_ref[...])
    w = wr_ref[...]
    w_hi = w.astype(BF16)
    w_lo = (w - w_hi.astype(F32)).astype(BF16)
    h_hi = h.astype(BF16)
    h_lo = (h - h_hi.astype(F32)).astype(BF16)
    lg = (jnp.dot(h_hi, w_hi, preferred_element_type=F32)
          + jnp.dot(h_lo, w_hi, preferred_element_type=F32)
          + jnp.dot(h_hi, w_lo, preferred_element_type=F32)) + br_ref[...]

    lane = lax.broadcasted_iota(jnp.int32, (tm, LANES), 1).astype(F32)
    no_lane = float(LANES)
    gl = jnp.where(lane < N_GROUPS, lg, NEG_BIG)
    gmax = jnp.max(gl, axis=-1, keepdims=True)
    g_idx = jnp.min(jnp.where(gl == gmax, lane, no_lane), axis=-1, keepdims=True)
    g_p = 1.0 / jnp.sum(jnp.exp(gl - gmax), axis=-1, keepdims=True)
    lo_lane = N_GROUPS + g_idx * EXPERTS_PER_GROUP
    in_grp = jnp.logical_and(lane >= lo_lane, lane < lo_lane + EXPERTS_PER_GROUP)
    ev = jnp.where(in_grp, lg, NEG_BIG)
    v1 = jnp.max(ev, axis=-1, keepdims=True)
    i1 = jnp.min(jnp.where(ev == v1, lane, no_lane), axis=-1, keepdims=True)
    ev2 = jnp.where(lane == i1, NEG_BIG, ev)
    v2 = jnp.max(ev2, axis=-1, keepdims=True)
    i2 = jnp.min(jnp.where(ev2 == v2, lane, no_lane), axis=-1, keepdims=True)
    e2 = jnp.exp(v2 - v1)
    den = 1.0 + e2
    w1 = (1.0 / den) * g_p
    w2 = (e2 / den) * g_p
    id1 = i1 - N_GROUPS
    id2 = i2 - N_GROUPS

    oh1 = lane == id1
    oh2 = lane == id2
    both = jnp.where(jnp.logical_or(oh1, oh2), 1.0, 0.0)
    r = lax.broadcasted_iota(jnp.int32, (tm, tm), 0)
    c = lax.broadcasted_iota(jnp.int32, (tm, tm), 1)
    tril = jnp.where(c < r, 1.0, 0.0).astype(BF16)
    before = jnp.dot(tril, both.astype(BF16), preferred_element_type=F32) + carry_ref[...]
    rank1 = jnp.sum(jnp.where(oh1, before, 0.0), axis=-1, keepdims=True)
    rank2 = jnp.sum(jnp.where(oh2, before, 0.0), axis=-1, keepdims=True)
    carry_ref[...] = carry_ref[...] + jnp.sum(both, axis=0, keepdims=True)

    info = jnp.zeros((tm, LANES), F32)
    for col, val in enumerate((id1, id2, rank1, rank2, w1, w2)):
        info = jnp.where(lane == col, val, info)
    _store_row_tiles(h_ref, _pack_halves(h))
    info_ref[...] = info
    idx_all[:, pl.ds(pl.multiple_of(i * tm, tm), tm)] = jnp.transpose(info)[0:8]

    @pl.when(i == pl.num_programs(0) - 1)
    def _():
        plan = _dispatch_plan_tile(carry_ref[...])
        plan_ref[...] = jnp.transpose(plan)[0:8].astype(jnp.int32)
        first_row = plan[0:N_EXPERTS, PLAN_BSTART:PLAN_BSTART + 1] * float(EXPERT_BLOCK)
        chunk = 1024
        expert = lax.broadcasted_iota(jnp.int32, (N_EXPERTS, chunk), 0).astype(F32)
        for c0 in range(0, idx_all.shape[1], chunk):
            cols = slice(c0, c0 + chunk)
            rows = []
            for k in range(2):
                ids = idx_all[k:k + 1, cols]
                base = jnp.sum(jnp.where(expert == ids, first_row, 0.0), axis=0, keepdims=True)
                rows.append(base + idx_all[2 + k:3 + k, cols])
            slots_ref[:, cols] = jnp.concatenate(rows, axis=0).astype(jnp.int32)


def _router(x2, g, mod, seq, wr, br, tm=256):
    assert x2.shape[1] == 2 * ROW_TILE * LANES
    t, d = x2.shape
    per = seq // tm
    return pl.pallas_call(
        _router_kernel,
        grid=(t // tm,),
        in_specs=[
            pl.BlockSpec((tm, d), lambda i: (i, 0)),
            pl.BlockSpec((1, d), lambda i: (0, 0)),
            pl.BlockSpec((None, 1, d), lambda i: ((i // per) * 6 + 3, 0, 0)),
            pl.BlockSpec((None, 1, d), lambda i: ((i // per) * 6 + 4, 0, 0)),
            pl.BlockSpec((d, LANES), lambda i: (0, 0)),
            pl.BlockSpec((1, LANES), lambda i: (0, 0)),
        ],
        out_specs=(
            pl.BlockSpec((tm * ROW_TILE, LANES), lambda i: (i, 0)),
            pl.BlockSpec((tm, LANES), lambda i: (i, 0)),
            pl.BlockSpec((2, t), lambda i: (0, 0)),
            pl.BlockSpec((8, PLAN_COLS), lambda i: (0, 0)),
        ),
        out_shape=(
            jax.ShapeDtypeStruct((t * ROW_TILE, LANES), jnp.uint32),
            jax.ShapeDtypeStruct((t, LANES), F32),
            jax.ShapeDtypeStruct((2, t), jnp.int32),
            jax.ShapeDtypeStruct((8, PLAN_COLS), jnp.int32),
        ),
        scratch_shapes=[pltpu.VMEM((1, LANES), F32), pltpu.VMEM((8, t), F32)],
        compiler_params=_cparams(("arbitrary",)),
        name="moe_router",
    )(x2, g.reshape(1, d), mod, mod, wr, br)


GATHER_GROUP = 8


def _plan(plan, row, col):
    return plan[row * PLAN_COLS + col]


def _expert_kernel(plan, slots, h_hbm, wg_hbm, wu_hbm, wd_hbm, ys_ref,
                   row_tok, xbuf, xsem, sg, su, sd, wsem, wgb, wub, wdb, *, layer, n_tok):
    i = pl.program_id(0)
    nu = _plan(plan, PLAN_NUSED, 0)
    n_rows = row_tok.shape[0]

    def weight_copies(e, st):
        return (pltpu.make_async_copy(wg_hbm.at[layer, e], sg.at[st], wsem.at[st, 0]),
                pltpu.make_async_copy(wu_hbm.at[layer, e], su.at[st], wsem.at[st, 1]),
                pltpu.make_async_copy(wd_hbm.at[layer, e], sd.at[st], wsem.at[st, 2]))

    def n_groups(blk):
        return (_plan(plan, PLAN_NVALID, blk) + GATHER_GROUP - 1) // GATHER_GROUP

    def start_gather(blk, slot):
        base = blk * EXPERT_BLOCK

        def body(g, carry):
            for k in range(GATHER_GROUP):
                r = g * GATHER_GROUP + k
                tok = row_tok[base + r]
                src = pl.multiple_of(tok * ROW_TILE, ROW_TILE)
                dst = pl.multiple_of(r * ROW_TILE, ROW_TILE)
                pltpu.make_async_copy(h_hbm.at[pl.ds(src, ROW_TILE), :],
                                      xbuf.at[slot, pl.ds(dst, ROW_TILE), :], xsem.at[slot]).start()
            return carry

        lax.fori_loop(0, n_groups(blk), body, 0)

    def wait_gather(blk, slot):
        span = GATHER_GROUP * ROW_TILE

        def body(g, carry):
            pltpu.make_async_copy(h_hbm.at[pl.ds(0, span), :],
                                  xbuf.at[slot, pl.ds(0, span), :], xsem.at[slot]).wait()
            return carry

        lax.fori_loop(0, n_groups(blk), body, 0)

    @pl.when(i == 0)
    def _():
        for cp in weight_copies(_plan(plan, PLAN_EXP, 0), 0):
            cp.start(priority=1)
        second = _plan(plan, PLAN_NEXT, 0)

        @pl.when(second >= 0)
        def _():
            for cp in weight_copies(second, 1):
                cp.start(priority=1)

        xbuf[...] = jnp.zeros(xbuf.shape, xbuf.dtype)

        def pad_body(e, carry):
            end = _plan(plan, PLAN_BSTART, e) * EXPERT_BLOCK + _plan(plan, PLAN_COUNT, e)
            for k in range(GATHER_GROUP - 1):
                row_tok[jnp.minimum(end + k, n_rows - 1)] = 0
            return carry

        lax.fori_loop(0, N_EXPERTS, pad_body, 0)

        def fill_body(t, carry):
            row_tok[slots[0, t]] = t
            row_tok[slots[1, t]] = t
            return carry

        lax.fori_loop(0, n_tok, fill_body, 0, unroll=8)
        start_gather(0, 0)

    @pl.when(i < nu)
    def _():
        slot = i % 2

        @pl.when(i + 1 < nu)
        def _():
            start_gather(i + 1, 1 - slot)

        first = _plan(plan, PLAN_FIRST, i)

        @pl.when(first > 0)
        def _():
            st = (first - 1) % 2
            nxt2 = _plan(plan, PLAN_NEXT2, i)
            cps = weight_copies(_plan(plan, PLAN_EXP, i), st)
            nxt_cps = weight_copies(jnp.maximum(nxt2, 0), st)
            for cp, ncp, stage, dst in zip(cps, nxt_cps, (sg, su, sd), (wgb, wub, wdb)):
                cp.wait()
                dst[...] = stage[st].astype(BF16)

                @pl.when(nxt2 >= 0)
                def _():
                    ncp.start(priority=1)

        wait_gather(i, slot)
        parts = [_unpack_halves(w) for w in _load_row_tiles(xbuf.at[slot], EXPERT_BLOCK)]
        x_lo = jnp.concatenate([lo.astype(BF16) for lo, _ in parts], axis=1)
        x_hi = jnp.concatenate([hi.astype(BF16) for _, hi in parts], axis=1)
        half = wgb.shape[0] // 2
        g = (jnp.dot(x_lo, wgb[:half], preferred_element_type=F32)
             + jnp.dot(x_hi, wgb[half:], preferred_element_type=F32))
        u = (jnp.dot(x_lo, wub[:half], preferred_element_type=F32)
             + jnp.dot(x_hi, wub[half:], preferred_element_type=F32))
        hid = (_silu(g) * u).astype(BF16)
        _store_row_tiles(ys_ref, _pack_halves(jnp.dot(hid, wdb[...], preferred_element_type=F32)))

    @pl.when(i >= nu)
    def _():
        ys_ref[...] = jnp.zeros(ys_ref.shape, ys_ref.dtype)


def _expert_ffn(h2, w_gate, w_up, w_down, layer, plan, slots, n_blocks):
    t = h2.shape[0] // ROW_TILE
    d, f = w_gate.shape[2], w_gate.shape[3]
    n_rows = n_blocks * EXPERT_BLOCK
    grid_spec = pltpu.PrefetchScalarGridSpec(
        num_scalar_prefetch=2,
        grid=(n_blocks,),
        in_specs=[pl.BlockSpec(memory_space=pl.ANY)] * 4,
        out_specs=pl.BlockSpec((EXPERT_BLOCK * ROW_TILE, LANES), lambda i, *_: (i, 0)),
        scratch_shapes=[
            pltpu.SMEM((n_rows,), jnp.int32),
            pltpu.VMEM((2, EXPERT_BLOCK * ROW_TILE, LANES), jnp.uint32),
            pltpu.SemaphoreType.DMA((2,)),
            pltpu.VMEM((2, d, f), F32),
            pltpu.VMEM((2, d, f), F32),
            pltpu.VMEM((2, f, d), F32),
            pltpu.SemaphoreType.DMA((2, 3)),
            pltpu.VMEM((d, f), BF16),
            pltpu.VMEM((d, f), BF16),
            pltpu.VMEM((f, d), BF16),
        ],
    )
    weight_bytes = 3 * d * f * (2 * 4 + 2)
    vmem = weight_bytes + 8 * 1024 * 1024
    return pl.pallas_call(
        functools.partial(_expert_kernel, layer=layer, n_tok=t),
        grid_spec=grid_spec,
        out_shape=jax.ShapeDtypeStruct((n_rows * ROW_TILE, LANES), jnp.uint32),
        compiler_params=_cparams(("arbitrary",), vmem),
        name="moe_experts",
    )(plan, slots, h2, w_gate, w_up, w_down)


def _combine_kernel(slots, x_ref, info_ref, gate_ref, ys_hbm, *rest, tm, with_next):
    if with_next:
        ng_ref, nsh_ref, nsc_ref, o_ref, h_ref, buf, sem = rest
    else:
        o_ref, buf, sem = rest
    i = pl.program_id(0)
    last = pl.num_programs(0) - 1

    def row_copy(slot, k, src_row, r):
        src = pl.multiple_of(src_row * ROW_TILE, ROW_TILE)
        dst = pl.multiple_of(r * ROW_TILE, ROW_TILE)
        return pltpu.make_async_copy(ys_hbm.at[pl.ds(src, ROW_TILE), :],
                                     buf.at[slot, k, pl.ds(dst, ROW_TILE), :], sem.at[slot])

    def wait_rows(slot):
        for k in range(2):
            pltpu.make_async_copy(ys_hbm.at[pl.ds(0, tm * ROW_TILE), :], buf.at[slot, k], sem.at[slot]).wait()

    def start_gather(step, slot):
        base = step * tm

        def body(r, carry):
            for k in range(2):
                row_copy(slot, k, slots[k, base + r], r).start(priority=k)
            return carry

        lax.fori_loop(0, tm, body, 0, unroll=8)

    @pl.when(i == 0)
    def _():
        start_gather(0, 0)

    slot = i % 2

    @pl.when(i < last)
    def _():
        start_gather(i + 1, 1 - slot)

    wait_rows(slot)
    info = info_ref[...]
    w0 = info[:, 4:5]
    w1 = info[:, 5:6]
    half = x_ref.shape[1] // 2
    a_tiles = _load_row_tiles(buf.at[slot, 0], tm)
    b_tiles = _load_row_tiles(buf.at[slot, 1], tm)
    for s in range(ROW_TILE):
        a_lo, a_hi = _unpack_halves(a_tiles[s])
        b_lo, b_hi = _unpack_halves(b_tiles[s])
        lo = slice(s * LANES, (s + 1) * LANES)
        hi = slice(half + s * LANES, half + (s + 1) * LANES)
        o_ref[:, lo] = x_ref[:, lo] + gate_ref[:, lo] * (w0 * a_lo + w1 * b_lo)
        o_ref[:, hi] = x_ref[:, hi] + gate_ref[:, hi] * (w0 * a_hi + w1 * b_hi)
    if with_next:
        h_ref[...] = _norm_mod(o_ref[...], ng_ref[...], nsh_ref[...], nsc_ref[...]).astype(h_ref.dtype)


def _combine(x2, info, mod, seq, ys, slots, next_norm=None, tm=512):
    t, d = x2.shape
    per = seq // tm
    row = lambda i, *_: (i, 0)
    mod_row = lambda k: (lambda i, *_: ((i // per) * 6 + k, 0, 0))
    in_specs = [
        pl.BlockSpec((tm, d), row),
        pl.BlockSpec((tm, LANES), row),
        pl.BlockSpec((None, 1, d), mod_row(5)),
        pl.BlockSpec(memory_space=pl.ANY),
    ]
    args = [slots, x2, info, mod, ys]
    out_specs = pl.BlockSpec((tm, d), row)
    out_shape = jax.ShapeDtypeStruct((t, d), F32)
    if next_norm is not None:
        next_g, next_mod = next_norm
        in_specs += [pl.BlockSpec((1, d), lambda i, *_: (0, 0)),
                     pl.BlockSpec((None, 1, d), mod_row(0)), pl.BlockSpec((None, 1, d), mod_row(1))]
        args += [next_g.reshape(1, d), next_mod, next_mod]
        out_specs = (out_specs, pl.BlockSpec((tm, d), row))
        out_shape = (out_shape, jax.ShapeDtypeStruct((t, d), BF16))
    grid_spec = pltpu.PrefetchScalarGridSpec(
        num_scalar_prefetch=1,
        grid=(t // tm,),
        in_specs=in_specs,
        out_specs=out_specs,
        scratch_shapes=[
            pltpu.VMEM((2, 2, tm * ROW_TILE, LANES), jnp.uint32),
            pltpu.SemaphoreType.DMA((2,)),
        ],
    )
    return pl.pallas_call(
        functools.partial(_combine_kernel, tm=tm, with_next=next_norm is not None),
        grid_spec=grid_spec,
        out_shape=out_shape,
        compiler_params=_cparams(("arbitrary",)),
        name="moe_combine",
    )(*args)


def _hier_moe(x2, norm_g, mod, seq, w_rg, b_rg, w_re, b_re, w_gate, w_up, w_down, layer, next_norm=None):
    t, d = x2.shape
    pad = LANES - N_GROUPS - N_EXPERTS
    wr = jnp.concatenate([w_rg, w_re, jnp.zeros((d, pad), F32)], axis=1)
    br = jnp.concatenate([b_rg, b_re, jnp.zeros((pad,), F32)]).reshape(1, LANES)
    h2, info, slots, plan = _router(x2, norm_g, mod, seq, wr, br)
    n_assign = 2 * t
    n_blocks = (n_assign + N_EXPERTS * (EXPERT_BLOCK - 1) + EXPERT_BLOCK - 1) // EXPERT_BLOCK
    assert n_blocks <= PLAN_COLS
    plan = plan.reshape(-1)
    ys = _expert_ffn(h2, w_gate, w_up, w_down, layer, plan, slots, n_blocks)
    return _combine(x2, info, mod, seq, ys, slots, next_norm)


def kernel(x, c, ada_w, ada_b, norm1_g, norm2_g, dsa_w_in, dsa_q_gain, dsa_k_gain, dsa_w_out, mla_w_in, mla_cq_gain, mla_ckv_gain, mla_w_q_up, mla_w_kv_up, mla_q_gain, mla_k_gain, mla_w_out, router_group_w, router_group_b, router_expert_w, router_expert_b, expert_w_gate, expert_w_up, expert_w_down):
    b, s, d = x.shape
    t = b * s
    mods = _ada_mod(c, ada_w, ada_b)
    x2 = x.reshape(t, d)

    mod = mods[0]
    h = _normmod(x2, norm1_g[0], mod, s, 0, 1)
    gains = (dsa_q_gain[0], dsa_k_gain[0], dsa_k_gain[0])
    qkv = [[_dsa_proj(h, dsa_w_in[0], gains[w], g, w, b) for w in range(3)] for g in range(len(DIL_PAIRS))]
    og2 = _dsa_attn(*qkv[2], 2)
    og1 = _dsa_attn(*qkv[1], 1)
    o = _dsa_attn(*qkv[0], 0, others=(og1, og2))
    x2 = _resid_mm(o.reshape(t, DSA_WIDTH), dsa_w_out[0], x2, mod, s, 2)
    x2, h = _hier_moe(x2, norm2_g[0], mod, s, router_group_w[0], router_group_b[0],
                      router_expert_w[0], router_expert_b[0],
                      expert_w_gate, expert_w_up, expert_w_down, 0, next_norm=(norm1_g[1], mods[1]))

    mod = mods[1]
    cq, ckv, kpe = _mla_in(h, mla_w_in, mla_cq_gain[0], mla_ckv_gain[0])
    tab = _rope_tables(s)
    q = _mla_q(cq, mla_w_q_up, mla_q_gain[0], tab, s)
    k, v = _mla_kv(ckv, mla_w_kv_up[0], kpe, mla_k_gain[0], tab, s)
    o = _mla_attn(q.reshape(b, s, -1), k.reshape(b, s, -1), v.reshape(b, s, -1))
    x2 = _resid_mm(o.reshape(t, MLA_HEADS * MLA_V), mla_w_out[0], x2, mod, s, 2)
    x2 = _hier_moe(x2, norm2_g[1], mod, s, router_group_w[1], router_group_b[1],
                   router_expert_w[1], router_expert_b[1],
                   expert_w_gate, expert_w_up, expert_w_down, 1)
    return x2.reshape(b, s, d)
```

```python
import functools
import math

import jax
import jax.numpy as jnp
import numpy as np
from jax import lax
from jax.experimental import pallas as pl
from jax.experimental.pallas import tpu as pltpu

F32 = jnp.float32
BF16 = jnp.bfloat16

D_MODEL = 2048
EPS = 1e-6
LANES = 128
NEG_BIG = -1e30
LOG2_E = math.log2(math.e)
LN_2 = math.log(2.0)

DIL_PAIRS = ((128, 1), (512, 4), (2048, 16))
DSA_HEADS = 8
DSA_HEAD_DIM = 128
DSA_WIDTH = DSA_HEADS * DSA_HEAD_DIM
BAND = 128

MLA_HEADS = 16
MLA_Q_LORA = 512
MLA_KV_LORA = 512
MLA_NOPE = 128
MLA_ROPE = 64
MLA_V = 128
MLA_QK = MLA_NOPE + MLA_ROPE
MLA_QK_PAD = 256
ROPE_THETA = 10000.0

N_GROUPS = 4
EXPERTS_PER_GROUP = 16
N_EXPERTS = N_GROUPS * EXPERTS_PER_GROUP
D_EXPERT = 768
EXPERT_BLOCK = 128

VMEM_LIMIT = 48 * 1024 * 1024


def _cparams(sem, vmem=VMEM_LIMIT):
    return pltpu.CompilerParams(dimension_semantics=sem, vmem_limit_bytes=vmem)


def _silu(x):
    return x * (1.0 / (1.0 + jnp.exp(-x)))


def _pack_halves(x):
    n = x.shape[1] // 2
    xb = x.astype(BF16).astype(F32)
    lo = lax.bitcast_convert_type(xb[:, :n], jnp.uint32) >> 16
    hi = lax.bitcast_convert_type(xb[:, n:], jnp.uint32) & jnp.uint32(0xFFFF0000)
    return hi | lo


def _unpack_halves(w):
    lo = lax.bitcast_convert_type(w << 16, F32)
    hi = lax.bitcast_convert_type(w & jnp.uint32(0xFFFF0000), F32)
    return lo, hi


ROW_TILE = 8


def _store_row_tiles(ref, words):
    rows = words.shape[0]
    for s in range(ROW_TILE):
        ref[pl.ds(s, rows, stride=ROW_TILE), :] = words[:, s * LANES:(s + 1) * LANES]


def _load_row_tiles(ref, rows):
    return [ref[pl.ds(s, rows, stride=ROW_TILE), :] for s in range(ROW_TILE)]


def _norm_mod(x, g, shift, scale):
    ms = jnp.mean(x * x, axis=-1, keepdims=True)
    y = x * lax.rsqrt(ms + EPS) * g
    return y * (1.0 + scale) + shift


def _ada_kernel(c_ref, w_ref, b_ref, o_ref):
    ca = _silu(c_ref[...])
    hi = ca.astype(BF16)
    lo = (ca - hi.astype(F32)).astype(BF16)
    lhs = jnp.concatenate([hi, lo], axis=0)
    res = jnp.dot(lhs, w_ref[...].astype(BF16), preferred_element_type=F32)
    o_ref[...] = res[:8] + res[8:] + b_ref[...]


def _ada_mod(c, ada_w, ada_b):
    depth, d, n = ada_w.shape
    b = c.shape[0]
    c8 = jnp.pad(c, ((0, 8 - b), (0, 0)))
    tn = 1024
    out = pl.pallas_call(
        _ada_kernel,
        grid=(depth, n // tn),
        in_specs=[
            pl.BlockSpec((8, d), lambda i, j: (0, 0)),
            pl.BlockSpec((None, d, tn), lambda i, j: (i, 0, j)),
            pl.BlockSpec((None, 1, tn), lambda i, j: (i, 0, j)),
        ],
        out_specs=pl.BlockSpec((None, 8, tn), lambda i, j: (i, 0, j)),
        out_shape=jax.ShapeDtypeStruct((depth, 8, n), F32),
        compiler_params=_cparams(("arbitrary", "arbitrary")),
        name="ada_mod",
    )(c8, ada_w, ada_b.reshape(depth, 1, n))
    return out[:, :b].reshape(depth, b * 6, 1, d)


def _normmod_kernel(x_ref, g_ref, sh_ref, sc_ref, o_ref):
    o_ref[...] = _norm_mod(x_ref[...], g_ref[...], sh_ref[...], sc_ref[...]).astype(o_ref.dtype)


def _normmod(x2, g, mod, seq, k_shift, k_scale, tm=1024):
    t, d = x2.shape
    per = seq // tm
    return pl.pallas_call(
        _normmod_kernel,
        grid=(t // tm,),
        in_specs=[
            pl.BlockSpec((tm, d), lambda i: (i, 0)),
            pl.BlockSpec((1, d), lambda i: (0, 0)),
            pl.BlockSpec((None, 1, d), lambda i: ((i // per) * 6 + k_shift, 0, 0)),
            pl.BlockSpec((None, 1, d), lambda i: ((i // per) * 6 + k_scale, 0, 0)),
        ],
        out_specs=pl.BlockSpec((tm, d), lambda i: (i, 0)),
        out_shape=jax.ShapeDtypeStruct((t, d), BF16),
        compiler_params=_cparams(("arbitrary",)),
        name="normmod",
    )(x2, g.reshape(1, d), mod, mod)


def _cast_weight_once(w_ref, wb_ref):
    @pl.when(pl.program_id(1) == 0)
    def _():
        wb_ref[...] = w_ref[...].astype(BF16)


def _dsa_proj_kernel(a_ref, w_ref, g_ref, o_ref, wb_ref, res_ref, *, dilation, normed, gain_scale):
    @pl.when(pl.program_id(0) == 0)
    def _():
        wb_ref[...] = w_ref[...].astype(BF16)

    gain = g_ref[...] * gain_scale
    tm = a_ref.shape[0]
    pr = 512
    n_parts = tm // pr
    accs = [jnp.dot(a_ref[p * pr:(p + 1) * pr, :], wb_ref[...], preferred_element_type=F32)
            for p in range(n_parts)]
    for p, acc in enumerate(accs):
        blks = []
        for h in range(DSA_HEADS):
            blk = acc[:, h * LANES:(h + 1) * LANES]
            if normed:
                ms = jnp.mean(blk * blk, axis=-1, keepdims=True)
                blk = blk * lax.rsqrt(ms + EPS) * gain
            blks.append(blk)
        for h, blk in enumerate(blks):
            hs = slice(h * LANES, (h + 1) * LANES)
            if dilation == 1:
                o_ref[0, p * pr:(p + 1) * pr, hs] = blk.astype(o_ref.dtype)
            else:
                res_ref[h, p * pr:(p + 1) * pr, :] = blk
                sub_rows = pr // dilation
                for r in range(dilation):
                    sub = res_ref[h, pl.ds(p * pr + r, sub_rows, stride=dilation), :]
                    o_ref[r, p * sub_rows:(p + 1) * sub_rows, hs] = sub.astype(o_ref.dtype)


def _dsa_proj(h, w_in, gain, group, which, batch, tm=1024, tn=DSA_WIDTH):
    t, d = h.shape
    dil = DIL_PAIRS[group][1]
    seq = t // batch
    per = seq // tm
    gain_scale = LOG2_E / math.sqrt(DSA_HEAD_DIM) if which == 0 else 1.0
    return pl.pallas_call(
        functools.partial(_dsa_proj_kernel, dilation=dil, normed=which != 2, gain_scale=gain_scale),
        grid=(t // tm,),
        in_specs=[
            pl.BlockSpec((tm, d), lambda i: (i, 0)),
            pl.BlockSpec((d, tn), lambda i: (0, group * 3 + which), pipeline_mode=pl.Buffered(1)),
            pl.BlockSpec((1, LANES), lambda i: (0, 0)),
        ],
        out_specs=pl.BlockSpec((None, dil, tm // dil, tn), lambda i: (i // per, 0, i % per, 0)),
        out_shape=jax.ShapeDtypeStruct((batch, dil, seq // dil, tn), BF16),
        scratch_shapes=[pltpu.VMEM((d, tn), BF16), pltpu.VMEM((DSA_HEADS, tm, LANES), F32)],
        compiler_params=_cparams(("arbitrary",)),
        name=f"dsa_proj_g{group}_{'qkv'[which]}",
    )(h, w_in, gain.reshape(1, LANES))


def _resid_mm_kernel(a_ref, w_ref, x_ref, gate_ref, o_ref, wb_ref):
    _cast_weight_once(w_ref, wb_ref)
    y = jnp.dot(a_ref[...], wb_ref[...], preferred_element_type=F32)
    o_ref[...] = x_ref[...] + gate_ref[...] * y


def _resid_mm(a, w, x2, mod, seq, k_gate, tn=1024):
    t, k = a.shape
    n = w.shape[1]
    tm = 1024 if k <= 1024 else 512
    per = seq // tm
    return pl.pallas_call(
        _resid_mm_kernel,
        grid=(n // tn, t // tm),
        in_specs=[
            pl.BlockSpec((tm, k), lambda j, i: (i, 0)),
            pl.BlockSpec((k, tn), lambda j, i: (0, j)),
            pl.BlockSpec((tm, tn), lambda j, i: (i, j)),
            pl.BlockSpec((None, 1, tn), lambda j, i: ((i // per) * 6 + k_gate, 0, j)),
        ],
        out_specs=pl.BlockSpec((tm, tn), lambda j, i: (i, j)),
        out_shape=jax.ShapeDtypeStruct((t, n), F32),
        scratch_shapes=[pltpu.VMEM((k, tn), BF16)],
        compiler_params=_cparams(("arbitrary", "arbitrary")),
        name="resid_mm",
    )(a, w, x2, mod)


def _alibi_slope(head_slot, group):
    n = len(DIL_PAIRS) * DSA_HEADS
    return 2.0 ** (-8.0 * (head_slot * len(DIL_PAIRS) + group + 1.0) / n)


def _dsa_bias_table(group, has_prev):
    window, d = DIL_PAIRS[group]
    steps = window // d
    qi = np.arange(BAND)[:, None]
    if has_prev:
        kj = np.arange(2 * BAND)[None, :]
        delta = qi + BAND - kj
        prev_key = np.broadcast_to(kj < BAND, delta.shape)
    else:
        kj = np.arange(BAND)[None, :]
        delta = qi - kj
        prev_key = np.zeros(delta.shape, bool)
    inside = (delta >= 0) & (delta <= steps)
    tabs = []
    for first in (True, False):
        valid = inside & ~(prev_key & first)
        per_head = [np.where(valid, -_alibi_slope(h, group) * LOG2_E * (delta * d), NEG_BIG)
                    for h in range(DSA_HEADS)]
        tabs.append(np.stack(per_head))
    return jnp.asarray(np.stack(tabs), F32)


def _dsa_attn_kernel(*refs, dilation, has_prev, merge):
    refs = list(refs)
    q_ref, kc_ref, vc_ref, bias_ref = refs[:4]
    pos = 4
    if has_prev:
        kp_ref, vp_ref = refs[pos:pos + 2]
        pos += 2
    if merge:
        other = refs[pos:pos + 4]
        pos += 4
        o_ref = refs[pos]
    else:
        o_ref, lse_ref = refs[pos:pos + 2]

    r = pl.program_id(2)
    lane = lax.broadcasted_iota(jnp.int32, (BAND, LANES), 1)
    nt = (((1,), (1,)), ((), ()))
    nk = 2 * BAND if has_prev else BAND
    ones = jnp.ones((nk, LANES), BF16)
    n_batch = q_ref.shape[0]

    def head_scores(bi, h):
        hs = slice(h * LANES, (h + 1) * LANES)
        q = q_ref[bi, :, hs]
        if has_prev:
            k = jnp.concatenate([kp_ref[bi, :, hs], kc_ref[bi, :, hs]], axis=0)
        else:
            k = kc_ref[bi, :, hs]
        return lax.dot_general(q, k, nt, preferred_element_type=F32) + bias_ref[h]

    def head_values(bi, h):
        hs = slice(h * LANES, (h + 1) * LANES)
        if has_prev:
            v = jnp.concatenate([vp_ref[bi, :, hs], vc_ref[bi, :, hs]], axis=0)
        else:
            v = vc_ref[bi, :, hs]
        return jnp.concatenate([v, ones], axis=1)

    results = []
    for bi in range(n_batch):
        scores = [head_scores(bi, h) for h in range(DSA_HEADS)]
        maxes = [jnp.max(s, axis=-1, keepdims=True) for s in scores]
        probs = [jnp.exp2(s - m).astype(BF16) for s, m in zip(scores, maxes)]
        accs = [jnp.dot(p, head_values(bi, h), preferred_element_type=F32) for h, p in enumerate(probs)]
        outs = []
        lse_tile = jnp.zeros((BAND, LANES), F32)
        for h in range(DSA_HEADS):
            l = accs[h][:, LANES:]
            o = accs[h][:, :LANES] / l
            lse = maxes[h] * LN_2 + jnp.log(l)
            if merge:
                lses = [lse] + [other[2 * g + 1][bi, :, h * 16:h * 16 + 1] for g in range(2)]
                parts = [o] + [other[2 * g][bi, h] for g in range(2)]
                top = jnp.maximum(jnp.maximum(lses[0], lses[1]), lses[2])
                es = [jnp.exp(x - top) for x in lses]
                den = es[0] + es[1] + es[2]
                o = ((es[0] * parts[0] + es[1] * parts[1] + es[2] * parts[2]) / den).astype(o_ref.dtype)
            else:
                in_head = jnp.logical_and(lane >= h * 16, lane < (h + 1) * 16)
                lse_tile = jnp.where(in_head, lse, lse_tile)
            outs.append(o)
        results.append((outs, lse_tile))

    for bi, (outs, lse_tile) in enumerate(results):
        for h, o in enumerate(outs):
            if merge:
                o_ref[bi, :, h * LANES:(h + 1) * LANES] = o
            else:
                o_ref[bi, h, pl.ds(r, BAND, stride=dilation), :] = o
        if not merge:
            lse_ref[bi, pl.ds(r, BAND, stride=dilation), :] = lse_tile


def _dsa_attn(q_g, k_g, v_g, group, others=None):
    b, d, sub_len, _ = q_g.shape
    s = d * sub_len
    nb = sub_len // BAND
    has_prev = nb > 1
    merge = others is not None
    assert not merge or d == 1
    bias = _dsa_bias_table(group, has_prev)

    span = BAND * d
    bb = 4 if span <= 4 * BAND else 2
    while b % bb:
        bb //= 2
    cur = lambda bi, n, r: (bi, r, n, 0)
    prev = lambda bi, n, r: (bi, r, jnp.maximum(n - 1, 0), 0)
    blk = (bb, None, BAND, DSA_WIDTH)
    in_specs = [pl.BlockSpec(blk, cur), pl.BlockSpec(blk, cur), pl.BlockSpec(blk, cur),
                pl.BlockSpec((None,) + bias.shape[1:], lambda bi, n, r: (jnp.minimum(n, 1), 0, 0, 0))]
    args = [q_g, k_g, v_g, bias]
    if has_prev:
        in_specs += [pl.BlockSpec(blk, prev), pl.BlockSpec(blk, prev)]
        args += [k_g, v_g]
    nat = lambda bi, n, r: (bi, n, 0)
    nat_heads = lambda bi, n, r: (bi, 0, n, 0)
    if merge:
        for o_g, lse_g in others:
            in_specs += [pl.BlockSpec((bb, DSA_HEADS, BAND, LANES), nat_heads),
                         pl.BlockSpec((bb, BAND, LANES), nat)]
            args += [o_g, lse_g]
        out_shape = jax.ShapeDtypeStruct((b, s, DSA_WIDTH), BF16)
        out_specs = pl.BlockSpec((bb, BAND, DSA_WIDTH), nat)
    else:
        out_shape = (jax.ShapeDtypeStruct((b, DSA_HEADS, s, LANES), F32), jax.ShapeDtypeStruct((b, s, LANES), F32))
        out_specs = (pl.BlockSpec((bb, DSA_HEADS, span, LANES), nat_heads),
                     pl.BlockSpec((bb, span, LANES), nat))
    return pl.pallas_call(
        functools.partial(_dsa_attn_kernel, dilation=d, has_prev=has_prev, merge=merge),
        grid=(b // bb, nb, d),
        in_specs=in_specs,
        out_specs=out_specs,
        out_shape=out_shape,
        compiler_params=_cparams(("arbitrary", "arbitrary", "arbitrary")),
        name=f"dsa_attn_g{group}",
    )(*args)


def _rope_tables(seq):
    half = MLA_ROPE // 2
    inv = np.float32(ROPE_THETA) ** (-np.arange(half, dtype=np.float32) / np.float32(half))
    ang = np.arange(seq, dtype=np.float32)[:, None] * inv[None, :]
    cos, sin = np.cos(ang), np.sin(ang)
    z = np.zeros((seq, LANES - MLA_ROPE), np.float32)
    zh = np.zeros((seq, half), np.float32)
    cos_t = np.concatenate([cos, cos, z], axis=1)
    sin_a = np.concatenate([zh, sin, z], axis=1)
    sin_b = np.concatenate([-sin, zh, z], axis=1)
    return jnp.asarray(np.concatenate([cos_t, sin_a, sin_b], axis=1), F32)


def _rope_lanes(x, tab):
    half = MLA_ROPE // 2
    cos_t = tab[:, 0:LANES]
    sin_a = tab[:, LANES:2 * LANES]
    sin_b = tab[:, 2 * LANES:3 * LANES]
    return (x * cos_t + pltpu.roll(x, half, 1) * sin_a
            + pltpu.roll(x, LANES - half, 1) * sin_b)


def _mla_in_kernel(a_ref, w_ref, cqg_ref, ckvg_ref, cq_ref, ckv_ref, kpe_ref, wb_ref):
    n = w_ref.shape[1]

    @pl.when(pl.program_id(0) == 0)
    def _():
        wb_ref[...] = jnp.zeros(wb_ref.shape, BF16)
        wb_ref[:, :n] = w_ref[...].astype(BF16)

    acc = jnp.dot(a_ref[...], wb_ref[...], preferred_element_type=F32)
    cq = acc[:, :MLA_Q_LORA]
    cq_ref[...] = (cq * lax.rsqrt(jnp.mean(cq * cq, axis=-1, keepdims=True) + EPS)
                   * cqg_ref[...]).astype(BF16)
    ckv = acc[:, MLA_Q_LORA:MLA_Q_LORA + MLA_KV_LORA]
    ckv_ref[...] = (ckv * lax.rsqrt(jnp.mean(ckv * ckv, axis=-1, keepdims=True) + EPS)
                    * ckvg_ref[...]).astype(BF16)
    kpe_ref[...] = acc[:, MLA_Q_LORA + MLA_KV_LORA:]


def _mla_in(h, w_in, cq_gain, ckv_gain, tm=1024):
    t, d = h.shape
    n = w_in.shape[2]
    n_pad = MLA_Q_LORA + MLA_KV_LORA + LANES
    return pl.pallas_call(
        _mla_in_kernel,
        grid=(t // tm,),
        in_specs=[
            pl.BlockSpec((tm, d), lambda i: (i, 0)),
            pl.BlockSpec((None, d, n), lambda i: (0, 0, 0)),
            pl.BlockSpec((1, MLA_Q_LORA), lambda i: (0, 0)),
            pl.BlockSpec((1, MLA_KV_LORA), lambda i: (0, 0)),
        ],
        out_specs=(
            pl.BlockSpec((tm, MLA_Q_LORA), lambda i: (i, 0)),
            pl.BlockSpec((tm, MLA_KV_LORA), lambda i: (i, 0)),
            pl.BlockSpec((tm, LANES), lambda i: (i, 0)),
        ),
        out_shape=(
            jax.ShapeDtypeStruct((t, MLA_Q_LORA), BF16),
            jax.ShapeDtypeStruct((t, MLA_KV_LORA), BF16),
            jax.ShapeDtypeStruct((t, LANES), F32),
        ),
        scratch_shapes=[pltpu.VMEM((d, n_pad), BF16)],
        compiler_params=_cparams(("arbitrary",)),
        name="mla_in",
    )(h, w_in, cq_gain.reshape(1, -1), ckv_gain.reshape(1, -1))


def _mla_q_kernel(a_ref, w_ref, g0_ref, g1_ref, tab_ref, o_ref, wb_ref):
    @pl.when(pl.program_id(0) == 0)
    def _():
        for h in range(MLA_HEADS):
            src = h * MLA_QK
            dst = h * MLA_QK_PAD
            rope_cols = w_ref[:, src + MLA_NOPE:src + MLA_QK].astype(BF16)
            wb_ref[:, dst:dst + MLA_NOPE] = w_ref[:, src:src + MLA_NOPE].astype(BF16)
            wb_ref[:, dst + MLA_NOPE:dst + MLA_QK] = rope_cols
            wb_ref[:, dst + MLA_QK:dst + MLA_QK_PAD] = rope_cols

    acc = jnp.dot(a_ref[...], wb_ref[...], preferred_element_type=F32)
    tab = tab_ref[...]
    cos_t = tab[:, 0:LANES]
    sin_t = tab[:, LANES:2 * LANES] + tab[:, 2 * LANES:3 * LANES]
    scale = math.log2(math.e) / math.sqrt(MLA_QK)
    g0 = g0_ref[...] * scale
    g1 = g1_ref[...] * scale
    outs = []
    for h in range(MLA_HEADS):
        base = h * MLA_QK_PAD
        x0 = acc[:, base:base + LANES]
        x1 = acc[:, base + LANES:base + 2 * LANES]
        ss = jnp.sum(x0 * x0 + 0.5 * (x1 * x1), axis=-1, keepdims=True)
        rs = lax.rsqrt(ss / MLA_QK + EPS)
        x1n = x1 * rs * g1
        q_rot = x1n * cos_t + pltpu.roll(x1n, MLA_ROPE // 2, 1) * sin_t
        outs.append(((x0 * rs * g0).astype(BF16), q_rot.astype(BF16)))
    for h, (q_nope, q_rope) in enumerate(outs):
        base = h * MLA_QK_PAD
        o_ref[:, base:base + LANES] = q_nope
        o_ref[:, base + LANES:base + 2 * LANES] = q_rope


def _mla_q(cq, w_q_up, q_gain, tab, seq, tm=512):
    t, k = cq.shape
    n = MLA_HEADS * MLA_QK_PAD
    per = seq // tm
    g0 = q_gain[:MLA_NOPE].reshape(1, LANES)
    g1 = jnp.concatenate([q_gain[MLA_NOPE:], q_gain[MLA_NOPE:]]).reshape(1, LANES)
    return pl.pallas_call(
        _mla_q_kernel,
        grid=(t // tm,),
        in_specs=[
            pl.BlockSpec((tm, k), lambda i: (i, 0)),
            pl.BlockSpec((None, k, MLA_HEADS * MLA_QK), lambda i: (0, 0, 0)),
            pl.BlockSpec((1, LANES), lambda i: (0, 0)),
            pl.BlockSpec((1, LANES), lambda i: (0, 0)),
            pl.BlockSpec((tm, 3 * LANES), lambda i: (i % per, 0)),
        ],
        out_specs=pl.BlockSpec((tm, n), lambda i: (i, 0)),
        out_shape=jax.ShapeDtypeStruct((t, n), BF16),
        scratch_shapes=[pltpu.VMEM((k, n), BF16)],
        compiler_params=_cparams(("arbitrary",)),
        name="mla_q_up",
    )(cq, w_q_up, g0, g1, tab)


def _mla_kv_kernel(a_ref, w_ref, kpe_ref, g0_ref, g1_ref, tab_ref, k_ref, v_ref, wb_ref):
    @pl.when(pl.program_id(0) == 0)
    def _():
        wb_ref[...] = w_ref[...].astype(BF16)

    acc = jnp.dot(a_ref[...], wb_ref[...], preferred_element_type=F32)
    tab = tab_ref[...]
    kpe = kpe_ref[...]
    ss_pe = jnp.sum(kpe * kpe, axis=-1, keepdims=True)
    g0 = g0_ref[...]
    kpe_rot = _rope_lanes(kpe * g1_ref[...], tab)
    outs = []
    for h in range(MLA_HEADS):
        base = h * (MLA_NOPE + MLA_V)
        kn = acc[:, base:base + MLA_NOPE]
        ss = jnp.sum(kn * kn, axis=-1, keepdims=True) + ss_pe
        rs = lax.rsqrt(ss / MLA_QK + EPS)
        outs.append(((kn * rs * g0).astype(BF16), (kpe_rot * rs).astype(BF16),
                     acc[:, base + MLA_NOPE:base + MLA_NOPE + MLA_V].astype(BF16)))
    ones = jnp.ones((acc.shape[0], MLA_V), BF16)
    for h, (k_nope, k_rope, v) in enumerate(outs):
        kb = h * MLA_QK_PAD
        k_ref[:, kb:kb + LANES] = k_nope
        k_ref[:, kb + LANES:kb + 2 * LANES] = k_rope
        vb = h * 2 * MLA_V
        v_ref[:, vb:vb + MLA_V] = v
        v_ref[:, vb + MLA_V:vb + 2 * MLA_V] = ones


def _mla_kv(ckv, w_kv_up, kpe, k_gain, tab, seq, tm=512):
    t, k = ckv.shape
    n = w_kv_up.shape[1]
    per = seq // tm
    g0 = k_gain[:MLA_NOPE].reshape(1, LANES)
    g1 = jnp.pad(k_gain[MLA_NOPE:], (0, LANES - MLA_ROPE)).reshape(1, LANES)
    return pl.pallas_call(
        _mla_kv_kernel,
        grid=(t // tm,),
        in_specs=[
            pl.BlockSpec((tm, k), lambda i: (i, 0)),
            pl.BlockSpec((k, n), lambda i: (0, 0)),
            pl.BlockSpec((tm, LANES), lambda i: (i, 0)),
            pl.BlockSpec((1, LANES), lambda i: (0, 0)),
            pl.BlockSpec((1, LANES), lambda i: (0, 0)),
            pl.BlockSpec((tm, 3 * LANES), lambda i: (i % per, 0)),
        ],
        out_specs=(
            pl.BlockSpec((tm, MLA_HEADS * MLA_QK_PAD), lambda i: (i, 0)),
            pl.BlockSpec((tm, MLA_HEADS * 2 * MLA_V), lambda i: (i, 0)),
        ),
        out_shape=(
            jax.ShapeDtypeStruct((t, MLA_HEADS * MLA_QK_PAD), BF16),
            jax.ShapeDtypeStruct((t, MLA_HEADS * 2 * MLA_V), BF16),
        ),
        scratch_shapes=[pltpu.VMEM((k, n), BF16)],
        compiler_params=_cparams(("arbitrary",)),
        name="mla_kv_up",
    )(ckv, w_kv_up, kpe, g0, g1, tab)


def _mla_attn_kernel(q_ref, k_ref, v_ref, o_ref, *, tq, heads):
    seq = q_ref.shape[0]
    nt = (((1,), (1,)), ((), ()))
    r = lax.broadcasted_iota(jnp.int32, (tq, tq), 0)
    c = lax.broadcasted_iota(jnp.int32, (tq, tq), 1)
    causal = c <= r
    for qi in range(seq // tq):
        rows = slice(qi * tq, (qi + 1) * tq)
        outs = []
        for hh in range(heads):
            qk_cols = slice(hh * MLA_QK_PAD, (hh + 1) * MLA_QK_PAD)
            v_cols = slice(hh * 2 * MLA_V, (hh + 1) * 2 * MLA_V)
            q = q_ref[rows, qk_cols]
            scores = []
            for j in range(qi + 1):
                s = lax.dot_general(q, k_ref[j * tq:(j + 1) * tq, qk_cols], nt, preferred_element_type=F32)
                if j == qi:
                    s = jnp.where(causal, s, NEG_BIG)
                scores.append(s)
            top = scores[0]
            for s in scores[1:]:
                top = jnp.maximum(top, s)
            m = jnp.max(top, axis=-1, keepdims=True)
            acc = None
            for j, s in enumerate(scores):
                p = jnp.exp2(s - m).astype(BF16)
                pv = jnp.dot(p, v_ref[j * tq:(j + 1) * tq, v_cols], preferred_element_type=F32)
                acc = pv if acc is None else acc + pv
            outs.append((acc[:, :MLA_V] / acc[:, MLA_V:]).astype(o_ref.dtype))
        for hh in range(heads):
            o_ref[rows, hh * MLA_V:(hh + 1) * MLA_V] = outs[hh]


def _mla_attn(q, k, v, tq=256, heads=4):
    b, s, _ = q.shape
    return pl.pallas_call(
        functools.partial(_mla_attn_kernel, tq=tq, heads=heads),
        grid=(b, MLA_HEADS // heads),
        in_specs=[
            pl.BlockSpec((None, s, heads * MLA_QK_PAD), lambda bi, h: (bi, 0, h)),
            pl.BlockSpec((None, s, heads * MLA_QK_PAD), lambda bi, h: (bi, 0, h)),
            pl.BlockSpec((None, s, heads * 2 * MLA_V), lambda bi, h: (bi, 0, h)),
        ],
        out_specs=pl.BlockSpec((None, s, heads * MLA_V), lambda bi, h: (bi, 0, h)),
        out_shape=jax.ShapeDtypeStruct((b, s, MLA_HEADS * MLA_V), BF16),
        compiler_params=_cparams(("arbitrary", "arbitrary")),
        name="mla_attn",
    )(q, k, v)


PLAN_COLS = 256
(PLAN_EXP, PLAN_FIRST, PLAN_NEXT, PLAN_NUSED, PLAN_NVALID, PLAN_BSTART, PLAN_COUNT,
 PLAN_NEXT2) = range(8)


def _dispatch_plan_tile(cnt):
    nblk = jnp.floor((cnt + (EXPERT_BLOCK - 1.0)) * (1.0 / EXPERT_BLOCK))
    e_r = lax.broadcasted_iota(jnp.int32, (LANES, LANES), 0)
    e_c = lax.broadcasted_iota(jnp.int32, (LANES, LANES), 1)
    before = jnp.where(e_r < e_c, 1.0, 0.0).astype(BF16)
    used = jnp.where(nblk > 0.0, 1.0, 0.0)
    sub = lax.broadcasted_iota(jnp.int32, (8, LANES), 0)
    prefix = jnp.dot(jnp.where(sub == 0, nblk, used).astype(BF16), before, preferred_element_type=F32)
    bstart = prefix[0:1]
    ordinal = prefix[1:2]
    bend = bstart + nblk
    n_used = jnp.max(bend, axis=-1, keepdims=True)
    row = lax.broadcasted_iota(jnp.int32, (PLAN_COLS, LANES), 0).astype(F32)
    lane = lax.broadcasted_iota(jnp.int32, (PLAN_COLS, LANES), 1).astype(F32)
    is_exp = lane < N_EXPERTS
    row1 = row[:, 0:1]
    done = jnp.logical_and(bend <= row, is_exp)
    blk_exp = jnp.minimum(jnp.sum(jnp.where(done, 1.0, 0.0), axis=-1, keepdims=True), N_EXPERTS - 1.0)
    mine = lane == blk_exp
    bstart_of = jnp.sum(jnp.where(mine, bstart, 0.0), axis=-1, keepdims=True)
    cnt_of = jnp.sum(jnp.where(mine, cnt, 0.0), axis=-1, keepdims=True)
    ord_of = jnp.sum(jnp.where(mine, ordinal, 0.0), axis=-1, keepdims=True)
    valid = row1 < n_used
    first = jnp.where(jnp.logical_and(valid, row1 == bstart_of), ord_of + 1.0, 0.0)
    nvalid = jnp.clip(cnt_of - EXPERT_BLOCK * (row1 - bstart_of), 0.0, float(EXPERT_BLOCK))
    nvalid = jnp.where(valid, nvalid, 0.0)
    usable = jnp.logical_and(nblk > 0.0, is_exp)

    def next_used(after):
        nxt = jnp.min(jnp.where(jnp.logical_and(lane > after, usable), lane, 999.0), axis=-1, keepdims=True)
        return jnp.where(nxt > 998.0, -1.0, nxt)

    nxt = next_used(blk_exp)
    nxt2 = jnp.where(nxt < 0.0, -1.0, next_used(nxt))
    bstart_col = jnp.sum(jnp.where(lane < row, nblk, 0.0), axis=-1, keepdims=True)
    cnt_col = jnp.sum(jnp.where(lane == row, cnt, 0.0), axis=-1, keepdims=True)
    tile = jnp.zeros((PLAN_COLS, LANES), F32)
    cols = {PLAN_EXP: blk_exp, PLAN_FIRST: first, PLAN_NEXT: nxt, PLAN_NUSED: n_used,
            PLAN_NVALID: nvalid, PLAN_BSTART: bstart_col, PLAN_COUNT: cnt_col, PLAN_NEXT2: nxt2}
    for k, val in cols.items():
        tile = jnp.where(lane == k, val, tile)
    return tile


def _router_kernel(x_ref, g_ref, sh_ref, sc_ref, wr_ref, br_ref, h_ref, info_ref, slots_ref, plan_ref,
                   carry_ref, idx_all):
    i = pl.program_id(0)
    tm = x_ref.shape[0]

    @pl.when(i == 0)
    def _():
        carry_ref[...] = jnp.zeros(carry_ref.shape, F32)

    h = _norm_mod(x_ref[...], g_ref[...], sh_ref[...], sc_ref[...])
    w = wr_ref[...]
    w_hi = w.astype(BF16)
    w_lo = (w - w_hi.astype(F32)).astype(BF16)
    h_hi = h.astype(BF16)
    h_lo = (h - h_hi.astype(F32)).astype(BF16)
    lg = (jnp.dot(h_hi, w_hi, preferred_element_type=F32)
          + jnp.dot(h_lo, w_hi, preferred_element_type=F32)
          + jnp.dot(h_hi, w_lo, preferred_element_type=F32)) + br_ref[...]

    lane = lax.broadcasted_iota(jnp.int32, (tm, LANES), 1).astype(F32)
    no_lane = float(LANES)
    gl = jnp.where(lane < N_GROUPS, lg, NEG_BIG)
    gmax = jnp.max(gl, axis=-1, keepdims=True)
    g_idx = jnp.min(jnp.where(gl == gmax, lane, no_lane), axis=-1, keepdims=True)
    g_p = 1.0 / jnp.sum(jnp.exp(gl - gmax), axis=-1, keepdims=True)
    lo_lane = N_GROUPS + g_idx * EXPERTS_PER_GROUP
    in_grp = jnp.logical_and(lane >= lo_lane, lane < lo_lane + EXPERTS_PER_GROUP)
    ev = jnp.where(in_grp, lg, NEG_BIG)
    v1 = jnp.max(ev, axis=-1, keepdims=True)
    i1 = jnp.min(jnp.where(ev == v1, lane, no_lane), axis=-1, keepdims=True)
    ev2 = jnp.where(lane == i1, NEG_BIG, ev)
    v2 = jnp.max(ev2, axis=-1, keepdims=True)
    i2 = jnp.min(jnp.where(ev2 == v2, lane, no_lane), axis=-1, keepdims=True)
    e2 = jnp.exp(v2 - v1)
    den = 1.0 + e2
    w1 = (1.0 / den) * g_p
    w2 = (e2 / den) * g_p
    id1 = i1 - N_GROUPS
    id2 = i2 - N_GROUPS

    oh1 = lane == id1
    oh2 = lane == id2
    both = jnp.where(jnp.logical_or(oh1, oh2), 1.0, 0.0)
    r = lax.broadcasted_iota(jnp.int32, (tm, tm), 0)
    c = lax.broadcasted_iota(jnp.int32, (tm, tm), 1)
    tril = jnp.where(c < r, 1.0, 0.0).astype(BF16)
    before = jnp.dot(tril, both.astype(BF16), preferred_element_type=F32) + carry_ref[...]
    rank1 = jnp.sum(jnp.where(oh1, before, 0.0), axis=-1, keepdims=True)
    rank2 = jnp.sum(jnp.where(oh2, before, 0.0), axis=-1, keepdims=True)
    carry_ref[...] = carry_ref[...] + jnp.sum(both, axis=0, keepdims=True)

    info = jnp.zeros((tm, LANES), F32)
    for col, val in enumerate((id1, id2, rank1, rank2, w1, w2)):
        info = jnp.where(lane == col, val, info)
    _store_row_tiles(h_ref, _pack_halves(h))
    info_ref[...] = info
    idx_all[:, pl.ds(pl.multiple_of(i * tm, tm), tm)] = jnp.transpose(info)[0:8]

    @pl.when(i == pl.num_programs(0) - 1)
    def _():
        plan = _dispatch_plan_tile(carry_ref[...])
        plan_ref[...] = jnp.transpose(plan)[0:8].astype(jnp.int32)
        first_row = plan[0:N_EXPERTS, PLAN_BSTART:PLAN_BSTART + 1] * float(EXPERT_BLOCK)
        chunk = 1024
        expert = lax.broadcasted_iota(jnp.int32, (N_EXPERTS, chunk), 0).astype(F32)
        for c0 in range(0, idx_all.shape[1], chunk):
            cols = slice(c0, c0 + chunk)
            rows = []
            for k in range(2):
                ids = idx_all[k:k + 1, cols]
                base = jnp.sum(jnp.where(expert == ids, first_row, 0.0), axis=0, keepdims=True)
                rows.append(base + idx_all[2 + k:3 + k, cols])
            slots_ref[:, cols] = jnp.concatenate(rows, axis=0).astype(jnp.int32)


def _router(x2, g, mod, seq, wr, br, tm=256):
    assert x2.shape[1] == 2 * ROW_TILE * LANES
    t, d = x2.shape
    per = seq // tm
    return pl.pallas_call(
        _router_kernel,
        grid=(t // tm,),
        in_specs=[
            pl.BlockSpec((tm, d), lambda i: (i, 0)),
            pl.BlockSpec((1, d), lambda i: (0, 0)),
            pl.BlockSpec((None, 1, d), lambda i: ((i // per) * 6 + 3, 0, 0)),
            pl.BlockSpec((None, 1, d), lambda i: ((i // per) * 6 + 4, 0, 0)),
            pl.BlockSpec((d, LANES), lambda i: (0, 0)),
            pl.BlockSpec((1, LANES), lambda i: (0, 0)),
        ],
        out_specs=(
            pl.BlockSpec((tm * ROW_TILE, LANES), lambda i: (i, 0)),
            pl.BlockSpec((tm, LANES), lambda i: (i, 0)),
            pl.BlockSpec((2, t), lambda i: (0, 0)),
            pl.BlockSpec((8, PLAN_COLS), lambda i: (0, 0)),
        ),
        out_shape=(
            jax.ShapeDtypeStruct((t * ROW_TILE, LANES), jnp.uint32),
            jax.ShapeDtypeStruct((t, LANES), F32),
            jax.ShapeDtypeStruct((2, t), jnp.int32),
            jax.ShapeDtypeStruct((8, PLAN_COLS), jnp.int32),
        ),
        scratch_shapes=[pltpu.VMEM((1, LANES), F32), pltpu.VMEM((8, t), F32)],
        compiler_params=_cparams(("arbitrary",)),
        name="moe_router",
    )(x2, g.reshape(1, d), mod, mod, wr, br)


GATHER_GROUP = 8


def _plan(plan, row, col):
    return plan[row * PLAN_COLS + col]


def _expert_kernel(plan, slots, h_hbm, wg_hbm, wu_hbm, wd_hbm, ys_ref,
                   row_tok, xbuf, xsem, sg, su, sd, wsem, wgb, wub, wdb, *, layer, n_tok):
    i = pl.program_id(0)
    nu = _plan(plan, PLAN_NUSED, 0)
    n_rows = row_tok.shape[0]

    def weight_copies(e, st):
        return (pltpu.make_async_copy(wg_hbm.at[layer, e], sg.at[st], wsem.at[st, 0]),
                pltpu.make_async_copy(wu_hbm.at[layer, e], su.at[st], wsem.at[st, 1]),
                pltpu.make_async_copy(wd_hbm.at[layer, e], sd.at[st], wsem.at[st, 2]))

    def n_groups(blk):
        return (_plan(plan, PLAN_NVALID, blk) + GATHER_GROUP - 1) // GATHER_GROUP

    def start_gather(blk, slot):
        base = blk * EXPERT_BLOCK

        def body(g, carry):
            for k in range(GATHER_GROUP):
                r = g * GATHER_GROUP + k
                tok = row_tok[base + r]
                src = pl.multiple_of(tok * ROW_TILE, ROW_TILE)
                dst = pl.multiple_of(r * ROW_TILE, ROW_TILE)
                pltpu.make_async_copy(h_hbm.at[pl.ds(src, ROW_TILE), :],
                                      xbuf.at[slot, pl.ds(dst, ROW_TILE), :], xsem.at[slot]).start()
            return carry

        lax.fori_loop(0, n_groups(blk), body, 0)

    def wait_gather(blk, slot):
        span = GATHER_GROUP * ROW_TILE

        def body(g, carry):
            pltpu.make_async_copy(h_hbm.at[pl.ds(0, span), :],
                                  xbuf.at[slot, pl.ds(0, span), :], xsem.at[slot]).wait()
            return carry

        lax.fori_loop(0, n_groups(blk), body, 0)

    @pl.when(i == 0)
    def _():
        for cp in weight_copies(_plan(plan, PLAN_EXP, 0), 0):
            cp.start(priority=1)
        second = _plan(plan, PLAN_NEXT, 0)

        @pl.when(second >= 0)
        def _():
            for cp in weight_copies(second, 1):
                cp.start(priority=1)

        xbuf[...] = jnp.zeros(xbuf.shape, xbuf.dtype)

        def pad_body(e, carry):
            end = _plan(plan, PLAN_BSTART, e) * EXPERT_BLOCK + _plan(plan, PLAN_COUNT, e)
            for k in range(GATHER_GROUP - 1):
                row_tok[jnp.minimum(end + k, n_rows - 1)] = 0
            return carry

        lax.fori_loop(0, N_EXPERTS, pad_body, 0)

        def fill_body(t, carry):
            row_tok[slots[0, t]] = t
            row_tok[slots[1, t]] = t
            return carry

        lax.fori_loop(0, n_tok, fill_body, 0, unroll=8)
        start_gather(0, 0)

    @pl.when(i < nu)
    def _():
        slot = i % 2

        @pl.when(i + 1 < nu)
        def _():
            start_gather(i + 1, 1 - slot)

        first = _plan(plan, PLAN_FIRST, i)

        @pl.when(first > 0)
        def _():
            st = (first - 1) % 2
            nxt2 = _plan(plan, PLAN_NEXT2, i)
            cps = weight_copies(_plan(plan, PLAN_EXP, i), st)
            nxt_cps = weight_copies(jnp.maximum(nxt2, 0), st)
            for cp, ncp, stage, dst in zip(cps, nxt_cps, (sg, su, sd), (wgb, wub, wdb)):
                cp.wait()
                dst[...] = stage[st].astype(BF16)

                @pl.when(nxt2 >= 0)
                def _():
                    ncp.start(priority=1)

        wait_gather(i, slot)
        parts = [_unpack_halves(w) for w in _load_row_tiles(xbuf.at[slot], EXPERT_BLOCK)]
        x_lo = jnp.concatenate([lo.astype(BF16) for lo, _ in parts], axis=1)
        x_hi = jnp.concatenate([hi.astype(BF16) for _, hi in parts], axis=1)
        half = wgb.shape[0] // 2
        g = (jnp.dot(x_lo, wgb[:half], preferred_element_type=F32)
             + jnp.dot(x_hi, wgb[half:], preferred_element_type=F32))
        u = (jnp.dot(x_lo, wub[:half], preferred_element_type=F32)
             + jnp.dot(x_hi, wub[half:], preferred_element_type=F32))
        hid = (_silu(g) * u).astype(BF16)
        _store_row_tiles(ys_ref, _pack_halves(jnp.dot(hid, wdb[...], preferred_element_type=F32)))

    @pl.when(i >= nu)
    def _():
        ys_ref[...] = jnp.zeros(ys_ref.shape, ys_ref.dtype)


def _expert_ffn(h2, w_gate, w_up, w_down, layer, plan, slots, n_blocks):
    t = h2.shape[0] // ROW_TILE
    d, f = w_gate.shape[2], w_gate.shape[3]
    n_rows = n_blocks * EXPERT_BLOCK
    grid_spec = pltpu.PrefetchScalarGridSpec(
        num_scalar_prefetch=2,
        grid=(n_blocks,),
        in_specs=[pl.BlockSpec(memory_space=pl.ANY)] * 4,
        out_specs=pl.BlockSpec((EXPERT_BLOCK * ROW_TILE, LANES), lambda i, *_: (i, 0)),
        scratch_shapes=[
            pltpu.SMEM((n_rows,), jnp.int32),
            pltpu.VMEM((2, EXPERT_BLOCK * ROW_TILE, LANES), jnp.uint32),
            pltpu.SemaphoreType.DMA((2,)),
            pltpu.VMEM((2, d, f), F32),
            pltpu.VMEM((2, d, f), F32),
            pltpu.VMEM((2, f, d), F32),
            pltpu.SemaphoreType.DMA((2, 3)),
            pltpu.VMEM((d, f), BF16),
            pltpu.VMEM((d, f), BF16),
            pltpu.VMEM((f, d), BF16),
        ],
    )
    weight_bytes = 3 * d * f * (2 * 4 + 2)
    vmem = weight_bytes + 8 * 1024 * 1024
    return pl.pallas_call(
        functools.partial(_expert_kernel, layer=layer, n_tok=t),
        grid_spec=grid_spec,
        out_shape=jax.ShapeDtypeStruct((n_rows * ROW_TILE, LANES), jnp.uint32),
        compiler_params=_cparams(("arbitrary",), vmem),
        name="moe_experts",
    )(plan, slots, h2, w_gate, w_up, w_down)


def _combine_kernel(slots, x_ref, info_ref, gate_ref, ys_hbm, *rest, tm, with_next):
    if with_next:
        ng_ref, nsh_ref, nsc_ref, o_ref, h_ref, buf, sem = rest
    else:
        o_ref, buf, sem = rest
    i = pl.program_id(0)
    last = pl.num_programs(0) - 1

    def row_copy(slot, k, src_row, r):
        src = pl.multiple_of(src_row * ROW_TILE, ROW_TILE)
        dst = pl.multiple_of(r * ROW_TILE, ROW_TILE)
        return pltpu.make_async_copy(ys_hbm.at[pl.ds(src, ROW_TILE), :],
                                     buf.at[slot, k, pl.ds(dst, ROW_TILE), :], sem.at[slot])

    def wait_rows(slot):
        for k in range(2):
            pltpu.make_async_copy(ys_hbm.at[pl.ds(0, tm * ROW_TILE), :], buf.at[slot, k], sem.at[slot]).wait()

    def start_gather(step, slot):
        base = step * tm

        def body(r, carry):
            for k in range(2):
                row_copy(slot, k, slots[k, base + r], r).start(priority=k)
            return carry

        lax.fori_loop(0, tm, body, 0, unroll=8)

    @pl.when(i == 0)
    def _():
        start_gather(0, 0)

    slot = i % 2

    @pl.when(i < last)
    def _():
        start_gather(i + 1, 1 - slot)

    wait_rows(slot)
    info = info_ref[...]
    w0 = info[:, 4:5]
    w1 = info[:, 5:6]
    half = x_ref.shape[1] // 2
    a_tiles = _load_row_tiles(buf.at[slot, 0], tm)
    b_tiles = _load_row_tiles(buf.at[slot, 1], tm)
    for s in range(ROW_TILE):
        a_lo, a_hi = _unpack_halves(a_tiles[s])
        b_lo, b_hi = _unpack_halves(b_tiles[s])
        lo = slice(s * LANES, (s + 1) * LANES)
        hi = slice(half + s * LANES, half + (s + 1) * LANES)
        o_ref[:, lo] = x_ref[:, lo] + gate_ref[:, lo] * (w0 * a_lo + w1 * b_lo)
        o_ref[:, hi] = x_ref[:, hi] + gate_ref[:, hi] * (w0 * a_hi + w1 * b_hi)
    if with_next:
        h_ref[...] = _norm_mod(o_ref[...], ng_ref[...], nsh_ref[...], nsc_ref[...]).astype(h_ref.dtype)


def _combine(x2, info, mod, seq, ys, slots, next_norm=None, tm=256):
    t, d = x2.shape
    per = seq // tm
    row = lambda i, *_: (i, 0)
    mod_row = lambda k: (lambda i, *_: ((i // per) * 6 + k, 0, 0))
    in_specs = [
        pl.BlockSpec((tm, d), row),
        pl.BlockSpec((tm, LANES), row),
        pl.BlockSpec((None, 1, d), mod_row(5)),
        pl.BlockSpec(memory_space=pl.ANY),
    ]
    args = [slots, x2, info, mod, ys]
    out_specs = pl.BlockSpec((tm, d), row)
    out_shape = jax.ShapeDtypeStruct((t, d), F32)
    if next_norm is not None:
        next_g, next_mod = next_norm
        in_specs += [pl.BlockSpec((1, d), lambda i, *_: (0, 0)),
                     pl.BlockSpec((None, 1, d), mod_row(0)), pl.BlockSpec((None, 1, d), mod_row(1))]
        args += [next_g.reshape(1, d), next_mod, next_mod]
        out_specs = (out_specs, pl.BlockSpec((tm, d), row))
        out_shape = (out_shape, jax.ShapeDtypeStruct((t, d), BF16))
    grid_spec = pltpu.PrefetchScalarGridSpec(
        num_scalar_prefetch=1,
        grid=(t // tm,),
        in_specs=in_specs,
        out_specs=out_specs,
        scratch_shapes=[
            pltpu.VMEM((2, 2, tm * ROW_TILE, LANES), jnp.uint32),
            pltpu.SemaphoreType.DMA((2,)),
        ],
    )
    return pl.pallas_call(
        functools.partial(_combine_kernel, tm=tm, with_next=next_norm is not None),
        grid_spec=grid_spec,
        out_shape=out_shape,
        compiler_params=_cparams(("arbitrary",)),
        name="moe_combine",
    )(*args)


def _hier_moe(x2, norm_g, mod, seq, w_rg, b_rg, w_re, b_re, w_gate, w_up, w_down, layer, next_norm=None):
    t, d = x2.shape
    pad = LANES - N_GROUPS - N_EXPERTS
    wr = jnp.concatenate([w_rg, w_re, jnp.zeros((d, pad), F32)], axis=1)
    br = jnp.concatenate([b_rg, b_re, jnp.zeros((pad,), F32)]).reshape(1, LANES)
    h2, info, slots, plan = _router(x2, norm_g, mod, seq, wr, br)
    n_assign = 2 * t
    n_blocks = (n_assign + N_EXPERTS * (EXPERT_BLOCK - 1) + EXPERT_BLOCK - 1) // EXPERT_BLOCK
    assert n_blocks <= PLAN_COLS
    plan = plan.reshape(-1)
    ys = _expert_ffn(h2, w_gate, w_up, w_down, layer, plan, slots, n_blocks)
    return _combine(x2, info, mod, seq, ys, slots, next_norm)


def kernel(x, c, ada_w, ada_b, norm1_g, norm2_g, dsa_w_in, dsa_q_gain, dsa_k_gain, dsa_w_out, mla_w_in, mla_cq_gain, mla_ckv_gain, mla_w_q_up, mla_w_kv_up, mla_q_gain, mla_k_gain, mla_w_out, router_group_w, router_group_b, router_expert_w, router_expert_b, expert_w_gate, expert_w_up, expert_w_down):
    b, s, d = x.shape
    t = b * s
    mods = _ada_mod(c, ada_w, ada_b)
    x2 = x.reshape(t, d)

    mod = mods[0]
    h = _normmod(x2, norm1_g[0], mod, s, 0, 1)
    gains = (dsa_q_gain[0], dsa_k_gain[0], dsa_k_gain[0])
    qkv = [[_dsa_proj(h, dsa_w_in[0], gains[w], g, w, b) for w in range(3)] for g in range(len(DIL_PAIRS))]
    og2 = _dsa_attn(*qkv[2], 2)
    og1 = _dsa_attn(*qkv[1], 1)
    o = _dsa_attn(*qkv[0], 0, others=(og1, og2))
    x2 = _resid_mm(o.reshape(t, DSA_WIDTH), dsa_w_out[0], x2, mod, s, 2)
    x2, h = _hier_moe(x2, norm2_g[0], mod, s, router_group_w[0], router_group_b[0],
                      router_expert_w[0], router_expert_b[0],
                      expert_w_gate, expert_w_up, expert_w_down, 0, next_norm=(norm1_g[1], mods[1]))

    mod = mods[1]
    cq, ckv, kpe = _mla_in(h, mla_w_in, mla_cq_gain[0], mla_ckv_gain[0])
    tab = _rope_tables(s)
    q = _mla_q(cq, mla_w_q_up, mla_q_gain[0], tab, s)
    k, v = _mla_kv(ckv, mla_w_kv_up[0], kpe, mla_k_gain[0], tab, s)
    o = _mla_attn(q.reshape(b, s, -1), k.reshape(b, s, -1), v.reshape(b, s, -1))
    x2 = _resid_mm(o.reshape(t, MLA_HEADS * MLA_V), mla_w_out[0], x2, mod, s, 2)
    x2 = _hier_moe(x2, norm2_g[1], mod, s, router_group_w[1], router_group_b[1],
                   router_expert_w[1], router_expert_b[1],
                   expert_w_gate, expert_w_up, expert_w_down, 1)
    return x2.reshape(b, s, d)
```

```python
import functools
import math

import jax
import jax.numpy as jnp
import numpy as np
from jax import lax
from jax.experimental import pallas as pl
from jax.experimental.pallas import tpu as pltpu

F32 = jnp.float32
BF16 = jnp.bfloat16

D_MODEL = 2048
EPS = 1e-6
LANES = 128
NEG_BIG = -1e30
LOG2_E = math.log2(math.e)
LN_2 = math.log(2.0)

DIL_PAIRS = ((128, 1), (512, 4), (2048, 16))
DSA_HEADS = 8
DSA_HEAD_DIM = 128
DSA_WIDTH = DSA_HEADS * DSA_HEAD_DIM
BAND = 128

MLA_HEADS = 16
MLA_Q_LORA = 512
MLA_KV_LORA = 512
MLA_NOPE = 128
MLA_ROPE = 64
MLA_V = 128
MLA_QK = MLA_NOPE + MLA_ROPE
MLA_QK_PAD = 256
ROPE_THETA = 10000.0

N_GROUPS = 4
EXPERTS_PER_GROUP = 16
N_EXPERTS = N_GROUPS * EXPERTS_PER_GROUP
D_EXPERT = 768
EXPERT_BLOCK = 128

VMEM_LIMIT = 48 * 1024 * 1024


def _cparams(sem, vmem=VMEM_LIMIT):
    return pltpu.CompilerParams(dimension_semantics=sem, vmem_limit_bytes=vmem)


def _silu(x):
    return x * (1.0 / (1.0 + jnp.exp(-x)))


def _pack_halves(x):
    n = x.shape[1] // 2
    xb = x.astype(BF16).astype(F32)
    lo = lax.bitcast_convert_type(xb[:, :n], jnp.uint32) >> 16
    hi = lax.bitcast_convert_type(xb[:, n:], jnp.uint32) & jnp.uint32(0xFFFF0000)
    return hi | lo


def _unpack_halves(w):
    lo = lax.bitcast_convert_type(w << 16, F32)
    hi = lax.bitcast_convert_type(w & jnp.uint32(0xFFFF0000), F32)
    return lo, hi


ROW_TILE = 8


def _store_row_tiles(ref, words):
    rows = words.shape[0]
    for s in range(ROW_TILE):
        ref[pl.ds(s, rows, stride=ROW_TILE), :] = words[:, s * LANES:(s + 1) * LANES]


def _load_row_tiles(ref, rows):
    return [ref[pl.ds(s, rows, stride=ROW_TILE), :] for s in range(ROW_TILE)]


def _norm_mod(x, g, shift, scale):
    ms = jnp.mean(x * x, axis=-1, keepdims=True)
    y = x * lax.rsqrt(ms + EPS) * g
    return y * (1.0 + scale) + shift


def _ada_kernel(c_ref, w_ref, b_ref, o_ref):
    ca = _silu(c_ref[...])
    hi = ca.astype(BF16)
    lo = (ca - hi.astype(F32)).astype(BF16)
    lhs = jnp.concatenate([hi, lo], axis=0)
    res = jnp.dot(lhs, w_ref[...].astype(BF16), preferred_element_type=F32)
    o_ref[...] = res[:8] + res[8:] + b_ref[...]


def _ada_mod(c, ada_w, ada_b):
    depth, d, n = ada_w.shape
    b = c.shape[0]
    c8 = jnp.pad(c, ((0, 8 - b), (0, 0)))
    tn = 1024
    out = pl.pallas_call(
        _ada_kernel,
        grid=(depth, n // tn),
        in_specs=[
            pl.BlockSpec((8, d), lambda i, j: (0, 0)),
            pl.BlockSpec((None, d, tn), lambda i, j: (i, 0, j)),
            pl.BlockSpec((None, 1, tn), lambda i, j: (i, 0, j)),
        ],
        out_specs=pl.BlockSpec((None, 8, tn), lambda i, j: (i, 0, j)),
        out_shape=jax.ShapeDtypeStruct((depth, 8, n), F32),
        compiler_params=_cparams(("arbitrary", "arbitrary")),
        name="ada_mod",
    )(c8, ada_w, ada_b.reshape(depth, 1, n))
    return out[:, :b].reshape(depth, b * 6, 1, d)


def _normmod_kernel(x_ref, g_ref, sh_ref, sc_ref, o_ref):
    o_ref[...] = _norm_mod(x_ref[...], g_ref[...], sh_ref[...], sc_ref[...]).astype(o_ref.dtype)


def _normmod(x2, g, mod, seq, k_shift, k_scale, tm=1024):
    t, d = x2.shape
    per = seq // tm
    return pl.pallas_call(
        _normmod_kernel,
        grid=(t // tm,),
        in_specs=[
            pl.BlockSpec((tm, d), lambda i: (i, 0)),
            pl.BlockSpec((1, d), lambda i: (0, 0)),
            pl.BlockSpec((None, 1, d), lambda i: ((i // per) * 6 + k_shift, 0, 0)),
            pl.BlockSpec((None, 1, d), lambda i: ((i // per) * 6 + k_scale, 0, 0)),
        ],
        out_specs=pl.BlockSpec((tm, d), lambda i: (i, 0)),
        out_shape=jax.ShapeDtypeStruct((t, d), BF16),
        compiler_params=_cparams(("arbitrary",)),
        name="normmod",
    )(x2, g.reshape(1, d), mod, mod)


def _cast_weight_once(w_ref, wb_ref):
    @pl.when(pl.program_id(1) == 0)
    def _():
        wb_ref[...] = w_ref[...].astype(BF16)


def _dsa_proj_kernel(a_ref, w_ref, g_ref, o_ref, wb_ref, res_ref, *, dilation, normed, gain_scale):
    @pl.when(pl.program_id(0) == 0)
    def _():
        wb_ref[...] = w_ref[...].astype(BF16)

    gain = g_ref[...] * gain_scale
    tm = a_ref.shape[0]
    pr = 512
    n_parts = tm // pr
    accs = [jnp.dot(a_ref[p * pr:(p + 1) * pr, :], wb_ref[...], preferred_element_type=F32)
            for p in range(n_parts)]
    for p, acc in enumerate(accs):
        blks = []
        for h in range(DSA_HEADS):
            blk = acc[:, h * LANES:(h + 1) * LANES]
            if normed:
                ms = jnp.mean(blk * blk, axis=-1, keepdims=True)
                blk = blk * lax.rsqrt(ms + EPS) * gain
            blks.append(blk)
        for h, blk in enumerate(blks):
            hs = slice(h * LANES, (h + 1) * LANES)
            if dilation == 1:
                o_ref[0, p * pr:(p + 1) * pr, hs] = blk.astype(o_ref.dtype)
            else:
                res_ref[h, p * pr:(p + 1) * pr, :] = blk
                sub_rows = pr // dilation
                for r in range(dilation):
                    sub = res_ref[h, pl.ds(p * pr + r, sub_rows, stride=dilation), :]
                    o_ref[r, p * sub_rows:(p + 1) * sub_rows, hs] = sub.astype(o_ref.dtype)


def _dsa_proj(h, w_in, gain, group, which, batch, tm=1024, tn=DSA_WIDTH):
    t, d = h.shape
    dil = DIL_PAIRS[group][1]
    seq = t // batch
    per = seq // tm
    gain_scale = LOG2_E / math.sqrt(DSA_HEAD_DIM) if which == 0 else 1.0
    return pl.pallas_call(
        functools.partial(_dsa_proj_kernel, dilation=dil, normed=which != 2, gain_scale=gain_scale),
        grid=(t // tm,),
        in_specs=[
            pl.BlockSpec((tm, d), lambda i: (i, 0)),
            pl.BlockSpec((d, tn), lambda i: (0, group * 3 + which), pipeline_mode=pl.Buffered(1)),
            pl.BlockSpec((1, LANES), lambda i: (0, 0)),
        ],
        out_specs=pl.BlockSpec((None, dil, tm // dil, tn), lambda i: (i // per, 0, i % per, 0)),
        out_shape=jax.ShapeDtypeStruct((batch, dil, seq // dil, tn), BF16),
        scratch_shapes=[pltpu.VMEM((d, tn), BF16), pltpu.VMEM((DSA_HEADS, tm, LANES), F32)],
        compiler_params=_cparams(("arbitrary",)),
        name=f"dsa_proj_g{group}_{'qkv'[which]}",
    )(h, w_in, gain.reshape(1, LANES))


def _resid_mm_kernel(a_ref, w_ref, x_ref, gate_ref, o_ref, wb_ref):
    _cast_weight_once(w_ref, wb_ref)
    y = jnp.dot(a_ref[...], wb_ref[...], preferred_element_type=F32)
    o_ref[...] = x_ref[...] + gate_ref[...] * y


def _resid_mm(a, w, x2, mod, seq, k_gate, tn=1024):
    t, k = a.shape
    n = w.shape[1]
    tm = 1024
    vmem = VMEM_LIMIT if k <= 1024 else 56 * 1024 * 1024
    per = seq // tm
    return pl.pallas_call(
        _resid_mm_kernel,
        grid=(n // tn, t // tm),
        in_specs=[
            pl.BlockSpec((tm, k), lambda j, i: (i, 0)),
            pl.BlockSpec((k, tn), lambda j, i: (0, j)),
            pl.BlockSpec((tm, tn), lambda j, i: (i, j)),
            pl.BlockSpec((None, 1, tn), lambda j, i: ((i // per) * 6 + k_gate, 0, j)),
        ],
        out_specs=pl.BlockSpec((tm, tn), lambda j, i: (i, j)),
        out_shape=jax.ShapeDtypeStruct((t, n), F32),
        scratch_shapes=[pltpu.VMEM((k, tn), BF16)],
        compiler_params=_cparams(("arbitrary", "arbitrary"), vmem),
        name="resid_mm",
    )(a, w, x2, mod)


def _alibi_slope(head_slot, group):
    n = len(DIL_PAIRS) * DSA_HEADS
    return 2.0 ** (-8.0 * (head_slot * len(DIL_PAIRS) + group + 1.0) / n)


def _dsa_bias_table(group, has_prev):
    window, d = DIL_PAIRS[group]
    steps = window // d
    qi = np.arange(BAND)[:, None]
    if has_prev:
        kj = np.arange(2 * BAND)[None, :]
        delta = qi + BAND - kj
        prev_key = np.broadcast_to(kj < BAND, delta.shape)
    else:
        kj = np.arange(BAND)[None, :]
        delta = qi - kj
        prev_key = np.zeros(delta.shape, bool)
    inside = (delta >= 0) & (delta <= steps)
    tabs = []
    for first in (True, False):
        valid = inside & ~(prev_key & first)
        per_head = [np.where(valid, -_alibi_slope(h, group) * LOG2_E * (delta * d), NEG_BIG)
                    for h in range(DSA_HEADS)]
        tabs.append(np.stack(per_head))
    return jnp.asarray(np.stack(tabs), F32)


def _dsa_attn_kernel(*refs, dilation, has_prev, merge):
    refs = list(refs)
    q_ref, kc_ref, vc_ref, bias_ref = refs[:4]
    pos = 4
    if has_prev:
        kp_ref, vp_ref = refs[pos:pos + 2]
        pos += 2
    if merge:
        other = refs[pos:pos + 4]
        pos += 4
        o_ref = refs[pos]
    else:
        o_ref, lse_ref = refs[pos:pos + 2]

    r = pl.program_id(2)
    lane = lax.broadcasted_iota(jnp.int32, (BAND, LANES), 1)
    nt = (((1,), (1,)), ((), ()))
    nk = 2 * BAND if has_prev else BAND
    ones = jnp.ones((nk, LANES), BF16)
    n_batch = q_ref.shape[0]

    def head_scores(bi, h):
        hs = slice(h * LANES, (h + 1) * LANES)
        q = q_ref[bi, :, hs]
        if has_prev:
            k = jnp.concatenate([kp_ref[bi, :, hs], kc_ref[bi, :, hs]], axis=0)
        else:
            k = kc_ref[bi, :, hs]
        return lax.dot_general(q, k, nt, preferred_element_type=F32) + bias_ref[h]

    def head_values(bi, h):
        hs = slice(h * LANES, (h + 1) * LANES)
        if has_prev:
            v = jnp.concatenate([vp_ref[bi, :, hs], vc_ref[bi, :, hs]], axis=0)
        else:
            v = vc_ref[bi, :, hs]
        return jnp.concatenate([v, ones], axis=1)

    results = []
    for bi in range(n_batch):
        scores = [head_scores(bi, h) for h in range(DSA_HEADS)]
        maxes = [jnp.max(s, axis=-1, keepdims=True) for s in scores]
        probs = [jnp.exp2(s - m).astype(BF16) for s, m in zip(scores, maxes)]
        accs = [jnp.dot(p, head_values(bi, h), preferred_element_type=F32) for h, p in enumerate(probs)]
        outs = []
        lse_tile = jnp.zeros((BAND, LANES), F32)
        for h in range(DSA_HEADS):
            l = accs[h][:, LANES:]
            o = accs[h][:, :LANES] / l
            lse = maxes[h] * LN_2 + jnp.log(l)
            if merge:
                lses = [lse] + [other[2 * g + 1][bi, :, h * 16:h * 16 + 1] for g in range(2)]
                parts = [o] + [other[2 * g][bi, h] for g in range(2)]
                top = jnp.maximum(jnp.maximum(lses[0], lses[1]), lses[2])
                es = [jnp.exp(x - top) for x in lses]
                den = es[0] + es[1] + es[2]
                o = ((es[0] * parts[0] + es[1] * parts[1] + es[2] * parts[2]) / den).astype(o_ref.dtype)
            else:
                in_head = jnp.logical_and(lane >= h * 16, lane < (h + 1) * 16)
                lse_tile = jnp.where(in_head, lse, lse_tile)
            outs.append(o)
        results.append((outs, lse_tile))

    for bi, (outs, lse_tile) in enumerate(results):
        for h, o in enumerate(outs):
            if merge:
                o_ref[bi, :, h * LANES:(h + 1) * LANES] = o
            else:
                o_ref[bi, h, pl.ds(r, BAND, stride=dilation), :] = o
        if not merge:
            lse_ref[bi, pl.ds(r, BAND, stride=dilation), :] = lse_tile


def _dsa_attn(q_g, k_g, v_g, group, others=None):
    b, d, sub_len, _ = q_g.shape
    s = d * sub_len
    nb = sub_len // BAND
    has_prev = nb > 1
    merge = others is not None
    assert not merge or d == 1
    bias = _dsa_bias_table(group, has_prev)

    span = BAND * d
    bb = 4 if span <= 4 * BAND else 2
    while b % bb:
        bb //= 2
    cur = lambda bi, n, r: (bi, r, n, 0)
    prev = lambda bi, n, r: (bi, r, jnp.maximum(n - 1, 0), 0)
    blk = (bb, None, BAND, DSA_WIDTH)
    in_specs = [pl.BlockSpec(blk, cur), pl.BlockSpec(blk, cur), pl.BlockSpec(blk, cur),
                pl.BlockSpec((None,) + bias.shape[1:], lambda bi, n, r: (jnp.minimum(n, 1), 0, 0, 0))]
    args = [q_g, k_g, v_g, bias]
    if has_prev:
        in_specs += [pl.BlockSpec(blk, prev), pl.BlockSpec(blk, prev)]
        args += [k_g, v_g]
    nat = lambda bi, n, r: (bi, n, 0)
    nat_heads = lambda bi, n, r: (bi, 0, n, 0)
    if merge:
        for o_g, lse_g in others:
            in_specs += [pl.BlockSpec((bb, DSA_HEADS, BAND, LANES), nat_heads),
                         pl.BlockSpec((bb, BAND, LANES), nat)]
            args += [o_g, lse_g]
        out_shape = jax.ShapeDtypeStruct((b, s, DSA_WIDTH), BF16)
        out_specs = pl.BlockSpec((bb, BAND, DSA_WIDTH), nat)
    else:
        out_shape = (jax.ShapeDtypeStruct((b, DSA_HEADS, s, LANES), F32), jax.ShapeDtypeStruct((b, s, LANES), F32))
        out_specs = (pl.BlockSpec((bb, DSA_HEADS, span, LANES), nat_heads),
                     pl.BlockSpec((bb, span, LANES), nat))
    return pl.pallas_call(
        functools.partial(_dsa_attn_kernel, dilation=d, has_prev=has_prev, merge=merge),
        grid=(b // bb, nb, d),
        in_specs=in_specs,
        out_specs=out_specs,
        out_shape=out_shape,
        compiler_params=_cparams(("arbitrary", "arbitrary", "arbitrary")),
        name=f"dsa_attn_g{group}",
    )(*args)


def _rope_tables(seq):
    half = MLA_ROPE // 2
    inv = np.float32(ROPE_THETA) ** (-np.arange(half, dtype=np.float32) / np.float32(half))
    ang = np.arange(seq, dtype=np.float32)[:, None] * inv[None, :]
    cos, sin = np.cos(ang), np.sin(ang)
    z = np.zeros((seq, LANES - MLA_ROPE), np.float32)
    zh = np.zeros((seq, half), np.float32)
    cos_t = np.concatenate([cos, cos, z], axis=1)
    sin_a = np.concatenate([zh, sin, z], axis=1)
    sin_b = np.concatenate([-sin, zh, z], axis=1)
    return jnp.asarray(np.concatenate([cos_t, sin_a, sin_b], axis=1), F32)


def _rope_lanes(x, tab):
    half = MLA_ROPE // 2
    cos_t = tab[:, 0:LANES]
    sin_a = tab[:, LANES:2 * LANES]
    sin_b = tab[:, 2 * LANES:3 * LANES]
    return (x * cos_t + pltpu.roll(x, half, 1) * sin_a
            + pltpu.roll(x, LANES - half, 1) * sin_b)


def _mla_in_kernel(a_ref, w_ref, cqg_ref, ckvg_ref, cq_ref, ckv_ref, kpe_ref, wb_ref):
    n = w_ref.shape[1]

    @pl.when(pl.program_id(0) == 0)
    def _():
        wb_ref[...] = jnp.zeros(wb_ref.shape, BF16)
        wb_ref[:, :n] = w_ref[...].astype(BF16)

    acc = jnp.dot(a_ref[...], wb_ref[...], preferred_element_type=F32)
    cq = acc[:, :MLA_Q_LORA]
    cq_ref[...] = (cq * lax.rsqrt(jnp.mean(cq * cq, axis=-1, keepdims=True) + EPS)
                   * cqg_ref[...]).astype(BF16)
    ckv = acc[:, MLA_Q_LORA:MLA_Q_LORA + MLA_KV_LORA]
    ckv_ref[...] = (ckv * lax.rsqrt(jnp.mean(ckv * ckv, axis=-1, keepdims=True) + EPS)
                    * ckvg_ref[...]).astype(BF16)
    kpe_ref[...] = acc[:, MLA_Q_LORA + MLA_KV_LORA:]


def _mla_in(h, w_in, cq_gain, ckv_gain, tm=1024):
    t, d = h.shape
    n = w_in.shape[2]
    n_pad = MLA_Q_LORA + MLA_KV_LORA + LANES
    return pl.pallas_call(
        _mla_in_kernel,
        grid=(t // tm,),
        in_specs=[
            pl.BlockSpec((tm, d), lambda i: (i, 0)),
            pl.BlockSpec((None, d, n), lambda i: (0, 0, 0)),
            pl.BlockSpec((1, MLA_Q_LORA), lambda i: (0, 0)),
            pl.BlockSpec((1, MLA_KV_LORA), lambda i: (0, 0)),
        ],
        out_specs=(
            pl.BlockSpec((tm, MLA_Q_LORA), lambda i: (i, 0)),
            pl.BlockSpec((tm, MLA_KV_LORA), lambda i: (i, 0)),
            pl.BlockSpec((tm, LANES), lambda i: (i, 0)),
        ),
        out_shape=(
            jax.ShapeDtypeStruct((t, MLA_Q_LORA), BF16),
            jax.ShapeDtypeStruct((t, MLA_KV_LORA), BF16),
            jax.ShapeDtypeStruct((t, LANES), F32),
        ),
        scratch_shapes=[pltpu.VMEM((d, n_pad), BF16)],
        compiler_params=_cparams(("arbitrary",)),
        name="mla_in",
    )(h, w_in, cq_gain.reshape(1, -1), ckv_gain.reshape(1, -1))


def _mla_q_kernel(a_ref, w_ref, g0_ref, g1_ref, tab_ref, o_ref, wb_ref):
    @pl.when(pl.program_id(0) == 0)
    def _():
        for h in range(MLA_HEADS):
            src = h * MLA_QK
            dst = h * MLA_QK_PAD
            rope_cols = w_ref[:, src + MLA_NOPE:src + MLA_QK].astype(BF16)
            wb_ref[:, dst:dst + MLA_NOPE] = w_ref[:, src:src + MLA_NOPE].astype(BF16)
            wb_ref[:, dst + MLA_NOPE:dst + MLA_QK] = rope_cols
            wb_ref[:, dst + MLA_QK:dst + MLA_QK_PAD] = rope_cols

    acc = jnp.dot(a_ref[...], wb_ref[...], preferred_element_type=F32)
    tab = tab_ref[...]
    cos_t = tab[:, 0:LANES]
    sin_t = tab[:, LANES:2 * LANES] + tab[:, 2 * LANES:3 * LANES]
    scale = math.log2(math.e) / math.sqrt(MLA_QK)
    g0 = g0_ref[...] * scale
    g1 = g1_ref[...] * scale
    outs = []
    for h in range(MLA_HEADS):
        base = h * MLA_QK_PAD
        x0 = acc[:, base:base + LANES]
        x1 = acc[:, base + LANES:base + 2 * LANES]
        ss = jnp.sum(x0 * x0 + 0.5 * (x1 * x1), axis=-1, keepdims=True)
        rs = lax.rsqrt(ss / MLA_QK + EPS)
        x1n = x1 * rs * g1
        q_rot = x1n * cos_t + pltpu.roll(x1n, MLA_ROPE // 2, 1) * sin_t
        outs.append(((x0 * rs * g0).astype(BF16), q_rot.astype(BF16)))
    for h, (q_nope, q_rope) in enumerate(outs):
        base = h * MLA_QK_PAD
        o_ref[:, base:base + LANES] = q_nope
        o_ref[:, base + LANES:base + 2 * LANES] = q_rope


def _mla_q(cq, w_q_up, q_gain, tab, seq, tm=512):
    t, k = cq.shape
    n = MLA_HEADS * MLA_QK_PAD
    per = seq // tm
    g0 = q_gain[:MLA_NOPE].reshape(1, LANES)
    g1 = jnp.concatenate([q_gain[MLA_NOPE:], q_gain[MLA_NOPE:]]).reshape(1, LANES)
    return pl.pallas_call(
        _mla_q_kernel,
        grid=(t // tm,),
        in_specs=[
            pl.BlockSpec((tm, k), lambda i: (i, 0)),
            pl.BlockSpec((None, k, MLA_HEADS * MLA_QK), lambda i: (0, 0, 0)),
            pl.BlockSpec((1, LANES), lambda i: (0, 0)),
            pl.BlockSpec((1, LANES), lambda i: (0, 0)),
            pl.BlockSpec((tm, 3 * LANES), lambda i: (i % per, 0)),
        ],
        out_specs=pl.BlockSpec((tm, n), lambda i: (i, 0)),
        out_shape=jax.ShapeDtypeStruct((t, n), BF16),
        scratch_shapes=[pltpu.VMEM((k, n), BF16)],
        compiler_params=_cparams(("arbitrary",)),
        name="mla_q_up",
    )(cq, w_q_up, g0, g1, tab)


def _mla_kv_kernel(a_ref, w_ref, kpe_ref, g0_ref, g1_ref, tab_ref, k_ref, v_ref, wb_ref):
    @pl.when(pl.program_id(0) == 0)
    def _():
        wb_ref[...] = w_ref[...].astype(BF16)

    acc = jnp.dot(a_ref[...], wb_ref[...], preferred_element_type=F32)
    tab = tab_ref[...]
    kpe = kpe_ref[...]
    ss_pe = jnp.sum(kpe * kpe, axis=-1, keepdims=True)
    g0 = g0_ref[...]
    kpe_rot = _rope_lanes(kpe * g1_ref[...], tab)
    outs = []
    for h in range(MLA_HEADS):
        base = h * (MLA_NOPE + MLA_V)
        kn = acc[:, base:base + MLA_NOPE]
        ss = jnp.sum(kn * kn, axis=-1, keepdims=True) + ss_pe
        rs = lax.rsqrt(ss / MLA_QK + EPS)
        outs.append(((kn * rs * g0).astype(BF16), (kpe_rot * rs).astype(BF16),
                     acc[:, base + MLA_NOPE:base + MLA_NOPE + MLA_V].astype(BF16)))
    ones = jnp.ones((acc.shape[0], MLA_V), BF16)
    for h, (k_nope, k_rope, v) in enumerate(outs):
        kb = h * MLA_QK_PAD
        k_ref[:, kb:kb + LANES] = k_nope
        k_ref[:, kb + LANES:kb + 2 * LANES] = k_rope
        vb = h * 2 * MLA_V
        v_ref[:, vb:vb + MLA_V] = v
        v_ref[:, vb + MLA_V:vb + 2 * MLA_V] = ones


def _mla_kv(ckv, w_kv_up, kpe, k_gain, tab, seq, tm=512):
    t, k = ckv.shape
    n = w_kv_up.shape[1]
    per = seq // tm
    g0 = k_gain[:MLA_NOPE].reshape(1, LANES)
    g1 = jnp.pad(k_gain[MLA_NOPE:], (0, LANES - MLA_ROPE)).reshape(1, LANES)
    return pl.pallas_call(
        _mla_kv_kernel,
        grid=(t // tm,),
        in_specs=[
            pl.BlockSpec((tm, k), lambda i: (i, 0)),
            pl.BlockSpec((k, n), lambda i: (0, 0)),
            pl.BlockSpec((tm, LANES), lambda i: (i, 0)),
            pl.BlockSpec((1, LANES), lambda i: (0, 0)),
            pl.BlockSpec((1, LANES), lambda i: (0, 0)),
            pl.BlockSpec((tm, 3 * LANES), lambda i: (i % per, 0)),
        ],
        out_specs=(
            pl.BlockSpec((tm, MLA_HEADS * MLA_QK_PAD), lambda i: (i, 0)),
            pl.BlockSpec((tm, MLA_HEADS * 2 * MLA_V), lambda i: (i, 0)),
        ),
        out_shape=(
            jax.ShapeDtypeStruct((t, MLA_HEADS * MLA_QK_PAD), BF16),
            jax.ShapeDtypeStruct((t, MLA_HEADS * 2 * MLA_V), BF16),
        ),
        scratch_shapes=[pltpu.VMEM((k, n), BF16)],
        compiler_params=_cparams(("arbitrary",)),
        name="mla_kv_up",
    )(ckv, w_kv_up, kpe, g0, g1, tab)


def _mla_attn_kernel(q_ref, k_ref, v_ref, o_ref, *, tq, heads):
    seq = q_ref.shape[0]
    nt = (((1,), (1,)), ((), ()))
    r = lax.broadcasted_iota(jnp.int32, (tq, tq), 0)
    c = lax.broadcasted_iota(jnp.int32, (tq, tq), 1)
    causal = c <= r
    for qi in range(seq // tq):
        rows = slice(qi * tq, (qi + 1) * tq)
        outs = []
        for hh in range(heads):
            qk_cols = slice(hh * MLA_QK_PAD, (hh + 1) * MLA_QK_PAD)
            v_cols = slice(hh * 2 * MLA_V, (hh + 1) * 2 * MLA_V)
            q = q_ref[rows, qk_cols]
            scores = []
            for j in range(qi + 1):
                s = lax.dot_general(q, k_ref[j * tq:(j + 1) * tq, qk_cols], nt, preferred_element_type=F32)
                if j == qi:
                    s = jnp.where(causal, s, NEG_BIG)
                scores.append(s)
            top = scores[0]
            for s in scores[1:]:
                top = jnp.maximum(top, s)
            m = jnp.max(top, axis=-1, keepdims=True)
            acc = None
            for j, s in enumerate(scores):
                p = jnp.exp2(s - m).astype(BF16)
                pv = jnp.dot(p, v_ref[j * tq:(j + 1) * tq, v_cols], preferred_element_type=F32)
                acc = pv if acc is None else acc + pv
            outs.append((acc[:, :MLA_V] / acc[:, MLA_V:]).astype(o_ref.dtype))
        for hh in range(heads):
            o_ref[rows, hh * MLA_V:(hh + 1) * MLA_V] = outs[hh]


def _mla_attn(q, k, v, tq=256, heads=4):
    b, s, _ = q.shape
    return pl.pallas_call(
        functools.partial(_mla_attn_kernel, tq=tq, heads=heads),
        grid=(b, MLA_HEADS // heads),
        in_specs=[
            pl.BlockSpec((None, s, heads * MLA_QK_PAD), lambda bi, h: (bi, 0, h)),
            pl.BlockSpec((None, s, heads * MLA_QK_PAD), lambda bi, h: (bi, 0, h)),
            pl.BlockSpec((None, s, heads * 2 * MLA_V), lambda bi, h: (bi, 0, h)),
        ],
        out_specs=pl.BlockSpec((None, s, heads * MLA_V), lambda bi, h: (bi, 0, h)),
        out_shape=jax.ShapeDtypeStruct((b, s, MLA_HEADS * MLA_V), BF16),
        compiler_params=_cparams(("arbitrary", "arbitrary")),
        name="mla_attn",
    )(q, k, v)


PLAN_COLS = 256
(PLAN_EXP, PLAN_FIRST, PLAN_NEXT, PLAN_NUSED, PLAN_NVALID, PLAN_BSTART, PLAN_COUNT,
 PLAN_NEXT2) = range(8)


def _dispatch_plan_tile(cnt):
    nblk = jnp.floor((cnt + (EXPERT_BLOCK - 1.0)) * (1.0 / EXPERT_BLOCK))
    e_r = lax.broadcasted_iota(jnp.int32, (LANES, LANES), 0)
    e_c = lax.broadcasted_iota(jnp.int32, (LANES, LANES), 1)
    before = jnp.where(e_r < e_c, 1.0, 0.0).astype(BF16)
    used = jnp.where(nblk > 0.0, 1.0, 0.0)
    sub = lax.broadcasted_iota(jnp.int32, (8, LANES), 0)
    prefix = jnp.dot(jnp.where(sub == 0, nblk, used).astype(BF16), before, preferred_element_type=F32)
    bstart = prefix[0:1]
    ordinal = prefix[1:2]
    bend = bstart + nblk
    n_used = jnp.max(bend, axis=-1, keepdims=True)
    row = lax.broadcasted_iota(jnp.int32, (PLAN_COLS, LANES), 0).astype(F32)
    lane = lax.broadcasted_iota(jnp.int32, (PLAN_COLS, LANES), 1).astype(F32)
    is_exp = lane < N_EXPERTS
    row1 = row[:, 0:1]
    done = jnp.logical_and(bend <= row, is_exp)
    blk_exp = jnp.minimum(jnp.sum(jnp.where(done, 1.0, 0.0), axis=-1, keepdims=True), N_EXPERTS - 1.0)
    mine = lane == blk_exp
    bstart_of = jnp.sum(jnp.where(mine, bstart, 0.0), axis=-1, keepdims=True)
    cnt_of = jnp.sum(jnp.where(mine, cnt, 0.0), axis=-1, keepdims=True)
    ord_of = jnp.sum(jnp.where(mine, ordinal, 0.0), axis=-1, keepdims=True)
    valid = row1 < n_used
    first = jnp.where(jnp.logical_and(valid, row1 == bstart_of), ord_of + 1.0, 0.0)
    nvalid = jnp.clip(cnt_of - EXPERT_BLOCK * (row1 - bstart_of), 0.0, float(EXPERT_BLOCK))
    nvalid = jnp.where(valid, nvalid, 0.0)
    usable = jnp.logical_and(nblk > 0.0, is_exp)

    def next_used(after):
        nxt = jnp.min(jnp.where(jnp.logical_and(lane > after, usable), lane, 999.0), axis=-1, keepdims=True)
        return jnp.where(nxt > 998.0, -1.0, nxt)

    nxt = next_used(blk_exp)
    nxt2 = jnp.where(nxt < 0.0, -1.0, next_used(nxt))
    bstart_col = jnp.sum(jnp.where(lane < row, nblk, 0.0), axis=-1, keepdims=True)
    cnt_col = jnp.sum(jnp.where(lane == row, cnt, 0.0), axis=-1, keepdims=True)
    tile = jnp.zeros((PLAN_COLS, LANES), F32)
    cols = {PLAN_EXP: blk_exp, PLAN_FIRST: first, PLAN_NEXT: nxt, PLAN_NUSED: n_used,
            PLAN_NVALID: nvalid, PLAN_BSTART: bstart_col, PLAN_COUNT: cnt_col, PLAN_NEXT2: nxt2}
    for k, val in cols.items():
        tile = jnp.where(lane == k, val, tile)
    return tile


def _router_kernel(x_ref, g_ref, sh_ref, sc_ref, wr_ref, br_ref, h_ref, info_ref, slots_ref, plan_ref,
                   carry_ref, idx_all):
    i = pl.program_id(0)
    tm = x_ref.shape[0]

    @pl.when(i == 0)
    def _():
        carry_ref[...] = jnp.zeros(carry_ref.shape, F32)

    h = _norm_mod(x_ref[...], g_ref[...], sh_ref[...], sc_ref[...])
    w = wr_ref[...]
    w_hi = w.astype(BF16)
    w_lo = (w - w_hi.astype(F32)).astype(BF16)
    h_hi = h.astype(BF16)
    h_lo = (h - h_hi.astype(F32)).astype(BF16)
    lg = (jnp.dot(h_hi, w_hi, preferred_element_type=F32)
          + jnp.dot(h_lo, w_hi, preferred_element_type=F32)
          + jnp.dot(h_hi, w_lo, preferred_element_type=F32)) + br_ref[...]

    lane = lax.broadcasted_iota(jnp.int32, (tm, LANES), 1).astype(F32)
    no_lane = float(LANES)
    gl = jnp.where(lane < N_GROUPS, lg, NEG_BIG)
    gmax = jnp.max(gl, axis=-1, keepdims=True)
    g_idx = jnp.min(jnp.where(gl == gmax, lane, no_lane), axis=-1, keepdims=True)
    g_p = 1.0 / jnp.sum(jnp.exp(gl - gmax), axis=-1, keepdims=True)
    lo_lane = N_GROUPS + g_idx * EXPERTS_PER_GROUP
    in_grp = jnp.logical_and(lane >= lo_lane, lane < lo_lane + EXPERTS_PER_GROUP)
    ev = jnp.where(in_grp, lg, NEG_BIG)
    v1 = jnp.max(ev, axis=-1, keepdims=True)
    i1 = jnp.min(jnp.where(ev == v1, lane, no_lane), axis=-1, keepdims=True)
    ev2 = jnp.where(lane == i1, NEG_BIG, ev)
    v2 = jnp.max(ev2, axis=-1, keepdims=True)
    i2 = jnp.min(jnp.where(ev2 == v2, lane, no_lane), axis=-1, keepdims=True)
    e2 = jnp.exp(v2 - v1)
    den = 1.0 + e2
    w1 = (1.0 / den) * g_p
    w2 = (e2 / den) * g_p
    id1 = i1 - N_GROUPS
    id2 = i2 - N_GROUPS

    oh1 = lane == id1
    oh2 = lane == id2
    both = jnp.where(jnp.logical_or(oh1, oh2), 1.0, 0.0)
    r = lax.broadcasted_iota(jnp.int32, (tm, tm), 0)
    c = lax.broadcasted_iota(jnp.int32, (tm, tm), 1)
    tril = jnp.where(c < r, 1.0, 0.0).astype(BF16)
    before = jnp.dot(tril, both.astype(BF16), preferred_element_type=F32) + carry_ref[...]
    rank1 = jnp.sum(jnp.where(oh1, before, 0.0), axis=-1, keepdims=True)
    rank2 = jnp.sum(jnp.where(oh2, before, 0.0), axis=-1, keepdims=True)
    carry_ref[...] = carry_ref[...] + jnp.sum(both, axis=0, keepdims=True)

    info = jnp.zeros((tm, LANES), F32)
    for col, val in enumerate((id1, id2, rank1, rank2, w1, w2)):
        info = jnp.where(lane == col, val, info)
    _store_row_tiles(h_ref, _pack_halves(h))
    info_ref[...] = info
    idx_all[:, pl.ds(pl.multiple_of(i * tm, tm), tm)] = jnp.transpose(info)[0:8]

    @pl.when(i == pl.num_programs(0) - 1)
    def _():
        plan = _dispatch_plan_tile(carry_ref[...])
        plan_ref[...] = jnp.transpose(plan)[0:8].astype(jnp.int32)
        first_row = plan[0:N_EXPERTS, PLAN_BSTART:PLAN_BSTART + 1] * float(EXPERT_BLOCK)
        chunk = 1024
        expert = lax.broadcasted_iota(jnp.int32, (N_EXPERTS, chunk), 0).astype(F32)
        for c0 in range(0, idx_all.shape[1], chunk):
            cols = slice(c0, c0 + chunk)
            rows = []
            for k in range(2):
                ids = idx_all[k:k + 1, cols]
                base = jnp.sum(jnp.where(expert == ids, first_row, 0.0), axis=0, keepdims=True)
                rows.append(base + idx_all[2 + k:3 + k, cols])
            slots_ref[:, cols] = jnp.concatenate(rows, axis=0).astype(jnp.int32)


def _router(x2, g, mod, seq, wr, br, tm=256):
    assert x2.shape[1] == 2 * ROW_TILE * LANES
    t, d = x2.shape
    per = seq // tm
    return pl.pallas_call(
        _router_kernel,
        grid=(t // tm,),
        in_specs=[
            pl.BlockSpec((tm, d), lambda i: (i, 0)),
            pl.BlockSpec((1, d), lambda i: (0, 0)),
            pl.BlockSpec((None, 1, d), lambda i: ((i // per) * 6 + 3, 0, 0)),
            pl.BlockSpec((None, 1, d), lambda i: ((i // per) * 6 + 4, 0, 0)),
            pl.BlockSpec((d, LANES), lambda i: (0, 0)),
            pl.BlockSpec((1, LANES), lambda i: (0, 0)),
        ],
        out_specs=(
            pl.BlockSpec((tm * ROW_TILE, LANES), lambda i: (i, 0)),
            pl.BlockSpec((tm, LANES), lambda i: (i, 0)),
            pl.BlockSpec((2, t), lambda i: (0, 0)),
            pl.BlockSpec((8, PLAN_COLS), lambda i: (0, 0)),
        ),
        out_shape=(
            jax.ShapeDtypeStruct((t * ROW_TILE, LANES), jnp.uint32),
            jax.ShapeDtypeStruct((t, LANES), F32),
            jax.ShapeDtypeStruct((2, t), jnp.int32),
            jax.ShapeDtypeStruct((8, PLAN_COLS), jnp.int32),
        ),
        scratch_shapes=[pltpu.VMEM((1, LANES), F32), pltpu.VMEM((8, t), F32)],
        compiler_params=_cparams(("arbitrary",)),
        name="moe_router",
    )(x2, g.reshape(1, d), mod, mod, wr, br)


GATHER_GROUP = 8


def _plan(plan, row, col):
    return plan[row * PLAN_COLS + col]


def _expert_kernel(plan, slots, h_hbm, wg_hbm, wu_hbm, wd_hbm, ys_ref,
                   row_tok, xbuf, xsem, sg, su, sd, wsem, wgb, wub, wdb, *, layer, n_tok):
    i = pl.program_id(0)
    nu = _plan(plan, PLAN_NUSED, 0)
    n_rows = row_tok.shape[0]

    def weight_copies(e, st):
        return (pltpu.make_async_copy(wg_hbm.at[layer, e], sg.at[st], wsem.at[st, 0]),
                pltpu.make_async_copy(wu_hbm.at[layer, e], su.at[st], wsem.at[st, 1]),
                pltpu.make_async_copy(wd_hbm.at[layer, e], sd.at[st], wsem.at[st, 2]))

    def n_groups(blk):
        return (_plan(plan, PLAN_NVALID, blk) + GATHER_GROUP - 1) // GATHER_GROUP

    def start_gather(blk, slot):
        base = blk * EXPERT_BLOCK

        def body(g, carry):
            for k in range(GATHER_GROUP):
                r = g * GATHER_GROUP + k
                tok = row_tok[base + r]
                src = pl.multiple_of(tok * ROW_TILE, ROW_TILE)
                dst = pl.multiple_of(r * ROW_TILE, ROW_TILE)
                pltpu.make_async_copy(h_hbm.at[pl.ds(src, ROW_TILE), :],
                                      xbuf.at[slot, pl.ds(dst, ROW_TILE), :], xsem.at[slot]).start()
            return carry

        lax.fori_loop(0, n_groups(blk), body, 0)

    def wait_gather(blk, slot):
        span = GATHER_GROUP * ROW_TILE

        def body(g, carry):
            pltpu.make_async_copy(h_hbm.at[pl.ds(0, span), :],
                                  xbuf.at[slot, pl.ds(0, span), :], xsem.at[slot]).wait()
            return carry

        lax.fori_loop(0, n_groups(blk), body, 0)

    @pl.when(i == 0)
    def _():
        for cp in weight_copies(_plan(plan, PLAN_EXP, 0), 0):
            cp.start(priority=1)
        second = _plan(plan, PLAN_NEXT, 0)

        @pl.when(second >= 0)
        def _():
            for cp in weight_copies(second, 1):
                cp.start(priority=1)

        xbuf[...] = jnp.zeros(xbuf.shape, xbuf.dtype)

        def pad_body(e, carry):
            end = _plan(plan, PLAN_BSTART, e) * EXPERT_BLOCK + _plan(plan, PLAN_COUNT, e)
            for k in range(GATHER_GROUP - 1):
                row_tok[jnp.minimum(end + k, n_rows - 1)] = 0
            return carry

        lax.fori_loop(0, N_EXPERTS, pad_body, 0)

        def fill_body(t, carry):
            row_tok[slots[0, t]] = t
            row_tok[slots[1, t]] = t
            return carry

        lax.fori_loop(0, n_tok, fill_body, 0, unroll=8)
        start_gather(0, 0)

    @pl.when(i < nu)
    def _():
        slot = i % 2

        @pl.when(i + 1 < nu)
        def _():
            start_gather(i + 1, 1 - slot)

        first = _plan(plan, PLAN_FIRST, i)

        @pl.when(first > 0)
        def _():
            st = (first - 1) % 2
            nxt2 = _plan(plan, PLAN_NEXT2, i)
            cps = weight_copies(_plan(plan, PLAN_EXP, i), st)
            nxt_cps = weight_copies(jnp.maximum(nxt2, 0), st)
            for cp, ncp, stage, dst in zip(cps, nxt_cps, (sg, su, sd), (wgb, wub, wdb)):
                cp.wait()
                dst[...] = stage[st].astype(BF16)

                @pl.when(nxt2 >= 0)
                def _():
                    ncp.start(priority=1)

        wait_gather(i, slot)
        parts = [_unpack_halves(w) for w in _load_row_tiles(xbuf.at[slot], EXPERT_BLOCK)]
        x_lo = jnp.concatenate([lo.astype(BF16) for lo, _ in parts], axis=1)
        x_hi = jnp.concatenate([hi.astype(BF16) for _, hi in parts], axis=1)
        half = wgb.shape[0] // 2
        g = (jnp.dot(x_lo, wgb[:half], preferred_element_type=F32)
             + jnp.dot(x_hi, wgb[half:], preferred_element_type=F32))
        u = (jnp.dot(x_lo, wub[:half], preferred_element_type=F32)
             + jnp.dot(x_hi, wub[half:], preferred_element_type=F32))
        hid = (_silu(g) * u).astype(BF16)
        _store_row_tiles(ys_ref, _pack_halves(jnp.dot(hid, wdb[...], preferred_element_type=F32)))

    @pl.when(i >= nu)
    def _():
        ys_ref[...] = jnp.zeros(ys_ref.shape, ys_ref.dtype)


def _expert_ffn(h2, w_gate, w_up, w_down, layer, plan, slots, n_blocks):
    t = h2.shape[0] // ROW_TILE
    d, f = w_gate.shape[2], w_gate.shape[3]
    n_rows = n_blocks * EXPERT_BLOCK
    grid_spec = pltpu.PrefetchScalarGridSpec(
        num_scalar_prefetch=2,
        grid=(n_blocks,),
        in_specs=[pl.BlockSpec(memory_space=pl.ANY)] * 4,
        out_specs=pl.BlockSpec((EXPERT_BLOCK * ROW_TILE, LANES), lambda i, *_: (i, 0)),
        scratch_shapes=[
            pltpu.SMEM((n_rows,), jnp.int32),
            pltpu.VMEM((2, EXPERT_BLOCK * ROW_TILE, LANES), jnp.uint32),
            pltpu.SemaphoreType.DMA((2,)),
            pltpu.VMEM((2, d, f), F32),
            pltpu.VMEM((2, d, f), F32),
            pltpu.VMEM((2, f, d), F32),
            pltpu.SemaphoreType.DMA((2, 3)),
            pltpu.VMEM((d, f), BF16),
            pltpu.VMEM((d, f), BF16),
            pltpu.VMEM((f, d), BF16),
        ],
    )
    weight_bytes = 3 * d * f * (2 * 4 + 2)
    vmem = weight_bytes + 8 * 1024 * 1024
    return pl.pallas_call(
        functools.partial(_expert_kernel, layer=layer, n_tok=t),
        grid_spec=grid_spec,
        out_shape=jax.ShapeDtypeStruct((n_rows * ROW_TILE, LANES), jnp.uint32),
        compiler_params=_cparams(("arbitrary",), vmem),
        name="moe_experts",
    )(plan, slots, h2, w_gate, w_up, w_down)


def _combine_kernel(slots, x_ref, info_ref, gate_ref, ys_hbm, *rest, tm, with_next):
    if with_next:
        ng_ref, nsh_ref, nsc_ref, o_ref, h_ref, buf, sem = rest
    else:
        o_ref, buf, sem = rest
    i = pl.program_id(0)
    last = pl.num_programs(0) - 1

    def row_copy(slot, k, src_row, r):
        src = pl.multiple_of(src_row * ROW_TILE, ROW_TILE)
        dst = pl.multiple_of(r * ROW_TILE, ROW_TILE)
        return pltpu.make_async_copy(ys_hbm.at[pl.ds(src, ROW_TILE), :],
                                     buf.at[slot, k, pl.ds(dst, ROW_TILE), :], sem.at[slot])

    def wait_rows(slot):
        for k in range(2):
            pltpu.make_async_copy(ys_hbm.at[pl.ds(0, tm * ROW_TILE), :], buf.at[slot, k], sem.at[slot]).wait()

    def start_gather(step, slot):
        base = step * tm

        def body(r, carry):
            for k in range(2):
                row_copy(slot, k, slots[k, base + r], r).start(priority=k)
            return carry

        lax.fori_loop(0, tm, body, 0, unroll=8)

    @pl.when(i == 0)
    def _():
        start_gather(0, 0)

    slot = i % 2

    @pl.when(i < last)
    def _():
        start_gather(i + 1, 1 - slot)

    wait_rows(slot)
    info = info_ref[...]
    w0 = info[:, 4:5]
    w1 = info[:, 5:6]
    half = x_ref.shape[1] // 2
    a_tiles = _load_row_tiles(buf.at[slot, 0], tm)
    b_tiles = _load_row_tiles(buf.at[slot, 1], tm)
    for s in range(ROW_TILE):
        a_lo, a_hi = _unpack_halves(a_tiles[s])
        b_lo, b_hi = _unpack_halves(b_tiles[s])
        lo = slice(s * LANES, (s + 1) * LANES)
        hi = slice(half + s * LANES, half + (s + 1) * LANES)
        o_ref[:, lo] = x_ref[:, lo] + gate_ref[:, lo] * (w0 * a_lo + w1 * b_lo)
        o_ref[:, hi] = x_ref[:, hi] + gate_ref[:, hi] * (w0 * a_hi + w1 * b_hi)
    if with_next:
        h_ref[...] = _norm_mod(o_ref[...], ng_ref[...], nsh_ref[...], nsc_ref[...]).astype(h_ref.dtype)


def _combine(x2, info, mod, seq, ys, slots, next_norm=None, tm=256):
    t, d = x2.shape
    per = seq // tm
    row = lambda i, *_: (i, 0)
    mod_row = lambda k: (lambda i, *_: ((i // per) * 6 + k, 0, 0))
    in_specs = [
        pl.BlockSpec((tm, d), row),
        pl.BlockSpec((tm, LANES), row),
        pl.BlockSpec((None, 1, d), mod_row(5)),
        pl.BlockSpec(memory_space=pl.ANY),
    ]
    args = [slots, x2, info, mod, ys]
    out_specs = pl.BlockSpec((tm, d), row)
    out_shape = jax.ShapeDtypeStruct((t, d), F32)
    if next_norm is not None:
        next_g, next_mod = next_norm
        in_specs += [pl.BlockSpec((1, d), lambda i, *_: (0, 0)),
                     pl.BlockSpec((None, 1, d), mod_row(0)), pl.BlockSpec((None, 1, d), mod_row(1))]
        args += [next_g.reshape(1, d), next_mod, next_mod]
        out_specs = (out_specs, pl.BlockSpec((tm, d), row))
        out_shape = (out_shape, jax.ShapeDtypeStruct((t, d), BF16))
    grid_spec = pltpu.PrefetchScalarGridSpec(
        num_scalar_prefetch=1,
        grid=(t // tm,),
        in_specs=in_specs,
        out_specs=out_specs,
        scratch_shapes=[
            pltpu.VMEM((2, 2, tm * ROW_TILE, LANES), jnp.uint32),
            pltpu.SemaphoreType.DMA((2,)),
        ],
    )
    return pl.pallas_call(
        functools.partial(_combine_kernel, tm=tm, with_next=next_norm is not None),
        grid_spec=grid_spec,
        out_shape=out_shape,
        compiler_params=_cparams(("arbitrary",)),
        name="moe_combine",
    )(*args)


def _hier_moe(x2, norm_g, mod, seq, w_rg, b_rg, w_re, b_re, w_gate, w_up, w_down, layer, next_norm=None):
    t, d = x2.shape
    pad = LANES - N_GROUPS - N_EXPERTS
    wr = jnp.concatenate([w_rg, w_re, jnp.zeros((d, pad), F32)], axis=1)
    br = jnp.concatenate([b_rg, b_re, jnp.zeros((pad,), F32)]).reshape(1, LANES)
    h2, info, slots, plan = _router(x2, norm_g, mod, seq, wr, br)
    n_assign = 2 * t
    n_blocks = (n_assign + N_EXPERTS * (EXPERT_BLOCK - 1) + EXPERT_BLOCK - 1) // EXPERT_BLOCK
    assert n_blocks <= PLAN_COLS
    plan = plan.reshape(-1)
    ys = _expert_ffn(h2, w_gate, w_up, w_down, layer, plan, slots, n_blocks)
    return _combine(x2, info, mod, seq, ys, slots, next_norm)


def kernel(x, c, ada_w, ada_b, norm1_g, norm2_g, dsa_w_in, dsa_q_gain, dsa_k_gain, dsa_w_out, mla_w_in, mla_cq_gain, mla_ckv_gain, mla_w_q_up, mla_w_kv_up, mla_q_gain, mla_k_gain, mla_w_out, router_group_w, router_group_b, router_expert_w, router_expert_b, expert_w_gate, expert_w_up, expert_w_down):
    b, s, d = x.shape
    t = b * s
    mods = _ada_mod(c, ada_w, ada_b)
    x2 = x.reshape(t, d)

    mod = mods[0]
    h = _normmod(x2, norm1_g[0], mod, s, 0, 1)
    gains = (dsa_q_gain[0], dsa_k_gain[0], dsa_k_gain[0])
    qkv = [[_dsa_proj(h, dsa_w_in[0], gains[w], g, w, b) for w in range(3)] for g in range(len(DIL_PAIRS))]
    og2 = _dsa_attn(*qkv[2], 2)
    og1 = _dsa_attn(*qkv[1], 1)
    o = _dsa_attn(*qkv[0], 0, others=(og1, og2))
    x2 = _resid_mm(o.reshape(t, DSA_WIDTH), dsa_w_out[0], x2, mod, s, 2)
    x2, h = _hier_moe(x2, norm2_g[0], mod, s, router_group_w[0], router_group_b[0],
                      router_expert_w[0], router_expert_b[0],
                      expert_w_gate, expert_w_up, expert_w_down, 0, next_norm=(norm1_g[1], mods[1]))

    mod = mods[1]
    cq, ckv, kpe = _mla_in(h, mla_w_in, mla_cq_gain[0], mla_ckv_gain[0])
    tab = _rope_tables(s)
    q = _mla_q(cq, mla_w_q_up, mla_q_gain[0], tab, s)
    k, v = _mla_kv(ckv, mla_w_kv_up[0], kpe, mla_k_gain[0], tab, s)
    o = _mla_attn(q.reshape(b, s, -1), k.reshape(b, s, -1), v.reshape(b, s, -1))
    x2 = _resid_mm(o.reshape(t, MLA_HEADS * MLA_V), mla_w_out[0], x2, mod, s, 2)
    x2 = _hier_moe(x2, norm2_g[1], mod, s, router_group_w[1], router_group_b[1],
                   router_expert_w[1], router_expert_b[1],
                   expert_w_gate, expert_w_up, expert_w_down, 1)
    return x2.reshape(b, s, d)
```
